```python
import math
import jax, jax.numpy as jnp
from jax import lax
import numpy as np

D_MODEL = 2048
BATCH = 1
SEQ = 16384
DEPTH = 1
DEC_BATCH = 128
DEC_SEQ = 1
PAST_LEN = 16384
PAGE_SIZE = 128

D_MIX = D_MODEL
ATT_W = D_MIX // 2
HEAD_DIM = 64
N_HEADS = ATT_W // HEAD_DIM
N_KV_HEADS = 4
Q_PER_KV = N_HEADS // N_KV_HEADS
WINDOW = 128
ATT_BLOCK = WINDOW
SSM_W = D_MIX - ATT_W
SSM_HEAD_DIM = 64
N_SSM_HEADS = SSM_W // SSM_HEAD_DIM
N_SSM_GROUPS = 2
SSM_HEADS_PER_GROUP = N_SSM_HEADS // N_SSM_GROUPS
D_STATE = 128
CONV_W = 4
CONV_DIM = SSM_W + 2 * N_SSM_GROUPS * D_STATE
SSD_CHUNK = 128
D_FF = -(-8 * D_MODEL // (3 * 256)) * 256
IN_W = ATT_W + 2 * N_KV_HEADS * HEAD_DIM + SSM_W + CONV_DIM + N_SSM_HEADS
EPS = 1e-6

kernel_name = 'hybrid_swa_sink_ssd_adaln_step'


def _rmsnorm(x, g):
    xf = x.astype(jnp.float32)
    y = xf * lax.rsqrt(jnp.mean(xf * xf, axis=-1, keepdims=True) + EPS)
    return (y * g.astype(jnp.float32)).astype(x.dtype)


def _alibi_slopes():
    s = 2.0 ** (-8.0 * np.arange(1, N_HEADS + 1) / N_HEADS)
    return jnp.asarray(s.astype(np.float32)).reshape(N_KV_HEADS, Q_PER_KV)


def _sink_attend(q, k, v, dist, valid, sinks):
    f32 = jnp.float32
    s = jnp.einsum('...tkgd,...skd->...kgts', q.astype(f32), k.astype(f32)) * (HEAD_DIM ** -0.5)
    s = s - _alibi_slopes()[:, :, None, None] * dist[..., None, None, :, :].astype(f32)
    s = jnp.where(valid[..., None, None, :, :], s, -jnp.inf)
    sink = sinks.astype(f32)[:, :, None, None]
    m = jnp.maximum(jnp.max(s, axis=-1, keepdims=True), sink)
    p = jnp.exp(s - m)
    denom = jnp.sum(p, axis=-1, keepdims=True) + jnp.exp(sink - m)
    return jnp.einsum('...kgts,...skd->...tkgd', p / denom, v.astype(f32))


def _attn_prompt(q, k, v, sinks):
    b, L = q.shape[:2]
    nb = L // ATT_BLOCK
    qb = q.reshape(b, nb, ATT_BLOCK, N_KV_HEADS, Q_PER_KV, HEAD_DIM)

    def with_prev(z):
        z = z.reshape(b, nb, ATT_BLOCK, N_KV_HEADS, HEAD_DIM)
        prev = jnp.pad(z[:, :-1], ((0, 0), (1, 0), (0, 0), (0, 0), (0, 0)))
        return jnp.concatenate([prev, z], axis=2)

    a = jnp.arange(ATT_BLOCK)[:, None]
    j = jnp.arange(2 * ATT_BLOCK)[None, :]
    dist = a + ATT_BLOCK - j
    blk = jnp.arange(nb)[:, None, None]
    valid = (dist >= 0) & (dist < WINDOW) & ((blk > 0) | (j >= ATT_BLOCK))
    o = _sink_attend(qb, with_prev(k), with_prev(v), dist[None], valid, sinks)
    return o.reshape(b, L, ATT_W)


def _attn_sample(q, k, v, past_k, past_v, sinks):
    db, L = q.shape[:2]
    wb = past_k.shape[1]
    kk = jnp.concatenate([past_k.astype(k.dtype), k], axis=1)
    vv = jnp.concatenate([past_v.astype(v.dtype), v], axis=1)
    dist = jnp.arange(L)[:, None] + wb - jnp.arange(wb + L)[None, :]
    valid = (dist >= 0) & (dist < WINDOW)
    o = _sink_attend(q, kk, vv, dist, valid, sinks)
    return o.reshape(db, L, ATT_W), kk[:, L:], vv[:, L:]


def _causal_conv(xbc, prefix, w, b):
    L = xbc.shape[1]
    xp = jnp.concatenate([prefix.astype(xbc.dtype), xbc], axis=1)
    y = sum(xp[:, i:i + L] * w[i] for i in range(CONV_W)) + b
    return jax.nn.silu(y), xp[:, L:]


def _ssd(x, dt, a, bmat, cmat, h0):
    b, L = x.shape[:2]
    T = min(SSD_CHUNK, L)
    pad = -(-L // T) * T - L
    if pad:
        x = jnp.pad(x, ((0, 0), (0, pad), (0, 0), (0, 0)))
        dt = jnp.pad(dt, ((0, 0), (0, pad), (0, 0)))
        bmat = jnp.pad(bmat, ((0, 0), (0, pad), (0, 0), (0, 0)))
        cmat = jnp.pad(cmat, ((0, 0), (0, pad), (0, 0), (0, 0)))
    Lp = L + pad
    nc = Lp // T
    G, R, P, N = N_SSM_GROUPS, SSM_HEADS_PER_GROUP, SSM_HEAD_DIM, D_STATE
    X = (x * dt[..., None]).reshape(b, nc, T, G, R, P)
    acum = jnp.cumsum((dt * a).reshape(b, nc, T, G, R), axis=2)
    Bc = bmat.reshape(b, nc, T, G, N)
    Cc = cmat.reshape(b, nc, T, G, N)
    acum_t = jnp.moveaxis(acum, 2, -1)
    seg = acum_t[..., :, None] - acum_t[..., None, :]
    causal = jnp.tril(jnp.ones((T, T), dtype=bool))
    lmat = jnp.exp(jnp.where(causal, seg, -jnp.inf))
    cb = jnp.einsum('bclgn,bcsgn->bcgls', Cc, Bc)
    y_diag = jnp.einsum('bcgrls,bcsgrp->bclgrp', cb[:, :, :, None] * lmat, X)
    decay_end = jnp.exp(acum[:, :, -1:] - acum)
    chunk_states = jnp.einsum('bclgn,bclgrp->bcgrpn', Bc, X * decay_end[..., None])
    chunk_decay = jnp.exp(acum[:, :, -1])

    def step(h, inp):
        s_c, d_c = inp
        return h * d_c[..., None, None] + s_c, h

    h_final, h_prev = lax.scan(step, h0.reshape(b, G, R, P, N),
                               (jnp.moveaxis(chunk_states, 1, 0), jnp.moveaxis(chunk_decay, 1, 0)))
    h_prev = jnp.moveaxis(h_prev, 0, 1)
    y_off = jnp.einsum('bclgn,bcgrpn->bclgrp', Cc, h_prev) * jnp.exp(acum)[..., None]
    y = (y_diag + y_off).reshape(b, Lp, N_SSM_HEADS, P)[:, :L]
    return y, h_final.reshape(b, N_SSM_HEADS, P, N)


def _mixer(u, past_k, past_v, conv_prefix, h0, w_in, attn_sinks, g_attn_out, conv_w, conv_b,
           dt_bias, a_log, d_skip, g_ssm_out, w_out):
    f32 = jnp.float32
    b, L, _ = u.shape
    kv_w = N_KV_HEADS * HEAD_DIM
    cuts = [ATT_W, ATT_W + kv_w, ATT_W + 2 * kv_w, ATT_W + 2 * kv_w + SSM_W,
            ATT_W + 2 * kv_w + SSM_W + CONV_DIM]
    q, k, v, z, xbc, dt_raw = jnp.split(u @ w_in, cuts, axis=-1)
    q = q.reshape(b, L, N_KV_HEADS, Q_PER_KV, HEAD_DIM)
    k = k.reshape(b, L, N_KV_HEADS, HEAD_DIM)
    v = v.reshape(b, L, N_KV_HEADS, HEAD_DIM)
    sinks = attn_sinks.reshape(N_KV_HEADS, Q_PER_KV)
    if past_k is None:
        att = _attn_prompt(q, k, v, sinks)
        wbuf = min(WINDOW, L)
        new_k, new_v = k[:, L - wbuf:], v[:, L - wbuf:]
    else:
        att, new_k, new_v = _attn_sample(q, k, v, past_k, past_v, sinks)
    att = _rmsnorm(att.astype(u.dtype), g_attn_out)
    xbc, new_conv = _causal_conv(xbc, conv_prefix, conv_w, conv_b)
    gn = N_SSM_GROUPS * D_STATE
    xs, bm, cm = jnp.split(xbc, [SSM_W, SSM_W + gn], axis=-1)
    xs = xs.astype(f32).reshape(b, L, N_SSM_HEADS, SSM_HEAD_DIM)
    dt = jax.nn.softplus(dt_raw.astype(f32) + dt_bias.astype(f32))
    a = -jnp.exp(a_log.astype(f32))
    y, h_new = _ssd(xs, dt, a,
                    bm.astype(f32).reshape(b, L, N_SSM_GROUPS, D_STATE),
                    cm.astype(f32).reshape(b, L, N_SSM_GROUPS, D_STATE),
                    h0.astype(f32))
    y = y + d_skip.astype(f32)[:, None] * xs
    y = y.reshape(b, L, SSM_W) * jax.nn.silu(z.astype(f32))
    ssm = _rmsnorm(y, g_ssm_out).astype(u.dtype)
    out = jnp.concatenate([att, ssm], axis=-1) @ w_out
    return out, new_k, new_v, new_conv, h_new


def _layer(x, c, past_k, past_v, conv_prefix, h0, w_ada, b_ada, g_pre_mix, g_post_mix, w_in,
           attn_sinks, g_attn_out, conv_w, conv_b, dt_bias, a_log, d_skip, g_ssm_out, w_out,
           g_pre_ffn, g_post_ffn, w_gate, w_up, w_down):
    mod = (jax.nn.silu(c) @ w_ada + b_ada)[:, None, :]
    sh1, sc1, gt1, sh2, sc2, gt2 = jnp.split(mod, 6, axis=-1)
    u = _rmsnorm(x, g_pre_mix) * (1 + sc1) + sh1
    mix, nk, nv, nconv, nh = _mixer(u, past_k, past_v, conv_prefix, h0, w_in, attn_sinks,
                                    g_attn_out, conv_w, conv_b, dt_bias, a_log, d_skip,
                                    g_ssm_out, w_out)
    x = x + gt1 * _rmsnorm(mix, g_post_mix)
    u = _rmsnorm(x, g_pre_ffn) * (1 + sc2) + sh2
    f = (jax.nn.silu(u @ w_gate) * (u @ w_up)) @ w_down
    x = x + gt2 * _rmsnorm(f, g_post_ffn)
    return x, nk, nv, nconv, nh


def setup_inputs(seed: int = 0) -> dict:
    key = jax.random.key(seed)
    ks = iter(jax.random.split(key, 40))
    f32 = jnp.float32

    def nrm(shape, scale):
        return jax.random.normal(next(ks), shape, f32) * scale

    def gain(shape):
        return 1.0 + nrm(shape, 0.02)

    wbuf = min(WINDOW, PAST_LEN)
    dt0 = jnp.exp(jax.random.uniform(next(ks), (DEPTH, N_SSM_HEADS), f32,
                                     math.log(1e-3), math.log(1e-1)))
    return {
        'x_prompt': nrm((BATCH, SEQ, D_MODEL), 1.0),
        'x_sample': nrm((DEC_BATCH, DEC_SEQ, D_MODEL), 1.0),
        'cache_k': nrm((DEPTH, DEC_BATCH, wbuf, N_KV_HEADS, HEAD_DIM), 1.0),
        'cache_v': nrm((DEPTH, DEC_BATCH, wbuf, N_KV_HEADS, HEAD_DIM), 1.0),
        'state_conv': nrm((DEPTH, DEC_BATCH, CONV_W - 1, CONV_DIM), 1.0),
        'state_ssm': nrm((DEPTH, DEC_BATCH, N_SSM_HEADS, SSM_HEAD_DIM, D_STATE), 0.5),
        'c_prompt': nrm((BATCH, D_MODEL), 1.0),
        'c_sample': nrm((DEC_BATCH, D_MODEL), 1.0),
        'w_ada': nrm((DEPTH, D_MODEL, 6 * D_MODEL), 0.5 * D_MODEL ** -0.5),
        'b_ada': nrm((DEPTH, 6 * D_MODEL), 0.01),
        'g_pre_mix': gain((DEPTH, D_MODEL)),
        'g_post_mix': gain((DEPTH, D_MODEL)),
        'w_in': nrm((DEPTH, D_MODEL, IN_W), D_MODEL ** -0.5),
        'attn_sinks': nrm((DEPTH, N_HEADS), 1.0),
        'g_attn_out': gain((DEPTH, ATT_W)),
        'conv_w': nrm((DEPTH, CONV_W, CONV_DIM), CONV_W ** -0.5),
        'conv_b': nrm((DEPTH, CONV_DIM), 0.02),
        'dt_bias': dt0 + jnp.log(-jnp.expm1(-dt0)),
        'a_log': jnp.log(jax.random.uniform(next(ks), (DEPTH, N_SSM_HEADS), f32, 1.0, 16.0)),
        'd_skip': 1.0 + nrm((DEPTH, N_SSM_HEADS), 0.1),
        'g_ssm_out': gain((DEPTH, SSM_W)),
        'w_out': nrm((DEPTH, D_MIX, D_MODEL), D_MIX ** -0.5),
        'g_pre_ffn': gain((DEPTH, D_MODEL)),
        'g_post_ffn': gain((DEPTH, D_MODEL)),
        'w_gate': nrm((DEPTH, D_MODEL, D_FF), D_MODEL ** -0.5),
        'w_up': nrm((DEPTH, D_MODEL, D_FF), D_MODEL ** -0.5),
        'w_down': nrm((DEPTH, D_FF, D_MODEL), D_FF ** -0.5),
    }


def reference(x_prompt, x_sample, cache_k, cache_v, state_conv, state_ssm, c_prompt, c_sample,
              w_ada, b_ada, g_pre_mix, g_post_mix, w_in, attn_sinks, g_attn_out, conv_w, conv_b,
              dt_bias, a_log, d_skip, g_ssm_out, w_out, g_pre_ffn, g_post_ffn, w_gate, w_up, w_down):
    weights = (w_ada, b_ada, g_pre_mix, g_post_mix, w_in, attn_sinks, g_attn_out, conv_w, conv_b,
               dt_bias, a_log, d_skip, g_ssm_out, w_out, g_pre_ffn, g_post_ffn, w_gate, w_up, w_down)
    yp, ys = x_prompt, x_sample
    bp = x_prompt.shape[0]
    kp_l, vp_l, cp_l, hp_l, ks_l, vs_l, cs_l, hs_l = [], [], [], [], [], [], [], []
    for l in range(DEPTH):
        lw = [w[l] for w in weights]
        conv0 = jnp.zeros((bp, CONV_W - 1, CONV_DIM), yp.dtype)
        h0 = jnp.zeros((bp, N_SSM_HEADS, SSM_HEAD_DIM, D_STATE), jnp.float32)
        yp, kp, vp, cp, hp = _layer(yp, c_prompt, None, None, conv0, h0, *lw)
        ys, ksm, vsm, csm, hsm = _layer(ys, c_sample, cache_k[l], cache_v[l], state_conv[l],
                                        state_ssm[l], *lw)
        kp_l.append(kp); vp_l.append(vp); cp_l.append(cp); hp_l.append(hp)
        ks_l.append(ksm); vs_l.append(vsm); cs_l.append(csm); hs_l.append(hsm)
    k_prompt = jnp.stack(kp_l).astype(cache_k.dtype)
    v_prompt = jnp.stack(vp_l).astype(cache_v.dtype)
    conv_prompt = jnp.stack(cp_l).astype(state_conv.dtype)
    ssm_prompt = jnp.stack(hp_l).astype(state_ssm.dtype)
    k_sample = jnp.stack(ks_l).astype(cache_k.dtype)
    v_sample = jnp.stack(vs_l).astype(cache_v.dtype)
    conv_sample = jnp.stack(cs_l).astype(state_conv.dtype)
    ssm_sample = jnp.stack(hs_l).astype(state_ssm.dtype)
    return (yp, ys, k_prompt, v_prompt, conv_prompt, ssm_prompt,
            k_sample, v_sample, conv_sample, ssm_sample)
```

```python
import functools

import numpy as np
import jax
import jax.numpy as jnp
from jax import lax
from jax.experimental import pallas as pl
from jax.experimental.pallas import tpu as pltpu

F32 = jnp.float32
BF16 = jnp.bfloat16

D_MODEL = 2048
ATT_W = 1024
HEAD_DIM = 64
N_HEADS = 16
N_KV = 4
Q_PER_KV = 4
KV_W = N_KV * HEAD_DIM
WINDOW = 128
SSM_W = 1024
SSM_P = 64
SSM_H = 16
SSM_G = 2
D_STATE = 128
CONV_W = 4
CONV_DIM = SSM_W + 2 * SSM_G * D_STATE
XBCDT_W = CONV_DIM + 128
D_FF = 5632
EPS = 1e-6
CHUNK = 128
NEG_INF = float("-inf")

VMEM_LIMIT = 56 * 1024 * 1024
LANES = 128

ROWS_IN = 512
ROWS_ATT = 512
ROWS_SSD = 512
ROWS_OUT = 512
ROWS_FFN = 512
FF_BLK = 512
SEQ_BLK = 8


def _params(sem, vmem=VMEM_LIMIT):
    return pltpu.CompilerParams(dimension_semantics=sem, vmem_limit_bytes=vmem)


def _silu(v):
    return v * (1.0 / (1.0 + jnp.exp(-v)))


def _softplus(v):
    return jnp.maximum(v, 0.0) + jnp.log1p(jnp.exp(-jnp.abs(v)))


def _rms(v, g):
    return v * lax.rsqrt(jnp.mean(v * v, axis=-1, keepdims=True) + EPS) * g


def _split3(v):
    hi = v.astype(BF16)
    r1 = v - hi.astype(F32)
    mid = r1.astype(BF16)
    lo = (r1 - mid.astype(F32)).astype(BF16)
    return hi, mid, lo


def _dot(a, b):
    return jnp.dot(a, b, preferred_element_type=F32)


def _dot_nt(a, b):
    return lax.dot_general(a, b, (((1,), (1,)), ((), ())), preferred_element_type=F32)


def _dot_tn(a, b):
    return lax.dot_general(a, b, (((0,), (0,)), ((), ())), preferred_element_type=F32)


def _expand_heads(v, e_mat):
    hi, mid, lo = _split3(v)
    return _dot(hi, e_mat) + _dot(mid, e_mat) + _dot(lo, e_mat)


def _head_expand_matrix():
    k = lax.broadcasted_iota(jnp.int32, (LANES, SSM_W), 0)
    c = lax.broadcasted_iota(jnp.int32, (LANES, SSM_W), 1)
    return jnp.where((c // SSM_P) == k, 1.0, 0.0).astype(BF16)


def _ada_kernel(cs_ref, cp_ref, w_ref, b_ref, ms_ref, mp_ref):
    w = w_ref[...].astype(BF16)
    b = b_ref[...]
    ms_ref[...] = _dot(_silu(cs_ref[...]).astype(BF16), w) + b
    mp_ref[...] = _dot(_silu(cp_ref[...]).astype(BF16), w) + b


def _ada(c_s, c_p8, w_ada, b_ada):
    n = w_ada.shape[1]
    bn = 1024
    return pl.pallas_call(
        _ada_kernel,
        grid=(n // bn,),
        in_specs=[
            pl.BlockSpec(c_s.shape, lambda j: (0, 0)),
            pl.BlockSpec(c_p8.shape, lambda j: (0, 0)),
            pl.BlockSpec((D_MODEL, bn), lambda j: (0, j)),
            pl.BlockSpec((1, bn), lambda j: (0, j)),
        ],
        out_specs=[
            pl.BlockSpec((c_s.shape[0], bn), lambda j: (0, j)),
            pl.BlockSpec((c_p8.shape[0], bn), lambda j: (0, j)),
        ],
        out_shape=[
            jax.ShapeDtypeStruct((c_s.shape[0], n), F32),
            jax.ShapeDtypeStruct((c_p8.shape[0], n), F32),
        ],
        compiler_params=_params(("parallel",)),
        name="ada",
    )(c_s, c_p8, w_ada, b_ada)


def _mod_spec(per_row, rows, col):
    if per_row:
        return pl.BlockSpec((rows, D_MODEL), lambda i, *_: (i, col))
    return pl.BlockSpec((8, D_MODEL), lambda i, *_: (0, col))


def _mod_rows(ref, per_row):
    return ref[...] if per_row else ref[0:1, :]


def _inproj_kernel(x_ref, g_ref, sh_ref, sc_ref, wq_ref, wkv_ref, wz_ref, wx_ref,
                   q_ref, kv_ref, z_ref, xb_ref, kvlast_ref, *, per_row, rows):
    u = _rms(x_ref[...], g_ref[...]) * (1.0 + _mod_rows(sc_ref, per_row)) + _mod_rows(sh_ref, per_row)
    u = u.astype(BF16)
    q_ref[...] = (_dot(u, wq_ref[...]) * (HEAD_DIM ** -0.5)).astype(BF16)
    kv = _dot(u, wkv_ref[...])
    kv_ref[...] = kv.astype(BF16)
    z_ref[...] = _dot(u, wz_ref[...])
    xb_ref[...] = _dot(u, wx_ref[...])

    @pl.when(pl.program_id(0) == pl.num_programs(0) - 1)
    def _():
        kvlast_ref[...] = kv[rows - WINDOW:, :]


def _const_spec(shape):
    return pl.BlockSpec(shape, lambda i, *_: (0,) * len(shape), pipeline_mode=pl.Buffered(1))


def _inproj(x, g, mod, per_row, wq, wkv, wz, wx):
    n_rows = x.shape[0]
    rows = min(ROWS_IN, n_rows)
    row_spec = lambda w: pl.BlockSpec((rows, w), lambda i: (i, 0))
    return pl.pallas_call(
        functools.partial(_inproj_kernel, per_row=per_row, rows=rows),
        grid=(n_rows // rows,),
        in_specs=[
            row_spec(D_MODEL),
            _const_spec((1, D_MODEL)),
            _mod_spec(per_row, rows, 0),
            _mod_spec(per_row, rows, 1),
            _const_spec(wq.shape), _const_spec(wkv.shape), _const_spec(wz.shape), _const_spec(wx.shape),
        ],
        out_specs=[
            row_spec(ATT_W), row_spec(2 * KV_W), row_spec(SSM_W), row_spec(XBCDT_W),
            pl.BlockSpec((WINDOW, 2 * KV_W), lambda i: (0, 0)),
        ],
        out_shape=[
            jax.ShapeDtypeStruct((n_rows, ATT_W), BF16),
            jax.ShapeDtypeStruct((n_rows, 2 * KV_W), BF16),
            jax.ShapeDtypeStruct((n_rows, SSM_W), F32),
            jax.ShapeDtypeStruct((n_rows, XBCDT_W), F32),
            jax.ShapeDtypeStruct((WINDOW, 2 * KV_W), F32),
        ],
        compiler_params=_params(("arbitrary",)),
        name="inproj",
    )(x, g, mod, mod, wq, wkv, wz, wx)


def _pair_blockdiag(lo_src, hi_src, lo_mask):
    zero = jnp.zeros_like(lo_src)
    return jnp.concatenate([jnp.where(lo_mask, lo_src, zero), jnp.where(lo_mask, zero, hi_src)], axis=0)


def _attn_prompt_kernel(slopes_ref, sinks_ref, q_ref, kvc_ref, kvp_ref, g_ref, o_ref,
                        bias_scr, kv_scr, att_scr, *, n_sub):
    i = pl.program_id(0)
    blk = WINDOW

    @pl.when(i == 0)
    def _():
        a = lax.broadcasted_iota(jnp.int32, (blk, 2 * blk), 0)
        j = lax.broadcasted_iota(jnp.int32, (blk, 2 * blk), 1)
        dist = a + blk - j
        valid = (dist >= 0) & (dist < WINDOW)
        distf = dist.astype(F32)
        for h in range(N_HEADS):
            b = jnp.where(valid, -(slopes_ref[h] * distf), NEG_INF)
            bias_scr[1, h] = b
            bias_scr[0, h] = jnp.where(j >= blk, b, NEG_INF)

    kv_scr[0:blk, :] = kvp_ref[...]
    kv_scr[blk:, :] = kvc_ref[...]
    lo_kv = lax.broadcasted_iota(jnp.int32, (2 * blk, LANES), 1) < HEAD_DIM
    lo_q = lax.broadcasted_iota(jnp.int32, (blk, LANES), 1) < HEAD_DIM

    def body(b, carry):
        r0 = pl.multiple_of(b * blk, blk)
        kv2 = kv_scr[pl.ds(r0, 2 * blk), :]
        variant = jnp.where(jnp.logical_and(i == 0, b == 0), 0, 1)
        for s in range(2):
            ks = kv2[:, s * LANES:(s + 1) * LANES].astype(F32)
            vs = kv2[:, KV_W + s * LANES:KV_W + (s + 1) * LANES].astype(F32)
            kr = pltpu.roll(ks, HEAD_DIM, 1)
            vr = pltpu.roll(vs, HEAD_DIM, 1)
            for e in range(2):
                kvh = 2 * s + e
                kbd = _pair_blockdiag(ks if e == 0 else kr, kr if e == 0 else ks, lo_kv).astype(BF16)
                vbd = _pair_blockdiag(vs if e == 0 else vr, vr if e == 0 else vs, lo_kv).astype(BF16)
                for pr in range(2):
                    c0 = kvh * Q_PER_KV * HEAD_DIM + pr * LANES
                    sc = _dot_nt(q_ref[pl.ds(r0, blk), c0:c0 + LANES], kbd)
                    ps, ls = [], []
                    for t in range(2):
                        h = kvh * Q_PER_KV + 2 * pr + t
                        st = sc[:, t * 2 * blk:(t + 1) * 2 * blk] + bias_scr[variant, h]
                        sink = sinks_ref[h]
                        m = jnp.maximum(jnp.max(st, axis=-1, keepdims=True), sink)
                        p = jnp.exp(st - m)
                        ls.append(jnp.sum(p, axis=-1, keepdims=True) + jnp.exp(sink - m))
                        ps.append(p.astype(BF16))
                    o = _dot(jnp.concatenate(ps, axis=1), vbd)
                    att_scr[:, c0:c0 + LANES] = o * jnp.where(lo_q, 1.0 / ls[0], 1.0 / ls[1])
        o_ref[pl.ds(r0, blk), :] = _rms(att_scr[...], g_ref[...]).astype(BF16)
        return carry

    lax.fori_loop(0, n_sub, body, 0)


def _attn_prompt(q, kv, slopes, sinks, g_att):
    n_rows = q.shape[0]
    rows = ROWS_ATT
    n_sub = rows // WINDOW
    smem = pl.BlockSpec(memory_space=pltpu.SMEM)
    return pl.pallas_call(
        functools.partial(_attn_prompt_kernel, n_sub=n_sub),
        grid=(n_rows // rows,),
        in_specs=[
            smem, smem,
            pl.BlockSpec((rows, ATT_W), lambda i: (i, 0)),
            pl.BlockSpec((rows, 2 * KV_W), lambda i: (i, 0)),
            pl.BlockSpec((WINDOW, 2 * KV_W), lambda i: (jnp.maximum(i * n_sub - 1, 0), 0)),
            pl.BlockSpec((1, ATT_W), lambda i: (0, 0)),
        ],
        out_specs=pl.BlockSpec((rows, ATT_W), lambda i: (i, 0)),
        out_shape=jax.ShapeDtypeStruct((n_rows, ATT_W), BF16),
        scratch_shapes=[
            pltpu.VMEM((2, N_HEADS, WINDOW, 2 * WINDOW), F32),
            pltpu.VMEM((rows + WINDOW, 2 * KV_W), BF16),
            pltpu.VMEM((WINDOW, ATT_W), F32),
        ],
        compiler_params=_params(("arbitrary",)),
        name="attn_prompt",
    )(slopes, sinks, q, kv, kv, g_att)


def _attn_sample_kernel(q_ref, kn_ref, vn_ref, ck_ref, cv_ref, slope_ref, sink_ref, g_ref,
                        o_ref, ko_ref, vo_ref, *, n_seq):
    r16 = lax.broadcasted_iota(jnp.int32, (N_HEADS, ATT_W), 0)
    c16 = lax.broadcasted_iota(jnp.int32, (N_HEADS, ATT_W), 1)
    own_head = (c16 // HEAD_DIM) == r16
    row_w = lax.broadcasted_iota(jnp.int32, (WINDOW, KV_W), 0)
    jj = lax.broadcasted_iota(jnp.int32, (N_HEADS, WINDOW), 1)
    bias = -(slope_ref[...] * (WINDOW - 1 - jj).astype(F32))
    sink = sink_ref[...]

    def fold4(v):
        return (v[:, 0:KV_W] + v[:, KV_W:2 * KV_W]) + (v[:, 2 * KV_W:3 * KV_W] + v[:, 3 * KV_W:])

    for b in range(n_seq):
        kw = jnp.where(row_w == WINDOW - 1, kn_ref[b:b + 1, :], pltpu.roll(ck_ref[b], WINDOW - 1, 0))
        vw = jnp.where(row_w == WINDOW - 1, vn_ref[b:b + 1, :], pltpu.roll(cv_ref[b], WINDOW - 1, 0))
        ko_ref[b] = kw
        vo_ref[b] = vw
        qb = jnp.broadcast_to(q_ref[b:b + 1, :].astype(F32), (N_HEADS, ATT_W))
        qbd = fold4(jnp.where(own_head, qb, 0.0))
        st = _dot_nt(qbd.astype(BF16), kw.astype(BF16)) + bias
        m = jnp.maximum(jnp.max(st, axis=-1, keepdims=True), sink)
        p = jnp.exp(st - m)
        l = jnp.sum(p, axis=-1, keepdims=True) + jnp.exp(sink - m)
        o = _dot(p.astype(BF16), vw.astype(BF16)) * (1.0 / l)
        o4 = jnp.concatenate([o, o, o, o], axis=1)
        att_row = jnp.sum(jnp.where(own_head, o4, 0.0), axis=0, keepdims=True)
        o_ref[b:b + 1, :] = _rms(att_row, g_ref[...]).astype(BF16)


def _attn_sample(q, kn, vn, ck, cv, slope_col, sink_col, g_att):
    n = q.shape[0]
    nb = SEQ_BLK
    cache_spec = pl.BlockSpec((nb, WINDOW, KV_W), lambda i: (i, 0, 0))
    row_spec = lambda w: pl.BlockSpec((nb, w), lambda i: (i, 0))
    full = lambda a: pl.BlockSpec(a.shape, lambda i: (0,) * a.ndim)
    return pl.pallas_call(
        functools.partial(_attn_sample_kernel, n_seq=nb),
        grid=(n // nb,),
        in_specs=[row_spec(ATT_W), row_spec(KV_W), row_spec(KV_W), cache_spec, cache_spec,
                  full(slope_col), full(sink_col), full(g_att)],
        out_specs=[row_spec(ATT_W), cache_spec, cache_spec],
        out_shape=[
            jax.ShapeDtypeStruct((n, ATT_W), BF16),
            jax.ShapeDtypeStruct(ck.shape, F32),
            jax.ShapeDtypeStruct(cv.shape, F32),
        ],
        compiler_params=_params(("parallel",)),
        name="attn_sample",
    )(q, kn, vn, ck, cv, slope_col, sink_col, g_att)


def _ssd_prompt_kernel(xb_ref, xprev_ref, z_ref, cw_ref, cb_ref, dtb_ref, alog_ref, dskip_ref, g_ref,
                       y_ref, hout_ref, xp_scr, xc_scr, ht_scr, e_scr, tri_scr, yd_scr, *, rows):
    i = pl.program_id(0)
    T = CHUNK

    @pl.when(i == 0)
    def _():
        ht_scr[...] = jnp.zeros_like(ht_scr)
        e_scr[...] = _head_expand_matrix()
        l = lax.broadcasted_iota(jnp.int32, (T, T), 0)
        s = lax.broadcasted_iota(jnp.int32, (T, T), 1)
        tri_scr[...] = jnp.where(s <= l, 1.0, 0.0).astype(BF16)

    xp_scr[0:8, :] = jnp.where(i == 0, 0.0, xprev_ref[:, 0:CONV_DIM])
    xp_scr[8:, :] = xb_ref[:, 0:CONV_DIM]
    acc = cb_ref[...] + xp_scr[pl.ds(8 - (CONV_W - 1), rows), :] * cw_ref[0:1, :]
    for k in range(1, CONV_W):
        acc = acc + xp_scr[pl.ds(8 - (CONV_W - 1) + k, rows), :] * cw_ref[k:k + 1, :]
    xc_scr[...] = _silu(acc)

    a_neg = -jnp.exp(alog_ref[...])
    e_mat = e_scr[...]
    tri = tri_scr[...]
    causal = lax.broadcasted_iota(jnp.int32, (T, T), 0) >= lax.broadcasted_iota(jnp.int32, (T, T), 1)
    lo = lax.broadcasted_iota(jnp.int32, (T, LANES), 1) < SSM_P
    gw = SSM_W // SSM_G

    for c in range(rows // T):
        rs = slice(c * T, (c + 1) * T)
        dt = _softplus(xb_ref[rs, CONV_DIM:XBCDT_W] + dtb_ref[...])
        dta = dt * a_neg
        hi, mid, lw = _split3(dta)
        acum = _dot(tri, hi) + _dot(tri, mid) + _dot(tri, lw)
        acum_e = _expand_heads(acum, e_mat)
        dt_e = _expand_heads(dt, e_mat)
        acum_t = acum.T
        xs = xc_scr[rs, 0:SSM_W]
        bm = xc_scr[rs, SSM_W:SSM_W + SSM_G * D_STATE]
        cm = xc_scr[rs, SSM_W + SSM_G * D_STATE:CONV_DIM]
        xdt = xs * dt_e
        last = acum_e[T - 1:T, :]
        xdt_b = xdt.astype(BF16)
        xdec_b = (xdt * jnp.exp(last - acum_e)).astype(BF16)
        bm_b = bm.astype(BF16)
        cm_b = cm.astype(BF16)
        cbs = [_dot_nt(cm_b[:, g * D_STATE:(g + 1) * D_STATE], bm_b[:, g * D_STATE:(g + 1) * D_STATE])
               for g in range(SSM_G)]
        for pr in range(SSM_H // 2):
            g = (2 * pr) // (SSM_H // SSM_G)
            ws = []
            for t in range(2):
                h = 2 * pr + t
                seg = jnp.broadcast_to(acum[:, h:h + 1], (T, T)) - acum_t[h:h + 1, :]
                ws.append((cbs[g] * jnp.exp(jnp.where(causal, seg, NEG_INF))).astype(BF16))
            xsl = xdt_b[:, pr * LANES:(pr + 1) * LANES]
            yd_scr[:, pr * LANES:(pr + 1) * LANES] = _dot(jnp.concatenate(ws, axis=1),
                                                          _pair_blockdiag(xsl, xsl, lo))
        yoff = []
        for g in range(SSM_G):
            gs = slice(g * gw, (g + 1) * gw)
            ht_g = ht_scr[:, gs]
            yoff.append(_dot(cm_b[:, g * D_STATE:(g + 1) * D_STATE], ht_g.astype(BF16)))
            cst = _dot_tn(bm_b[:, g * D_STATE:(g + 1) * D_STATE], xdec_b[:, gs])
            ht_scr[:, gs] = ht_g * jnp.exp(last[:, gs]) + cst
        y = yd_scr[...] + jnp.concatenate(yoff, axis=1) * jnp.exp(acum_e)
        y = y + dskip_ref[...] * xs
        y = y * _silu(z_ref[rs, :])
        y_ref[rs, :] = _rms(y, g_ref[...]).astype(BF16)

    @pl.when(i == pl.num_programs(0) - 1)
    def _():
        hout_ref[...] = ht_scr[...].T


def _ssd_prompt(xb, z, cw8, cb, dtb, alog, dskip_e, g_ssm):
    n_rows = xb.shape[0]
    rows = ROWS_SSD
    full = lambda a: pl.BlockSpec(a.shape, lambda i: (0,) * a.ndim)
    return pl.pallas_call(
        functools.partial(_ssd_prompt_kernel, rows=rows),
        grid=(n_rows // rows,),
        in_specs=[
            pl.BlockSpec((rows, XBCDT_W), lambda i: (i, 0)),
            pl.BlockSpec((8, XBCDT_W), lambda i: (jnp.maximum(i * (rows // 8) - 1, 0), 0)),
            pl.BlockSpec((rows, SSM_W), lambda i: (i, 0)),
            full(cw8), full(cb), full(dtb), full(alog), full(dskip_e), full(g_ssm),
        ],
        out_specs=[
            pl.BlockSpec((rows, SSM_W), lambda i: (i, 0)),
            pl.BlockSpec((SSM_W, D_STATE), lambda i: (0, 0)),
        ],
        out_shape=[
            jax.ShapeDtypeStruct((n_rows, SSM_W), BF16),
            jax.ShapeDtypeStruct((SSM_W, D_STATE), F32),
        ],
        scratch_shapes=[
            pltpu.VMEM((rows + 8, CONV_DIM), F32),
            pltpu.VMEM((rows, CONV_DIM), F32),
            pltpu.VMEM((D_STATE, SSM_W), F32),
            pltpu.VMEM((LANES, SSM_W), BF16),
            pltpu.VMEM((CHUNK, CHUNK), BF16),
            pltpu.VMEM((CHUNK, SSM_W), F32),
        ],
        compiler_params=_params(("arbitrary",)),
        name="ssd_prompt",
    )(xb, xb, z, cw8, cb, dtb, alog, dskip_e, g_ssm)


def _ssd_sample_kernel(xb_ref, sconv_ref, z_ref, h0_ref, cw_ref, cb_ref, dtb_ref, alog_ref, dskip_ref, g_ref,
                       y_ref, conv_ref, hout_ref, yoff_scr, *, n_seq):
    gw = SSM_W // SSM_G
    gn = SSM_G * D_STATE
    x_new = xb_ref[:, 0:CONV_DIM]
    taps = [sconv_ref[:, k * CONV_DIM:(k + 1) * CONV_DIM] for k in range(CONV_W - 1)] + [x_new]
    acc = cb_ref[...]
    for k in range(CONV_W):
        acc = acc + taps[k] * cw_ref[k:k + 1, :]
    for k in range(1, CONV_W):
        conv_ref[:, (k - 1) * CONV_DIM:k * CONV_DIM] = taps[k]
    xc = _silu(acc)
    xs = xc[:, 0:SSM_W]
    bm = xc[:, SSM_W:SSM_W + gn]
    cm = xc[:, SSM_W + gn:CONV_DIM]

    e_mat = _head_expand_matrix()
    dt = _softplus(xb_ref[:, CONV_DIM:XBCDT_W] + dtb_ref[...])
    dt_e = _expand_heads(dt, e_mat)
    dec_e = jnp.exp(_expand_heads(dt * (-jnp.exp(alog_ref[...])), e_mat))
    xdt = xs * dt_e

    lane_w = lax.broadcasted_iota(jnp.int32, (n_seq, SSM_W), 1)
    first_grp = lane_w < gw
    cbv = [jnp.sum(cm[:, g * D_STATE:(g + 1) * D_STATE] * bm[:, g * D_STATE:(g + 1) * D_STATE],
                   axis=-1, keepdims=True) for g in range(SSM_G)]
    cb_e = jnp.where(first_grp, cbv[0], cbv[1])

    pad = jnp.zeros((LANES - n_seq, SSM_W), F32)
    xdt_t = jnp.concatenate([xdt, pad], axis=0).T
    dec_t = jnp.concatenate([dec_e, pad], axis=0).T
    sub8 = lax.broadcasted_iota(jnp.int32, (8, D_STATE), 0)
    lane_r = lax.broadcasted_iota(jnp.int32, (8, SSM_W), 1)

    for b in range(n_seq):
        h0 = h0_ref[b]
        c_row = jnp.broadcast_to(cm[b:b + 1, :], (8, gn))
        c8 = jnp.where(sub8 == 0, c_row[:, 0:D_STATE], jnp.where(sub8 == 1, c_row[:, D_STATE:], 0.0))
        r = _dot_nt(c8.astype(BF16), h0.astype(BF16))
        yoff_scr[b:b + 1, :] = jnp.where(lane_r[0:1] < gw, r[0:1, :], r[1:2, :])
        b_row = bm[b:b + 1, :]
        for g in range(SSM_G):
            rs = slice(g * gw, (g + 1) * gw)
            dcol = jnp.broadcast_to(dec_t[rs, b:b + 1], (gw, D_STATE))
            xcol = jnp.broadcast_to(xdt_t[rs, b:b + 1], (gw, D_STATE))
            hout_ref[b, rs, :] = h0[rs, :] * dcol + xcol * b_row[:, g * D_STATE:(g + 1) * D_STATE]

    y = cb_e * xdt + yoff_scr[...] * dec_e
    y = y + dskip_ref[...] * xs
    y = y * _silu(z_ref[...])
    y_ref[...] = _rms(y, g_ref[...]).astype(BF16)


def _ssd_sample(xb, sconv, z, h0, cw8, cb, dtb, alog, dskip_e, g_ssm):
    n = xb.shape[0]
    nb = SEQ_BLK
    row_spec = lambda w: pl.BlockSpec((nb, w), lambda i: (i, 0))
    st_spec = pl.BlockSpec((nb, SSM_W, D_STATE), lambda i: (i, 0, 0))
    full = lambda a: pl.BlockSpec(a.shape, lambda i: (0,) * a.ndim)
    return pl.pallas_call(
        functools.partial(_ssd_sample_kernel, n_seq=nb),
        grid=(n // nb,),
        in_specs=[row_spec(XBCDT_W), row_spec((CONV_W - 1) * CONV_DIM), row_spec(SSM_W), st_spec,
                  full(cw8), full(cb), full(dtb), full(alog), full(dskip_e), full(g_ssm)],
        out_specs=[row_spec(SSM_W), row_spec((CONV_W - 1) * CONV_DIM), st_spec],
        out_shape=[
            jax.ShapeDtypeStruct((n, SSM_W), BF16),
            jax.ShapeDtypeStruct((n, (CONV_W - 1) * CONV_DIM), F32),
            jax.ShapeDtypeStruct(h0.shape, F32),
        ],
        scratch_shapes=[pltpu.VMEM((nb, SSM_W), F32)],
        compiler_params=_params(("parallel",)),
        name="ssd_sample",
    )(xb, sconv, z, h0, cw8, cb, dtb, alog, dskip_e, g_ssm)


def _outproj_kernel(att_ref, ssm_ref, x_ref, woa_ref, wos_ref, gpost_ref, gt1_ref, gpre_ref, sh2_ref, sc2_ref,
                    x1_ref, u2_ref, *, per_row):
    mix = _dot(att_ref[...], woa_ref[...]) + _dot(ssm_ref[...], wos_ref[...])
    x1 = x_ref[...] + _mod_rows(gt1_ref, per_row) * _rms(mix, gpost_ref[...])
    x1_ref[...] = x1
    u2 = _rms(x1, gpre_ref[...]) * (1.0 + _mod_rows(sc2_ref, per_row)) + _mod_rows(sh2_ref, per_row)
    u2_ref[...] = u2.astype(BF16)


def _outproj(att, ssm, x, mod, per_row, woa, wos, g_post, g_pre):
    n_rows = x.shape[0]
    rows = min(ROWS_OUT, n_rows)
    row_spec = lambda w: pl.BlockSpec((rows, w), lambda i: (i, 0))
    return pl.pallas_call(
        functools.partial(_outproj_kernel, per_row=per_row),
        grid=(n_rows // rows,),
        in_specs=[
            row_spec(ATT_W), row_spec(SSM_W), row_spec(D_MODEL),
            _const_spec(woa.shape), _const_spec(wos.shape),
            _const_spec((1, D_MODEL)),
            _mod_spec(per_row, rows, 2),
            _const_spec((1, D_MODEL)),
            _mod_spec(per_row, rows, 3),
            _mod_spec(per_row, rows, 4),
        ],
        out_specs=[row_spec(D_MODEL), row_spec(D_MODEL)],
        out_shape=[
            jax.ShapeDtypeStruct((n_rows, D_MODEL), F32),
            jax.ShapeDtypeStruct((n_rows, D_MODEL), BF16),
        ],
        compiler_params=_params(("parallel",)),
        name="outproj",
    )(att, ssm, x, woa, wos, g_post, mod, g_pre, mod, mod)


def _ffn_kernel(u_ref, x1_ref, wg_ref, wu_ref, wd_ref, gpost_ref, gt2_ref, y_ref, *, per_row):
    j = pl.program_id(1)
    u = u_ref[...]
    hid = (_silu(_dot(u, wg_ref[...])) * _dot(u, wu_ref[...])).astype(BF16)
    part = _dot(hid, wd_ref[...])

    @pl.when(j == 0)
    def _():
        y_ref[...] = part

    @pl.when(j > 0)
    def _():
        y_ref[...] += part

    @pl.when(j == pl.num_programs(1) - 1)
    def _():
        y_ref[...] = x1_ref[...] + _mod_rows(gt2_ref, per_row) * _rms(y_ref[...], gpost_ref[...])


def _ffn(u2, x1, mod, per_row, wg, wu, wd, g_post):
    n_rows = x1.shape[0]
    rows = min(ROWS_FFN, n_rows)
    fb = FF_BLK
    return pl.pallas_call(
        functools.partial(_ffn_kernel, per_row=per_row),
        grid=(n_rows // rows, D_FF // fb),
        in_specs=[
            pl.BlockSpec((rows, D_MODEL), lambda i, j: (i, 0)),
            pl.BlockSpec((rows, D_MODEL), lambda i, j: (i, 0)),
            pl.BlockSpec((D_MODEL, fb), lambda i, j: (0, j)),
            pl.BlockSpec((D_MODEL, fb), lambda i, j: (0, j)),
            pl.BlockSpec((fb, D_MODEL), lambda i, j: (j, 0)),
            pl.BlockSpec((1, D_MODEL), lambda i, j: (0, 0)),
            _mod_spec(per_row, rows, 5),
        ],
        out_specs=pl.BlockSpec((rows, D_MODEL), lambda i, j: (i, 0)),
        out_shape=jax.ShapeDtypeStruct((n_rows, D_MODEL), F32),
        compiler_params=_params(("parallel", "arbitrary")),
        name="ffn",
    )(u2, x1, wg, wu, wd, g_post, mod)


def _alibi_slopes():
    return (2.0 ** (-8.0 * np.arange(1, N_HEADS + 1) / N_HEADS)).astype(np.float32)


def kernel(x_prompt, x_sample, cache_k, cache_v, state_conv, state_ssm, c_prompt, c_sample, w_ada, b_ada, g_pre_mix, g_post_mix, w_in, attn_sinks, g_attn_out, conv_w, conv_b, dt_bias, a_log, d_skip, g_ssm_out, w_out, g_pre_ffn, g_post_ffn, w_gate, w_up, w_down):
    assert w_ada.shape[0] == 1, "one layer"
    n_p = x_prompt.shape[1]
    n_s = x_sample.shape[0]
    row = lambda v: v.reshape(1, -1)

    w_in_b = w_in[0].astype(BF16)
    o_k = ATT_W
    o_z = ATT_W + 2 * KV_W
    o_x = o_z + SSM_W
    o_dt = o_x + CONV_DIM
    wq = w_in_b[:, :o_k]
    wkv = w_in_b[:, o_k:o_z]
    wz = w_in_b[:, o_z:o_x]
    wx = jnp.pad(w_in_b[:, o_x:], ((0, 0), (0, LANES - SSM_H)))
    perm_heads = np.arange(N_HEADS).reshape(N_KV, Q_PER_KV).T.reshape(-1)
    perm_cols = (perm_heads[:, None] * HEAD_DIM + np.arange(HEAD_DIM)[None, :]).reshape(-1)
    wq_s = wq[:, perm_cols]
    w_out_b = w_out[0].astype(BF16)
    woa = w_out_b[:ATT_W]
    wos = w_out_b[ATT_W:]
    woa_s = woa[perm_cols]
    wg = w_gate[0].astype(BF16)
    wu = w_up[0].astype(BF16)
    wd = w_down[0].astype(BF16)
    g_att = row(g_attn_out[0])
    g_att_s = row(g_attn_out[0][perm_cols])
    slopes = jnp.asarray(_alibi_slopes())
    sinks = attn_sinks[0]
    cw8 = jnp.pad(conv_w[0], ((0, 8 - CONV_W), (0, 0)))
    cb = row(conv_b[0])
    dtb = jnp.pad(row(dt_bias[0]), ((0, 0), (0, LANES - SSM_H)))
    alog = jnp.pad(row(a_log[0]), ((0, 0), (0, LANES - SSM_H)))
    dskip_e = row(jnp.repeat(d_skip[0], SSM_P))
    g_ssm = row(g_ssm_out[0])

    c_p8 = jnp.pad(c_prompt, ((0, 8 - c_prompt.shape[0]), (0, 0)))
    mod_s, mod_p = _ada(c_sample, c_p8, w_ada[0], row(b_ada[0]))

    xp = x_prompt[0]
    q, kv, z, xb, kv_last = _inproj(xp, row(g_pre_mix[0]), mod_p, False, wq, wkv, wz, wx)
    att = _attn_prompt(q, kv, slopes, sinks, g_att)
    ssm, h_p = _ssd_prompt(xb, z, cw8, cb, dtb, alog, dskip_e, g_ssm)
    x1, u2 = _outproj(att, ssm, xp, mod_p, False, woa, wos, row(g_post_mix[0]), row(g_pre_ffn[0]))
    y_p = _ffn(u2, x1, mod_p, False, wg, wu, wd, row(g_post_ffn[0]))

    xs_ = x_sample[:, 0, :]
    q_s, _, z_s, xb_s, kv_new = _inproj(xs_, row(g_pre_mix[0]), mod_s, True, wq_s, wkv, wz, wx)
    att_s, k_s, v_s = _attn_sample(
        q_s, kv_new[:, :KV_W], kv_new[:, KV_W:],
        cache_k[0].reshape(n_s, WINDOW, KV_W), cache_v[0].reshape(n_s, WINDOW, KV_W),
        slopes[perm_heads].reshape(N_HEADS, 1), sinks[perm_heads].reshape(N_HEADS, 1), g_att_s)
    ssm_s, conv_s, h_s = _ssd_sample(
        xb_s, state_conv[0].reshape(n_s, (CONV_W - 1) * CONV_DIM), z_s,
        state_ssm[0].reshape(n_s, SSM_W, D_STATE), cw8, cb, dtb, alog, dskip_e, g_ssm)
    x1_s, u2_s = _outproj(att_s, ssm_s, xs_, mod_s, True, woa_s, wos, row(g_post_mix[0]), row(g_pre_ffn[0]))
    y_s = _ffn(u2_s, x1_s, mod_s, True, wg, wu, wd, row(g_post_ffn[0]))

    return (
        y_p[None],
        y_s[:, None, :],
        kv_last[:, :KV_W].reshape(1, 1, WINDOW, N_KV, HEAD_DIM),
        kv_last[:, KV_W:].reshape(1, 1, WINDOW, N_KV, HEAD_DIM),
        xb[n_p - (CONV_W - 1):, :CONV_DIM].reshape(1, 1, CONV_W - 1, CONV_DIM),
        h_p.reshape(1, 1, SSM_H, SSM_P, D_STATE),
        k_s.reshape(1, n_s, WINDOW, N_KV, HEAD_DIM),
        v_s.reshape(1, n_s, WINDOW, N_KV, HEAD_DIM),
        conv_s.reshape(1, n_s, CONV_W - 1, CONV_DIM),
        h_s.reshape(1, n_s, SSM_H, SSM_P, D_STATE),
    )
```

```python
import functools

import numpy as np
import jax
import jax.numpy as jnp
from jax import lax
from jax.experimental import pallas as pl
from jax.experimental.pallas import tpu as pltpu

F32 = jnp.float32
BF16 = jnp.bfloat16

D_MODEL = 2048
ATT_W = 1024
HEAD_DIM = 64
N_HEADS = 16
N_KV = 4
Q_PER_KV = 4
KV_W = N_KV * HEAD_DIM
WINDOW = 128
SSM_W = 1024
SSM_P = 64
SSM_H = 16
SSM_G = 2
D_STATE = 128
CONV_W = 4
CONV_DIM = SSM_W + 2 * SSM_G * D_STATE
XBCDT_W = CONV_DIM + 128
D_FF = 5632
EPS = 1e-6
CHUNK = 128
NEG_INF = float("-inf")

VMEM_LIMIT = 56 * 1024 * 1024
LANES = 128

ROWS_IN = 512
ROWS_ATT = 512
ROWS_SSD = 512
ROWS_OUT = 512
ROWS_FFN = 512
FF_BLK = 512
SEQ_BLK = 8


def _params(sem, vmem=VMEM_LIMIT):
    return pltpu.CompilerParams(dimension_semantics=sem, vmem_limit_bytes=vmem)


def _silu(v):
    h = 0.5 * v
    return h + h * jnp.tanh(h)


def _softplus(v):
    return jnp.maximum(v, 0.0) + jnp.log1p(jnp.exp(-jnp.abs(v)))


def _rms(v, g):
    return v * lax.rsqrt(jnp.mean(v * v, axis=-1, keepdims=True) + EPS) * g


def _split3(v):
    hi = v.astype(BF16)
    r1 = v - hi.astype(F32)
    mid = r1.astype(BF16)
    lo = (r1 - mid.astype(F32)).astype(BF16)
    return hi, mid, lo


def _dot(a, b):
    return jnp.dot(a, b, preferred_element_type=F32)


def _dot_nt(a, b):
    return lax.dot_general(a, b, (((1,), (1,)), ((), ())), preferred_element_type=F32)


def _dot_tn(a, b):
    return lax.dot_general(a, b, (((0,), (0,)), ((), ())), preferred_element_type=F32)


def _expand_heads(v, e_mat):
    hi, mid, lo = _split3(v)
    return _dot(hi, e_mat) + _dot(mid, e_mat) + _dot(lo, e_mat)


def _head_expand_matrix():
    k = lax.broadcasted_iota(jnp.int32, (LANES, SSM_W), 0)
    c = lax.broadcasted_iota(jnp.int32, (LANES, SSM_W), 1)
    return jnp.where((c // SSM_P) == k, 1.0, 0.0).astype(BF16)


def _ada_kernel(cs_ref, cp_ref, w_ref, b_ref, ms_ref, mp_ref):
    w = w_ref[...].astype(BF16)
    b = b_ref[...]
    ms_ref[...] = _dot(_silu(cs_ref[...]).astype(BF16), w) + b
    mp_ref[...] = _dot(_silu(cp_ref[...]).astype(BF16), w) + b


def _ada(c_s, c_p8, w_ada, b_ada):
    n = w_ada.shape[1]
    bn = 1024
    return pl.pallas_call(
        _ada_kernel,
        grid=(n // bn,),
        in_specs=[
            pl.BlockSpec(c_s.shape, lambda j: (0, 0)),
            pl.BlockSpec(c_p8.shape, lambda j: (0, 0)),
            pl.BlockSpec((D_MODEL, bn), lambda j: (0, j)),
            pl.BlockSpec((1, bn), lambda j: (0, j)),
        ],
        out_specs=[
            pl.BlockSpec((c_s.shape[0], bn), lambda j: (0, j)),
            pl.BlockSpec((c_p8.shape[0], bn), lambda j: (0, j)),
        ],
        out_shape=[
            jax.ShapeDtypeStruct((c_s.shape[0], n), F32),
            jax.ShapeDtypeStruct((c_p8.shape[0], n), F32),
        ],
        compiler_params=_params(("parallel",)),
        name="ada",
    )(c_s, c_p8, w_ada, b_ada)


def _mod_spec(per_row, rows, col):
    if per_row:
        return pl.BlockSpec((rows, D_MODEL), lambda i, *_: (i, col))
    return pl.BlockSpec((8, D_MODEL), lambda i, *_: (0, col))


def _mod_rows(ref, per_row):
    return ref[...] if per_row else ref[0:1, :]


def _inproj_kernel(x_ref, g_ref, sh_ref, sc_ref, wq_ref, wkv_ref, wz_ref, wx_ref,
                   q_ref, kv_ref, z_ref, xb_ref, kvlast_ref, *, per_row, rows):
    u = _rms(x_ref[...], g_ref[...]) * (1.0 + _mod_rows(sc_ref, per_row)) + _mod_rows(sh_ref, per_row)
    u = u.astype(BF16)
    q_ref[...] = (_dot(u, wq_ref[...]) * (HEAD_DIM ** -0.5)).astype(BF16)
    kv = _dot(u, wkv_ref[...])
    kv_ref[...] = kv.astype(BF16)
    z_ref[...] = _dot(u, wz_ref[...])
    xb_ref[...] = _dot(u, wx_ref[...])

    @pl.when(pl.program_id(0) == pl.num_programs(0) - 1)
    def _():
        kvlast_ref[...] = kv[rows - WINDOW:, :]


def _const_spec(shape):
    return pl.BlockSpec(shape, lambda i, *_: (0,) * len(shape), pipeline_mode=pl.Buffered(1))


def _inproj(x, g, mod, per_row, wq, wkv, wz, wx):
    n_rows = x.shape[0]
    rows = min(ROWS_IN, n_rows)
    row_spec = lambda w: pl.BlockSpec((rows, w), lambda i: (i, 0))
    return pl.pallas_call(
        functools.partial(_inproj_kernel, per_row=per_row, rows=rows),
        grid=(n_rows // rows,),
        in_specs=[
            row_spec(D_MODEL),
            _const_spec((1, D_MODEL)),
            _mod_spec(per_row, rows, 0),
            _mod_spec(per_row, rows, 1),
            _const_spec(wq.shape), _const_spec(wkv.shape), _const_spec(wz.shape), _const_spec(wx.shape),
        ],
        out_specs=[
            row_spec(ATT_W), row_spec(2 * KV_W), row_spec(SSM_W), row_spec(XBCDT_W),
            pl.BlockSpec((WINDOW, 2 * KV_W), lambda i: (0, 0)),
        ],
        out_shape=[
            jax.ShapeDtypeStruct((n_rows, ATT_W), BF16),
            jax.ShapeDtypeStruct((n_rows, 2 * KV_W), BF16),
            jax.ShapeDtypeStruct((n_rows, SSM_W), F32),
            jax.ShapeDtypeStruct((n_rows, XBCDT_W), F32),
            jax.ShapeDtypeStruct((WINDOW, 2 * KV_W), F32),
        ],
        compiler_params=_params(("arbitrary",)),
        name="inproj",
    )(x, g, mod, mod, wq, wkv, wz, wx)


def _pair_blockdiag(lo_src, hi_src, keep_lo, keep_hi):
    zero = jnp.zeros_like(lo_src)
    return jnp.concatenate([jnp.where(keep_lo, lo_src, zero), jnp.where(keep_hi, hi_src, zero)], axis=0)


def _attn_prompt_kernel(slopes_ref, sinks_ref, q_ref, kvc_ref, kvp_ref, g_ref, o_ref,
                        bias_scr, kv_scr, *, n_sub):
    i = pl.program_id(0)
    blk = WINDOW

    @pl.when(i == 0)
    def _():
        a = lax.broadcasted_iota(jnp.int32, (blk, 2 * blk), 0)
        j = lax.broadcasted_iota(jnp.int32, (blk, 2 * blk), 1)
        dist = a + blk - j
        valid = (dist >= 0) & (dist < WINDOW)
        distf = dist.astype(F32)
        for h in range(N_HEADS):
            kvh, g = divmod(h, Q_PER_KV)
            pr, t = divmod(g, 2)
            b = jnp.where(valid, -(slopes_ref[h] * distf), NEG_INF)
            b = jnp.where(j == 0, sinks_ref[h], b)
            rs = slice(pr * blk, (pr + 1) * blk)
            cs = slice(t * 2 * blk, (t + 1) * 2 * blk)
            bias_scr[1, kvh, rs, cs] = b
            bias_scr[0, kvh, rs, cs] = jnp.where((j >= blk) | (j == 0), b, NEG_INF)

    kv_scr[0:blk, :] = kvp_ref[...]
    kv_scr[blk:, :] = kvc_ref[...]
    lane = lax.broadcasted_iota(jnp.int32, (2 * blk, LANES), 1)
    not_sink = jnp.where(lax.broadcasted_iota(jnp.int32, (2 * blk, LANES), 0) != 0, 1.0, 0.0)
    keep_lo = lane < HEAD_DIM
    keep_hi = lane >= HEAD_DIM
    ones_cols = jnp.concatenate([jnp.where(lane < HEAD_DIM, 1.0, 0.0),
                                 jnp.where(lane >= HEAD_DIM, 1.0, 0.0)], axis=0).astype(BF16)
    n_chunk = 4
    rc = 2 * blk // n_chunk

    def body(b, carry):
        r0 = pl.multiple_of(b * blk, blk)
        kv2 = kv_scr[pl.ds(r0, 2 * blk), :]
        variant = jnp.where(jnp.logical_and(i == 0, b == 0), 0, 1)
        vbds, scores = [], []
        for s in range(2):
            ks = kv2[:, s * LANES:(s + 1) * LANES].astype(F32) * not_sink
            vs = kv2[:, KV_W + s * LANES:KV_W + (s + 1) * LANES].astype(F32) * not_sink
            kr = pltpu.roll(ks, HEAD_DIM, 1)
            vr = pltpu.roll(vs, HEAD_DIM, 1)
            for e in range(2):
                kvh = 2 * s + e
                c0 = kvh * Q_PER_KV * HEAD_DIM
                kbd = _pair_blockdiag(ks if e == 0 else kr, kr if e == 0 else ks, keep_lo, keep_hi)
                vbd = _pair_blockdiag(vs if e == 0 else vr, vr if e == 0 else vs, keep_lo, keep_hi)
                vbds.append(jnp.concatenate([vbd.astype(BF16), ones_cols], axis=1))
                q2 = jnp.concatenate([q_ref[pl.ds(r0, blk), c0:c0 + LANES],
                                      q_ref[pl.ds(r0, blk), c0 + LANES:c0 + 2 * LANES]], axis=0)
                scores.append(_dot_nt(q2, kbd.astype(BF16)) + bias_scr[variant, kvh])
        probs = []
        for kvh in range(N_KV):
            chunks = []
            for c in range(n_chunk):
                st = scores[kvh][c * rc:(c + 1) * rc, :]
                sl = st[:, 0:2 * blk]
                sr = st[:, 2 * blk:]
                p_l = jnp.exp(sl - jnp.max(sl, axis=-1, keepdims=True))
                p_r = jnp.exp(sr - jnp.max(sr, axis=-1, keepdims=True))
                chunks.append(jnp.concatenate([p_l, p_r], axis=1).astype(BF16))
            probs.append(jnp.concatenate(chunks, axis=0))
        outs = []
        for kvh in range(N_KV):
            o2 = _dot(probs[kvh], vbds[kvh])
            o = o2[:, 0:LANES] * (1.0 / o2[:, LANES:])
            outs += [o[0:blk], o[blk:]]
        att = jnp.concatenate(outs, axis=1)
        o_ref[pl.ds(r0, blk), :] = _rms(att, g_ref[...]).astype(BF16)
        return carry

    lax.fori_loop(0, n_sub, body, 0, unroll=2)


def _attn_prompt(q, kv, slopes, sinks, g_att):
    n_rows = q.shape[0]
    rows = ROWS_ATT
    n_sub = rows // WINDOW
    smem = pl.BlockSpec(memory_space=pltpu.SMEM)
    return pl.pallas_call(
        functools.partial(_attn_prompt_kernel, n_sub=n_sub),
        grid=(n_rows // rows,),
        in_specs=[
            smem, smem,
            pl.BlockSpec((rows, ATT_W), lambda i: (i, 0)),
            pl.BlockSpec((rows, 2 * KV_W), lambda i: (i, 0)),
            pl.BlockSpec((WINDOW, 2 * KV_W), lambda i: (jnp.maximum(i * n_sub - 1, 0), 0)),
            pl.BlockSpec((1, ATT_W), lambda i: (0, 0)),
        ],
        out_specs=pl.BlockSpec((rows, ATT_W), lambda i: (i, 0)),
        out_shape=jax.ShapeDtypeStruct((n_rows, ATT_W), BF16),
        scratch_shapes=[
            pltpu.VMEM((2, N_KV, 2 * WINDOW, 4 * WINDOW), F32),
            pltpu.VMEM((rows + WINDOW, 2 * KV_W), BF16),
        ],
        compiler_params=_params(("arbitrary",)),
        name="attn_prompt",
    )(slopes, sinks, q, kv, kv, g_att)


def _attn_sample_kernel(q_ref, kn_ref, vn_ref, ck_ref, cv_ref, slope_ref, sink_ref, g_ref,
                        o_ref, ko_ref, vo_ref, *, n_seq):
    r16 = lax.broadcasted_iota(jnp.int32, (N_HEADS, ATT_W), 0)
    c16 = lax.broadcasted_iota(jnp.int32, (N_HEADS, ATT_W), 1)
    own_head = (c16 // HEAD_DIM) == r16
    row_w = lax.broadcasted_iota(jnp.int32, (WINDOW, KV_W), 0)
    jj = lax.broadcasted_iota(jnp.int32, (N_HEADS, WINDOW), 1)
    bias = -(slope_ref[...] * (WINDOW - 1 - jj).astype(F32))
    sink = sink_ref[...]

    def fold4(v):
        return (v[:, 0:KV_W] + v[:, KV_W:2 * KV_W]) + (v[:, 2 * KV_W:3 * KV_W] + v[:, 3 * KV_W:])

    for b in range(n_seq):
        kw = jnp.where(row_w == WINDOW - 1, kn_ref[b:b + 1, :], pltpu.roll(ck_ref[b], WINDOW - 1, 0))
        vw = jnp.where(row_w == WINDOW - 1, vn_ref[b:b + 1, :], pltpu.roll(cv_ref[b], WINDOW - 1, 0))
        ko_ref[b] = kw
        vo_ref[b] = vw
        qb = jnp.broadcast_to(q_ref[b:b + 1, :].astype(F32), (N_HEADS, ATT_W))
        qbd = fold4(jnp.where(own_head, qb, 0.0))
        st = _dot_nt(qbd.astype(BF16), kw.astype(BF16)) + bias
        m = jnp.maximum(jnp.max(st, axis=-1, keepdims=True), sink)
        p = jnp.exp(st - m)
        l = jnp.sum(p, axis=-1, keepdims=True) + jnp.exp(sink - m)
        o = _dot(p.astype(BF16), vw.astype(BF16)) * (1.0 / l)
        o4 = jnp.concatenate([o, o, o, o], axis=1)
        att_row = jnp.sum(jnp.where(own_head, o4, 0.0), axis=0, keepdims=True)
        o_ref[b:b + 1, :] = _rms(att_row, g_ref[...]).astype(BF16)


def _attn_sample(q, kn, vn, ck, cv, slope_col, sink_col, g_att):
    n = q.shape[0]
    nb = SEQ_BLK
    cache_spec = pl.BlockSpec((nb, WINDOW, KV_W), lambda i: (i, 0, 0))
    row_spec = lambda w: pl.BlockSpec((nb, w), lambda i: (i, 0))
    full = lambda a: pl.BlockSpec(a.shape, lambda i: (0,) * a.ndim)
    return pl.pallas_call(
        functools.partial(_attn_sample_kernel, n_seq=nb),
        grid=(n // nb,),
        in_specs=[row_spec(ATT_W), row_spec(KV_W), row_spec(KV_W), cache_spec, cache_spec,
                  full(slope_col), full(sink_col), full(g_att)],
        out_specs=[row_spec(ATT_W), cache_spec, cache_spec],
        out_shape=[
            jax.ShapeDtypeStruct((n, ATT_W), BF16),
            jax.ShapeDtypeStruct(ck.shape, F32),
            jax.ShapeDtypeStruct(cv.shape, F32),
        ],
        compiler_params=_params(("parallel",)),
        name="attn_sample",
    )(q, kn, vn, ck, cv, slope_col, sink_col, g_att)


def _ssd_prompt_kernel(xb_ref, xprev_ref, z_ref, cw_ref, cb_ref, dtb_ref, alog_ref, dskip_ref, g_ref,
                       y_ref, hout_ref, xp_scr, xc_scr, ht_scr, e_scr, tri_scr, yd_scr, *, rows):
    i = pl.program_id(0)
    T = CHUNK

    @pl.when(i == 0)
    def _():
        ht_scr[...] = jnp.zeros_like(ht_scr)
        e_scr[...] = _head_expand_matrix()
        l = lax.broadcasted_iota(jnp.int32, (T, T), 0)
        s = lax.broadcasted_iota(jnp.int32, (T, T), 1)
        tri_scr[...] = jnp.where(s <= l, 1.0, 0.0).astype(BF16)

    for s in range(CONV_DIM // LANES):
        ls = slice(s * LANES, (s + 1) * LANES)
        xp_scr[s, 0:8, :] = jnp.where(i == 0, 0.0, xprev_ref[:, ls])
        xp_scr[s, 8:, :] = xb_ref[:, ls]
        acc = cb_ref[:, ls] + xp_scr[s, pl.ds(8 - (CONV_W - 1), rows), :] * cw_ref[0:1, ls]
        for k in range(1, CONV_W):
            acc = acc + xp_scr[s, pl.ds(8 - (CONV_W - 1) + k, rows), :] * cw_ref[k:k + 1, ls]
        xc_scr[:, ls] = _silu(acc)

    a_neg = -jnp.exp(alog_ref[...])
    e_mat = e_scr[...]
    tri = tri_scr[...]
    causal = lax.broadcasted_iota(jnp.int32, (T, T), 0) >= lax.broadcasted_iota(jnp.int32, (T, T), 1)
    lo = lax.broadcasted_iota(jnp.int32, (T, LANES), 1) < SSM_P
    gw = SSM_W // SSM_G

    for c in range(rows // T):
        rs = slice(c * T, (c + 1) * T)
        dt = _softplus(xb_ref[rs, CONV_DIM:XBCDT_W] + dtb_ref[...])
        dta = dt * a_neg
        hi, mid, lw = _split3(dta)
        acum = _dot(tri, hi) + _dot(tri, mid) + _dot(tri, lw)
        acum_e = _expand_heads(acum, e_mat)
        dt_e = _expand_heads(dt, e_mat)
        acum_t = acum.T
        xs = xc_scr[rs, 0:SSM_W]
        bm = xc_scr[rs, SSM_W:SSM_W + SSM_G * D_STATE]
        cm = xc_scr[rs, SSM_W + SSM_G * D_STATE:CONV_DIM]
        xdt = xs * dt_e
        last = acum_e[T - 1:T, :]
        xdt_b = xdt.astype(BF16)
        xdec_b = (xdt * jnp.exp(last - acum_e)).astype(BF16)
        bm_b = bm.astype(BF16)
        cm_b = cm.astype(BF16)
        cbs = [_dot_nt(cm_b[:, g * D_STATE:(g + 1) * D_STATE], bm_b[:, g * D_STATE:(g + 1) * D_STATE])
               for g in range(SSM_G)]
        for pr in range(SSM_H // 2):
            g = (2 * pr) // (SSM_H // SSM_G)
            ws = []
            for t in range(2):
                h = 2 * pr + t
                seg = jnp.broadcast_to(acum[:, h:h + 1], (T, T)) - acum_t[h:h + 1, :]
                ws.append((cbs[g] * jnp.exp(jnp.where(causal, seg, NEG_INF))).astype(BF16))
            xsl = xdt_b[:, pr * LANES:(pr + 1) * LANES]
            yd_scr[:, pr * LANES:(pr + 1) * LANES] = _dot(jnp.concatenate(ws, axis=1),
                                                          _pair_blockdiag(xsl, xsl, lo, ~lo))
        yoff = []
        for g in range(SSM_G):
            gs = slice(g * gw, (g + 1) * gw)
            ht_g = ht_scr[:, gs]
            yoff.append(_dot(cm_b[:, g * D_STATE:(g + 1) * D_STATE], ht_g.astype(BF16)))
            cst = _dot_tn(bm_b[:, g * D_STATE:(g + 1) * D_STATE], xdec_b[:, gs])
            ht_scr[:, gs] = ht_g * jnp.exp(last[:, gs]) + cst
        y = yd_scr[...] + jnp.concatenate(yoff, axis=1) * jnp.exp(acum_e)
        y = y + dskip_ref[...] * xs
        y = y * _silu(z_ref[rs, :])
        y_ref[rs, :] = _rms(y, g_ref[...]).astype(BF16)

    @pl.when(i == pl.num_programs(0) - 1)
    def _():
        hout_ref[...] = ht_scr[...].T


def _ssd_prompt(xb, z, cw8, cb, dtb, alog, dskip_e, g_ssm):
    n_rows = xb.shape[0]
    rows = ROWS_SSD
    full = lambda a: pl.BlockSpec(a.shape, lambda i: (0,) * a.ndim)
    return pl.pallas_call(
        functools.partial(_ssd_prompt_kernel, rows=rows),
        grid=(n_rows // rows,),
        in_specs=[
            pl.BlockSpec((rows, XBCDT_W), lambda i: (i, 0)),
            pl.BlockSpec((8, XBCDT_W), lambda i: (jnp.maximum(i * (rows // 8) - 1, 0), 0)),
            pl.BlockSpec((rows, SSM_W), lambda i: (i, 0)),
            full(cw8), full(cb), full(dtb), full(alog), full(dskip_e), full(g_ssm),
        ],
        out_specs=[
            pl.BlockSpec((rows, SSM_W), lambda i: (i, 0)),
            pl.BlockSpec((SSM_W, D_STATE), lambda i: (0, 0)),
        ],
        out_shape=[
            jax.ShapeDtypeStruct((n_rows, SSM_W), BF16),
            jax.ShapeDtypeStruct((SSM_W, D_STATE), F32),
        ],
        scratch_shapes=[
            pltpu.VMEM((CONV_DIM // LANES, rows + 8, LANES), F32),
            pltpu.VMEM((rows, CONV_DIM), F32),
            pltpu.VMEM((D_STATE, SSM_W), F32),
            pltpu.VMEM((LANES, SSM_W), BF16),
            pltpu.VMEM((CHUNK, CHUNK), BF16),
            pltpu.VMEM((CHUNK, SSM_W), F32),
        ],
        compiler_params=_params(("arbitrary",)),
        name="ssd_prompt",
    )(xb, xb, z, cw8, cb, dtb, alog, dskip_e, g_ssm)


def _ssd_sample_kernel(xb_ref, sconv_ref, z_ref, h0_ref, cw_ref, cb_ref, dtb_ref, alog_ref, dskip_ref, g_ref,
                       y_ref, conv_ref, hout_ref, yoff_scr, *, n_seq):
    gw = SSM_W // SSM_G
    gn = SSM_G * D_STATE
    x_new = xb_ref[:, 0:CONV_DIM]
    taps = [sconv_ref[:, k * CONV_DIM:(k + 1) * CONV_DIM] for k in range(CONV_W - 1)] + [x_new]
    acc = cb_ref[...]
    for k in range(CONV_W):
        acc = acc + taps[k] * cw_ref[k:k + 1, :]
    for k in range(1, CONV_W):
        conv_ref[:, (k - 1) * CONV_DIM:k * CONV_DIM] = taps[k]
    xc = _silu(acc)
    xs = xc[:, 0:SSM_W]
    bm = xc[:, SSM_W:SSM_W + gn]
    cm = xc[:, SSM_W + gn:CONV_DIM]

    e_mat = _head_expand_matrix()
    dt = _softplus(xb_ref[:, CONV_DIM:XBCDT_W] + dtb_ref[...])
    dt_e = _expand_heads(dt, e_mat)
    dec_e = jnp.exp(_expand_heads(dt * (-jnp.exp(alog_ref[...])), e_mat))
    xdt = xs * dt_e

    lane_w = lax.broadcasted_iota(jnp.int32, (n_seq, SSM_W), 1)
    first_grp = lane_w < gw
    cbv = [jnp.sum(cm[:, g * D_STATE:(g + 1) * D_STATE] * bm[:, g * D_STATE:(g + 1) * D_STATE],
                   axis=-1, keepdims=True) for g in range(SSM_G)]
    cb_e = jnp.where(first_grp, cbv[0], cbv[1])

    pad = jnp.zeros((LANES - n_seq, SSM_W), F32)
    xdt_t = jnp.concatenate([xdt, pad], axis=0).T
    dec_t = jnp.concatenate([dec_e, pad], axis=0).T
    sub8 = lax.broadcasted_iota(jnp.int32, (8, D_STATE), 0)
    lane_r = lax.broadcasted_iota(jnp.int32, (8, SSM_W), 1)

    for b in range(n_seq):
        h0 = h0_ref[b]
        c_row = jnp.broadcast_to(cm[b:b + 1, :], (8, gn))
        c8 = jnp.where(sub8 == 0, c_row[:, 0:D_STATE], jnp.where(sub8 == 1, c_row[:, D_STATE:], 0.0))
        r = _dot_nt(c8.astype(BF16), h0.astype(BF16))
        yoff_scr[b:b + 1, :] = jnp.where(lane_r[0:1] < gw, r[0:1, :], r[1:2, :])
        b_row = bm[b:b + 1, :]
        for g in range(SSM_G):
            rs = slice(g * gw, (g + 1) * gw)
            dcol = jnp.broadcast_to(dec_t[rs, b:b + 1], (gw, D_STATE))
            xcol = jnp.broadcast_to(xdt_t[rs, b:b + 1], (gw, D_STATE))
            hout_ref[b, rs, :] = h0[rs, :] * dcol + xcol * b_row[:, g * D_STATE:(g + 1) * D_STATE]

    y = cb_e * xdt + yoff_scr[...] * dec_e
    y = y + dskip_ref[...] * xs
    y = y * _silu(z_ref[...])
    y_ref[...] = _rms(y, g_ref[...]).astype(BF16)


def _ssd_sample(xb, sconv, z, h0, cw8, cb, dtb, alog, dskip_e, g_ssm):
    n = xb.shape[0]
    nb = SEQ_BLK
    row_spec = lambda w: pl.BlockSpec((nb, w), lambda i: (i, 0))
    st_spec = pl.BlockSpec((nb, SSM_W, D_STATE), lambda i: (i, 0, 0))
    full = lambda a: pl.BlockSpec(a.shape, lambda i: (0,) * a.ndim)
    return pl.pallas_call(
        functools.partial(_ssd_sample_kernel, n_seq=nb),
        grid=(n // nb,),
        in_specs=[row_spec(XBCDT_W), row_spec((CONV_W - 1) * CONV_DIM), row_spec(SSM_W), st_spec,
                  full(cw8), full(cb), full(dtb), full(alog), full(dskip_e), full(g_ssm)],
        out_specs=[row_spec(SSM_W), row_spec((CONV_W - 1) * CONV_DIM), st_spec],
        out_shape=[
            jax.ShapeDtypeStruct((n, SSM_W), BF16),
            jax.ShapeDtypeStruct((n, (CONV_W - 1) * CONV_DIM), F32),
            jax.ShapeDtypeStruct(h0.shape, F32),
        ],
        scratch_shapes=[pltpu.VMEM((nb, SSM_W), F32)],
        compiler_params=_params(("parallel",)),
        name="ssd_sample",
    )(xb, sconv, z, h0, cw8, cb, dtb, alog, dskip_e, g_ssm)


def _outproj_kernel(att_ref, ssm_ref, x_ref, woa_ref, wos_ref, gpost_ref, gt1_ref, gpre_ref, sh2_ref, sc2_ref,
                    x1_ref, u2_ref, *, per_row):
    mix = _dot(att_ref[...], woa_ref[...]) + _dot(ssm_ref[...], wos_ref[...])
    x1 = x_ref[...] + _mod_rows(gt1_ref, per_row) * _rms(mix, gpost_ref[...])
    x1_ref[...] = x1
    u2 = _rms(x1, gpre_ref[...]) * (1.0 + _mod_rows(sc2_ref, per_row)) + _mod_rows(sh2_ref, per_row)
    u2_ref[...] = u2.astype(BF16)


def _outproj(att, ssm, x, mod, per_row, woa, wos, g_post, g_pre):
    n_rows = x.shape[0]
    rows = min(ROWS_OUT, n_rows)
    row_spec = lambda w: pl.BlockSpec((rows, w), lambda i: (i, 0))
    return pl.pallas_call(
        functools.partial(_outproj_kernel, per_row=per_row),
        grid=(n_rows // rows,),
        in_specs=[
            row_spec(ATT_W), row_spec(SSM_W), row_spec(D_MODEL),
            _const_spec(woa.shape), _const_spec(wos.shape),
            _const_spec((1, D_MODEL)),
            _mod_spec(per_row, rows, 2),
            _const_spec((1, D_MODEL)),
            _mod_spec(per_row, rows, 3),
            _mod_spec(per_row, rows, 4),
        ],
        out_specs=[row_spec(D_MODEL), row_spec(D_MODEL)],
        out_shape=[
            jax.ShapeDtypeStruct((n_rows, D_MODEL), F32),
            jax.ShapeDtypeStruct((n_rows, D_MODEL), BF16),
        ],
        compiler_params=_params(("parallel",)),
        name="outproj",
    )(att, ssm, x, woa, wos, g_post, mod, g_pre, mod, mod)


def _ffn_kernel(u_ref, x1_ref, wg_ref, wu_ref, wd_ref, gpost_ref, gt2_ref, y_ref, *, per_row):
    j = pl.program_id(1)

    @pl.when(j == 0)
    def _():
        y_ref[...] = jnp.zeros_like(y_ref)

    u = u_ref[...]
    hid = (_silu(_dot(u, wg_ref[...])) * _dot(u, wu_ref[...])).astype(BF16)
    y_ref[...] += _dot(hid, wd_ref[...])

    @pl.when(j == pl.num_programs(1) - 1)
    def _():
        y_ref[...] = x1_ref[...] + _mod_rows(gt2_ref, per_row) * _rms(y_ref[...], gpost_ref[...])


def _ffn(u2, x1, mod, per_row, wg, wu, wd, g_post):
    n_rows = x1.shape[0]
    rows = min(ROWS_FFN, n_rows)
    fb = FF_BLK
    return pl.pallas_call(
        functools.partial(_ffn_kernel, per_row=per_row),
        grid=(n_rows // rows, D_FF // fb),
        in_specs=[
            pl.BlockSpec((rows, D_MODEL), lambda i, j: (i, 0)),
            pl.BlockSpec((rows, D_MODEL), lambda i, j: (i, 0)),
            pl.BlockSpec((D_MODEL, fb), lambda i, j: (0, j)),
            pl.BlockSpec((D_MODEL, fb), lambda i, j: (0, j)),
            pl.BlockSpec((fb, D_MODEL), lambda i, j: (j, 0)),
            pl.BlockSpec((1, D_MODEL), lambda i, j: (0, 0)),
            _mod_spec(per_row, rows, 5),
        ],
        out_specs=pl.BlockSpec((rows, D_MODEL), lambda i, j: (i, 0)),
        out_shape=jax.ShapeDtypeStruct((n_rows, D_MODEL), F32),
        compiler_params=_params(("parallel", "arbitrary")),
        name="ffn",
    )(u2, x1, wg, wu, wd, g_post, mod)


def _alibi_slopes():
    return (2.0 ** (-8.0 * np.arange(1, N_HEADS + 1) / N_HEADS)).astype(np.float32)


def kernel(x_prompt, x_sample, cache_k, cache_v, state_conv, state_ssm, c_prompt, c_sample, w_ada, b_ada, g_pre_mix, g_post_mix, w_in, attn_sinks, g_attn_out, conv_w, conv_b, dt_bias, a_log, d_skip, g_ssm_out, w_out, g_pre_ffn, g_post_ffn, w_gate, w_up, w_down):
    assert w_ada.shape[0] == 1, "one layer"
    n_p = x_prompt.shape[1]
    n_s = x_sample.shape[0]
    row = lambda v: v.reshape(1, -1)

    w_in_b = w_in[0].astype(BF16)
    o_k = ATT_W
    o_z = ATT_W + 2 * KV_W
    o_x = o_z + SSM_W
    o_dt = o_x + CONV_DIM
    wq = w_in_b[:, :o_k]
    wkv = w_in_b[:, o_k:o_z]
    wz = w_in_b[:, o_z:o_x]
    wx = jnp.pad(w_in_b[:, o_x:], ((0, 0), (0, LANES - SSM_H)))
    perm_heads = np.arange(N_HEADS).reshape(N_KV, Q_PER_KV).T.reshape(-1)
    perm_cols = (perm_heads[:, None] * HEAD_DIM + np.arange(HEAD_DIM)[None, :]).reshape(-1)
    wq_s = wq[:, perm_cols]
    w_out_b = w_out[0].astype(BF16)
    woa = w_out_b[:ATT_W]
    wos = w_out_b[ATT_W:]
    woa_s = woa[perm_cols]
    wg = w_gate[0].astype(BF16)
    wu = w_up[0].astype(BF16)
    wd = w_down[0].astype(BF16)
    g_att = row(g_attn_out[0])
    g_att_s = row(g_attn_out[0][perm_cols])
    slopes = jnp.asarray(_alibi_slopes())
    sinks = attn_sinks[0]
    cw8 = jnp.pad(conv_w[0], ((0, 8 - CONV_W), (0, 0)))
    cb = row(conv_b[0])
    dtb = jnp.pad(row(dt_bias[0]), ((0, 0), (0, LANES - SSM_H)))
    alog = jnp.pad(row(a_log[0]), ((0, 0), (0, LANES - SSM_H)))
    dskip_e = row(jnp.repeat(d_skip[0], SSM_P))
    g_ssm = row(g_ssm_out[0])

    c_p8 = jnp.pad(c_prompt, ((0, 8 - c_prompt.shape[0]), (0, 0)))
    mod_s, mod_p = _ada(c_sample, c_p8, w_ada[0], row(b_ada[0]))

    xp = x_prompt[0]
    q, kv, z, xb, kv_last = _inproj(xp, row(g_pre_mix[0]), mod_p, False, wq, wkv, wz, wx)
    att = _attn_prompt(q, kv, slopes, sinks, g_att)
    ssm, h_p = _ssd_prompt(xb, z, cw8, cb, dtb, alog, dskip_e, g_ssm)
    x1, u2 = _outproj(att, ssm, xp, mod_p, False, woa, wos, row(g_post_mix[0]), row(g_pre_ffn[0]))
    y_p = _ffn(u2, x1, mod_p, False, wg, wu, wd, row(g_post_ffn[0]))

    xs_ = x_sample[:, 0, :]
    q_s, _, z_s, xb_s, kv_new = _inproj(xs_, row(g_pre_mix[0]), mod_s, True, wq_s, wkv, wz, wx)
    att_s, k_s, v_s = _attn_sample(
        q_s, kv_new[:, :KV_W], kv_new[:, KV_W:],
        cache_k[0].reshape(n_s, WINDOW, KV_W), cache_v[0].reshape(n_s, WINDOW, KV_W),
        slopes[perm_heads].reshape(N_HEADS, 1), sinks[perm_heads].reshape(N_HEADS, 1), g_att_s)
    ssm_s, conv_s, h_s = _ssd_sample(
        xb_s, state_conv[0].reshape(n_s, (CONV_W - 1) * CONV_DIM), z_s,
        state_ssm[0].reshape(n_s, SSM_W, D_STATE), cw8, cb, dtb, alog, dskip_e, g_ssm)
    x1_s, u2_s = _outproj(att_s, ssm_s, xs_, mod_s, True, woa_s, wos, row(g_post_mix[0]), row(g_pre_ffn[0]))
    y_s = _ffn(u2_s, x1_s, mod_s, True, wg, wu, wd, row(g_post_ffn[0]))

    return (
        y_p[None],
        y_s[:, None, :],
        kv_last[:, :KV_W].reshape(1, 1, WINDOW, N_KV, HEAD_DIM),
        kv_last[:, KV_W:].reshape(1, 1, WINDOW, N_KV, HEAD_DIM),
        xb[n_p - (CONV_W - 1):, :CONV_DIM].reshape(1, 1, CONV_W - 1, CONV_DIM),
        h_p.reshape(1, 1, SSM_H, SSM_P, D_STATE),
        k_s.reshape(1, n_s, WINDOW, N_KV, HEAD_DIM),
        v_s.reshape(1, n_s, WINDOW, N_KV, HEAD_DIM),
        conv_s.reshape(1, n_s, CONV_W - 1, CONV_DIM),
        h_s.reshape(1, n_s, SSM_H, SSM_P, D_STATE),
    )
```

```python
import functools

import numpy as np
import jax
import jax.numpy as jnp
from jax import lax
from jax.experimental import pallas as pl
from jax.experimental.pallas import tpu as pltpu

F32 = jnp.float32
BF16 = jnp.bfloat16

D_MODEL = 2048
ATT_W = 1024
HEAD_DIM = 64
N_HEADS = 16
N_KV = 4
Q_PER_KV = 4
KV_W = N_KV * HEAD_DIM
WINDOW = 128
SSM_W = 1024
SSM_P = 64
SSM_H = 16
SSM_G = 2
D_STATE = 128
CONV_W = 4
CONV_DIM = SSM_W + 2 * SSM_G * D_STATE
XBCDT_W = CONV_DIM + 128
D_FF = 5632
EPS = 1e-6
CHUNK = 128
NEG_INF = float("-inf")

VMEM_LIMIT = 56 * 1024 * 1024
LANES = 128

ROWS_IN = 512
ROWS_ATT = 512
ROWS_SSD = 512
ROWS_OUT = 512
SUB_OUT = 128
ROWS_FFN = 512
FF_BLK = 512
SEQ_BLK = 8


def _params(sem, vmem=VMEM_LIMIT):
    return pltpu.CompilerParams(dimension_semantics=sem, vmem_limit_bytes=vmem)


def _silu(v):
    h = 0.5 * v
    return h + h * jnp.tanh(h)


def _softplus(v):
    return jnp.maximum(v, 0.0) + jnp.log1p(jnp.exp(-jnp.abs(v)))


def _rms(v, g):
    return v * lax.rsqrt(jnp.mean(v * v, axis=-1, keepdims=True) + EPS) * g


def _split3(v):
    hi = v.astype(BF16)
    r1 = v - hi.astype(F32)
    mid = r1.astype(BF16)
    lo = (r1 - mid.astype(F32)).astype(BF16)
    return hi, mid, lo


def _dot(a, b):
    return jnp.dot(a, b, preferred_element_type=F32)


def _dot_nt(a, b):
    return lax.dot_general(a, b, (((1,), (1,)), ((), ())), preferred_element_type=F32)


def _dot_tn(a, b):
    return lax.dot_general(a, b, (((0,), (0,)), ((), ())), preferred_element_type=F32)


def _expand_heads(v, e_mat):
    hi, mid, lo = _split3(v)
    return _dot(hi, e_mat) + _dot(mid, e_mat) + _dot(lo, e_mat)


def _head_expand_matrix():
    k = lax.broadcasted_iota(jnp.int32, (LANES, SSM_W), 0)
    c = lax.broadcasted_iota(jnp.int32, (LANES, SSM_W), 1)
    return jnp.where((c // SSM_P) == k, 1.0, 0.0).astype(BF16)


def _ada_kernel(cs_ref, cp_ref, w_ref, b_ref, ms_ref, mp_ref):
    w = w_ref[...].astype(BF16)
    b = b_ref[...]
    ms_ref[...] = _dot(_silu(cs_ref[...]).astype(BF16), w) + b
    mp_ref[...] = _dot(_silu(cp_ref[...]).astype(BF16), w) + b


def _ada(c_s, c_p8, w_ada, b_ada):
    n = w_ada.shape[1]
    bn = 1024
    return pl.pallas_call(
        _ada_kernel,
        grid=(n // bn,),
        in_specs=[
            pl.BlockSpec(c_s.shape, lambda j: (0, 0)),
            pl.BlockSpec(c_p8.shape, lambda j: (0, 0)),
            pl.BlockSpec((D_MODEL, bn), lambda j: (0, j)),
            pl.BlockSpec((1, bn), lambda j: (0, j)),
        ],
        out_specs=[
            pl.BlockSpec((c_s.shape[0], bn), lambda j: (0, j)),
            pl.BlockSpec((c_p8.shape[0], bn), lambda j: (0, j)),
        ],
        out_shape=[
            jax.ShapeDtypeStruct((c_s.shape[0], n), F32),
            jax.ShapeDtypeStruct((c_p8.shape[0], n), F32),
        ],
        compiler_params=_params(("parallel",)),
        name="ada",
    )(c_s, c_p8, w_ada, b_ada)


def _mod_spec(per_row, rows, col):
    if per_row:
        return pl.BlockSpec((rows, D_MODEL), lambda i, *_: (i, col))
    return pl.BlockSpec((8, D_MODEL), lambda i, *_: (0, col))


def _mod_rows(ref, per_row):
    return ref[...] if per_row else ref[0:1, :]


def _wsplit_kernel(w_ref, wq_ref, wkv_ref, wz_ref, wx_ref):
    o_z = ATT_W + 2 * KV_W
    o_x = o_z + SSM_W
    o_dt = o_x + CONV_DIM
    wq_ref[...] = w_ref[:, 0:ATT_W].astype(BF16)
    wkv_ref[...] = w_ref[:, ATT_W:o_z].astype(BF16)
    wz_ref[...] = w_ref[:, o_z:o_x].astype(BF16)
    wx_ref[:, 0:CONV_DIM] = w_ref[:, o_x:o_dt].astype(BF16)
    wx_ref[:, CONV_DIM:] = jnp.zeros((w_ref.shape[0], LANES), BF16)
    wx_ref[:, CONV_DIM:CONV_DIM + SSM_H] = w_ref[:, o_dt:o_dt + SSM_H].astype(BF16)


def _wsplit(w_in0):
    rows = 256
    k_dim, n_in = w_in0.shape
    widths = (ATT_W, 2 * KV_W, SSM_W, XBCDT_W)
    return pl.pallas_call(
        _wsplit_kernel,
        grid=(k_dim // rows,),
        in_specs=[pl.BlockSpec((rows, n_in), lambda i: (i, 0))],
        out_specs=[pl.BlockSpec((rows, w), lambda i: (i, 0)) for w in widths],
        out_shape=[jax.ShapeDtypeStruct((k_dim, w), BF16) for w in widths],
        compiler_params=_params(("parallel",)),
        name="wsplit",
    )(w_in0)


def _inproj_kernel(x_ref, g_ref, sh_ref, sc_ref, wq_ref, wkv_ref, wz_ref, wx_ref,
                   q_ref, kv_ref, z_ref, xb_ref, kvlast_ref, *, per_row, rows):
    u = _rms(x_ref[...], g_ref[...]) * (1.0 + _mod_rows(sc_ref, per_row)) + _mod_rows(sh_ref, per_row)
    u = u.astype(BF16)
    q_ref[...] = (_dot(u, wq_ref[...]) * (HEAD_DIM ** -0.5)).astype(BF16)
    kv = _dot(u, wkv_ref[...])
    kv_ref[...] = kv.astype(BF16)
    z_ref[...] = _dot(u, wz_ref[...])
    xb_ref[...] = _dot(u, wx_ref[...])

    @pl.when(pl.program_id(0) == pl.num_programs(0) - 1)
    def _():
        kvlast_ref[...] = kv[rows - WINDOW:, :]


def _const_spec(shape):
    return pl.BlockSpec(shape, lambda i, *_: (0,) * len(shape), pipeline_mode=pl.Buffered(1))


def _inproj(x, g, mod, per_row, wq, wkv, wz, wx):
    n_rows = x.shape[0]
    rows = min(ROWS_IN, n_rows)
    row_spec = lambda w: pl.BlockSpec((rows, w), lambda i: (i, 0))
    return pl.pallas_call(
        functools.partial(_inproj_kernel, per_row=per_row, rows=rows),
        grid=(n_rows // rows,),
        in_specs=[
            row_spec(D_MODEL),
            _const_spec((1, D_MODEL)),
            _mod_spec(per_row, rows, 0),
            _mod_spec(per_row, rows, 1),
            _const_spec(wq.shape), _const_spec(wkv.shape), _const_spec(wz.shape), _const_spec(wx.shape),
        ],
        out_specs=[
            row_spec(ATT_W), row_spec(2 * KV_W), row_spec(SSM_W), row_spec(XBCDT_W),
            pl.BlockSpec((WINDOW, 2 * KV_W), lambda i: (0, 0)),
        ],
        out_shape=[
            jax.ShapeDtypeStruct((n_rows, ATT_W), BF16),
            jax.ShapeDtypeStruct((n_rows, 2 * KV_W), BF16),
            jax.ShapeDtypeStruct((n_rows, SSM_W), F32),
            jax.ShapeDtypeStruct((n_rows, XBCDT_W), F32),
            jax.ShapeDtypeStruct((WINDOW, 2 * KV_W), F32),
        ],
        compiler_params=_params(("arbitrary",)),
        name="inproj",
    )(x, g, mod, mod, wq, wkv, wz, wx)


def _pair_blockdiag(lo_src, hi_src, keep_lo, keep_hi):
    zero = jnp.zeros_like(lo_src)
    return jnp.concatenate([jnp.where(keep_lo, lo_src, zero), jnp.where(keep_hi, hi_src, zero)], axis=0)


def _attn_prompt_kernel(slopes_ref, sinks_ref, q_ref, kvc_ref, kvp_ref, g_ref, o_ref,
                        bias_scr, kv_scr, *, n_sub):
    i = pl.program_id(0)
    blk = WINDOW

    @pl.when(i == 0)
    def _():
        a = lax.broadcasted_iota(jnp.int32, (blk, 2 * blk), 0)
        j = lax.broadcasted_iota(jnp.int32, (blk, 2 * blk), 1)
        dist = a + blk - j
        valid = (dist >= 0) & (dist < WINDOW)
        distf = dist.astype(F32)
        for h in range(N_HEADS):
            kvh, g = divmod(h, Q_PER_KV)
            pr, t = divmod(g, 2)
            b = jnp.where(valid, -(slopes_ref[h] * distf), NEG_INF)
            b = jnp.where(j == 0, sinks_ref[h], b)
            rs = slice(pr * blk, (pr + 1) * blk)
            cs = slice(t * 2 * blk, (t + 1) * 2 * blk)
            bias_scr[1, kvh, rs, cs] = b
            bias_scr[0, kvh, rs, cs] = jnp.where((j >= blk) | (j == 0), b, NEG_INF)

    kv_scr[0:blk, :] = kvp_ref[...]
    kv_scr[blk:, :] = kvc_ref[...]
    lane = lax.broadcasted_iota(jnp.int32, (2 * blk, LANES), 1)
    not_sink = jnp.where(lax.broadcasted_iota(jnp.int32, (2 * blk, LANES), 0) != 0, 1.0, 0.0)
    keep_lo = lane < HEAD_DIM
    keep_hi = lane >= HEAD_DIM
    ones_cols = jnp.concatenate([jnp.where(lane < HEAD_DIM, 1.0, 0.0),
                                 jnp.where(lane >= HEAD_DIM, 1.0, 0.0)], axis=0).astype(BF16)
    n_chunk = 4
    rc = 2 * blk // n_chunk

    def body(b, carry):
        r0 = pl.multiple_of(b * blk, blk)
        kv2 = kv_scr[pl.ds(r0, 2 * blk), :]
        variant = jnp.where(jnp.logical_and(i == 0, b == 0), 0, 1)
        vbds, scores = [], []
        for s in range(2):
            ks = kv2[:, s * LANES:(s + 1) * LANES].astype(F32) * not_sink
            vs = kv2[:, KV_W + s * LANES:KV_W + (s + 1) * LANES].astype(F32) * not_sink
            kr = pltpu.roll(ks, HEAD_DIM, 1)
            vr = pltpu.roll(vs, HEAD_DIM, 1)
            for e in range(2):
                kvh = 2 * s + e
                c0 = kvh * Q_PER_KV * HEAD_DIM
                kbd = _pair_blockdiag(ks if e == 0 else kr, kr if e == 0 else ks, keep_lo, keep_hi)
                vbd = _pair_blockdiag(vs if e == 0 else vr, vr if e == 0 else vs, keep_lo, keep_hi)
                vbds.append(jnp.concatenate([vbd.astype(BF16), ones_cols], axis=1))
                q2 = jnp.concatenate([q_ref[pl.ds(r0, blk), c0:c0 + LANES],
                                      q_ref[pl.ds(r0, blk), c0 + LANES:c0 + 2 * LANES]], axis=0)
                scores.append(_dot_nt(q2, kbd.astype(BF16)) + bias_scr[variant, kvh])
        probs = []
        for kvh in range(N_KV):
            chunks = []
            for c in range(n_chunk):
                st = scores[kvh][c * rc:(c + 1) * rc, :]
                sl = st[:, 0:2 * blk]
                sr = st[:, 2 * blk:]
                p_l = jnp.exp(sl - jnp.max(sl, axis=-1, keepdims=True))
                p_r = jnp.exp(sr - jnp.max(sr, axis=-1, keepdims=True))
                chunks.append(jnp.concatenate([p_l, p_r], axis=1).astype(BF16))
            probs.append(jnp.concatenate(chunks, axis=0))
        outs = []
        for kvh in range(N_KV):
            o2 = _dot(probs[kvh], vbds[kvh])
            o = o2[:, 0:LANES] * (1.0 / o2[:, LANES:])
            outs += [o[0:blk], o[blk:]]
        att = jnp.concatenate(outs, axis=1)
        o_ref[pl.ds(r0, blk), :] = _rms(att, g_ref[...]).astype(BF16)
        return carry

    lax.fori_loop(0, n_sub, body, 0, unroll=2)


def _attn_prompt(q, kv, slopes, sinks, g_att):
    n_rows = q.shape[0]
    rows = ROWS_ATT
    n_sub = rows // WINDOW
    smem = pl.BlockSpec(memory_space=pltpu.SMEM)
    return pl.pallas_call(
        functools.partial(_attn_prompt_kernel, n_sub=n_sub),
        grid=(n_rows // rows,),
        in_specs=[
            smem, smem,
            pl.BlockSpec((rows, ATT_W), lambda i: (i, 0)),
            pl.BlockSpec((rows, 2 * KV_W), lambda i: (i, 0)),
            pl.BlockSpec((WINDOW, 2 * KV_W), lambda i: (jnp.maximum(i * n_sub - 1, 0), 0)),
            pl.BlockSpec((1, ATT_W), lambda i: (0, 0)),
        ],
        out_specs=pl.BlockSpec((rows, ATT_W), lambda i: (i, 0)),
        out_shape=jax.ShapeDtypeStruct((n_rows, ATT_W), BF16),
        scratch_shapes=[
            pltpu.VMEM((2, N_KV, 2 * WINDOW, 4 * WINDOW), F32),
            pltpu.VMEM((rows + WINDOW, 2 * KV_W), BF16),
        ],
        compiler_params=_params(("arbitrary",)),
        name="attn_prompt",
    )(slopes, sinks, q, kv, kv, g_att)


def _attn_sample_kernel(q_ref, kn_ref, vn_ref, ck_ref, cv_ref, slope_ref, sink_ref, g_ref,
                        o_ref, ko_ref, vo_ref, att_scr, *, n_seq):
    r16 = lax.broadcasted_iota(jnp.int32, (N_HEADS, ATT_W), 0)
    c16 = lax.broadcasted_iota(jnp.int32, (N_HEADS, ATT_W), 1)
    own_head = (c16 // HEAD_DIM) == r16
    newest = lax.broadcasted_iota(jnp.int32, (KV_W, WINDOW), 1) == WINDOW - 1
    jj = lax.broadcasted_iota(jnp.int32, (N_HEADS, WINDOW), 1)
    bias = -(slope_ref[...] * (WINDOW - 1 - jj).astype(F32))
    pad = jnp.zeros((LANES - n_seq, KV_W), F32)
    kn_t = jnp.concatenate([kn_ref[...], pad], axis=0).T
    vn_t = jnp.concatenate([vn_ref[...], pad], axis=0).T

    def fold4(v):
        return (v[:, 0:KV_W] + v[:, KV_W:2 * KV_W]) + (v[:, 2 * KV_W:3 * KV_W] + v[:, 3 * KV_W:])

    scores, values = [], []
    for b in range(n_seq):
        kw = jnp.where(newest, jnp.broadcast_to(kn_t[:, b:b + 1], (KV_W, WINDOW)),
                       pltpu.roll(ck_ref[b], WINDOW - 1, 1))
        vw = jnp.where(newest, jnp.broadcast_to(vn_t[:, b:b + 1], (KV_W, WINDOW)),
                       pltpu.roll(cv_ref[b], WINDOW - 1, 1))
        ko_ref[b] = kw
        vo_ref[b] = vw
        qb = jnp.broadcast_to(q_ref[b:b + 1, :].astype(F32), (N_HEADS, ATT_W))
        qbd = fold4(jnp.where(own_head, qb, 0.0))
        scores.append(_dot(qbd.astype(BF16), kw.astype(BF16)))
        values.append(vw.astype(BF16))
    st = jnp.concatenate(scores, axis=0) + jnp.concatenate([bias] * n_seq, axis=0)
    sink = jnp.concatenate([sink_ref[...]] * n_seq, axis=0)
    m = jnp.maximum(jnp.max(st, axis=-1, keepdims=True), sink)
    p = jnp.exp(st - m)
    inv_l = 1.0 / (jnp.sum(p, axis=-1, keepdims=True) + jnp.exp(sink - m))
    pb = p.astype(BF16)
    for b in range(n_seq):
        hs = slice(b * N_HEADS, (b + 1) * N_HEADS)
        o = _dot_nt(pb[hs, :], values[b]) * inv_l[hs, :]
        o4 = jnp.concatenate([o, o, o, o], axis=1)
        att_scr[b:b + 1, :] = jnp.sum(jnp.where(own_head, o4, 0.0), axis=0, keepdims=True)
    o_ref[...] = _rms(att_scr[...], g_ref[...]).astype(BF16)


def _attn_sample(q, kn, vn, ck, cv, slope_col, sink_col, g_att):
    n = q.shape[0]
    nb = SEQ_BLK
    cache_spec = pl.BlockSpec((nb, KV_W, WINDOW), lambda i: (i, 0, 0))
    row_spec = lambda w: pl.BlockSpec((nb, w), lambda i: (i, 0))
    full = lambda a: pl.BlockSpec(a.shape, lambda i: (0,) * a.ndim)
    return pl.pallas_call(
        functools.partial(_attn_sample_kernel, n_seq=nb),
        grid=(n // nb,),
        in_specs=[row_spec(ATT_W), row_spec(KV_W), row_spec(KV_W), cache_spec, cache_spec,
                  full(slope_col), full(sink_col), full(g_att)],
        out_specs=[row_spec(ATT_W), cache_spec, cache_spec],
        out_shape=[
            jax.ShapeDtypeStruct((n, ATT_W), BF16),
            jax.ShapeDtypeStruct(ck.shape, F32),
            jax.ShapeDtypeStruct(cv.shape, F32),
        ],
        scratch_shapes=[pltpu.VMEM((nb, ATT_W), F32)],
        compiler_params=_params(("parallel",)),
        name="attn_sample",
    )(q, kn, vn, ck, cv, slope_col, sink_col, g_att)


def _ssd_prompt_kernel(xb_ref, xprev_ref, z_ref, cw_ref, cb_ref, dtb_ref, alog_ref, dskip_ref, g_ref,
                       y_ref, hout_ref, xp_scr, xc_scr, ht_scr, e_scr, tri_scr, yd_scr, *, rows):
    i = pl.program_id(0)
    T = CHUNK

    @pl.when(i == 0)
    def _():
        ht_scr[...] = jnp.zeros_like(ht_scr)
        e_scr[...] = _head_expand_matrix()
        l = lax.broadcasted_iota(jnp.int32, (T, T), 0)
        s = lax.broadcasted_iota(jnp.int32, (T, T), 1)
        tri_scr[...] = jnp.where(s <= l, 1.0, 0.0).astype(BF16)

    for s in range(CONV_DIM // LANES):
        ls = slice(s * LANES, (s + 1) * LANES)
        xp_scr[s, 0:8, :] = jnp.where(i == 0, 0.0, xprev_ref[:, ls])
        xp_scr[s, 8:, :] = xb_ref[:, ls]
        acc = cb_ref[:, ls] + xp_scr[s, pl.ds(8 - (CONV_W - 1), rows), :] * cw_ref[0:1, ls]
        for k in range(1, CONV_W):
            acc = acc + xp_scr[s, pl.ds(8 - (CONV_W - 1) + k, rows), :] * cw_ref[k:k + 1, ls]
        xc_scr[:, ls] = _silu(acc)

    a_neg = -jnp.exp(alog_ref[...])
    e_mat = e_scr[...]
    tri = tri_scr[...]
    causal = lax.broadcasted_iota(jnp.int32, (T, T), 0) >= lax.broadcasted_iota(jnp.int32, (T, T), 1)
    lo = lax.broadcasted_iota(jnp.int32, (T, LANES), 1) < SSM_P
    gw = SSM_W // SSM_G

    for c in range(rows // T):
        rs = slice(c * T, (c + 1) * T)
        dt = _softplus(xb_ref[rs, CONV_DIM:XBCDT_W] + dtb_ref[...])
        dta = dt * a_neg
        hi, mid, lw = _split3(dta)
        acum = _dot(tri, hi) + _dot(tri, mid) + _dot(tri, lw)
        acum_e = _expand_heads(acum, e_mat)
        dt_e = _expand_heads(dt, e_mat)
        acum_t = acum.T
        xs = xc_scr[rs, 0:SSM_W]
        bm = xc_scr[rs, SSM_W:SSM_W + SSM_G * D_STATE]
        cm = xc_scr[rs, SSM_W + SSM_G * D_STATE:CONV_DIM]
        xdt = xs * dt_e
        last = acum_e[T - 1:T, :]
        xdt_b = xdt.astype(BF16)
        xdec_b = (xdt * jnp.exp(last - acum_e)).astype(BF16)
        bm_b = bm.astype(BF16)
        cm_b = cm.astype(BF16)
        cbs = [_dot_nt(cm_b[:, g * D_STATE:(g + 1) * D_STATE], bm_b[:, g * D_STATE:(g + 1) * D_STATE])
               for g in range(SSM_G)]
        for pr in range(SSM_H // 2):
            g = (2 * pr) // (SSM_H // SSM_G)
            ws = []
            for t in range(2):
                h = 2 * pr + t
                seg = jnp.broadcast_to(acum[:, h:h + 1], (T, T)) - acum_t[h:h + 1, :]
                ws.append((cbs[g] * jnp.exp(jnp.where(causal, seg, NEG_INF))).astype(BF16))
            xsl = xdt_b[:, pr * LANES:(pr + 1) * LANES]
            yd_scr[:, pr * LANES:(pr + 1) * LANES] = _dot(jnp.concatenate(ws, axis=1),
                                                          _pair_blockdiag(xsl, xsl, lo, ~lo))
        yoff = []
        for g in range(SSM_G):
            gs = slice(g * gw, (g + 1) * gw)
            ht_g = ht_scr[:, gs]
            yoff.append(_dot(cm_b[:, g * D_STATE:(g + 1) * D_STATE], ht_g.astype(BF16)))
            cst = _dot_tn(bm_b[:, g * D_STATE:(g + 1) * D_STATE], xdec_b[:, gs])
            ht_scr[:, gs] = ht_g * jnp.exp(last[:, gs]) + cst
        y = yd_scr[...] + jnp.concatenate(yoff, axis=1) * jnp.exp(acum_e)
        y = y + dskip_ref[...] * xs
        y = y * _silu(z_ref[rs, :])
        y_ref[rs, :] = _rms(y, g_ref[...]).astype(BF16)

    @pl.when(i == pl.num_programs(0) - 1)
    def _():
        hout_ref[...] = ht_scr[...].T


def _ssd_prompt(xb, z, cw8, cb, dtb, alog, dskip_e, g_ssm):
    n_rows = xb.shape[0]
    rows = ROWS_SSD
    full = lambda a: pl.BlockSpec(a.shape, lambda i: (0,) * a.ndim)
    return pl.pallas_call(
        functools.partial(_ssd_prompt_kernel, rows=rows),
        grid=(n_rows // rows,),
        in_specs=[
            pl.BlockSpec((rows, XBCDT_W), lambda i: (i, 0)),
            pl.BlockSpec((8, XBCDT_W), lambda i: (jnp.maximum(i * (rows // 8) - 1, 0), 0)),
            pl.BlockSpec((rows, SSM_W), lambda i: (i, 0)),
            full(cw8), full(cb), full(dtb), full(alog), full(dskip_e), full(g_ssm),
        ],
        out_specs=[
            pl.BlockSpec((rows, SSM_W), lambda i: (i, 0)),
            pl.BlockSpec((SSM_W, D_STATE), lambda i: (0, 0)),
        ],
        out_shape=[
            jax.ShapeDtypeStruct((n_rows, SSM_W), BF16),
            jax.ShapeDtypeStruct((SSM_W, D_STATE), F32),
        ],
        scratch_shapes=[
            pltpu.VMEM((CONV_DIM // LANES, rows + 8, LANES), F32),
            pltpu.VMEM((rows, CONV_DIM), F32),
            pltpu.VMEM((D_STATE, SSM_W), F32),
            pltpu.VMEM((LANES, SSM_W), BF16),
            pltpu.VMEM((CHUNK, CHUNK), BF16),
            pltpu.VMEM((CHUNK, SSM_W), F32),
        ],
        compiler_params=_params(("arbitrary",)),
        name="ssd_prompt",
    )(xb, xb, z, cw8, cb, dtb, alog, dskip_e, g_ssm)


def _ssd_sample_kernel(xb_ref, sconv_ref, z_ref, h0_ref, cw_ref, cb_ref, dtb_ref, alog_ref, dskip_ref, g_ref,
                       y_ref, conv_ref, hout_ref, yoff_scr, *, n_seq):
    gw = SSM_W // SSM_G
    gn = SSM_G * D_STATE
    x_new = xb_ref[:, 0:CONV_DIM]
    taps = [sconv_ref[:, k * CONV_DIM:(k + 1) * CONV_DIM] for k in range(CONV_W - 1)] + [x_new]
    acc = cb_ref[...]
    for k in range(CONV_W):
        acc = acc + taps[k] * cw_ref[k:k + 1, :]
    for k in range(1, CONV_W):
        conv_ref[:, (k - 1) * CONV_DIM:k * CONV_DIM] = taps[k]
    xc = _silu(acc)
    xs = xc[:, 0:SSM_W]
    bm = xc[:, SSM_W:SSM_W + gn]
    cm = xc[:, SSM_W + gn:CONV_DIM]

    e_mat = _head_expand_matrix()
    dt = _softplus(xb_ref[:, CONV_DIM:XBCDT_W] + dtb_ref[...])
    dt_e = _expand_heads(dt, e_mat)
    dec_e = jnp.exp(_expand_heads(dt * (-jnp.exp(alog_ref[...])), e_mat))
    xdt = xs * dt_e

    lane_w = lax.broadcasted_iota(jnp.int32, (n_seq, SSM_W), 1)
    first_grp = lane_w < gw
    cbv = [jnp.sum(cm[:, g * D_STATE:(g + 1) * D_STATE] * bm[:, g * D_STATE:(g + 1) * D_STATE],
                   axis=-1, keepdims=True) for g in range(SSM_G)]
    cb_e = jnp.where(first_grp, cbv[0], cbv[1])

    pad = jnp.zeros((LANES - n_seq, SSM_W), F32)
    xdt_t = jnp.concatenate([xdt, pad], axis=0).T
    dec_t = jnp.concatenate([dec_e, pad], axis=0).T
    sub8 = lax.broadcasted_iota(jnp.int32, (8, D_STATE), 0)
    lane_r = lax.broadcasted_iota(jnp.int32, (8, SSM_W), 1)

    for b in range(n_seq):
        h0 = h0_ref[b]
        c_row = jnp.broadcast_to(cm[b:b + 1, :], (8, gn))
        c8 = jnp.where(sub8 == 0, c_row[:, 0:D_STATE], jnp.where(sub8 == 1, c_row[:, D_STATE:], 0.0))
        r = _dot_nt(c8.astype(BF16), h0.astype(BF16))
        yoff_scr[b:b + 1, :] = jnp.where(lane_r[0:1] < gw, r[0:1, :], r[1:2, :])
        b_row = bm[b:b + 1, :]
        for g in range(SSM_G):
            rs = slice(g * gw, (g + 1) * gw)
            dcol = jnp.broadcast_to(dec_t[rs, b:b + 1], (gw, D_STATE))
            xcol = jnp.broadcast_to(xdt_t[rs, b:b + 1], (gw, D_STATE))
            hout_ref[b, rs, :] = h0[rs, :] * dcol + xcol * b_row[:, g * D_STATE:(g + 1) * D_STATE]

    y = cb_e * xdt + yoff_scr[...] * dec_e
    y = y + dskip_ref[...] * xs
    y = y * _silu(z_ref[...])
    y_ref[...] = _rms(y, g_ref[...]).astype(BF16)


def _ssd_sample(xb, sconv, z, h0, cw8, cb, dtb, alog, dskip_e, g_ssm):
    n = xb.shape[0]
    nb = SEQ_BLK
    row_spec = lambda w: pl.BlockSpec((nb, w), lambda i: (i, 0))
    st_spec = pl.BlockSpec((nb, SSM_W, D_STATE), lambda i: (i, 0, 0))
    full = lambda a: pl.BlockSpec(a.shape, lambda i: (0,) * a.ndim)
    return pl.pallas_call(
        functools.partial(_ssd_sample_kernel, n_seq=nb),
        grid=(n // nb,),
        in_specs=[row_spec(XBCDT_W), row_spec((CONV_W - 1) * CONV_DIM), row_spec(SSM_W), st_spec,
                  full(cw8), full(cb), full(dtb), full(alog), full(dskip_e), full(g_ssm)],
        out_specs=[row_spec(SSM_W), row_spec((CONV_W - 1) * CONV_DIM), st_spec],
        out_shape=[
            jax.ShapeDtypeStruct((n, SSM_W), BF16),
            jax.ShapeDtypeStruct((n, (CONV_W - 1) * CONV_DIM), F32),
            jax.ShapeDtypeStruct(h0.shape, F32),
        ],
        scratch_shapes=[pltpu.VMEM((nb, SSM_W), F32)],
        compiler_params=_params(("parallel",)),
        name="ssd_sample",
    )(xb, sconv, z, h0, cw8, cb, dtb, alog, dskip_e, g_ssm)


def _outproj_kernel(att_ref, ssm_ref, x_ref, woa_ref, wos_ref, gpost_ref, gt1_ref, gpre_ref, sh2_ref, sc2_ref,
                    x1_ref, u2_ref, *, per_row, rows):
    sub = min(SUB_OUT, rows)
    for r0 in range(0, rows, sub):
        rs = slice(r0, r0 + sub)
        mod = lambda ref: ref[rs, :] if per_row else ref[0:1, :]
        mix = _dot(att_ref[rs, :], woa_ref[...]) + _dot(ssm_ref[rs, :], wos_ref[...])
        x1 = x_ref[rs, :] + mod(gt1_ref) * _rms(mix, gpost_ref[...])
        x1_ref[rs, :] = x1
        u2 = _rms(x1, gpre_ref[...]) * (1.0 + mod(sc2_ref)) + mod(sh2_ref)
        u2_ref[rs, :] = u2.astype(BF16)


def _outproj(att, ssm, x, mod, per_row, woa, wos, g_post, g_pre):
    n_rows = x.shape[0]
    rows = min(ROWS_OUT, n_rows)
    row_spec = lambda w: pl.BlockSpec((rows, w), lambda i: (i, 0))
    return pl.pallas_call(
        functools.partial(_outproj_kernel, per_row=per_row, rows=rows),
        grid=(n_rows // rows,),
        in_specs=[
            row_spec(ATT_W), row_spec(SSM_W), row_spec(D_MODEL),
            _const_spec(woa.shape), _const_spec(wos.shape),
            _const_spec((1, D_MODEL)),
            _mod_spec(per_row, rows, 2),
            _const_spec((1, D_MODEL)),
            _mod_spec(per_row, rows, 3),
            _mod_spec(per_row, rows, 4),
        ],
        out_specs=[row_spec(D_MODEL), row_spec(D_MODEL)],
        out_shape=[
            jax.ShapeDtypeStruct((n_rows, D_MODEL), F32),
            jax.ShapeDtypeStruct((n_rows, D_MODEL), BF16),
        ],
        compiler_params=_params(("parallel",)),
        name="outproj",
    )(att, ssm, x, woa, wos, g_post, mod, g_pre, mod, mod)


def _ffn_kernel(u_ref, x1_ref, wg_ref, wu_ref, wd_ref, gpost_ref, gt2_ref, y_ref, *, per_row):
    j = pl.program_id(1)

    @pl.when(j == 0)
    def _():
        y_ref[...] = jnp.zeros_like(y_ref)

    u = u_ref[...]
    hid = (_silu(_dot(u, wg_ref[...])) * _dot(u, wu_ref[...])).astype(BF16)
    y_ref[...] += _dot(hid, wd_ref[...])

    @pl.when(j == pl.num_programs(1) - 1)
    def _():
        y_ref[...] = x1_ref[...] + _mod_rows(gt2_ref, per_row) * _rms(y_ref[...], gpost_ref[...])


def _ffn(u2, x1, mod, per_row, wg, wu, wd, g_post):
    n_rows = x1.shape[0]
    rows = min(ROWS_FFN, n_rows)
    fb = FF_BLK
    return pl.pallas_call(
        functools.partial(_ffn_kernel, per_row=per_row),
        grid=(n_rows // rows, D_FF // fb),
        in_specs=[
            pl.BlockSpec((rows, D_MODEL), lambda i, j: (i, 0)),
            pl.BlockSpec((rows, D_MODEL), lambda i, j: (i, 0)),
            pl.BlockSpec((D_MODEL, fb), lambda i, j: (0, j)),
            pl.BlockSpec((D_MODEL, fb), lambda i, j: (0, j)),
            pl.BlockSpec((fb, D_MODEL), lambda i, j: (j, 0)),
            pl.BlockSpec((1, D_MODEL), lambda i, j: (0, 0)),
            _mod_spec(per_row, rows, 5),
        ],
        out_specs=pl.BlockSpec((rows, D_MODEL), lambda i, j: (i, 0)),
        out_shape=jax.ShapeDtypeStruct((n_rows, D_MODEL), F32),
        compiler_params=_params(("parallel", "arbitrary")),
        name="ffn",
    )(u2, x1, wg, wu, wd, g_post, mod)


def _alibi_slopes():
    return (2.0 ** (-8.0 * np.arange(1, N_HEADS + 1) / N_HEADS)).astype(np.float32)


def kernel(x_prompt, x_sample, cache_k, cache_v, state_conv, state_ssm, c_prompt, c_sample, w_ada, b_ada, g_pre_mix, g_post_mix, w_in, attn_sinks, g_attn_out, conv_w, conv_b, dt_bias, a_log, d_skip, g_ssm_out, w_out, g_pre_ffn, g_post_ffn, w_gate, w_up, w_down):
    assert w_ada.shape[0] == 1, "one layer"
    n_p = x_prompt.shape[1]
    n_s = x_sample.shape[0]
    row = lambda v: v.reshape(1, -1)

    wq, wkv, wz, wx = _wsplit(w_in[0])
    swap_heads = lambda a, lead, trail: a.reshape(*lead, N_KV, Q_PER_KV, HEAD_DIM, *trail).swapaxes(
        len(lead), len(lead) + 1).reshape(*lead, ATT_W, *trail)
    wq_s = swap_heads(wq, (D_MODEL,), ())
    woa = w_out[0, :ATT_W].astype(BF16)
    wos = w_out[0, ATT_W:].astype(BF16)
    woa_s = swap_heads(woa, (), (D_MODEL,))
    wg = w_gate[0].astype(BF16)
    wu = w_up[0].astype(BF16)
    wd = w_down[0].astype(BF16)
    g_att = row(g_attn_out[0])
    g_att_s = row(swap_heads(g_attn_out[0], (), ()))
    slopes_np = _alibi_slopes()
    slopes = jnp.asarray(slopes_np)
    slopes_s = jnp.asarray(slopes_np.reshape(N_KV, Q_PER_KV).T.reshape(N_HEADS, 1))
    sinks = attn_sinks[0]
    sinks_s = sinks.reshape(N_KV, Q_PER_KV).T.reshape(N_HEADS, 1)
    cw8 = jnp.pad(conv_w[0], ((0, 8 - CONV_W), (0, 0)))
    cb = row(conv_b[0])
    dtb = jnp.pad(row(dt_bias[0]), ((0, 0), (0, LANES - SSM_H)))
    alog = jnp.pad(row(a_log[0]), ((0, 0), (0, LANES - SSM_H)))
    dskip_e = row(jnp.repeat(d_skip[0], SSM_P))
    g_ssm = row(g_ssm_out[0])

    c_p8 = jnp.pad(c_prompt, ((0, 8 - c_prompt.shape[0]), (0, 0)))
    mod_s, mod_p = _ada(c_sample, c_p8, w_ada[0], row(b_ada[0]))

    xp = x_prompt[0]
    q, kv, z, xb, kv_last = _inproj(xp, row(g_pre_mix[0]), mod_p, False, wq, wkv, wz, wx)
    att = _attn_prompt(q, kv, slopes, sinks, g_att)
    ssm, h_p = _ssd_prompt(xb, z, cw8, cb, dtb, alog, dskip_e, g_ssm)
    x1, u2 = _outproj(att, ssm, xp, mod_p, False, woa, wos, row(g_post_mix[0]), row(g_pre_ffn[0]))
    y_p = _ffn(u2, x1, mod_p, False, wg, wu, wd, row(g_post_ffn[0]))

    keys_minor = lambda c: jnp.transpose(c, (0, 2, 3, 1)).reshape(n_s, KV_W, WINDOW)
    keys_major = lambda c: jnp.transpose(c.reshape(n_s, N_KV, HEAD_DIM, WINDOW), (0, 3, 1, 2))[None]
    xs_ = x_sample[:, 0, :]
    q_s, _, z_s, xb_s, kv_new = _inproj(xs_, row(g_pre_mix[0]), mod_s, True, wq_s, wkv, wz, wx)
    att_s, k_s, v_s = _attn_sample(
        q_s, kv_new[:, :KV_W], kv_new[:, KV_W:],
        keys_minor(cache_k[0]), keys_minor(cache_v[0]), slopes_s, sinks_s, g_att_s)
    ssm_s, conv_s, h_s = _ssd_sample(
        xb_s, state_conv[0].reshape(n_s, (CONV_W - 1) * CONV_DIM), z_s,
        state_ssm[0].reshape(n_s, SSM_W, D_STATE), cw8, cb, dtb, alog, dskip_e, g_ssm)
    x1_s, u2_s = _outproj(att_s, ssm_s, xs_, mod_s, True, woa_s, wos, row(g_post_mix[0]), row(g_pre_ffn[0]))
    y_s = _ffn(u2_s, x1_s, mod_s, True, wg, wu, wd, row(g_post_ffn[0]))

    return (
        y_p[None],
        y_s[:, None, :],
        kv_last[:, :KV_W].reshape(1, 1, WINDOW, N_KV, HEAD_DIM),
        kv_last[:, KV_W:].reshape(1, 1, WINDOW, N_KV, HEAD_DIM),
        xb[n_p - (CONV_W - 1):, :CONV_DIM].reshape(1, 1, CONV_W - 1, CONV_DIM),
        h_p.reshape(1, 1, SSM_H, SSM_P, D_STATE),
        keys_major(k_s),
        keys_major(v_s),
        conv_s.reshape(1, n_s, CONV_W - 1, CONV_DIM),
        h_s.reshape(1, n_s, SSM_H, SSM_P, D_STATE),
    )
```

```python
import functools

import numpy as np
import jax
import jax.numpy as jnp
from jax import lax
from jax.experimental import pallas as pl
from jax.experimental.pallas import tpu as pltpu

F32 = jnp.float32
BF16 = jnp.bfloat16

D_MODEL = 2048
ATT_W = 1024
HEAD_DIM = 64
N_HEADS = 16
N_KV = 4
Q_PER_KV = 4
KV_W = N_KV * HEAD_DIM
WINDOW = 128
SSM_W = 1024
SSM_P = 64
SSM_H = 16
SSM_G = 2
D_STATE = 128
CONV_W = 4
CONV_DIM = SSM_W + 2 * SSM_G * D_STATE
XBCDT_W = CONV_DIM + 128
D_FF = 5632
EPS = 1e-6
CHUNK = 128
NEG_INF = float("-inf")

VMEM_LIMIT = 56 * 1024 * 1024
LANES = 128

ROWS_IN = 512
ROWS_ATT = 512
ROWS_SSD = 512
ROWS_OUT = 512
SUB_OUT = 128
ROWS_FFN = 512
FF_BLK = 512
SEQ_BLK = 8


def _params(sem, vmem=VMEM_LIMIT):
    return pltpu.CompilerParams(dimension_semantics=sem, vmem_limit_bytes=vmem)


def _silu(v):
    h = 0.5 * v
    return h + h * jnp.tanh(h)


def _softplus(v):
    return jnp.maximum(v, 0.0) + jnp.log1p(jnp.exp(-jnp.abs(v)))


def _rms(v, g):
    return v * lax.rsqrt(jnp.mean(v * v, axis=-1, keepdims=True) + EPS) * g


def _split3(v):
    hi = v.astype(BF16)
    r1 = v - hi.astype(F32)
    mid = r1.astype(BF16)
    lo = (r1 - mid.astype(F32)).astype(BF16)
    return hi, mid, lo


def _dot(a, b):
    return jnp.dot(a, b, preferred_element_type=F32)


def _dot_nt(a, b):
    return lax.dot_general(a, b, (((1,), (1,)), ((), ())), preferred_element_type=F32)


def _dot_tn(a, b):
    return lax.dot_general(a, b, (((0,), (0,)), ((), ())), preferred_element_type=F32)


def _expand_heads(v, e_mat):
    hi, mid, lo = _split3(v)
    return _dot(hi, e_mat) + _dot(mid, e_mat) + _dot(lo, e_mat)


def _head_expand_matrix():
    k = lax.broadcasted_iota(jnp.int32, (LANES, SSM_W), 0)
    c = lax.broadcasted_iota(jnp.int32, (LANES, SSM_W), 1)
    return jnp.where((c // SSM_P) == k, 1.0, 0.0).astype(BF16)


def _ada_kernel(cs_ref, cp_ref, w_ref, b_ref, ms_ref, mp_ref):
    w = w_ref[...].astype(BF16)
    b = b_ref[...]
    ms_ref[...] = _dot(_silu(cs_ref[...]).astype(BF16), w) + b
    mp_ref[...] = _dot(_silu(cp_ref[...]).astype(BF16), w) + b


def _ada(c_s, c_p8, w_ada, b_ada):
    n = w_ada.shape[1]
    bn = 1024
    return pl.pallas_call(
        _ada_kernel,
        grid=(n // bn,),
        in_specs=[
            pl.BlockSpec(c_s.shape, lambda j: (0, 0)),
            pl.BlockSpec(c_p8.shape, lambda j: (0, 0)),
            pl.BlockSpec((D_MODEL, bn), lambda j: (0, j)),
            pl.BlockSpec((1, bn), lambda j: (0, j)),
        ],
        out_specs=[
            pl.BlockSpec((c_s.shape[0], bn), lambda j: (0, j)),
            pl.BlockSpec((c_p8.shape[0], bn), lambda j: (0, j)),
        ],
        out_shape=[
            jax.ShapeDtypeStruct((c_s.shape[0], n), F32),
            jax.ShapeDtypeStruct((c_p8.shape[0], n), F32),
        ],
        compiler_params=_params(("parallel",)),
        name="ada",
    )(c_s, c_p8, w_ada, b_ada)


def _mod_spec(per_row, rows, col):
    if per_row:
        return pl.BlockSpec((rows, D_MODEL), lambda i, *_: (i, col))
    return pl.BlockSpec((8, D_MODEL), lambda i, *_: (0, col))


def _mod_rows(ref, per_row):
    return ref[...] if per_row else ref[0:1, :]


WS_BLK = 256
W_ALL_COLS = 4608
COL_X, COL_Q, COL_Z, COL_KV = 0, 2048, 3072, 4096
N_WS_SRC = -(-(ATT_W + 2 * KV_W + SSM_W + CONV_DIM + SSM_H) // WS_BLK)


def _ws_dst_block(s):
    q_end = ATT_W // WS_BLK
    kv_end = q_end + 2 * KV_W // WS_BLK
    z_end = kv_end + SSM_W // WS_BLK
    return jnp.where(s < q_end, COL_Q // WS_BLK + s,
                     jnp.where(s < kv_end, COL_KV // WS_BLK + s - q_end,
                               jnp.where(s < z_end, COL_Z // WS_BLK + s - kv_end,
                                         COL_X // WS_BLK + s - z_end)))


def _wsplit_kernel(wt_ref, o_ref):
    s = pl.program_id(0)
    n_valid = jnp.where(s < N_WS_SRC - 1, WS_BLK, jnp.where(s == N_WS_SRC - 1, SSM_H, 0))
    lane = lax.broadcasted_iota(jnp.int32, o_ref.shape, 1)
    o_ref[...] = jnp.where(lane < n_valid, wt_ref[...].T, 0.0).astype(BF16)


def _wsplit(w_in_t):
    k_dim = w_in_t.shape[1]
    return pl.pallas_call(
        _wsplit_kernel,
        grid=(N_WS_SRC + 1,),
        in_specs=[pl.BlockSpec((WS_BLK, k_dim), lambda s: (jnp.minimum(s, N_WS_SRC - 1), 0))],
        out_specs=pl.BlockSpec((k_dim, WS_BLK), lambda s: (0, _ws_dst_block(s))),
        out_shape=jax.ShapeDtypeStruct((k_dim, W_ALL_COLS), BF16),
        compiler_params=_params(("arbitrary",)),
        name="wsplit",
    )(w_in_t)


def _inproj_kernel(x_ref, g_ref, sh_ref, sc_ref, wq_ref, wkv_ref, wz_ref, wx_ref,
                   q_ref, kv_ref, z_ref, xb_ref, kvlast_ref, *, per_row, rows):
    u = _rms(x_ref[...], g_ref[...]) * (1.0 + _mod_rows(sc_ref, per_row)) + _mod_rows(sh_ref, per_row)
    u = u.astype(BF16)
    q_ref[...] = (_dot(u, wq_ref[...]) * (HEAD_DIM ** -0.5)).astype(BF16)
    kv = _dot(u, wkv_ref[...])
    kv_ref[...] = kv.astype(BF16)
    z_ref[...] = _dot(u, wz_ref[...])
    xb_ref[...] = _dot(u, wx_ref[...])

    @pl.when(pl.program_id(0) == pl.num_programs(0) - 1)
    def _():
        kvlast_ref[...] = kv[rows - WINDOW:, :]


def _const_spec(shape):
    return pl.BlockSpec(shape, lambda i, *_: (0,) * len(shape), pipeline_mode=pl.Buffered(1))


def _inproj(x, g, mod, per_row, w_all, wq=None):
    n_rows = x.shape[0]
    rows = min(ROWS_IN, n_rows)
    row_spec = lambda w: pl.BlockSpec((rows, w), lambda i: (i, 0))
    seg_spec = lambda width, col: pl.BlockSpec((D_MODEL, width), lambda i: (0, col // width),
                                               pipeline_mode=pl.Buffered(1))
    q_arr, q_spec = (w_all, seg_spec(ATT_W, COL_Q)) if wq is None else (wq, _const_spec(wq.shape))
    return pl.pallas_call(
        functools.partial(_inproj_kernel, per_row=per_row, rows=rows),
        grid=(n_rows // rows,),
        in_specs=[
            row_spec(D_MODEL),
            _const_spec((1, D_MODEL)),
            _mod_spec(per_row, rows, 0),
            _mod_spec(per_row, rows, 1),
            q_spec, seg_spec(2 * KV_W, COL_KV), seg_spec(SSM_W, COL_Z), seg_spec(XBCDT_W, COL_X),
        ],
        out_specs=[
            row_spec(ATT_W), row_spec(2 * KV_W), row_spec(SSM_W), row_spec(XBCDT_W),
            pl.BlockSpec((WINDOW, 2 * KV_W), lambda i: (0, 0)),
        ],
        out_shape=[
            jax.ShapeDtypeStruct((n_rows, ATT_W), BF16),
            jax.ShapeDtypeStruct((n_rows, 2 * KV_W), BF16),
            jax.ShapeDtypeStruct((n_rows, SSM_W), F32),
            jax.ShapeDtypeStruct((n_rows, XBCDT_W), F32),
            jax.ShapeDtypeStruct((WINDOW, 2 * KV_W), F32),
        ],
        compiler_params=_params(("arbitrary",)),
        name="inproj",
    )(x, g, mod, mod, q_arr, w_all, w_all, w_all)


def _pair_blockdiag(lo_src, hi_src, keep_lo, keep_hi):
    zero = jnp.zeros_like(lo_src)
    return jnp.concatenate([jnp.where(keep_lo, lo_src, zero), jnp.where(keep_hi, hi_src, zero)], axis=0)


def _attn_prompt_kernel(slopes_ref, sinks_ref, q_ref, kvc_ref, kvp_ref, g_ref, o_ref,
                        bias_scr, kvdup_scr, *, n_sub):
    i = pl.program_id(0)
    blk = WINDOW

    @pl.when(i == 0)
    def _():
        a = lax.broadcasted_iota(jnp.int32, (blk, 2 * blk), 0)
        j = lax.broadcasted_iota(jnp.int32, (blk, 2 * blk), 1)
        dist = a + blk - j
        valid = (dist >= 0) & (dist < WINDOW)
        distf = dist.astype(F32)
        for h in range(N_HEADS):
            kvh, g = divmod(h, Q_PER_KV)
            pr, t = divmod(g, 2)
            b = jnp.where(valid, -(slopes_ref[h] * distf), NEG_INF)
            b = jnp.where(j == 0, sinks_ref[h], b)
            rs = slice(pr * blk, (pr + 1) * blk)
            cs = slice(t * 2 * blk, (t + 1) * 2 * blk)
            bias_scr[1, kvh, rs, cs] = b
            bias_scr[0, kvh, rs, cs] = jnp.where((j >= blk) | (j == 0), b, NEG_INF)

    for src_ref, r_lo, r_hi in ((kvp_ref, 0, blk), (kvc_ref, blk, kvdup_scr.shape[0])):
        lo_half = lax.broadcasted_iota(jnp.int32, (r_hi - r_lo, LANES), 1) < HEAD_DIM
        for part in range(2):
            for s in range(2):
                c_in = part * KV_W + s * LANES
                c_out = part * N_KV * LANES + 2 * s * LANES
                x = src_ref[:, c_in:c_in + LANES].astype(F32)
                xr = pltpu.roll(x, HEAD_DIM, 1)
                kvdup_scr[r_lo:r_hi, c_out:c_out + LANES] = jnp.where(lo_half, x, xr).astype(BF16)
                kvdup_scr[r_lo:r_hi, c_out + LANES:c_out + 2 * LANES] = jnp.where(lo_half, xr, x).astype(BF16)

    lane = lax.broadcasted_iota(jnp.int32, (2 * blk, LANES), 1)
    not_sink = lax.broadcasted_iota(jnp.int32, (2 * blk, LANES), 0) != 0
    keep_lo = (lane < HEAD_DIM) & not_sink
    keep_hi = (lane >= HEAD_DIM) & not_sink
    ones_cols = jnp.concatenate([jnp.where(lane < HEAD_DIM, 1.0, 0.0),
                                 jnp.where(lane >= HEAD_DIM, 1.0, 0.0)], axis=0).astype(BF16)
    n_chunk = 4
    rc = 2 * blk // n_chunk

    def body(b, carry):
        r0 = pl.multiple_of(b * blk, blk)
        variant = jnp.where(jnp.logical_and(i == 0, b == 0), 0, 1)
        vbds, scores = [], []
        for kvh in range(N_KV):
            c0 = kvh * Q_PER_KV * HEAD_DIM
            kd = kvdup_scr[pl.ds(r0, 2 * blk), kvh * LANES:(kvh + 1) * LANES]
            vd = kvdup_scr[pl.ds(r0, 2 * blk), (N_KV + kvh) * LANES:(N_KV + kvh + 1) * LANES]
            kbd = _pair_blockdiag(kd, kd, keep_lo, keep_hi)
            vbds.append(jnp.concatenate([_pair_blockdiag(vd, vd, keep_lo, keep_hi), ones_cols], axis=1))
            q2 = jnp.concatenate([q_ref[pl.ds(r0, blk), c0:c0 + LANES],
                                  q_ref[pl.ds(r0, blk), c0 + LANES:c0 + 2 * LANES]], axis=0)
            scores.append(_dot_nt(q2, kbd) + bias_scr[variant, kvh])
        probs = []
        for kvh in range(N_KV):
            chunks = []
            for c in range(n_chunk):
                st = scores[kvh][c * rc:(c + 1) * rc, :]
                sl = st[:, 0:2 * blk]
                sr = st[:, 2 * blk:]
                p_l = jnp.exp(sl - jnp.max(sl, axis=-1, keepdims=True))
                p_r = jnp.exp(sr - jnp.max(sr, axis=-1, keepdims=True))
                chunks.append(jnp.concatenate([p_l, p_r], axis=1).astype(BF16))
            probs.append(jnp.concatenate(chunks, axis=0))
        outs = []
        for kvh in range(N_KV):
            o2 = _dot(probs[kvh], vbds[kvh])
            o = o2[:, 0:LANES] * (1.0 / o2[:, LANES:])
            outs += [o[0:blk], o[blk:]]
        att = jnp.concatenate(outs, axis=1)
        o_ref[pl.ds(r0, blk), :] = _rms(att, g_ref[...]).astype(BF16)
        return carry

    lax.fori_loop(0, n_sub, body, 0, unroll=2)


def _attn_prompt(q, kv, slopes, sinks, g_att):
    n_rows = q.shape[0]
    rows = ROWS_ATT
    n_sub = rows // WINDOW
    smem = pl.BlockSpec(memory_space=pltpu.SMEM)
    return pl.pallas_call(
        functools.partial(_attn_prompt_kernel, n_sub=n_sub),
        grid=(n_rows // rows,),
        in_specs=[
            smem, smem,
            pl.BlockSpec((rows, ATT_W), lambda i: (i, 0)),
            pl.BlockSpec((rows, 2 * KV_W), lambda i: (i, 0)),
            pl.BlockSpec((WINDOW, 2 * KV_W), lambda i: (jnp.maximum(i * n_sub - 1, 0), 0)),
            pl.BlockSpec((1, ATT_W), lambda i: (0, 0)),
        ],
        out_specs=pl.BlockSpec((rows, ATT_W), lambda i: (i, 0)),
        out_shape=jax.ShapeDtypeStruct((n_rows, ATT_W), BF16),
        scratch_shapes=[
            pltpu.VMEM((2, N_KV, 2 * WINDOW, 4 * WINDOW), F32),
            pltpu.VMEM((rows + WINDOW, 2 * N_KV * LANES), BF16),
        ],
        compiler_params=_params(("arbitrary",)),
        name="attn_prompt",
    )(slopes, sinks, q, kv, kv, g_att)


def _attn_sample_kernel(q_ref, kn_ref, vn_ref, ck_ref, cv_ref, slope_ref, sink_ref, g_ref,
                        o_ref, ko_ref, vo_ref, att_scr, *, n_seq):
    r16 = lax.broadcasted_iota(jnp.int32, (N_HEADS, ATT_W), 0)
    c16 = lax.broadcasted_iota(jnp.int32, (N_HEADS, ATT_W), 1)
    own_head = (c16 // HEAD_DIM) == r16
    newest = lax.broadcasted_iota(jnp.int32, (KV_W, WINDOW), 1) == WINDOW - 1
    jj = lax.broadcasted_iota(jnp.int32, (N_HEADS, WINDOW), 1)
    bias = -(slope_ref[...] * (WINDOW - 1 - jj).astype(F32))
    pad = jnp.zeros((LANES - n_seq, KV_W), F32)
    kn_t = jnp.concatenate([kn_ref[...], pad], axis=0).T
    vn_t = jnp.concatenate([vn_ref[...], pad], axis=0).T

    def fold4(v):
        return (v[:, 0:KV_W] + v[:, KV_W:2 * KV_W]) + (v[:, 2 * KV_W:3 * KV_W] + v[:, 3 * KV_W:])

    scores, values = [], []
    for b in range(n_seq):
        kw = jnp.where(newest, jnp.broadcast_to(kn_t[:, b:b + 1], (KV_W, WINDOW)),
                       pltpu.roll(ck_ref[b], WINDOW - 1, 1))
        vw = jnp.where(newest, jnp.broadcast_to(vn_t[:, b:b + 1], (KV_W, WINDOW)),
                       pltpu.roll(cv_ref[b], WINDOW - 1, 1))
        ko_ref[b] = kw
        vo_ref[b] = vw
        qb = jnp.broadcast_to(q_ref[b:b + 1, :].astype(F32), (N_HEADS, ATT_W))
        qbd = fold4(jnp.where(own_head, qb, 0.0))
        scores.append(_dot(qbd.astype(BF16), kw.astype(BF16)))
        values.append(vw.astype(BF16))
    st = jnp.concatenate(scores, axis=0) + jnp.concatenate([bias] * n_seq, axis=0)
    sink = jnp.concatenate([sink_ref[...]] * n_seq, axis=0)
    m = jnp.maximum(jnp.max(st, axis=-1, keepdims=True), sink)
    p = jnp.exp(st - m)
    inv_l = 1.0 / (jnp.sum(p, axis=-1, keepdims=True) + jnp.exp(sink - m))
    pb = p.astype(BF16)
    for b in range(n_seq):
        hs = slice(b * N_HEADS, (b + 1) * N_HEADS)
        o = _dot_nt(pb[hs, :], values[b]) * inv_l[hs, :]
        o4 = jnp.concatenate([o, o, o, o], axis=1)
        att_scr[b:b + 1, :] = jnp.sum(jnp.where(own_head, o4, 0.0), axis=0, keepdims=True)
    o_ref[...] = _rms(att_scr[...], g_ref[...]).astype(BF16)


def _attn_sample(q, kn, vn, ck, cv, slope_col, sink_col, g_att):
    n = q.shape[0]
    nb = SEQ_BLK
    cache_spec = pl.BlockSpec((nb, KV_W, WINDOW), lambda i: (i, 0, 0))
    row_spec = lambda w: pl.BlockSpec((nb, w), lambda i: (i, 0))
    full = lambda a: pl.BlockSpec(a.shape, lambda i: (0,) * a.ndim)
    return pl.pallas_call(
        functools.partial(_attn_sample_kernel, n_seq=nb),
        grid=(n // nb,),
        in_specs=[row_spec(ATT_W), row_spec(KV_W), row_spec(KV_W), cache_spec, cache_spec,
                  full(slope_col), full(sink_col), full(g_att)],
        out_specs=[row_spec(ATT_W), cache_spec, cache_spec],
        out_shape=[
            jax.ShapeDtypeStruct((n, ATT_W), BF16),
            jax.ShapeDtypeStruct(ck.shape, F32),
            jax.ShapeDtypeStruct(cv.shape, F32),
        ],
        scratch_shapes=[pltpu.VMEM((nb, ATT_W), F32)],
        compiler_params=_params(("parallel",)),
        name="attn_sample",
    )(q, kn, vn, ck, cv, slope_col, sink_col, g_att)


def _ssd_prompt_kernel(xb_ref, xprev_ref, z_ref, cw_ref, cb_ref, dtb_ref, alog_ref, dskip_ref, g_ref,
                       y_ref, hout_ref, xp_scr, xc_scr, ht_scr, e_scr, tri_scr, yd_scr, *, rows):
    i = pl.program_id(0)
    T = CHUNK

    @pl.when(i == 0)
    def _():
        ht_scr[...] = jnp.zeros_like(ht_scr)
        e_scr[...] = _head_expand_matrix()
        l = lax.broadcasted_iota(jnp.int32, (T, T), 0)
        s = lax.broadcasted_iota(jnp.int32, (T, T), 1)
        tri_scr[...] = jnp.where(s <= l, 1.0, 0.0).astype(BF16)

    for s in range(CONV_DIM // LANES):
        ls = slice(s * LANES, (s + 1) * LANES)
        xp_scr[s, 0:8, :] = jnp.where(i == 0, 0.0, xprev_ref[:, ls])
        xp_scr[s, 8:, :] = xb_ref[:, ls]
        acc = cb_ref[:, ls] + xp_scr[s, pl.ds(8 - (CONV_W - 1), rows), :] * cw_ref[0:1, ls]
        for k in range(1, CONV_W):
            acc = acc + xp_scr[s, pl.ds(8 - (CONV_W - 1) + k, rows), :] * cw_ref[k:k + 1, ls]
        xc_scr[:, ls] = _silu(acc)

    a_neg = -jnp.exp(alog_ref[...])
    e_mat = e_scr[...]
    tri = tri_scr[...]
    causal = lax.broadcasted_iota(jnp.int32, (T, T), 0) >= lax.broadcasted_iota(jnp.int32, (T, T), 1)
    lo = lax.broadcasted_iota(jnp.int32, (T, LANES), 1) < SSM_P
    gw = SSM_W // SSM_G

    for c in range(rows // T):
        rs = slice(c * T, (c + 1) * T)
        dt = _softplus(xb_ref[rs, CONV_DIM:XBCDT_W] + dtb_ref[...])
        dta = dt * a_neg
        hi, mid, lw = _split3(dta)
        acum = _dot(tri, hi) + _dot(tri, mid) + _dot(tri, lw)
        acum_e = _expand_heads(acum, e_mat)
        dt_e = _expand_heads(dt, e_mat)
        acum_t = acum.T
        xs = xc_scr[rs, 0:SSM_W]
        bm = xc_scr[rs, SSM_W:SSM_W + SSM_G * D_STATE]
        cm = xc_scr[rs, SSM_W + SSM_G * D_STATE:CONV_DIM]
        xdt = xs * dt_e
        last = acum_e[T - 1:T, :]
        xdt_b = xdt.astype(BF16)
        xdec_b = (xdt * jnp.exp(last - acum_e)).astype(BF16)
        bm_b = bm.astype(BF16)
        cm_b = cm.astype(BF16)
        cbs = [_dot_nt(cm_b[:, g * D_STATE:(g + 1) * D_STATE], bm_b[:, g * D_STATE:(g + 1) * D_STATE])
               for g in range(SSM_G)]
        for pr in range(SSM_H // 2):
            g = (2 * pr) // (SSM_H // SSM_G)
            ws = []
            for t in range(2):
                h = 2 * pr + t
                seg = jnp.broadcast_to(acum[:, h:h + 1], (T, T)) - acum_t[h:h + 1, :]
                ws.append((cbs[g] * jnp.exp(jnp.where(causal, seg, NEG_INF))).astype(BF16))
            xsl = xdt_b[:, pr * LANES:(pr + 1) * LANES]
            yd_scr[:, pr * LANES:(pr + 1) * LANES] = _dot(jnp.concatenate(ws, axis=1),
                                                          _pair_blockdiag(xsl, xsl, lo, ~lo))
        yoff = []
        for g in range(SSM_G):
            gs = slice(g * gw, (g + 1) * gw)
            ht_g = ht_scr[:, gs]
            yoff.append(_dot(cm_b[:, g * D_STATE:(g + 1) * D_STATE], ht_g.astype(BF16)))
            cst = _dot_tn(bm_b[:, g * D_STATE:(g + 1) * D_STATE], xdec_b[:, gs])
            ht_scr[:, gs] = ht_g * jnp.exp(last[:, gs]) + cst
        y = yd_scr[...] + jnp.concatenate(yoff, axis=1) * jnp.exp(acum_e)
        y = y + dskip_ref[...] * xs
        y = y * _silu(z_ref[rs, :])
        y_ref[rs, :] = _rms(y, g_ref[...]).astype(BF16)

    @pl.when(i == pl.num_programs(0) - 1)
    def _():
        hout_ref[...] = ht_scr[...].T


def _ssd_prompt(xb, z, cw8, cb, dtb, alog, dskip_e, g_ssm):
    n_rows = xb.shape[0]
    rows = ROWS_SSD
    full = lambda a: pl.BlockSpec(a.shape, lambda i: (0,) * a.ndim)
    return pl.pallas_call(
        functools.partial(_ssd_prompt_kernel, rows=rows),
        grid=(n_rows // rows,),
        in_specs=[
            pl.BlockSpec((rows, XBCDT_W), lambda i: (i, 0)),
            pl.BlockSpec((8, XBCDT_W), lambda i: (jnp.maximum(i * (rows // 8) - 1, 0), 0)),
            pl.BlockSpec((rows, SSM_W), lambda i: (i, 0)),
            full(cw8), full(cb), full(dtb), full(alog), full(dskip_e), full(g_ssm),
        ],
        out_specs=[
            pl.BlockSpec((rows, SSM_W), lambda i: (i, 0)),
            pl.BlockSpec((SSM_W, D_STATE), lambda i: (0, 0)),
        ],
        out_shape=[
            jax.ShapeDtypeStruct((n_rows, SSM_W), BF16),
            jax.ShapeDtypeStruct((SSM_W, D_STATE), F32),
        ],
        scratch_shapes=[
            pltpu.VMEM((CONV_DIM // LANES, rows + 8, LANES), F32),
            pltpu.VMEM((rows, CONV_DIM), F32),
            pltpu.VMEM((D_STATE, SSM_W), F32),
            pltpu.VMEM((LANES, SSM_W), BF16),
            pltpu.VMEM((CHUNK, CHUNK), BF16),
            pltpu.VMEM((CHUNK, SSM_W), F32),
        ],
        compiler_params=_params(("arbitrary",)),
        name="ssd_prompt",
    )(xb, xb, z, cw8, cb, dtb, alog, dskip_e, g_ssm)


def _ssd_sample_kernel(xb_ref, sconv_ref, z_ref, h0_ref, cw_ref, cb_ref, dtb_ref, alog_ref, dskip_ref, g_ref,
                       y_ref, conv_ref, hout_ref, yoff_scr, *, n_seq):
    gw = SSM_W // SSM_G
    gn = SSM_G * D_STATE
    x_new = xb_ref[:, 0:CONV_DIM]
    taps = [sconv_ref[:, k * CONV_DIM:(k + 1) * CONV_DIM] for k in range(CONV_W - 1)] + [x_new]
    acc = cb_ref[...]
    for k in range(CONV_W):
        acc = acc + taps[k] * cw_ref[k:k + 1, :]
    for k in range(1, CONV_W):
        conv_ref[:, (k - 1) * CONV_DIM:k * CONV_DIM] = taps[k]
    xc = _silu(acc)
    xs = xc[:, 0:SSM_W]
    bm = xc[:, SSM_W:SSM_W + gn]
    cm = xc[:, SSM_W + gn:CONV_DIM]

    e_mat = _head_expand_matrix()
    dt = _softplus(xb_ref[:, CONV_DIM:XBCDT_W] + dtb_ref[...])
    dt_e = _expand_heads(dt, e_mat)
    dec_e = jnp.exp(_expand_heads(dt * (-jnp.exp(alog_ref[...])), e_mat))
    xdt = xs * dt_e

    lane_w = lax.broadcasted_iota(jnp.int32, (n_seq, SSM_W), 1)
    first_grp = lane_w < gw
    cbv = [jnp.sum(cm[:, g * D_STATE:(g + 1) * D_STATE] * bm[:, g * D_STATE:(g + 1) * D_STATE],
                   axis=-1, keepdims=True) for g in range(SSM_G)]
    cb_e = jnp.where(first_grp, cbv[0], cbv[1])

    pad = jnp.zeros((LANES - n_seq, SSM_W), F32)
    xdt_t = jnp.concatenate([xdt, pad], axis=0).T
    dec_t = jnp.concatenate([dec_e, pad], axis=0).T
    sub8 = lax.broadcasted_iota(jnp.int32, (8, D_STATE), 0)
    lane_r = lax.broadcasted_iota(jnp.int32, (8, SSM_W), 1)

    for b in range(n_seq):
        h0 = h0_ref[b]
        c_row = jnp.broadcast_to(cm[b:b + 1, :], (8, gn))
        c8 = jnp.where(sub8 == 0, c_row[:, 0:D_STATE], jnp.where(sub8 == 1, c_row[:, D_STATE:], 0.0))
        r = _dot_nt(c8.astype(BF16), h0.astype(BF16))
        yoff_scr[b:b + 1, :] = jnp.where(lane_r[0:1] < gw, r[0:1, :], r[1:2, :])
        b_row = bm[b:b + 1, :]
        for g in range(SSM_G):
            rs = slice(g * gw, (g + 1) * gw)
            dcol = jnp.broadcast_to(dec_t[rs, b:b + 1], (gw, D_STATE))
            xcol = jnp.broadcast_to(xdt_t[rs, b:b + 1], (gw, D_STATE))
            hout_ref[b, rs, :] = h0[rs, :] * dcol + xcol * b_row[:, g * D_STATE:(g + 1) * D_STATE]

    y = cb_e * xdt + yoff_scr[...] * dec_e
    y = y + dskip_ref[...] * xs
    y = y * _silu(z_ref[...])
    y_ref[...] = _rms(y, g_ref[...]).astype(BF16)


def _ssd_sample(xb, sconv, z, h0, cw8, cb, dtb, alog, dskip_e, g_ssm):
    n = xb.shape[0]
    nb = SEQ_BLK
    row_spec = lambda w: pl.BlockSpec((nb, w), lambda i: (i, 0))
    st_spec = pl.BlockSpec((nb, SSM_W, D_STATE), lambda i: (i, 0, 0))
    full = lambda a: pl.BlockSpec(a.shape, lambda i: (0,) * a.ndim)
    return pl.pallas_call(
        functools.partial(_ssd_sample_kernel, n_seq=nb),
        grid=(n // nb,),
        in_specs=[row_spec(XBCDT_W), row_spec((CONV_W - 1) * CONV_DIM), row_spec(SSM_W), st_spec,
                  full(cw8), full(cb), full(dtb), full(alog), full(dskip_e), full(g_ssm)],
        out_specs=[row_spec(SSM_W), row_spec((CONV_W - 1) * CONV_DIM), st_spec],
        out_shape=[
            jax.ShapeDtypeStruct((n, SSM_W), BF16),
            jax.ShapeDtypeStruct((n, (CONV_W - 1) * CONV_DIM), F32),
            jax.ShapeDtypeStruct(h0.shape, F32),
        ],
        scratch_shapes=[pltpu.VMEM((nb, SSM_W), F32)],
        compiler_params=_params(("parallel",)),
        name="ssd_sample",
    )(xb, sconv, z, h0, cw8, cb, dtb, alog, dskip_e, g_ssm)


def _outproj_kernel(att_ref, ssm_ref, x_ref, woa_ref, wos_ref, gpost_ref, gt1_ref, gpre_ref, sh2_ref, sc2_ref,
                    x1_ref, u2_ref, *, per_row, rows):
    sub = min(SUB_OUT, rows)
    for r0 in range(0, rows, sub):
        rs = slice(r0, r0 + sub)
        mod = lambda ref: ref[rs, :] if per_row else ref[0:1, :]
        mix = _dot(att_ref[rs, :], woa_ref[...]) + _dot(ssm_ref[rs, :], wos_ref[...])
        x1 = x_ref[rs, :] + mod(gt1_ref) * _rms(mix, gpost_ref[...])
        x1_ref[rs, :] = x1
        u2 = _rms(x1, gpre_ref[...]) * (1.0 + mod(sc2_ref)) + mod(sh2_ref)
        u2_ref[rs, :] = u2.astype(BF16)


def _outproj(att, ssm, x, mod, per_row, woa, wos, g_post, g_pre):
    n_rows = x.shape[0]
    rows = min(ROWS_OUT, n_rows)
    row_spec = lambda w: pl.BlockSpec((rows, w), lambda i: (i, 0))
    return pl.pallas_call(
        functools.partial(_outproj_kernel, per_row=per_row, rows=rows),
        grid=(n_rows // rows,),
        in_specs=[
            row_spec(ATT_W), row_spec(SSM_W), row_spec(D_MODEL),
            _const_spec(woa.shape), _const_spec(wos.shape),
            _const_spec((1, D_MODEL)),
            _mod_spec(per_row, rows, 2),
            _const_spec((1, D_MODEL)),
            _mod_spec(per_row, rows, 3),
            _mod_spec(per_row, rows, 4),
        ],
        out_specs=[row_spec(D_MODEL), row_spec(D_MODEL)],
        out_shape=[
            jax.ShapeDtypeStruct((n_rows, D_MODEL), F32),
            jax.ShapeDtypeStruct((n_rows, D_MODEL), BF16),
        ],
        compiler_params=_params(("parallel",)),
        name="outproj",
    )(att, ssm, x, woa, wos, g_post, mod, g_pre, mod, mod)


def _ffn_kernel(u_ref, x1_ref, wg_ref, wu_ref, wd_ref, gpost_ref, gt2_ref, y_ref, *, per_row):
    j = pl.program_id(1)

    @pl.when(j == 0)
    def _():
        y_ref[...] = jnp.zeros_like(y_ref)

    u = u_ref[...]
    hid = (_silu(_dot(u, wg_ref[...])) * _dot(u, wu_ref[...])).astype(BF16)
    y_ref[...] += _dot(hid, wd_ref[...])

    @pl.when(j == pl.num_programs(1) - 1)
    def _():
        y_ref[...] = x1_ref[...] + _mod_rows(gt2_ref, per_row) * _rms(y_ref[...], gpost_ref[...])


def _ffn(u2, x1, mod, per_row, wg, wu, wd, g_post):
    n_rows = x1.shape[0]
    rows = min(ROWS_FFN, n_rows)
    fb = FF_BLK
    return pl.pallas_call(
        functools.partial(_ffn_kernel, per_row=per_row),
        grid=(n_rows // rows, D_FF // fb),
        in_specs=[
            pl.BlockSpec((rows, D_MODEL), lambda i, j: (i, 0)),
            pl.BlockSpec((rows, D_MODEL), lambda i, j: (i, 0)),
            pl.BlockSpec((D_MODEL, fb), lambda i, j: (0, j)),
            pl.BlockSpec((D_MODEL, fb), lambda i, j: (0, j)),
            pl.BlockSpec((fb, D_MODEL), lambda i, j: (j, 0)),
            pl.BlockSpec((1, D_MODEL), lambda i, j: (0, 0)),
            _mod_spec(per_row, rows, 5),
        ],
        out_specs=pl.BlockSpec((rows, D_MODEL), lambda i, j: (i, 0)),
        out_shape=jax.ShapeDtypeStruct((n_rows, D_MODEL), F32),
        compiler_params=_params(("parallel", "arbitrary")),
        name="ffn",
    )(u2, x1, wg, wu, wd, g_post, mod)


def _alibi_slopes():
    return (2.0 ** (-8.0 * np.arange(1, N_HEADS + 1) / N_HEADS)).astype(np.float32)


def kernel(x_prompt, x_sample, cache_k, cache_v, state_conv, state_ssm, c_prompt, c_sample, w_ada, b_ada, g_pre_mix, g_post_mix, w_in, attn_sinks, g_attn_out, conv_w, conv_b, dt_bias, a_log, d_skip, g_ssm_out, w_out, g_pre_ffn, g_post_ffn, w_gate, w_up, w_down):
    assert w_ada.shape[0] == 1, "one layer"
    n_p = x_prompt.shape[1]
    n_s = x_sample.shape[0]
    row = lambda v: v.reshape(1, -1)

    w_all = _wsplit(jnp.transpose(w_in[0]))
    swap_heads = lambda a, lead, trail: a.reshape(*lead, N_KV, Q_PER_KV, HEAD_DIM, *trail).swapaxes(
        len(lead), len(lead) + 1).reshape(*lead, ATT_W, *trail)
    wq_s = swap_heads(w_all[:, COL_Q:COL_Q + ATT_W], (D_MODEL,), ())
    woa = w_out[0, :ATT_W].astype(BF16)
    wos = w_out[0, ATT_W:].astype(BF16)
    woa_s = swap_heads(woa, (), (D_MODEL,))
    wg = w_gate[0].astype(BF16)
    wu = w_up[0].astype(BF16)
    wd = w_down[0].astype(BF16)
    g_att = row(g_attn_out[0])
    g_att_s = row(swap_heads(g_attn_out[0], (), ()))
    slopes_np = _alibi_slopes()
    slopes = jnp.asarray(slopes_np)
    slopes_s = jnp.asarray(slopes_np.reshape(N_KV, Q_PER_KV).T.reshape(N_HEADS, 1))
    sinks = attn_sinks[0]
    sinks_s = sinks.reshape(N_KV, Q_PER_KV).T.reshape(N_HEADS, 1)
    cw8 = jnp.pad(conv_w[0], ((0, 8 - CONV_W), (0, 0)))
    cb = row(conv_b[0])
    dtb = jnp.pad(row(dt_bias[0]), ((0, 0), (0, LANES - SSM_H)))
    alog = jnp.pad(row(a_log[0]), ((0, 0), (0, LANES - SSM_H)))
    dskip_e = row(jnp.repeat(d_skip[0], SSM_P))
    g_ssm = row(g_ssm_out[0])

    c_p8 = jnp.pad(c_prompt, ((0, 8 - c_prompt.shape[0]), (0, 0)))
    mod_s, mod_p = _ada(c_sample, c_p8, w_ada[0], row(b_ada[0]))

    xp = x_prompt[0]
    q, kv, z, xb, kv_last = _inproj(xp, row(g_pre_mix[0]), mod_p, False, w_all)
    att = _attn_prompt(q, kv, slopes, sinks, g_att)
    ssm, h_p = _ssd_prompt(xb, z, cw8, cb, dtb, alog, dskip_e, g_ssm)
    x1, u2 = _outproj(att, ssm, xp, mod_p, False, woa, wos, row(g_post_mix[0]), row(g_pre_ffn[0]))
    y_p = _ffn(u2, x1, mod_p, False, wg, wu, wd, row(g_post_ffn[0]))

    keys_minor = lambda c: jnp.transpose(c, (0, 2, 3, 1)).reshape(n_s, KV_W, WINDOW)
    keys_major = lambda c: jnp.transpose(c.reshape(n_s, N_KV, HEAD_DIM, WINDOW), (0, 3, 1, 2))[None]
    xs_ = x_sample[:, 0, :]
    q_s, _, z_s, xb_s, kv_new = _inproj(xs_, row(g_pre_mix[0]), mod_s, True, w_all, wq_s)
    att_s, k_s, v_s = _attn_sample(
        q_s, kv_new[:, :KV_W], kv_new[:, KV_W:],
        keys_minor(cache_k[0]), keys_minor(cache_v[0]), slopes_s, sinks_s, g_att_s)
    ssm_s, conv_s, h_s = _ssd_sample(
        xb_s, state_conv[0].reshape(n_s, (CONV_W - 1) * CONV_DIM), z_s,
        state_ssm[0].reshape(n_s, SSM_W, D_STATE), cw8, cb, dtb, alog, dskip_e, g_ssm)
    x1_s, u2_s = _outproj(att_s, ssm_s, xs_, mod_s, True, woa_s, wos, row(g_post_mix[0]), row(g_pre_ffn[0]))
    y_s = _ffn(u2_s, x1_s, mod_s, True, wg, wu, wd, row(g_post_ffn[0]))

    return (
        y_p[None],
        y_s[:, None, :],
        kv_last[:, :KV_W].reshape(1, 1, WINDOW, N_KV, HEAD_DIM),
        kv_last[:, KV_W:].reshape(1, 1, WINDOW, N_KV, HEAD_DIM),
        xb[n_p - (CONV_W - 1):, :CONV_DIM].reshape(1, 1, CONV_W - 1, CONV_DIM),
        h_p.reshape(1, 1, SSM_H, SSM_P, D_STATE),
        keys_major(k_s),
        keys_major(v_s),
        conv_s.reshape(1, n_s, CONV_W - 1, CONV_DIM),
        h_s.reshape(1, n_s, SSM_H, SSM_P, D_STATE),
    )
```

```python
import functools

import numpy as np
import jax
import jax.numpy as jnp
from jax import lax
from jax.experimental import pallas as pl
from jax.experimental.pallas import tpu as pltpu

F32 = jnp.float32
BF16 = jnp.bfloat16

D_MODEL = 2048
ATT_W = 1024
HEAD_DIM = 64
N_HEADS = 16
N_KV = 4
Q_PER_KV = 4
KV_W = N_KV * HEAD_DIM
WINDOW = 128
SSM_W = 1024
SSM_P = 64
SSM_H = 16
SSM_G = 2
D_STATE = 128
CONV_W = 4
CONV_DIM = SSM_W + 2 * SSM_G * D_STATE
XBCDT_W = CONV_DIM + 128
D_FF = 5632
EPS = 1e-6
CHUNK = 128
NEG_INF = float("-inf")

VMEM_LIMIT = 56 * 1024 * 1024
VMEM_LIMIT_FFN = 60 * 1024 * 1024
LANES = 128

ROWS_IN = 512
ROWS_ATT = 512
ROWS_SSD = 512
ROWS_OUT = 512
SUB_OUT = 128
ROWS_FFN = 1024
SUB_FFN = 512
FF_BLK = 512
SEQ_BLK = 8


def _params(sem, vmem=VMEM_LIMIT):
    return pltpu.CompilerParams(dimension_semantics=sem, vmem_limit_bytes=vmem)


def _silu(v):
    h = 0.5 * v
    return h + h * jnp.tanh(h)


def _softplus(v):
    return jnp.maximum(v, 0.0) + jnp.log1p(jnp.exp(-jnp.abs(v)))


def _rms(v, g):
    return v * lax.rsqrt(jnp.mean(v * v, axis=-1, keepdims=True) + EPS) * g


def _split3(v):
    hi = v.astype(BF16)
    r1 = v - hi.astype(F32)
    mid = r1.astype(BF16)
    lo = (r1 - mid.astype(F32)).astype(BF16)
    return hi, mid, lo


def _dot(a, b):
    return jnp.dot(a, b, preferred_element_type=F32)


def _dot_nt(a, b):
    return lax.dot_general(a, b, (((1,), (1,)), ((), ())), preferred_element_type=F32)


def _dot_tn(a, b):
    return lax.dot_general(a, b, (((0,), (0,)), ((), ())), preferred_element_type=F32)


def _expand_heads(v, e_mat):
    hi, mid, lo = _split3(v)
    return _dot(hi, e_mat) + _dot(mid, e_mat) + _dot(lo, e_mat)


def _head_expand_matrix():
    k = lax.broadcasted_iota(jnp.int32, (LANES, SSM_W), 0)
    c = lax.broadcasted_iota(jnp.int32, (LANES, SSM_W), 1)
    return jnp.where((c // SSM_P) == k, 1.0, 0.0).astype(BF16)


def _ada_kernel(cs_ref, cp_ref, w_ref, b_ref, ms_ref, mp_ref):
    w = w_ref[...].astype(BF16)
    b = b_ref[...]
    ms_ref[...] = _dot(_silu(cs_ref[...]).astype(BF16), w) + b
    mp_ref[...] = _dot(_silu(cp_ref[...]).astype(BF16), w) + b


def _ada(c_s, c_p8, w_ada, b_ada):
    n = w_ada.shape[1]
    bn = 1024
    return pl.pallas_call(
        _ada_kernel,
        grid=(n // bn,),
        in_specs=[
            pl.BlockSpec(c_s.shape, lambda j: (0, 0)),
            pl.BlockSpec(c_p8.shape, lambda j: (0, 0)),
            pl.BlockSpec((D_MODEL, bn), lambda j: (0, j)),
            pl.BlockSpec((1, bn), lambda j: (0, j)),
        ],
        out_specs=[
            pl.BlockSpec((c_s.shape[0], bn), lambda j: (0, j)),
            pl.BlockSpec((c_p8.shape[0], bn), lambda j: (0, j)),
        ],
        out_shape=[
            jax.ShapeDtypeStruct((c_s.shape[0], n), F32),
            jax.ShapeDtypeStruct((c_p8.shape[0], n), F32),
        ],
        compiler_params=_params(("parallel",)),
        name="ada",
    )(c_s, c_p8, w_ada, b_ada)


def _mod_spec(per_row, rows, col):
    if per_row:
        return pl.BlockSpec((rows, D_MODEL), lambda i, *_: (i, col))
    return pl.BlockSpec((8, D_MODEL), lambda i, *_: (0, col))


def _mod_rows(ref, per_row):
    return ref[...] if per_row else ref[0:1, :]


WS_BLK = 256
W_ALL_COLS = 4608
COL_X, COL_Q, COL_Z, COL_KV = 0, 2048, 3072, 4096
N_WS_SRC = -(-(ATT_W + 2 * KV_W + SSM_W + CONV_DIM + SSM_H) // WS_BLK)


def _ws_dst_block(s):
    q_end = ATT_W // WS_BLK
    kv_end = q_end + 2 * KV_W // WS_BLK
    z_end = kv_end + SSM_W // WS_BLK
    return jnp.where(s < q_end, COL_Q // WS_BLK + s,
                     jnp.where(s < kv_end, COL_KV // WS_BLK + s - q_end,
                               jnp.where(s < z_end, COL_Z // WS_BLK + s - kv_end,
                                         COL_X // WS_BLK + s - z_end)))


def _wsplit_kernel(wt_ref, o_ref):
    s = pl.program_id(0)
    n_valid = jnp.where(s < N_WS_SRC - 1, WS_BLK, jnp.where(s == N_WS_SRC - 1, SSM_H, 0))
    lane = lax.broadcasted_iota(jnp.int32, o_ref.shape, 1)
    o_ref[...] = jnp.where(lane < n_valid, wt_ref[...].T, 0.0).astype(BF16)


def _wsplit(w_in_t):
    k_dim = w_in_t.shape[1]
    return pl.pallas_call(
        _wsplit_kernel,
        grid=(N_WS_SRC + 1,),
        in_specs=[pl.BlockSpec((WS_BLK, k_dim), lambda s: (jnp.minimum(s, N_WS_SRC - 1), 0))],
        out_specs=pl.BlockSpec((k_dim, WS_BLK), lambda s: (0, _ws_dst_block(s))),
        out_shape=jax.ShapeDtypeStruct((k_dim, W_ALL_COLS), BF16),
        compiler_params=_params(("arbitrary",)),
        name="wsplit",
    )(w_in_t)


def _inproj_kernel(x_ref, g_ref, sh_ref, sc_ref, wq_ref, wkv_ref, wz_ref, wx_ref,
                   q_ref, kv_ref, z_ref, xb_ref, kvlast_ref, *, per_row, rows):
    u = _rms(x_ref[...], g_ref[...]) * (1.0 + _mod_rows(sc_ref, per_row)) + _mod_rows(sh_ref, per_row)
    u = u.astype(BF16)
    q_ref[...] = (_dot(u, wq_ref[...]) * (HEAD_DIM ** -0.5)).astype(BF16)
    kv = _dot(u, wkv_ref[...])
    kv_ref[...] = kv.astype(BF16)
    z_ref[...] = _dot(u, wz_ref[...])
    xb_ref[...] = _dot(u, wx_ref[...])

    @pl.when(pl.program_id(0) == pl.num_programs(0) - 1)
    def _():
        kvlast_ref[...] = kv[rows - WINDOW:, :]


def _const_spec(shape):
    return pl.BlockSpec(shape, lambda i, *_: (0,) * len(shape), pipeline_mode=pl.Buffered(1))


def _inproj(x, g, mod, per_row, w_all, wq=None):
    n_rows = x.shape[0]
    rows = min(ROWS_IN, n_rows)
    row_spec = lambda w: pl.BlockSpec((rows, w), lambda i: (i, 0))
    seg_spec = lambda width, col: pl.BlockSpec((D_MODEL, width), lambda i: (0, col // width),
                                               pipeline_mode=pl.Buffered(1))
    q_arr, q_spec = (w_all, seg_spec(ATT_W, COL_Q)) if wq is None else (wq, _const_spec(wq.shape))
    return pl.pallas_call(
        functools.partial(_inproj_kernel, per_row=per_row, rows=rows),
        grid=(n_rows // rows,),
        in_specs=[
            row_spec(D_MODEL),
            _const_spec((1, D_MODEL)),
            _mod_spec(per_row, rows, 0),
            _mod_spec(per_row, rows, 1),
            q_spec, seg_spec(2 * KV_W, COL_KV), seg_spec(SSM_W, COL_Z), seg_spec(XBCDT_W, COL_X),
        ],
        out_specs=[
            row_spec(ATT_W), row_spec(2 * KV_W), row_spec(SSM_W), row_spec(XBCDT_W),
            pl.BlockSpec((WINDOW, 2 * KV_W), lambda i: (0, 0)),
        ],
        out_shape=[
            jax.ShapeDtypeStruct((n_rows, ATT_W), BF16),
            jax.ShapeDtypeStruct((n_rows, 2 * KV_W), BF16),
            jax.ShapeDtypeStruct((n_rows, SSM_W), F32),
            jax.ShapeDtypeStruct((n_rows, XBCDT_W), F32),
            jax.ShapeDtypeStruct((WINDOW, 2 * KV_W), F32),
        ],
        compiler_params=_params(("arbitrary",)),
        name="inproj",
    )(x, g, mod, mod, q_arr, w_all, w_all, w_all)


def _pair_blockdiag(lo_src, hi_src, keep_lo, keep_hi):
    zero = jnp.zeros_like(lo_src)
    return jnp.concatenate([jnp.where(keep_lo, lo_src, zero), jnp.where(keep_hi, hi_src, zero)], axis=0)


def _attn_prompt_kernel(slopes_ref, sinks_ref, q_ref, kvc_ref, kvp_ref, g_ref, o_ref,
                        bias_scr, kvdup_scr, *, n_sub):
    i = pl.program_id(0)
    blk = WINDOW

    @pl.when(i == 0)
    def _():
        a = lax.broadcasted_iota(jnp.int32, (blk, 2 * blk), 0)
        j = lax.broadcasted_iota(jnp.int32, (blk, 2 * blk), 1)
        dist = a + blk - j
        valid = (dist >= 0) & (dist < WINDOW)
        distf = dist.astype(F32)
        for h in range(N_HEADS):
            kvh, g = divmod(h, Q_PER_KV)
            pr, t = divmod(g, 2)
            b = jnp.where(valid, -(slopes_ref[h] * distf), NEG_INF)
            b = jnp.where(j == 0, sinks_ref[h], b)
            rs = slice(pr * blk, (pr + 1) * blk)
            cs = slice(t * 2 * blk, (t + 1) * 2 * blk)
            bias_scr[1, kvh, rs, cs] = b
            bias_scr[0, kvh, rs, cs] = jnp.where((j >= blk) | (j == 0), b, NEG_INF)

    for src_ref, r_lo, r_hi in ((kvp_ref, 0, blk), (kvc_ref, blk, kvdup_scr.shape[0])):
        lo_half = lax.broadcasted_iota(jnp.int32, (r_hi - r_lo, LANES), 1) < HEAD_DIM
        for part in range(2):
            for s in range(2):
                c_in = part * KV_W + s * LANES
                c_out = part * N_KV * LANES + 2 * s * LANES
                x = src_ref[:, c_in:c_in + LANES].astype(F32)
                xr = pltpu.roll(x, HEAD_DIM, 1)
                kvdup_scr[r_lo:r_hi, c_out:c_out + LANES] = jnp.where(lo_half, x, xr).astype(BF16)
                kvdup_scr[r_lo:r_hi, c_out + LANES:c_out + 2 * LANES] = jnp.where(lo_half, xr, x).astype(BF16)

    lane = lax.broadcasted_iota(jnp.int32, (2 * blk, LANES), 1)
    not_sink = lax.broadcasted_iota(jnp.int32, (2 * blk, LANES), 0) != 0
    keep_lo = (lane < HEAD_DIM) & not_sink
    keep_hi = (lane >= HEAD_DIM) & not_sink
    ones_cols = jnp.concatenate([jnp.where(lane < HEAD_DIM, 1.0, 0.0),
                                 jnp.where(lane >= HEAD_DIM, 1.0, 0.0)], axis=0).astype(BF16)
    n_chunk = 4
    rc = 2 * blk // n_chunk

    def body(b, carry):
        r0 = pl.multiple_of(b * blk, blk)
        variant = jnp.where(jnp.logical_and(i == 0, b == 0), 0, 1)
        vbds, scores = [], []
        for kvh in range(N_KV):
            c0 = kvh * Q_PER_KV * HEAD_DIM
            kd = kvdup_scr[pl.ds(r0, 2 * blk), kvh * LANES:(kvh + 1) * LANES]
            vd = kvdup_scr[pl.ds(r0, 2 * blk), (N_KV + kvh) * LANES:(N_KV + kvh + 1) * LANES]
            kbd = _pair_blockdiag(kd, kd, keep_lo, keep_hi)
            vbds.append(jnp.concatenate([_pair_blockdiag(vd, vd, keep_lo, keep_hi), ones_cols], axis=1))
            q2 = jnp.concatenate([q_ref[pl.ds(r0, blk), c0:c0 + LANES],
                                  q_ref[pl.ds(r0, blk), c0 + LANES:c0 + 2 * LANES]], axis=0)
            scores.append(_dot_nt(q2, kbd) + bias_scr[variant, kvh])
        probs = []
        for kvh in range(N_KV):
            chunks = []
            for c in range(n_chunk):
                st = scores[kvh][c * rc:(c + 1) * rc, :]
                sl = st[:, 0:2 * blk]
                sr = st[:, 2 * blk:]
                p_l = jnp.exp(sl - jnp.max(sl, axis=-1, keepdims=True))
                p_r = jnp.exp(sr - jnp.max(sr, axis=-1, keepdims=True))
                chunks.append(jnp.concatenate([p_l, p_r], axis=1).astype(BF16))
            probs.append(jnp.concatenate(chunks, axis=0))
        outs = []
        for kvh in range(N_KV):
            o2 = _dot(probs[kvh], vbds[kvh])
            o = o2[:, 0:LANES] * (1.0 / o2[:, LANES:])
            outs += [o[0:blk], o[blk:]]
        att = jnp.concatenate(outs, axis=1)
        o_ref[pl.ds(r0, blk), :] = _rms(att, g_ref[...]).astype(BF16)
        return carry

    lax.fori_loop(0, n_sub, body, 0, unroll=2)


def _attn_prompt(q, kv, slopes, sinks, g_att):
    n_rows = q.shape[0]
    rows = ROWS_ATT
    n_sub = rows // WINDOW
    smem = pl.BlockSpec(memory_space=pltpu.SMEM)
    return pl.pallas_call(
        functools.partial(_attn_prompt_kernel, n_sub=n_sub),
        grid=(n_rows // rows,),
        in_specs=[
            smem, smem,
            pl.BlockSpec((rows, ATT_W), lambda i: (i, 0)),
            pl.BlockSpec((rows, 2 * KV_W), lambda i: (i, 0)),
            pl.BlockSpec((WINDOW, 2 * KV_W), lambda i: (jnp.maximum(i * n_sub - 1, 0), 0)),
            pl.BlockSpec((1, ATT_W), lambda i: (0, 0)),
        ],
        out_specs=pl.BlockSpec((rows, ATT_W), lambda i: (i, 0)),
        out_shape=jax.ShapeDtypeStruct((n_rows, ATT_W), BF16),
        scratch_shapes=[
            pltpu.VMEM((2, N_KV, 2 * WINDOW, 4 * WINDOW), F32),
            pltpu.VMEM((rows + WINDOW, 2 * N_KV * LANES), BF16),
        ],
        compiler_params=_params(("arbitrary",)),
        name="attn_prompt",
    )(slopes, sinks, q, kv, kv, g_att)


def _attn_sample_kernel(q_ref, kn_ref, vn_ref, ck_ref, cv_ref, slope_ref, sink_ref, g_ref,
                        o_ref, ko_ref, vo_ref, att_scr, *, n_seq):
    r16 = lax.broadcasted_iota(jnp.int32, (N_HEADS, ATT_W), 0)
    c16 = lax.broadcasted_iota(jnp.int32, (N_HEADS, ATT_W), 1)
    own_head = (c16 // HEAD_DIM) == r16
    newest = lax.broadcasted_iota(jnp.int32, (KV_W, WINDOW), 1) == WINDOW - 1
    jj = lax.broadcasted_iota(jnp.int32, (N_HEADS, WINDOW), 1)
    bias = -(slope_ref[...] * (WINDOW - 1 - jj).astype(F32))
    pad = jnp.zeros((LANES - n_seq, KV_W), F32)
    kn_t = jnp.concatenate([kn_ref[...], pad], axis=0).T
    vn_t = jnp.concatenate([vn_ref[...], pad], axis=0).T

    def fold4(v):
        return (v[:, 0:KV_W] + v[:, KV_W:2 * KV_W]) + (v[:, 2 * KV_W:3 * KV_W] + v[:, 3 * KV_W:])

    scores, values = [], []
    for b in range(n_seq):
        kw = jnp.where(newest, jnp.broadcast_to(kn_t[:, b:b + 1], (KV_W, WINDOW)),
                       pltpu.roll(ck_ref[b], WINDOW - 1, 1))
        vw = jnp.where(newest, jnp.broadcast_to(vn_t[:, b:b + 1], (KV_W, WINDOW)),
                       pltpu.roll(cv_ref[b], WINDOW - 1, 1))
        ko_ref[b] = kw
        vo_ref[b] = vw
        qb = jnp.broadcast_to(q_ref[b:b + 1, :].astype(F32), (N_HEADS, ATT_W))
        qbd = fold4(jnp.where(own_head, qb, 0.0))
        scores.append(_dot(qbd.astype(BF16), kw.astype(BF16)))
        values.append(vw.astype(BF16))
    st = jnp.concatenate(scores, axis=0) + jnp.concatenate([bias] * n_seq, axis=0)
    sink = jnp.concatenate([sink_ref[...]] * n_seq, axis=0)
    m = jnp.maximum(jnp.max(st, axis=-1, keepdims=True), sink)
    p = jnp.exp(st - m)
    inv_l = 1.0 / (jnp.sum(p, axis=-1, keepdims=True) + jnp.exp(sink - m))
    pb = p.astype(BF16)
    for b in range(n_seq):
        hs = slice(b * N_HEADS, (b + 1) * N_HEADS)
        o = _dot_nt(pb[hs, :], values[b]) * inv_l[hs, :]
        o4 = jnp.concatenate([o, o, o, o], axis=1)
        att_scr[b:b + 1, :] = jnp.sum(jnp.where(own_head, o4, 0.0), axis=0, keepdims=True)
    o_ref[...] = _rms(att_scr[...], g_ref[...]).astype(BF16)


def _attn_sample(q, kn, vn, ck, cv, slope_col, sink_col, g_att):
    n = q.shape[0]
    nb = SEQ_BLK
    cache_spec = pl.BlockSpec((nb, KV_W, WINDOW), lambda i: (i, 0, 0))
    row_spec = lambda w: pl.BlockSpec((nb, w), lambda i: (i, 0))
    full = lambda a: pl.BlockSpec(a.shape, lambda i: (0,) * a.ndim)
    return pl.pallas_call(
        functools.partial(_attn_sample_kernel, n_seq=nb),
        grid=(n // nb,),
        in_specs=[row_spec(ATT_W), row_spec(KV_W), row_spec(KV_W), cache_spec, cache_spec,
                  full(slope_col), full(sink_col), full(g_att)],
        out_specs=[row_spec(ATT_W), cache_spec, cache_spec],
        out_shape=[
            jax.ShapeDtypeStruct((n, ATT_W), BF16),
            jax.ShapeDtypeStruct(ck.shape, F32),
            jax.ShapeDtypeStruct(cv.shape, F32),
        ],
        scratch_shapes=[pltpu.VMEM((nb, ATT_W), F32)],
        compiler_params=_params(("parallel",)),
        name="attn_sample",
    )(q, kn, vn, ck, cv, slope_col, sink_col, g_att)


def _ssd_prompt_kernel(xb_ref, xprev_ref, z_ref, cw_ref, cb_ref, dtb_ref, alog_ref, dskip_ref, g_ref,
                       y_ref, hout_ref, xp_scr, xc_scr, ht_scr, e_scr, tri_scr, yd_scr, *, rows):
    i = pl.program_id(0)
    T = CHUNK

    @pl.when(i == 0)
    def _():
        ht_scr[...] = jnp.zeros_like(ht_scr)
        e_scr[...] = _head_expand_matrix()
        l = lax.broadcasted_iota(jnp.int32, (T, T), 0)
        s = lax.broadcasted_iota(jnp.int32, (T, T), 1)
        tri_scr[...] = jnp.where(s <= l, 1.0, 0.0).astype(BF16)

    for s in range(CONV_DIM // LANES):
        ls = slice(s * LANES, (s + 1) * LANES)
        xp_scr[s, 0:8, :] = jnp.where(i == 0, 0.0, xprev_ref[:, ls])
        xp_scr[s, 8:, :] = xb_ref[:, ls]
        acc = cb_ref[:, ls] + xp_scr[s, pl.ds(8 - (CONV_W - 1), rows), :] * cw_ref[0:1, ls]
        for k in range(1, CONV_W):
            acc = acc + xp_scr[s, pl.ds(8 - (CONV_W - 1) + k, rows), :] * cw_ref[k:k + 1, ls]
        xc_scr[:, ls] = _silu(acc)

    a_neg = -jnp.exp(alog_ref[...])
    e_mat = e_scr[...]
    tri = tri_scr[...]
    causal = lax.broadcasted_iota(jnp.int32, (T, T), 0) >= lax.broadcasted_iota(jnp.int32, (T, T), 1)
    lo = lax.broadcasted_iota(jnp.int32, (T, LANES), 1) < SSM_P
    gw = SSM_W // SSM_G

    for c in range(rows // T):
        rs = slice(c * T, (c + 1) * T)
        dt = _softplus(xb_ref[rs, CONV_DIM:XBCDT_W] + dtb_ref[...])
        dta = dt * a_neg
        hi, mid, lw = _split3(dta)
        acum = _dot(tri, hi) + _dot(tri, mid) + _dot(tri, lw)
        acum_e = _expand_heads(acum, e_mat)
        dt_e = _expand_heads(dt, e_mat)
        acum_t = acum.T
        xs = xc_scr[rs, 0:SSM_W]
        bm = xc_scr[rs, SSM_W:SSM_W + SSM_G * D_STATE]
        cm = xc_scr[rs, SSM_W + SSM_G * D_STATE:CONV_DIM]
        xdt = xs * dt_e
        last = acum_e[T - 1:T, :]
        xdt_b = xdt.astype(BF16)
        xdec_b = (xdt * jnp.exp(last - acum_e)).astype(BF16)
        bm_b = bm.astype(BF16)
        cm_b = cm.astype(BF16)
        cbs = [_dot_nt(cm_b[:, g * D_STATE:(g + 1) * D_STATE], bm_b[:, g * D_STATE:(g + 1) * D_STATE])
               for g in range(SSM_G)]
        for pr in range(SSM_H // 2):
            g = (2 * pr) // (SSM_H // SSM_G)
            ws = []
            for t in range(2):
                h = 2 * pr + t
                seg = jnp.broadcast_to(acum[:, h:h + 1], (T, T)) - acum_t[h:h + 1, :]
                ws.append((cbs[g] * jnp.exp(jnp.where(causal, seg, NEG_INF))).astype(BF16))
            xsl = xdt_b[:, pr * LANES:(pr + 1) * LANES]
            yd_scr[:, pr * LANES:(pr + 1) * LANES] = _dot(jnp.concatenate(ws, axis=1),
                                                          _pair_blockdiag(xsl, xsl, lo, ~lo))
        yoff = []
        for g in range(SSM_G):
            gs = slice(g * gw, (g + 1) * gw)
            ht_g = ht_scr[:, gs]
            yoff.append(_dot(cm_b[:, g * D_STATE:(g + 1) * D_STATE], ht_g.astype(BF16)))
            cst = _dot_tn(bm_b[:, g * D_STATE:(g + 1) * D_STATE], xdec_b[:, gs])
            ht_scr[:, gs] = ht_g * jnp.exp(last[:, gs]) + cst
        y = yd_scr[...] + jnp.concatenate(yoff, axis=1) * jnp.exp(acum_e)
        y = y + dskip_ref[...] * xs
        y = y * _silu(z_ref[rs, :])
        y_ref[rs, :] = _rms(y, g_ref[...]).astype(BF16)

    @pl.when(i == pl.num_programs(0) - 1)
    def _():
        hout_ref[...] = ht_scr[...].T


def _ssd_prompt(xb, z, cw8, cb, dtb, alog, dskip_e, g_ssm):
    n_rows = xb.shape[0]
    rows = ROWS_SSD
    full = lambda a: pl.BlockSpec(a.shape, lambda i: (0,) * a.ndim)
    return pl.pallas_call(
        functools.partial(_ssd_prompt_kernel, rows=rows),
        grid=(n_rows // rows,),
        in_specs=[
            pl.BlockSpec((rows, XBCDT_W), lambda i: (i, 0)),
            pl.BlockSpec((8, XBCDT_W), lambda i: (jnp.maximum(i * (rows // 8) - 1, 0), 0)),
            pl.BlockSpec((rows, SSM_W), lambda i: (i, 0)),
            full(cw8), full(cb), full(dtb), full(alog), full(dskip_e), full(g_ssm),
        ],
        out_specs=[
            pl.BlockSpec((rows, SSM_W), lambda i: (i, 0)),
            pl.BlockSpec((SSM_W, D_STATE), lambda i: (0, 0)),
        ],
        out_shape=[
            jax.ShapeDtypeStruct((n_rows, SSM_W), BF16),
            jax.ShapeDtypeStruct((SSM_W, D_STATE), F32),
        ],
        scratch_shapes=[
            pltpu.VMEM((CONV_DIM // LANES, rows + 8, LANES), F32),
            pltpu.VMEM((rows, CONV_DIM), F32),
            pltpu.VMEM((D_STATE, SSM_W), F32),
            pltpu.VMEM((LANES, SSM_W), BF16),
            pltpu.VMEM((CHUNK, CHUNK), BF16),
            pltpu.VMEM((CHUNK, SSM_W), F32),
        ],
        compiler_params=_params(("arbitrary",)),
        name="ssd_prompt",
    )(xb, xb, z, cw8, cb, dtb, alog, dskip_e, g_ssm)


def _ssd_sample_kernel(xb_ref, sconv_ref, z_ref, h0_ref, cw_ref, cb_ref, dtb_ref, alog_ref, dskip_ref, g_ref,
                       y_ref, conv_ref, hout_ref, yoff_scr, *, n_seq):
    gw = SSM_W // SSM_G
    gn = SSM_G * D_STATE
    x_new = xb_ref[:, 0:CONV_DIM]
    taps = [sconv_ref[:, k * CONV_DIM:(k + 1) * CONV_DIM] for k in range(CONV_W - 1)] + [x_new]
    acc = cb_ref[...]
    for k in range(CONV_W):
        acc = acc + taps[k] * cw_ref[k:k + 1, :]
    for k in range(1, CONV_W):
        conv_ref[:, (k - 1) * CONV_DIM:k * CONV_DIM] = taps[k]
    xc = _silu(acc)
    xs = xc[:, 0:SSM_W]
    bm = xc[:, SSM_W:SSM_W + gn]
    cm = xc[:, SSM_W + gn:CONV_DIM]

    e_mat = _head_expand_matrix()
    dt = _softplus(xb_ref[:, CONV_DIM:XBCDT_W] + dtb_ref[...])
    dt_e = _expand_heads(dt, e_mat)
    dec_e = jnp.exp(_expand_heads(dt * (-jnp.exp(alog_ref[...])), e_mat))
    xdt = xs * dt_e

    lane_w = lax.broadcasted_iota(jnp.int32, (n_seq, SSM_W), 1)
    first_grp = lane_w < gw
    cbv = [jnp.sum(cm[:, g * D_STATE:(g + 1) * D_STATE] * bm[:, g * D_STATE:(g + 1) * D_STATE],
                   axis=-1, keepdims=True) for g in range(SSM_G)]
    cb_e = jnp.where(first_grp, cbv[0], cbv[1])

    pad = jnp.zeros((LANES - n_seq, SSM_W), F32)
    xdt_t = jnp.concatenate([xdt, pad], axis=0).T
    dec_t = jnp.concatenate([dec_e, pad], axis=0).T
    sub8 = lax.broadcasted_iota(jnp.int32, (8, D_STATE), 0)
    lane_r = lax.broadcasted_iota(jnp.int32, (8, SSM_W), 1)

    for b in range(n_seq):
        h0 = h0_ref[b]
        c_row = jnp.broadcast_to(cm[b:b + 1, :], (8, gn))
        c8 = jnp.where(sub8 == 0, c_row[:, 0:D_STATE], jnp.where(sub8 == 1, c_row[:, D_STATE:], 0.0))
        r = _dot_nt(c8.astype(BF16), h0.astype(BF16))
        yoff_scr[b:b + 1, :] = jnp.where(lane_r[0:1] < gw, r[0:1, :], r[1:2, :])
        b_row = bm[b:b + 1, :]
        for g in range(SSM_G):
            rs = slice(g * gw, (g + 1) * gw)
            dcol = jnp.broadcast_to(dec_t[rs, b:b + 1], (gw, D_STATE))
            xcol = jnp.broadcast_to(xdt_t[rs, b:b + 1], (gw, D_STATE))
            hout_ref[b, rs, :] = h0[rs, :] * dcol + xcol * b_row[:, g * D_STATE:(g + 1) * D_STATE]

    y = cb_e * xdt + yoff_scr[...] * dec_e
    y = y + dskip_ref[...] * xs
    y = y * _silu(z_ref[...])
    y_ref[...] = _rms(y, g_ref[...]).astype(BF16)


def _ssd_sample(xb, sconv, z, h0, cw8, cb, dtb, alog, dskip_e, g_ssm):
    n = xb.shape[0]
    nb = SEQ_BLK
    row_spec = lambda w: pl.BlockSpec((nb, w), lambda i: (i, 0))
    st_spec = pl.BlockSpec((nb, SSM_W, D_STATE), lambda i: (i, 0, 0))
    full = lambda a: pl.BlockSpec(a.shape, lambda i: (0,) * a.ndim)
    return pl.pallas_call(
        functools.partial(_ssd_sample_kernel, n_seq=nb),
        grid=(n // nb,),
        in_specs=[row_spec(XBCDT_W), row_spec((CONV_W - 1) * CONV_DIM), row_spec(SSM_W), st_spec,
                  full(cw8), full(cb), full(dtb), full(alog), full(dskip_e), full(g_ssm)],
        out_specs=[row_spec(SSM_W), row_spec((CONV_W - 1) * CONV_DIM), st_spec],
        out_shape=[
            jax.ShapeDtypeStruct((n, SSM_W), BF16),
            jax.ShapeDtypeStruct((n, (CONV_W - 1) * CONV_DIM), F32),
            jax.ShapeDtypeStruct(h0.shape, F32),
        ],
        scratch_shapes=[pltpu.VMEM((nb, SSM_W), F32)],
        compiler_params=_params(("parallel",)),
        name="ssd_sample",
    )(xb, sconv, z, h0, cw8, cb, dtb, alog, dskip_e, g_ssm)


def _outproj_kernel(att_ref, ssm_ref, x_ref, woa_ref, wos_ref, gpost_ref, gt1_ref, gpre_ref, sh2_ref, sc2_ref,
                    x1_ref, u2_ref, *, per_row, rows):
    sub = min(SUB_OUT, rows)
    for r0 in range(0, rows, sub):
        rs = slice(r0, r0 + sub)
        mod = lambda ref: ref[rs, :] if per_row else ref[0:1, :]
        mix = _dot(att_ref[rs, :], woa_ref[...]) + _dot(ssm_ref[rs, :], wos_ref[...])
        x1 = x_ref[rs, :] + mod(gt1_ref) * _rms(mix, gpost_ref[...])
        x1_ref[rs, :] = x1
        u2 = _rms(x1, gpre_ref[...]) * (1.0 + mod(sc2_ref)) + mod(sh2_ref)
        u2_ref[rs, :] = u2.astype(BF16)


def _outproj(att, ssm, x, mod, per_row, woa, wos, g_post, g_pre):
    n_rows = x.shape[0]
    rows = min(ROWS_OUT, n_rows)
    row_spec = lambda w: pl.BlockSpec((rows, w), lambda i: (i, 0))
    return pl.pallas_call(
        functools.partial(_outproj_kernel, per_row=per_row, rows=rows),
        grid=(n_rows // rows,),
        in_specs=[
            row_spec(ATT_W), row_spec(SSM_W), row_spec(D_MODEL),
            _const_spec(woa.shape), _const_spec(wos.shape),
            _const_spec((1, D_MODEL)),
            _mod_spec(per_row, rows, 2),
            _const_spec((1, D_MODEL)),
            _mod_spec(per_row, rows, 3),
            _mod_spec(per_row, rows, 4),
        ],
        out_specs=[row_spec(D_MODEL), row_spec(D_MODEL)],
        out_shape=[
            jax.ShapeDtypeStruct((n_rows, D_MODEL), F32),
            jax.ShapeDtypeStruct((n_rows, D_MODEL), BF16),
        ],
        compiler_params=_params(("parallel",)),
        name="outproj",
    )(att, ssm, x, woa, wos, g_post, mod, g_pre, mod, mod)


def _ffn_kernel(u_ref, x1_hbm, wg_ref, wu_ref, wd_ref, gpost_ref, gt2_ref, y_ref, x1_buf, x1_sem,
                *, per_row, rows):
    i = pl.program_id(0)
    j = pl.program_id(1)
    x1_copy = pltpu.make_async_copy(x1_hbm.at[pl.ds(i * rows, rows), :], x1_buf, x1_sem)
    sub = min(SUB_FFN, rows)

    last = pl.num_programs(1) - 1

    def d_ff_slice(first, final):
        for r0 in range(0, rows, sub):
            rs = slice(r0, r0 + sub)
            u = u_ref[rs, :]
            hid = (_silu(_dot(u, wg_ref[...])) * _dot(u, wu_ref[...])).astype(BF16)
            acc = _dot(hid, wd_ref[...])
            if not first:
                acc = y_ref[rs, :] + acc
            if final:
                gt2 = gt2_ref[rs, :] if per_row else gt2_ref[0:1, :]
                acc = x1_buf[rs, :] + gt2 * _rms(acc, gpost_ref[...])
            y_ref[rs, :] = acc

    @pl.when(j == 0)
    def _():
        x1_copy.start()
        d_ff_slice(True, False)

    @pl.when(jnp.logical_and(j > 0, j < last))
    def _():
        d_ff_slice(False, False)

    @pl.when(j == last)
    def _():
        x1_copy.wait()
        d_ff_slice(False, True)


def _ffn(u2, x1, mod, per_row, wg, wu, wd, g_post):
    n_rows = x1.shape[0]
    rows = min(ROWS_FFN, n_rows)
    fb = FF_BLK
    return pl.pallas_call(
        functools.partial(_ffn_kernel, per_row=per_row, rows=rows),
        grid=(n_rows // rows, D_FF // fb),
        in_specs=[
            pl.BlockSpec((rows, D_MODEL), lambda i, j: (i, 0)),
            pl.BlockSpec(memory_space=pl.ANY),
            pl.BlockSpec((D_MODEL, fb), lambda i, j: (0, j)),
            pl.BlockSpec((D_MODEL, fb), lambda i, j: (0, j)),
            pl.BlockSpec((fb, D_MODEL), lambda i, j: (j, 0)),
            pl.BlockSpec((1, D_MODEL), lambda i, j: (0, 0)),
            _mod_spec(per_row, rows, 5),
        ],
        out_specs=pl.BlockSpec((rows, D_MODEL), lambda i, j: (i, 0)),
        out_shape=jax.ShapeDtypeStruct((n_rows, D_MODEL), F32),
        scratch_shapes=[pltpu.VMEM((rows, D_MODEL), F32), pltpu.SemaphoreType.DMA],
        compiler_params=_params(("arbitrary", "arbitrary"), VMEM_LIMIT_FFN),
        name="ffn",
    )(u2, x1, wg, wu, wd, g_post, mod)


def _alibi_slopes():
    return (2.0 ** (-8.0 * np.arange(1, N_HEADS + 1) / N_HEADS)).astype(np.float32)


def kernel(x_prompt, x_sample, cache_k, cache_v, state_conv, state_ssm, c_prompt, c_sample, w_ada, b_ada, g_pre_mix, g_post_mix, w_in, attn_sinks, g_attn_out, conv_w, conv_b, dt_bias, a_log, d_skip, g_ssm_out, w_out, g_pre_ffn, g_post_ffn, w_gate, w_up, w_down):
    assert w_ada.shape[0] == 1, "one layer"
    n_p = x_prompt.shape[1]
    n_s = x_sample.shape[0]
    row = lambda v: v.reshape(1, -1)

    w_all = _wsplit(jnp.transpose(w_in[0]))
    swap_heads = lambda a, lead, trail: a.reshape(*lead, N_KV, Q_PER_KV, HEAD_DIM, *trail).swapaxes(
        len(lead), len(lead) + 1).reshape(*lead, ATT_W, *trail)
    wq_s = swap_heads(w_all[:, COL_Q:COL_Q + ATT_W], (D_MODEL,), ())
    woa = w_out[0, :ATT_W].astype(BF16)
    wos = w_out[0, ATT_W:].astype(BF16)
    woa_s = swap_heads(woa, (), (D_MODEL,))
    wg = w_gate[0].astype(BF16)
    wu = w_up[0].astype(BF16)
    wd = w_down[0].astype(BF16)
    g_att = row(g_attn_out[0])
    g_att_s = row(swap_heads(g_attn_out[0], (), ()))
    slopes_np = _alibi_slopes()
    slopes = jnp.asarray(slopes_np)
    slopes_s = jnp.asarray(slopes_np.reshape(N_KV, Q_PER_KV).T.reshape(N_HEADS, 1))
    sinks = attn_sinks[0]
    sinks_s = sinks.reshape(N_KV, Q_PER_KV).T.reshape(N_HEADS, 1)
    cw8 = jnp.pad(conv_w[0], ((0, 8 - CONV_W), (0, 0)))
    cb = row(conv_b[0])
    dtb = jnp.pad(row(dt_bias[0]), ((0, 0), (0, LANES - SSM_H)))
    alog = jnp.pad(row(a_log[0]), ((0, 0), (0, LANES - SSM_H)))
    dskip_e = row(jnp.repeat(d_skip[0], SSM_P))
    g_ssm = row(g_ssm_out[0])

    c_p8 = jnp.pad(c_prompt, ((0, 8 - c_prompt.shape[0]), (0, 0)))
    mod_s, mod_p = _ada(c_sample, c_p8, w_ada[0], row(b_ada[0]))

    xp = x_prompt[0]
    q, kv, z, xb, kv_last = _inproj(xp, row(g_pre_mix[0]), mod_p, False, w_all)
    att = _attn_prompt(q, kv, slopes, sinks, g_att)
    ssm, h_p = _ssd_prompt(xb, z, cw8, cb, dtb, alog, dskip_e, g_ssm)
    x1, u2 = _outproj(att, ssm, xp, mod_p, False, woa, wos, row(g_post_mix[0]), row(g_pre_ffn[0]))
    y_p = _ffn(u2, x1, mod_p, False, wg, wu, wd, row(g_post_ffn[0]))

    keys_minor = lambda c: jnp.transpose(c, (0, 2, 3, 1)).reshape(n_s, KV_W, WINDOW)
    keys_major = lambda c: jnp.transpose(c.reshape(n_s, N_KV, HEAD_DIM, WINDOW), (0, 3, 1, 2))[None]
    xs_ = x_sample[:, 0, :]
    q_s, _, z_s, xb_s, kv_new = _inproj(xs_, row(g_pre_mix[0]), mod_s, True, w_all, wq_s)
    att_s, k_s, v_s = _attn_sample(
        q_s, kv_new[:, :KV_W], kv_new[:, KV_W:],
        keys_minor(cache_k[0]), keys_minor(cache_v[0]), slopes_s, sinks_s, g_att_s)
    ssm_s, conv_s, h_s = _ssd_sample(
        xb_s, state_conv[0].reshape(n_s, (CONV_W - 1) * CONV_DIM), z_s,
        state_ssm[0].reshape(n_s, SSM_W, D_STATE), cw8, cb, dtb, alog, dskip_e, g_ssm)
    x1_s, u2_s = _outproj(att_s, ssm_s, xs_, mod_s, True, woa_s, wos, row(g_post_mix[0]), row(g_pre_ffn[0]))
    y_s = _ffn(u2_s, x1_s, mod_s, True, wg, wu, wd, row(g_post_ffn[0]))

    return (
        y_p[None],
        y_s[:, None, :],
        kv_last[:, :KV_W].reshape(1, 1, WINDOW, N_KV, HEAD_DIM),
        kv_last[:, KV_W:].reshape(1, 1, WINDOW, N_KV, HEAD_DIM),
        xb[n_p - (CONV_W - 1):, :CONV_DIM].reshape(1, 1, CONV_W - 1, CONV_DIM),
        h_p.reshape(1, 1, SSM_H, SSM_P, D_STATE),
        keys_major(k_s),
        keys_major(v_s),
        conv_s.reshape(1, n_s, CONV_W - 1, CONV_DIM),
        h_s.reshape(1, n_s, SSM_H, SSM_P, D_STATE),
    )
```

```python
import functools

import numpy as np
import jax
import jax.numpy as jnp
from jax import lax
from jax.experimental import pallas as pl
from jax.experimental.pallas import tpu as pltpu

F32 = jnp.float32
BF16 = jnp.bfloat16

D_MODEL = 2048
ATT_W = 1024
HEAD_DIM = 64
N_HEADS = 16
N_KV = 4
Q_PER_KV = 4
KV_W = N_KV * HEAD_DIM
WINDOW = 128
SSM_W = 1024
SSM_P = 64
SSM_H = 16
SSM_G = 2
D_STATE = 128
CONV_W = 4
CONV_DIM = SSM_W + 2 * SSM_G * D_STATE
XBCDT_W = CONV_DIM + 128
D_FF = 5632
EPS = 1e-6
CHUNK = 128
NEG_INF = float("-inf")

VMEM_LIMIT = 56 * 1024 * 1024
VMEM_LIMIT_FFN = 60 * 1024 * 1024
LANES = 128

ROWS_IN = 512
ROWS_ATT = 512
ROWS_SSD = 512
ROWS_OUT = 512
SUB_OUT = 128
ROWS_FFN = 1024
SUB_FFN = 512
FF_BLK = 512
SEQ_BLK = 8


def _params(sem, vmem=VMEM_LIMIT):
    return pltpu.CompilerParams(dimension_semantics=sem, vmem_limit_bytes=vmem)


def _silu(v):
    h = 0.5 * v
    return h + h * jnp.tanh(h)


def _softplus(v):
    return jnp.maximum(v, 0.0) + jnp.log1p(jnp.exp(-jnp.abs(v)))


def _rms(v, g):
    return v * lax.rsqrt(jnp.mean(v * v, axis=-1, keepdims=True) + EPS) * g


def _split3(v):
    hi = v.astype(BF16)
    r1 = v - hi.astype(F32)
    mid = r1.astype(BF16)
    lo = (r1 - mid.astype(F32)).astype(BF16)
    return hi, mid, lo


def _dot(a, b):
    return jnp.dot(a, b, preferred_element_type=F32)


def _dot_nt(a, b):
    return lax.dot_general(a, b, (((1,), (1,)), ((), ())), preferred_element_type=F32)


def _dot_tn(a, b):
    return lax.dot_general(a, b, (((0,), (0,)), ((), ())), preferred_element_type=F32)


def _expand_heads(v, e_mat):
    hi, mid, lo = _split3(v)
    return _dot(hi, e_mat) + _dot(mid, e_mat) + _dot(lo, e_mat)


def _head_expand_matrix():
    k = lax.broadcasted_iota(jnp.int32, (LANES, SSM_W), 0)
    c = lax.broadcasted_iota(jnp.int32, (LANES, SSM_W), 1)
    return jnp.where((c // SSM_P) == k, 1.0, 0.0).astype(BF16)


def _ada_kernel(cs_ref, cp_ref, w_ref, b_ref, ms_ref, mp_ref):
    w = w_ref[...].astype(BF16)
    b = b_ref[...]
    ms_ref[...] = _dot(_silu(cs_ref[...]).astype(BF16), w) + b
    mp_ref[...] = _dot(_silu(cp_ref[...]).astype(BF16), w) + b


def _ada(c_s, c_p8, w_ada, b_ada):
    n = w_ada.shape[1]
    bn = 1024
    return pl.pallas_call(
        _ada_kernel,
        grid=(n // bn,),
        in_specs=[
            pl.BlockSpec(c_s.shape, lambda j: (0, 0)),
            pl.BlockSpec(c_p8.shape, lambda j: (0, 0)),
            pl.BlockSpec((D_MODEL, bn), lambda j: (0, j)),
            pl.BlockSpec((1, bn), lambda j: (0, j)),
        ],
        out_specs=[
            pl.BlockSpec((c_s.shape[0], bn), lambda j: (0, j)),
            pl.BlockSpec((c_p8.shape[0], bn), lambda j: (0, j)),
        ],
        out_shape=[
            jax.ShapeDtypeStruct((c_s.shape[0], n), F32),
            jax.ShapeDtypeStruct((c_p8.shape[0], n), F32),
        ],
        compiler_params=_params(("parallel",)),
        name="ada",
    )(c_s, c_p8, w_ada, b_ada)


def _mod_spec(per_row, rows, col):
    if per_row:
        return pl.BlockSpec((rows, D_MODEL), lambda i, *_: (i, col))
    return pl.BlockSpec((8, D_MODEL), lambda i, *_: (0, col))


def _mod_rows(ref, per_row):
    return ref[...] if per_row else ref[0:1, :]


WS_BLK = 256
W_ALL_COLS = 4608
COL_X, COL_Q, COL_Z, COL_KV = 0, 2048, 3072, 4096
N_WS_SRC = -(-(ATT_W + 2 * KV_W + SSM_W + CONV_DIM + SSM_H) // WS_BLK)


def _ws_dst_block(s):
    q_end = ATT_W // WS_BLK
    kv_end = q_end + 2 * KV_W // WS_BLK
    z_end = kv_end + SSM_W // WS_BLK
    return jnp.where(s < q_end, COL_Q // WS_BLK + s,
                     jnp.where(s < kv_end, COL_KV // WS_BLK + s - q_end,
                               jnp.where(s < z_end, COL_Z // WS_BLK + s - kv_end,
                                         COL_X // WS_BLK + s - z_end)))


def _wsplit_kernel(wt_ref, o_ref):
    s = pl.program_id(0)
    n_valid = jnp.where(s < N_WS_SRC - 1, WS_BLK, jnp.where(s == N_WS_SRC - 1, SSM_H, 0))
    lane = lax.broadcasted_iota(jnp.int32, o_ref.shape, 1)
    o_ref[...] = jnp.where(lane < n_valid, wt_ref[...].T, 0.0).astype(BF16)


def _wsplit(w_in_t):
    k_dim = w_in_t.shape[1]
    return pl.pallas_call(
        _wsplit_kernel,
        grid=(N_WS_SRC + 1,),
        in_specs=[pl.BlockSpec((WS_BLK, k_dim), lambda s: (jnp.minimum(s, N_WS_SRC - 1), 0))],
        out_specs=pl.BlockSpec((k_dim, WS_BLK), lambda s: (0, _ws_dst_block(s))),
        out_shape=jax.ShapeDtypeStruct((k_dim, W_ALL_COLS), BF16),
        compiler_params=_params(("arbitrary",)),
        name="wsplit",
    )(w_in_t)


def _inproj_kernel(x_ref, g_ref, sh_ref, sc_ref, wq_ref, wkv_ref, wz_ref, wx_ref,
                   q_ref, kv_ref, z_ref, xb_ref, kvlast_ref, *, per_row, rows):
    u = _rms(x_ref[...], g_ref[...]) * (1.0 + _mod_rows(sc_ref, per_row)) + _mod_rows(sh_ref, per_row)
    u = u.astype(BF16)
    q_ref[...] = (_dot(u, wq_ref[...]) * (HEAD_DIM ** -0.5)).astype(BF16)
    kv = _dot(u, wkv_ref[...])
    kv_ref[...] = kv.astype(BF16)
    z_ref[...] = _dot(u, wz_ref[...])
    xb_ref[...] = _dot(u, wx_ref[...])

    @pl.when(pl.program_id(0) == pl.num_programs(0) - 1)
    def _():
        kvlast_ref[...] = kv[rows - WINDOW:, :]


def _const_spec(shape):
    return pl.BlockSpec(shape, lambda i, *_: (0,) * len(shape), pipeline_mode=pl.Buffered(1))


def _inproj(x, g, mod, per_row, w_all):
    n_rows = x.shape[0]
    rows = min(ROWS_IN, n_rows)
    row_spec = lambda w: pl.BlockSpec((rows, w), lambda i: (i, 0))
    seg_spec = lambda width, col: pl.BlockSpec((D_MODEL, width), lambda i: (0, col // width),
                                               pipeline_mode=pl.Buffered(1))
    return pl.pallas_call(
        functools.partial(_inproj_kernel, per_row=per_row, rows=rows),
        grid=(n_rows // rows,),
        in_specs=[
            row_spec(D_MODEL),
            _const_spec((1, D_MODEL)),
            _mod_spec(per_row, rows, 0),
            _mod_spec(per_row, rows, 1),
            seg_spec(ATT_W, COL_Q), seg_spec(2 * KV_W, COL_KV), seg_spec(SSM_W, COL_Z), seg_spec(XBCDT_W, COL_X),
        ],
        out_specs=[
            row_spec(ATT_W), row_spec(2 * KV_W), row_spec(SSM_W), row_spec(XBCDT_W),
            pl.BlockSpec((WINDOW, 2 * KV_W), lambda i: (0, 0)),
        ],
        out_shape=[
            jax.ShapeDtypeStruct((n_rows, ATT_W), BF16),
            jax.ShapeDtypeStruct((n_rows, 2 * KV_W), BF16),
            jax.ShapeDtypeStruct((n_rows, SSM_W), F32),
            jax.ShapeDtypeStruct((n_rows, XBCDT_W), F32),
            jax.ShapeDtypeStruct((WINDOW, 2 * KV_W), F32),
        ],
        compiler_params=_params(("arbitrary",)),
        name="inproj",
    )(x, g, mod, mod, w_all, w_all, w_all, w_all)


def _pair_blockdiag(lo_src, hi_src, keep_lo, keep_hi):
    zero = jnp.zeros_like(lo_src)
    return jnp.concatenate([jnp.where(keep_lo, lo_src, zero), jnp.where(keep_hi, hi_src, zero)], axis=0)


def _attn_prompt_kernel(slopes_ref, sinks_ref, q_ref, kvc_ref, kvp_ref, g_ref, wg_ref, wu_ref, wd_ref,
                        o_ref, wg_o, wu_o, wd_o, bias_scr, kvdup_scr, *, n_sub):
    i = pl.program_id(0)
    blk = WINDOW

    wg_o[...] = wg_ref[...].astype(BF16)
    wu_o[...] = wu_ref[...].astype(BF16)
    wd_o[...] = wd_ref[...].astype(BF16)

    @pl.when(i == 0)
    def _():
        a = lax.broadcasted_iota(jnp.int32, (blk, 2 * blk), 0)
        j = lax.broadcasted_iota(jnp.int32, (blk, 2 * blk), 1)
        dist = a + blk - j
        valid = (dist >= 0) & (dist < WINDOW)
        distf = dist.astype(F32)
        for h in range(N_HEADS):
            kvh, g = divmod(h, Q_PER_KV)
            pr, t = divmod(g, 2)
            b = jnp.where(valid, -(slopes_ref[h] * distf), NEG_INF)
            b = jnp.where(j == 0, sinks_ref[h], b)
            rs = slice(pr * blk, (pr + 1) * blk)
            cs = slice(t * 2 * blk, (t + 1) * 2 * blk)
            bias_scr[1, kvh, rs, cs] = b
            bias_scr[0, kvh, rs, cs] = jnp.where((j >= blk) | (j == 0), b, NEG_INF)

    for src_ref, r_lo, r_hi in ((kvp_ref, 0, blk), (kvc_ref, blk, kvdup_scr.shape[0])):
        lo_half = lax.broadcasted_iota(jnp.int32, (r_hi - r_lo, LANES), 1) < HEAD_DIM
        for part in range(2):
            for s in range(2):
                c_in = part * KV_W + s * LANES
                c_out = part * N_KV * LANES + 2 * s * LANES
                x = src_ref[:, c_in:c_in + LANES].astype(F32)
                xr = pltpu.roll(x, HEAD_DIM, 1)
                kvdup_scr[r_lo:r_hi, c_out:c_out + LANES] = jnp.where(lo_half, x, xr).astype(BF16)
                kvdup_scr[r_lo:r_hi, c_out + LANES:c_out + 2 * LANES] = jnp.where(lo_half, xr, x).astype(BF16)

    lane = lax.broadcasted_iota(jnp.int32, (2 * blk, LANES), 1)
    not_sink = lax.broadcasted_iota(jnp.int32, (2 * blk, LANES), 0) != 0
    keep_lo = (lane < HEAD_DIM) & not_sink
    keep_hi = (lane >= HEAD_DIM) & not_sink
    ones_cols = jnp.concatenate([jnp.where(lane < HEAD_DIM, 1.0, 0.0),
                                 jnp.where(lane >= HEAD_DIM, 1.0, 0.0)], axis=0).astype(BF16)
    n_chunk = 4
    rc = 2 * blk // n_chunk

    def body(b, carry):
        r0 = pl.multiple_of(b * blk, blk)
        variant = jnp.where(jnp.logical_and(i == 0, b == 0), 0, 1)
        vbds, scores = [], []
        for kvh in range(N_KV):
            c0 = kvh * Q_PER_KV * HEAD_DIM
            kd = kvdup_scr[pl.ds(r0, 2 * blk), kvh * LANES:(kvh + 1) * LANES]
            vd = kvdup_scr[pl.ds(r0, 2 * blk), (N_KV + kvh) * LANES:(N_KV + kvh + 1) * LANES]
            kbd = _pair_blockdiag(kd, kd, keep_lo, keep_hi)
            vbds.append(jnp.concatenate([_pair_blockdiag(vd, vd, keep_lo, keep_hi), ones_cols], axis=1))
            q2 = jnp.concatenate([q_ref[pl.ds(r0, blk), c0:c0 + LANES],
                                  q_ref[pl.ds(r0, blk), c0 + LANES:c0 + 2 * LANES]], axis=0)
            scores.append(_dot_nt(q2, kbd) + bias_scr[variant, kvh])
        probs = []
        for kvh in range(N_KV):
            chunks = []
            for c in range(n_chunk):
                st = scores[kvh][c * rc:(c + 1) * rc, :]
                sl = st[:, 0:2 * blk]
                sr = st[:, 2 * blk:]
                p_l = jnp.exp(sl - jnp.max(sl, axis=-1, keepdims=True))
                p_r = jnp.exp(sr - jnp.max(sr, axis=-1, keepdims=True))
                chunks.append(jnp.concatenate([p_l, p_r], axis=1).astype(BF16))
            probs.append(jnp.concatenate(chunks, axis=0))
        outs = []
        for kvh in range(N_KV):
            o2 = _dot(probs[kvh], vbds[kvh])
            o = o2[:, 0:LANES] * (1.0 / o2[:, LANES:])
            outs += [o[0:blk], o[blk:]]
        att = jnp.concatenate(outs, axis=1)
        o_ref[pl.ds(r0, blk), :] = _rms(att, g_ref[...]).astype(BF16)
        return carry

    lax.fori_loop(0, n_sub, body, 0, unroll=2)


def _attn_prompt(q, kv, slopes, sinks, g_att, w_gate, w_up, w_down):
    n_rows = q.shape[0]
    rows = ROWS_ATT
    n_sub = rows // WINDOW
    n_steps = n_rows // rows
    smem = pl.BlockSpec(memory_space=pltpu.SMEM)
    slab = lambda w: pl.BlockSpec((w.shape[0] // n_steps, w.shape[1]), lambda i: (i, 0))
    ffn_w = (w_gate, w_up, w_down)
    assert all(w.shape[0] % (16 * n_steps) == 0 for w in ffn_w)
    return pl.pallas_call(
        functools.partial(_attn_prompt_kernel, n_sub=n_sub),
        grid=(n_rows // rows,),
        in_specs=[
            smem, smem,
            pl.BlockSpec((rows, ATT_W), lambda i: (i, 0)),
            pl.BlockSpec((rows, 2 * KV_W), lambda i: (i, 0)),
            pl.BlockSpec((WINDOW, 2 * KV_W), lambda i: (jnp.maximum(i * n_sub - 1, 0), 0)),
            pl.BlockSpec((1, ATT_W), lambda i: (0, 0)),
            *[slab(w) for w in ffn_w],
        ],
        out_specs=[pl.BlockSpec((rows, ATT_W), lambda i: (i, 0)), *[slab(w) for w in ffn_w]],
        out_shape=[jax.ShapeDtypeStruct((n_rows, ATT_W), BF16),
                   *[jax.ShapeDtypeStruct(w.shape, BF16) for w in ffn_w]],
        scratch_shapes=[
            pltpu.VMEM((2, N_KV, 2 * WINDOW, 4 * WINDOW), F32),
            pltpu.VMEM((rows + WINDOW, 2 * N_KV * LANES), BF16),
        ],
        compiler_params=_params(("arbitrary",)),
        name="attn_prompt",
    )(slopes, sinks, q, kv, kv, g_att, *ffn_w)


def _attn_sample_kernel(q_ref, kn_ref, vn_ref, ck_ref, cv_ref, slope_ref, sink_ref, g_ref, perm_ref,
                        o_ref, ko_ref, vo_ref, att_scr, *, n_seq):
    q_sw = _dot(q_ref[...], perm_ref[...])
    r16 = lax.broadcasted_iota(jnp.int32, (N_HEADS, ATT_W), 0)
    c16 = lax.broadcasted_iota(jnp.int32, (N_HEADS, ATT_W), 1)
    own_head = (c16 // HEAD_DIM) == r16
    newest = lax.broadcasted_iota(jnp.int32, (KV_W, WINDOW), 1) == WINDOW - 1
    jj = lax.broadcasted_iota(jnp.int32, (N_HEADS, WINDOW), 1)
    bias = -(slope_ref[...] * (WINDOW - 1 - jj).astype(F32))
    pad = jnp.zeros((LANES - n_seq, KV_W), F32)
    kn_t = jnp.concatenate([kn_ref[...], pad], axis=0).T
    vn_t = jnp.concatenate([vn_ref[...], pad], axis=0).T

    def fold4(v):
        return (v[:, 0:KV_W] + v[:, KV_W:2 * KV_W]) + (v[:, 2 * KV_W:3 * KV_W] + v[:, 3 * KV_W:])

    scores, values = [], []
    for b in range(n_seq):
        kw = jnp.where(newest, jnp.broadcast_to(kn_t[:, b:b + 1], (KV_W, WINDOW)),
                       pltpu.roll(ck_ref[b], WINDOW - 1, 1))
        vw = jnp.where(newest, jnp.broadcast_to(vn_t[:, b:b + 1], (KV_W, WINDOW)),
                       pltpu.roll(cv_ref[b], WINDOW - 1, 1))
        ko_ref[b] = kw
        vo_ref[b] = vw
        qb = jnp.broadcast_to(q_sw[b:b + 1, :], (N_HEADS, ATT_W))
        qbd = fold4(jnp.where(own_head, qb, 0.0))
        scores.append(_dot(qbd.astype(BF16), kw.astype(BF16)))
        values.append(vw.astype(BF16))
    st = jnp.concatenate(scores, axis=0) + jnp.concatenate([bias] * n_seq, axis=0)
    sink = jnp.concatenate([sink_ref[...]] * n_seq, axis=0)
    m = jnp.maximum(jnp.max(st, axis=-1, keepdims=True), sink)
    p = jnp.exp(st - m)
    inv_l = 1.0 / (jnp.sum(p, axis=-1, keepdims=True) + jnp.exp(sink - m))
    pb = p.astype(BF16)
    for b in range(n_seq):
        hs = slice(b * N_HEADS, (b + 1) * N_HEADS)
        o = _dot_nt(pb[hs, :], values[b]) * inv_l[hs, :]
        o4 = jnp.concatenate([o, o, o, o], axis=1)
        att_scr[b:b + 1, :] = jnp.sum(jnp.where(own_head, o4, 0.0), axis=0, keepdims=True)
    normed = _rms(att_scr[...], g_ref[...]).astype(BF16)
    o_ref[...] = _dot_nt(normed, perm_ref[...]).astype(BF16)


def _attn_sample(q, kn, vn, ck, cv, slope_col, sink_col, g_att, perm):
    n = q.shape[0]
    nb = SEQ_BLK
    cache_spec = pl.BlockSpec((nb, KV_W, WINDOW), lambda i: (i, 0, 0))
    row_spec = lambda w: pl.BlockSpec((nb, w), lambda i: (i, 0))
    full = lambda a: pl.BlockSpec(a.shape, lambda i: (0,) * a.ndim)
    return pl.pallas_call(
        functools.partial(_attn_sample_kernel, n_seq=nb),
        grid=(n // nb,),
        in_specs=[row_spec(ATT_W), row_spec(KV_W), row_spec(KV_W), cache_spec, cache_spec,
                  full(slope_col), full(sink_col), full(g_att), full(perm)],
        out_specs=[row_spec(ATT_W), cache_spec, cache_spec],
        out_shape=[
            jax.ShapeDtypeStruct((n, ATT_W), BF16),
            jax.ShapeDtypeStruct(ck.shape, F32),
            jax.ShapeDtypeStruct(cv.shape, F32),
        ],
        scratch_shapes=[pltpu.VMEM((nb, ATT_W), F32)],
        compiler_params=_params(("parallel",)),
        name="attn_sample",
    )(q, kn, vn, ck, cv, slope_col, sink_col, g_att, perm)


def _ssd_prompt_kernel(xb_ref, xprev_ref, z_ref, cw_ref, cb_ref, dtb_ref, alog_ref, dskip_ref, g_ref,
                       y_ref, hout_ref, xp_scr, xc_scr, ht_scr, e_scr, tri_scr, yd_scr, *, rows):
    i = pl.program_id(0)
    T = CHUNK

    @pl.when(i == 0)
    def _():
        ht_scr[...] = jnp.zeros_like(ht_scr)
        e_scr[...] = _head_expand_matrix()
        l = lax.broadcasted_iota(jnp.int32, (T, T), 0)
        s = lax.broadcasted_iota(jnp.int32, (T, T), 1)
        tri_scr[...] = jnp.where(s <= l, 1.0, 0.0).astype(BF16)

    for s in range(CONV_DIM // LANES):
        ls = slice(s * LANES, (s + 1) * LANES)
        xp_scr[s, 0:8, :] = jnp.where(i == 0, 0.0, xprev_ref[:, ls])
        xp_scr[s, 8:, :] = xb_ref[:, ls]
        acc = cb_ref[:, ls] + xp_scr[s, pl.ds(8 - (CONV_W - 1), rows), :] * cw_ref[0:1, ls]
        for k in range(1, CONV_W):
            acc = acc + xp_scr[s, pl.ds(8 - (CONV_W - 1) + k, rows), :] * cw_ref[k:k + 1, ls]
        xc_scr[:, ls] = _silu(acc)

    a_neg = -jnp.exp(alog_ref[...])
    e_mat = e_scr[...]
    tri = tri_scr[...]
    causal = lax.broadcasted_iota(jnp.int32, (T, T), 0) >= lax.broadcasted_iota(jnp.int32, (T, T), 1)
    lo = lax.broadcasted_iota(jnp.int32, (T, LANES), 1) < SSM_P
    gw = SSM_W // SSM_G

    for c in range(rows // T):
        rs = slice(c * T, (c + 1) * T)
        dt = _softplus(xb_ref[rs, CONV_DIM:XBCDT_W] + dtb_ref[...])
        dta = dt * a_neg
        hi, mid, lw = _split3(dta)
        acum = _dot(tri, hi) + _dot(tri, mid) + _dot(tri, lw)
        acum_e = _expand_heads(acum, e_mat)
        dt_e = _expand_heads(dt, e_mat)
        acum_t = acum.T
        xs = xc_scr[rs, 0:SSM_W]
        bm = xc_scr[rs, SSM_W:SSM_W + SSM_G * D_STATE]
        cm = xc_scr[rs, SSM_W + SSM_G * D_STATE:CONV_DIM]
        xdt = xs * dt_e
        last = acum_e[T - 1:T, :]
        xdt_b = xdt.astype(BF16)
        xdec_b = (xdt * jnp.exp(last - acum_e)).astype(BF16)
        bm_b = bm.astype(BF16)
        cm_b = cm.astype(BF16)
        cbs = [_dot_nt(cm_b[:, g * D_STATE:(g + 1) * D_STATE], bm_b[:, g * D_STATE:(g + 1) * D_STATE])
               for g in range(SSM_G)]
        for pr in range(SSM_H // 2):
            g = (2 * pr) // (SSM_H // SSM_G)
            ws = []
            for t in range(2):
                h = 2 * pr + t
                seg = jnp.broadcast_to(acum[:, h:h + 1], (T, T)) - acum_t[h:h + 1, :]
                ws.append((cbs[g] * jnp.exp(jnp.where(causal, seg, NEG_INF))).astype(BF16))
            xsl = xdt_b[:, pr * LANES:(pr + 1) * LANES]
            yd_scr[:, pr * LANES:(pr + 1) * LANES] = _dot(jnp.concatenate(ws, axis=1),
                                                          _pair_blockdiag(xsl, xsl, lo, ~lo))
        yoff = []
        for g in range(SSM_G):
            gs = slice(g * gw, (g + 1) * gw)
            ht_g = ht_scr[:, gs]
            yoff.append(_dot(cm_b[:, g * D_STATE:(g + 1) * D_STATE], ht_g.astype(BF16)))
            cst = _dot_tn(bm_b[:, g * D_STATE:(g + 1) * D_STATE], xdec_b[:, gs])
            ht_scr[:, gs] = ht_g * jnp.exp(last[:, gs]) + cst
        y = yd_scr[...] + jnp.concatenate(yoff, axis=1) * jnp.exp(acum_e)
        y = y + dskip_ref[...] * xs
        y = y * _silu(z_ref[rs, :])
        y_ref[rs, :] = _rms(y, g_ref[...]).astype(BF16)

    @pl.when(i == pl.num_programs(0) - 1)
    def _():
        hout_ref[...] = ht_scr[...].T


def _ssd_prompt(xb, z, cw8, cb, dtb, alog, dskip_e, g_ssm):
    n_rows = xb.shape[0]
    rows = ROWS_SSD
    full = lambda a: pl.BlockSpec(a.shape, lambda i: (0,) * a.ndim)
    return pl.pallas_call(
        functools.partial(_ssd_prompt_kernel, rows=rows),
        grid=(n_rows // rows,),
        in_specs=[
            pl.BlockSpec((rows, XBCDT_W), lambda i: (i, 0)),
            pl.BlockSpec((8, XBCDT_W), lambda i: (jnp.maximum(i * (rows // 8) - 1, 0), 0)),
            pl.BlockSpec((rows, SSM_W), lambda i: (i, 0)),
            full(cw8), full(cb), full(dtb), full(alog), full(dskip_e), full(g_ssm),
        ],
        out_specs=[
            pl.BlockSpec((rows, SSM_W), lambda i: (i, 0)),
            pl.BlockSpec((SSM_W, D_STATE), lambda i: (0, 0)),
        ],
        out_shape=[
            jax.ShapeDtypeStruct((n_rows, SSM_W), BF16),
            jax.ShapeDtypeStruct((SSM_W, D_STATE), F32),
        ],
        scratch_shapes=[
            pltpu.VMEM((CONV_DIM // LANES, rows + 8, LANES), F32),
            pltpu.VMEM((rows, CONV_DIM), F32),
            pltpu.VMEM((D_STATE, SSM_W), F32),
            pltpu.VMEM((LANES, SSM_W), BF16),
            pltpu.VMEM((CHUNK, CHUNK), BF16),
            pltpu.VMEM((CHUNK, SSM_W), F32),
        ],
        compiler_params=_params(("arbitrary",)),
        name="ssd_prompt",
    )(xb, xb, z, cw8, cb, dtb, alog, dskip_e, g_ssm)


def _ssd_sample_kernel(xb_ref, sconv_ref, z_ref, h0_ref, cw_ref, cb_ref, dtb_ref, alog_ref, dskip_ref, g_ref,
                       y_ref, conv_ref, hout_ref, yoff_scr, *, n_seq):
    gw = SSM_W // SSM_G
    gn = SSM_G * D_STATE
    x_new = xb_ref[:, 0:CONV_DIM]
    taps = [sconv_ref[:, k * CONV_DIM:(k + 1) * CONV_DIM] for k in range(CONV_W - 1)] + [x_new]
    acc = cb_ref[...]
    for k in range(CONV_W):
        acc = acc + taps[k] * cw_ref[k:k + 1, :]
    for k in range(1, CONV_W):
        conv_ref[:, (k - 1) * CONV_DIM:k * CONV_DIM] = taps[k]
    xc = _silu(acc)
    xs = xc[:, 0:SSM_W]
    bm = xc[:, SSM_W:SSM_W + gn]
    cm = xc[:, SSM_W + gn:CONV_DIM]

    e_mat = _head_expand_matrix()
    dt = _softplus(xb_ref[:, CONV_DIM:XBCDT_W] + dtb_ref[...])
    dt_e = _expand_heads(dt, e_mat)
    dec_e = jnp.exp(_expand_heads(dt * (-jnp.exp(alog_ref[...])), e_mat))
    xdt = xs * dt_e

    lane_w = lax.broadcasted_iota(jnp.int32, (n_seq, SSM_W), 1)
    first_grp = lane_w < gw
    cbv = [jnp.sum(cm[:, g * D_STATE:(g + 1) * D_STATE] * bm[:, g * D_STATE:(g + 1) * D_STATE],
                   axis=-1, keepdims=True) for g in range(SSM_G)]
    cb_e = jnp.where(first_grp, cbv[0], cbv[1])

    pad = jnp.zeros((LANES - n_seq, SSM_W), F32)
    xdt_t = jnp.concatenate([xdt, pad], axis=0).T
    dec_t = jnp.concatenate([dec_e, pad], axis=0).T
    sub8 = lax.broadcasted_iota(jnp.int32, (8, D_STATE), 0)
    lane_r = lax.broadcasted_iota(jnp.int32, (8, SSM_W), 1)

    for b in range(n_seq):
        h0 = h0_ref[b]
        c_row = jnp.broadcast_to(cm[b:b + 1, :], (8, gn))
        c8 = jnp.where(sub8 == 0, c_row[:, 0:D_STATE], jnp.where(sub8 == 1, c_row[:, D_STATE:], 0.0))
        r = _dot_nt(c8.astype(BF16), h0.astype(BF16))
        yoff_scr[b:b + 1, :] = jnp.where(lane_r[0:1] < gw, r[0:1, :], r[1:2, :])
        b_row = bm[b:b + 1, :]
        for g in range(SSM_G):
            rs = slice(g * gw, (g + 1) * gw)
            dcol = jnp.broadcast_to(dec_t[rs, b:b + 1], (gw, D_STATE))
            xcol = jnp.broadcast_to(xdt_t[rs, b:b + 1], (gw, D_STATE))
            hout_ref[b, rs, :] = h0[rs, :] * dcol + xcol * b_row[:, g * D_STATE:(g + 1) * D_STATE]

    y = cb_e * xdt + yoff_scr[...] * dec_e
    y = y + dskip_ref[...] * xs
    y = y * _silu(z_ref[...])
    y_ref[...] = _rms(y, g_ref[...]).astype(BF16)


def _ssd_sample(xb, sconv, z, h0, cw8, cb, dtb, alog, dskip_e, g_ssm):
    n = xb.shape[0]
    nb = SEQ_BLK
    row_spec = lambda w: pl.BlockSpec((nb, w), lambda i: (i, 0))
    st_spec = pl.BlockSpec((nb, SSM_W, D_STATE), lambda i: (i, 0, 0))
    full = lambda a: pl.BlockSpec(a.shape, lambda i: (0,) * a.ndim)
    return pl.pallas_call(
        functools.partial(_ssd_sample_kernel, n_seq=nb),
        grid=(n // nb,),
        in_specs=[row_spec(XBCDT_W), row_spec((CONV_W - 1) * CONV_DIM), row_spec(SSM_W), st_spec,
                  full(cw8), full(cb), full(dtb), full(alog), full(dskip_e), full(g_ssm)],
        out_specs=[row_spec(SSM_W), row_spec((CONV_W - 1) * CONV_DIM), st_spec],
        out_shape=[
            jax.ShapeDtypeStruct((n, SSM_W), BF16),
            jax.ShapeDtypeStruct((n, (CONV_W - 1) * CONV_DIM), F32),
            jax.ShapeDtypeStruct(h0.shape, F32),
        ],
        scratch_shapes=[pltpu.VMEM((nb, SSM_W), F32)],
        compiler_params=_params(("parallel",)),
        name="ssd_sample",
    )(xb, sconv, z, h0, cw8, cb, dtb, alog, dskip_e, g_ssm)


def _outproj_kernel(att_ref, ssm_ref, x_ref, woa_ref, wos_ref, gpost_ref, gt1_ref, gpre_ref, sh2_ref, sc2_ref,
                    x1_ref, u2_ref, *, per_row, rows):
    sub = min(SUB_OUT, rows)
    for r0 in range(0, rows, sub):
        rs = slice(r0, r0 + sub)
        mod = lambda ref: ref[rs, :] if per_row else ref[0:1, :]
        mix = _dot(att_ref[rs, :], woa_ref[...]) + _dot(ssm_ref[rs, :], wos_ref[...])
        x1 = x_ref[rs, :] + mod(gt1_ref) * _rms(mix, gpost_ref[...])
        x1_ref[rs, :] = x1
        u2 = _rms(x1, gpre_ref[...]) * (1.0 + mod(sc2_ref)) + mod(sh2_ref)
        u2_ref[rs, :] = u2.astype(BF16)


def _outproj(att, ssm, x, mod, per_row, woa, wos, g_post, g_pre):
    n_rows = x.shape[0]
    rows = min(ROWS_OUT, n_rows)
    row_spec = lambda w: pl.BlockSpec((rows, w), lambda i: (i, 0))
    return pl.pallas_call(
        functools.partial(_outproj_kernel, per_row=per_row, rows=rows),
        grid=(n_rows // rows,),
        in_specs=[
            row_spec(ATT_W), row_spec(SSM_W), row_spec(D_MODEL),
            _const_spec(woa.shape), _const_spec(wos.shape),
            _const_spec((1, D_MODEL)),
            _mod_spec(per_row, rows, 2),
            _const_spec((1, D_MODEL)),
            _mod_spec(per_row, rows, 3),
            _mod_spec(per_row, rows, 4),
        ],
        out_specs=[row_spec(D_MODEL), row_spec(D_MODEL)],
        out_shape=[
            jax.ShapeDtypeStruct((n_rows, D_MODEL), F32),
            jax.ShapeDtypeStruct((n_rows, D_MODEL), BF16),
        ],
        compiler_params=_params(("parallel",)),
        name="outproj",
    )(att, ssm, x, woa, wos, g_post, mod, g_pre, mod, mod)


def _ffn_kernel(u_ref, x1_hbm, wg_ref, wu_ref, wd_ref, gpost_ref, gt2_ref, y_ref, x1_buf, x1_sem,
                *, per_row, rows):
    i = pl.program_id(0)
    j = pl.program_id(1)
    x1_copy = pltpu.make_async_copy(x1_hbm.at[pl.ds(i * rows, rows), :], x1_buf, x1_sem)
    sub = min(SUB_FFN, rows)

    last = pl.num_programs(1) - 1

    def d_ff_slice(first, final):
        for r0 in range(0, rows, sub):
            rs = slice(r0, r0 + sub)
            u = u_ref[rs, :]
            hid = (_silu(_dot(u, wg_ref[...])) * _dot(u, wu_ref[...])).astype(BF16)
            acc = _dot(hid, wd_ref[...])
            if not first:
                acc = y_ref[rs, :] + acc
            if final:
                gt2 = gt2_ref[rs, :] if per_row else gt2_ref[0:1, :]
                acc = x1_buf[rs, :] + gt2 * _rms(acc, gpost_ref[...])
            y_ref[rs, :] = acc

    @pl.when(j == 0)
    def _():
        x1_copy.start()
        d_ff_slice(True, False)

    @pl.when(jnp.logical_and(j > 0, j < last))
    def _():
        d_ff_slice(False, False)

    @pl.when(j == last)
    def _():
        x1_copy.wait()
        d_ff_slice(False, True)


def _ffn(u2, x1, mod, per_row, wg, wu, wd, g_post):
    n_rows = x1.shape[0]
    rows = min(ROWS_FFN, n_rows)
    fb = FF_BLK
    return pl.pallas_call(
        functools.partial(_ffn_kernel, per_row=per_row, rows=rows),
        grid=(n_rows // rows, D_FF // fb),
        in_specs=[
            pl.BlockSpec((rows, D_MODEL), lambda i, j: (i, 0)),
            pl.BlockSpec(memory_space=pl.ANY),
            pl.BlockSpec((D_MODEL, fb), lambda i, j: (0, j)),
            pl.BlockSpec((D_MODEL, fb), lambda i, j: (0, j)),
            pl.BlockSpec((fb, D_MODEL), lambda i, j: (j, 0)),
            pl.BlockSpec((1, D_MODEL), lambda i, j: (0, 0)),
            _mod_spec(per_row, rows, 5),
        ],
        out_specs=pl.BlockSpec((rows, D_MODEL), lambda i, j: (i, 0)),
        out_shape=jax.ShapeDtypeStruct((n_rows, D_MODEL), F32),
        scratch_shapes=[pltpu.VMEM((rows, D_MODEL), F32), pltpu.SemaphoreType.DMA],
        compiler_params=_params(("arbitrary", "arbitrary"), VMEM_LIMIT_FFN),
        name="ffn",
    )(u2, x1, wg, wu, wd, g_post, mod)


def _alibi_slopes():
    return (2.0 ** (-8.0 * np.arange(1, N_HEADS + 1) / N_HEADS)).astype(np.float32)


def kernel(x_prompt, x_sample, cache_k, cache_v, state_conv, state_ssm, c_prompt, c_sample, w_ada, b_ada, g_pre_mix, g_post_mix, w_in, attn_sinks, g_attn_out, conv_w, conv_b, dt_bias, a_log, d_skip, g_ssm_out, w_out, g_pre_ffn, g_post_ffn, w_gate, w_up, w_down):
    assert w_ada.shape[0] == 1, "one layer"
    n_p = x_prompt.shape[1]
    n_s = x_sample.shape[0]
    row = lambda v: v.reshape(1, -1)

    w_all = _wsplit(jnp.transpose(w_in[0]))
    woa = w_out[0, :ATT_W].astype(BF16)
    wos = w_out[0, ATT_W:].astype(BF16)
    g_att = row(g_attn_out[0])
    swapped_cols = np.arange(ATT_W).reshape(N_KV, Q_PER_KV, HEAD_DIM).swapaxes(0, 1).reshape(-1)
    perm = np.zeros((ATT_W, ATT_W), np.float32)
    perm[swapped_cols, np.arange(ATT_W)] = 1.0
    perm = jnp.asarray(perm, BF16)
    g_att_s = row(g_attn_out[0].reshape(N_KV, Q_PER_KV, HEAD_DIM).swapaxes(0, 1))
    slopes_np = _alibi_slopes()
    slopes = jnp.asarray(slopes_np)
    slopes_s = jnp.asarray(slopes_np.reshape(N_KV, Q_PER_KV).T.reshape(N_HEADS, 1))
    sinks = attn_sinks[0]
    sinks_s = sinks.reshape(N_KV, Q_PER_KV).T.reshape(N_HEADS, 1)
    cw8 = jnp.pad(conv_w[0], ((0, 8 - CONV_W), (0, 0)))
    cb = row(conv_b[0])
    dtb = jnp.pad(row(dt_bias[0]), ((0, 0), (0, LANES - SSM_H)))
    alog = jnp.pad(row(a_log[0]), ((0, 0), (0, LANES - SSM_H)))
    dskip_e = row(jnp.repeat(d_skip[0], SSM_P))
    g_ssm = row(g_ssm_out[0])

    c_p8 = jnp.pad(c_prompt, ((0, 8 - c_prompt.shape[0]), (0, 0)))
    mod_s, mod_p = _ada(c_sample, c_p8, w_ada[0], row(b_ada[0]))

    xp = x_prompt[0]
    q, kv, z, xb, kv_last = _inproj(xp, row(g_pre_mix[0]), mod_p, False, w_all)
    att, wg, wu, wd = _attn_prompt(q, kv, slopes, sinks, g_att, w_gate[0], w_up[0], w_down[0])
    ssm, h_p = _ssd_prompt(xb, z, cw8, cb, dtb, alog, dskip_e, g_ssm)
    x1, u2 = _outproj(att, ssm, xp, mod_p, False, woa, wos, row(g_post_mix[0]), row(g_pre_ffn[0]))
    y_p = _ffn(u2, x1, mod_p, False, wg, wu, wd, row(g_post_ffn[0]))

    keys_minor = lambda c: jnp.transpose(c, (0, 2, 3, 1)).reshape(n_s, KV_W, WINDOW)
    keys_major = lambda c: jnp.transpose(c.reshape(n_s, N_KV, HEAD_DIM, WINDOW), (0, 3, 1, 2))[None]
    xs_ = x_sample[:, 0, :]
    q_s, _, z_s, xb_s, kv_new = _inproj(xs_, row(g_pre_mix[0]), mod_s, True, w_all)
    att_s, k_s, v_s = _attn_sample(
        q_s, kv_new[:, :KV_W], kv_new[:, KV_W:],
        keys_minor(cache_k[0]), keys_minor(cache_v[0]), slopes_s, sinks_s, g_att_s, perm)
    ssm_s, conv_s, h_s = _ssd_sample(
        xb_s, state_conv[0].reshape(n_s, (CONV_W - 1) * CONV_DIM), z_s,
        state_ssm[0].reshape(n_s, SSM_W, D_STATE), cw8, cb, dtb, alog, dskip_e, g_ssm)
    x1_s, u2_s = _outproj(att_s, ssm_s, xs_, mod_s, True, woa, wos, row(g_post_mix[0]), row(g_pre_ffn[0]))
    y_s = _ffn(u2_s, x1_s, mod_s, True, wg, wu, wd, row(g_post_ffn[0]))

    return (
        y_p[None],
        y_s[:, None, :],
        kv_last[:, :KV_W].reshape(1, 1, WINDOW, N_KV, HEAD_DIM),
        kv_last[:, KV_W:].reshape(1, 1, WINDOW, N_KV, HEAD_DIM),
        xb[n_p - (CONV_W - 1):, :CONV_DIM].reshape(1, 1, CONV_W - 1, CONV_DIM),
        h_p.reshape(1, 1, SSM_H, SSM_P, D_STATE),
        keys_major(k_s),
        keys_major(v_s),
        conv_s.reshape(1, n_s, CONV_W - 1, CONV_DIM),
        h_s.reshape(1, n_s, SSM_H, SSM_P, D_STATE),
    )
```

```python
import functools

import numpy as np
import jax
import jax.numpy as jnp
from jax import lax
from jax.experimental import pallas as pl
from jax.experimental.pallas import tpu as pltpu

F32 = jnp.float32
BF16 = jnp.bfloat16

D_MODEL = 2048
ATT_W = 1024
HEAD_DIM = 64
N_HEADS = 16
N_KV = 4
Q_PER_KV = 4
KV_W = N_KV * HEAD_DIM
WINDOW = 128
SSM_W = 1024
SSM_P = 64
SSM_H = 16
SSM_G = 2
D_STATE = 128
CONV_W = 4
CONV_DIM = SSM_W + 2 * SSM_G * D_STATE
XBCDT_W = CONV_DIM + 128
D_FF = 5632
EPS = 1e-6
CHUNK = 128
NEG_INF = float("-inf")
LOG2E = 1.4426950408889634

VMEM_LIMIT = 56 * 1024 * 1024
VMEM_LIMIT_FFN = 60 * 1024 * 1024
LANES = 128

ROWS_IN = 512
ROWS_ATT = 512
ROWS_SSD = 512
ROWS_OUT = 512
SUB_OUT = 128
ROWS_FFN = 1024
SUB_FFN = 512
FF_BLK = 512
SEQ_BLK = 8


def _params(sem, vmem=VMEM_LIMIT):
    return pltpu.CompilerParams(dimension_semantics=sem, vmem_limit_bytes=vmem)


def _silu(v):
    h = 0.5 * v
    return h + h * jnp.tanh(h)


def _softplus(v):
    return jnp.maximum(v, 0.0) + jnp.log1p(jnp.exp(-jnp.abs(v)))


def _rms(v, g):
    return v * lax.rsqrt(jnp.mean(v * v, axis=-1, keepdims=True) + EPS) * g


def _split3(v):
    hi = v.astype(BF16)
    r1 = v - hi.astype(F32)
    mid = r1.astype(BF16)
    lo = (r1 - mid.astype(F32)).astype(BF16)
    return hi, mid, lo


def _dot(a, b):
    return jnp.dot(a, b, preferred_element_type=F32)


def _dot_nt(a, b):
    return lax.dot_general(a, b, (((1,), (1,)), ((), ())), preferred_element_type=F32)


def _dot_tn(a, b):
    return lax.dot_general(a, b, (((0,), (0,)), ((), ())), preferred_element_type=F32)


def _expand_heads(v, e_mat):
    hi, mid, lo = _split3(v)
    return _dot(hi, e_mat) + _dot(mid, e_mat) + _dot(lo, e_mat)


def _head_expand_matrix():
    k = lax.broadcasted_iota(jnp.int32, (LANES, SSM_W), 0)
    c = lax.broadcasted_iota(jnp.int32, (LANES, SSM_W), 1)
    return jnp.where((c // SSM_P) == k, 1.0, 0.0).astype(BF16)


def _ada_kernel(cs_ref, cp_ref, w_ref, b_ref, ms_ref, mp_ref):
    w = w_ref[...].astype(BF16)
    b = b_ref[...]
    ms_ref[...] = _dot(_silu(cs_ref[...]).astype(BF16), w) + b
    mp_ref[...] = _dot(_silu(cp_ref[...]).astype(BF16), w) + b


def _ada(c_s, c_p8, w_ada, b_ada):
    n = w_ada.shape[1]
    bn = 1024
    return pl.pallas_call(
        _ada_kernel,
        grid=(n // bn,),
        in_specs=[
            pl.BlockSpec(c_s.shape, lambda j: (0, 0)),
            pl.BlockSpec(c_p8.shape, lambda j: (0, 0)),
            pl.BlockSpec((D_MODEL, bn), lambda j: (0, j)),
            pl.BlockSpec((1, bn), lambda j: (0, j)),
        ],
        out_specs=[
            pl.BlockSpec((c_s.shape[0], bn), lambda j: (0, j)),
            pl.BlockSpec((c_p8.shape[0], bn), lambda j: (0, j)),
        ],
        out_shape=[
            jax.ShapeDtypeStruct((c_s.shape[0], n), F32),
            jax.ShapeDtypeStruct((c_p8.shape[0], n), F32),
        ],
        compiler_params=_params(("parallel",)),
        name="ada",
    )(c_s, c_p8, w_ada, b_ada)


def _mod_spec(per_row, rows, col):
    if per_row:
        return pl.BlockSpec((rows, D_MODEL), lambda i, *_: (i, col))
    return pl.BlockSpec((8, D_MODEL), lambda i, *_: (0, col))


def _mod_rows(ref, per_row):
    return ref[...] if per_row else ref[0:1, :]


WS_BLK = 256
W_ALL_COLS = 4608
COL_X, COL_Q, COL_Z, COL_KV = 0, 2048, 3072, 4096
N_WS_SRC = -(-(ATT_W + 2 * KV_W + SSM_W + CONV_DIM + SSM_H) // WS_BLK)


def _ws_dst_block(s):
    q_end = ATT_W // WS_BLK
    kv_end = q_end + 2 * KV_W // WS_BLK
    z_end = kv_end + SSM_W // WS_BLK
    return jnp.where(s < q_end, COL_Q // WS_BLK + s,
                     jnp.where(s < kv_end, COL_KV // WS_BLK + s - q_end,
                               jnp.where(s < z_end, COL_Z // WS_BLK + s - kv_end,
                                         COL_X // WS_BLK + s - z_end)))


def _wsplit_kernel(wt_ref, o_ref):
    s = pl.program_id(0)
    n_valid = jnp.where(s < N_WS_SRC - 1, WS_BLK, jnp.where(s == N_WS_SRC - 1, SSM_H, 0))
    lane = lax.broadcasted_iota(jnp.int32, o_ref.shape, 1)
    o_ref[...] = jnp.where(lane < n_valid, wt_ref[...].T, 0.0).astype(BF16)


def _wsplit(w_in_t):
    k_dim = w_in_t.shape[1]
    return pl.pallas_call(
        _wsplit_kernel,
        grid=(N_WS_SRC + 1,),
        in_specs=[pl.BlockSpec((WS_BLK, k_dim), lambda s: (jnp.minimum(s, N_WS_SRC - 1), 0))],
        out_specs=pl.BlockSpec((k_dim, WS_BLK), lambda s: (0, _ws_dst_block(s))),
        out_shape=jax.ShapeDtypeStruct((k_dim, W_ALL_COLS), BF16),
        compiler_params=_params(("arbitrary",)),
        name="wsplit",
    )(w_in_t)


def _inproj_kernel(x_ref, g_ref, sh_ref, sc_ref, wq_ref, wkv_ref, wz_ref, wx_ref, cw_ref, cb_ref,
                   q_ref, kv_ref, z_ref, xb_ref, kvlast_ref, xtail_ref, xp_scr, *, per_row, rows, fuse_conv):
    tile = 2 * LANES
    if fuse_conv:
        @pl.when(pl.program_id(0) == 0)
        def _():
            xp_scr[:, rows:rows + 8, :] = jnp.zeros((CONV_DIM // LANES, 8, LANES), F32)

    u = _rms(x_ref[...], g_ref[...]) * (1.0 + _mod_rows(sc_ref, per_row)) + _mod_rows(sh_ref, per_row)
    u = u.astype(BF16)

    def conv_slab(s, xs):
        ls = slice(s * LANES, (s + 1) * LANES)
        xtail_ref[:, ls] = xs[rows - 8:, :]
        xp_scr[s, 0:8, :] = xp_scr[s, rows:rows + 8, :]
        xp_scr[s, 8:, :] = xs
        acc = cb_ref[:, ls] + xp_scr[s, pl.ds(8 - (CONV_W - 1), rows), :] * cw_ref[0:1, ls]
        for k in range(1, CONV_W):
            acc = acc + xp_scr[s, pl.ds(8 - (CONV_W - 1) + k, rows), :] * cw_ref[k:k + 1, ls]
        xb_ref[:, ls] = _silu(acc)

    def post_z(c0, c1):
        def post(r):
            z_ref[:, c0:c1] = _silu(r) if fuse_conv else r
        return post

    def post_q(c0, c1):
        def post(r):
            q_ref[:, c0:c1] = (r * (HEAD_DIM ** -0.5)).astype(BF16)
        return post

    def post_kv(c0, c1):
        def post(r):
            kv_ref[:, c0:c1] = r.astype(BF16)
            kvlast_ref[:, c0:c1] = r[rows - WINDOW:, :]
        return post

    xb = _dot(u, wx_ref[...])
    xb_ref[:, CONV_DIM:] = xb[:, CONV_DIM:]
    if fuse_conv:
        slabs = [functools.partial(conv_slab, s, xb[:, s * LANES:(s + 1) * LANES]) for s in range(CONV_DIM // LANES)]
    else:
        xb_ref[:, 0:CONV_DIM] = xb[:, 0:CONV_DIM]
        xtail_ref[...] = xb[rows - 8:, 0:CONV_DIM]
        slabs = []

    jobs = []
    for w_ref, width, post_of in ((wz_ref, SSM_W, post_z), (wq_ref, ATT_W, post_q), (wkv_ref, 2 * KV_W, post_kv)):
        for c0 in range(0, width, tile):
            jobs.append((w_ref, c0, c0 + tile, post_of(c0, c0 + tile)))
    pending = None
    for n, (w_ref, c0, c1, post) in enumerate(jobs):
        r = _dot(u, w_ref[:, c0:c1])
        for slab in slabs[n * len(slabs) // len(jobs):(n + 1) * len(slabs) // len(jobs)]:
            slab()
        if pending is not None:
            pending[0](pending[1])
        pending = (post, r)
    pending[0](pending[1])


def _const_spec(shape):
    return pl.BlockSpec(shape, lambda i, *_: (0,) * len(shape), pipeline_mode=pl.Buffered(1))


def _inproj(x, g, mod, per_row, w_all, cw8, cb, fuse_conv):
    n_rows = x.shape[0]
    rows = min(ROWS_IN, n_rows)
    row_spec = lambda w: pl.BlockSpec((rows, w), lambda i: (i, 0))
    seg_spec = lambda width, col: pl.BlockSpec((D_MODEL, width), lambda i: (0, col // width),
                                               pipeline_mode=pl.Buffered(1))
    return pl.pallas_call(
        functools.partial(_inproj_kernel, per_row=per_row, rows=rows, fuse_conv=fuse_conv),
        grid=(n_rows // rows,),
        in_specs=[
            row_spec(D_MODEL),
            _const_spec((1, D_MODEL)),
            _mod_spec(per_row, rows, 0),
            _mod_spec(per_row, rows, 1),
            seg_spec(ATT_W, COL_Q), seg_spec(2 * KV_W, COL_KV), seg_spec(SSM_W, COL_Z), seg_spec(XBCDT_W, COL_X),
            _const_spec(cw8.shape), _const_spec(cb.shape),
        ],
        out_specs=[
            row_spec(ATT_W), row_spec(2 * KV_W), row_spec(SSM_W), row_spec(XBCDT_W),
            pl.BlockSpec((WINDOW, 2 * KV_W), lambda i: (0, 0)),
            pl.BlockSpec((8, CONV_DIM), lambda i: (0, 0)),
        ],
        out_shape=[
            jax.ShapeDtypeStruct((n_rows, ATT_W), BF16),
            jax.ShapeDtypeStruct((n_rows, 2 * KV_W), BF16),
            jax.ShapeDtypeStruct((n_rows, SSM_W), F32),
            jax.ShapeDtypeStruct((n_rows, XBCDT_W), F32),
            jax.ShapeDtypeStruct((WINDOW, 2 * KV_W), F32),
            jax.ShapeDtypeStruct((8, CONV_DIM), F32),
        ],
        scratch_shapes=[pltpu.VMEM((CONV_DIM // LANES, rows + 8, LANES), F32)],
        compiler_params=_params(("arbitrary",)),
        name="inproj",
    )(x, g, mod, mod, w_all, w_all, w_all, w_all, cw8, cb)


def _pair_blockdiag(lo_src, hi_src, keep_lo, keep_hi):
    zero = jnp.zeros_like(lo_src)
    return jnp.concatenate([jnp.where(keep_lo, lo_src, zero), jnp.where(keep_hi, hi_src, zero)], axis=0)


def _attn_prompt_kernel(slopes_ref, sinks_ref, q_ref, kvc_ref, kvp_ref, g_ref, wg_ref, wu_ref, wd_ref,
                        o_ref, wg_o, wu_o, wd_o, bias_scr, kvdup_scr, *, n_sub):
    i = pl.program_id(0)
    blk = WINDOW

    wg_o[...] = wg_ref[...].astype(BF16)
    wu_o[...] = wu_ref[...].astype(BF16)
    wd_o[...] = wd_ref[...].astype(BF16)

    @pl.when(i == 0)
    def _():
        a = lax.broadcasted_iota(jnp.int32, (blk, 2 * blk), 0)
        j = lax.broadcasted_iota(jnp.int32, (blk, 2 * blk), 1)
        dist = a + blk - j
        valid = (dist >= 0) & (dist < WINDOW)
        distf = dist.astype(F32)
        for h in range(N_HEADS):
            kvh, g = divmod(h, Q_PER_KV)
            pr, t = divmod(g, 2)
            b = jnp.where(valid, -(slopes_ref[h] * distf), NEG_INF)
            b = jnp.where(j == 0, sinks_ref[h], b)
            rs = slice(pr * blk, (pr + 1) * blk)
            cs = slice(t * 2 * blk, (t + 1) * 2 * blk)
            bias_scr[1, kvh, rs, cs] = b
            bias_scr[0, kvh, rs, cs] = jnp.where((j >= blk) | (j == 0), b, NEG_INF)

    for src_ref, r_lo, r_hi in ((kvp_ref, 0, blk), (kvc_ref, blk, kvdup_scr.shape[0])):
        lo_half = lax.broadcasted_iota(jnp.int32, (r_hi - r_lo, LANES), 1) < HEAD_DIM
        for part in range(2):
            for s in range(2):
                c_in = part * KV_W + s * LANES
                c_out = part * N_KV * LANES + 2 * s * LANES
                x = src_ref[:, c_in:c_in + LANES].astype(F32)
                xr = pltpu.roll(x, HEAD_DIM, 1)
                kvdup_scr[r_lo:r_hi, c_out:c_out + LANES] = jnp.where(lo_half, x, xr).astype(BF16)
                kvdup_scr[r_lo:r_hi, c_out + LANES:c_out + 2 * LANES] = jnp.where(lo_half, xr, x).astype(BF16)

    lane = lax.broadcasted_iota(jnp.int32, (2 * blk, LANES), 1)
    not_sink = lax.broadcasted_iota(jnp.int32, (2 * blk, LANES), 0) != 0
    keep_lo = (lane < HEAD_DIM) & not_sink
    keep_hi = (lane >= HEAD_DIM) & not_sink
    ones_cols = jnp.concatenate([jnp.where(lane < HEAD_DIM, 1.0, 0.0),
                                 jnp.where(lane >= HEAD_DIM, 1.0, 0.0)], axis=0).astype(BF16)
    n_chunk = 4
    rc = 2 * blk // n_chunk

    def body(b, carry):
        r0 = pl.multiple_of(b * blk, blk)
        variant = jnp.where(jnp.logical_and(i == 0, b == 0), 0, 1)
        vbds, scores = [], []
        for kvh in range(N_KV):
            c0 = kvh * Q_PER_KV * HEAD_DIM
            kd = kvdup_scr[pl.ds(r0, 2 * blk), kvh * LANES:(kvh + 1) * LANES]
            vd = kvdup_scr[pl.ds(r0, 2 * blk), (N_KV + kvh) * LANES:(N_KV + kvh + 1) * LANES]
            kbd = _pair_blockdiag(kd, kd, keep_lo, keep_hi)
            vbds.append(jnp.concatenate([_pair_blockdiag(vd, vd, keep_lo, keep_hi), ones_cols], axis=1))
            q2 = jnp.concatenate([q_ref[pl.ds(r0, blk), c0:c0 + LANES],
                                  q_ref[pl.ds(r0, blk), c0 + LANES:c0 + 2 * LANES]], axis=0)
            scores.append(_dot_nt(q2, kbd) + bias_scr[variant, kvh])
        probs = []
        for kvh in range(N_KV):
            chunks = []
            for c in range(n_chunk):
                st = scores[kvh][c * rc:(c + 1) * rc, :]
                sl = st[:, 0:2 * blk]
                sr = st[:, 2 * blk:]
                p_l = jnp.exp(sl - jnp.max(sl, axis=-1, keepdims=True))
                p_r = jnp.exp(sr - jnp.max(sr, axis=-1, keepdims=True))
                chunks.append(jnp.concatenate([p_l, p_r], axis=1).astype(BF16))
            probs.append(jnp.concatenate(chunks, axis=0))
        outs = []
        for kvh in range(N_KV):
            o2 = _dot(probs[kvh], vbds[kvh])
            o = o2[:, 0:LANES] * (1.0 / o2[:, LANES:])
            outs += [o[0:blk], o[blk:]]
        att = jnp.concatenate(outs, axis=1)
        o_ref[pl.ds(r0, blk), :] = _rms(att, g_ref[...]).astype(BF16)
        return carry

    lax.fori_loop(0, n_sub, body, 0, unroll=2)


def _attn_prompt(q, kv, slopes, sinks, g_att, w_gate, w_up, w_down):
    n_rows = q.shape[0]
    rows = ROWS_ATT
    n_sub = rows // WINDOW
    n_steps = n_rows // rows
    smem = pl.BlockSpec(memory_space=pltpu.SMEM)
    slab = lambda w: pl.BlockSpec((w.shape[0] // n_steps, w.shape[1]), lambda i: (i, 0))
    ffn_w = (w_gate, w_up, w_down)
    assert all(w.shape[0] % (16 * n_steps) == 0 for w in ffn_w)
    return pl.pallas_call(
        functools.partial(_attn_prompt_kernel, n_sub=n_sub),
        grid=(n_rows // rows,),
        in_specs=[
            smem, smem,
            pl.BlockSpec((rows, ATT_W), lambda i: (i, 0)),
            pl.BlockSpec((rows, 2 * KV_W), lambda i: (i, 0)),
            pl.BlockSpec((WINDOW, 2 * KV_W), lambda i: (jnp.maximum(i * n_sub - 1, 0), 0)),
            pl.BlockSpec((1, ATT_W), lambda i: (0, 0)),
            *[slab(w) for w in ffn_w],
        ],
        out_specs=[pl.BlockSpec((rows, ATT_W), lambda i: (i, 0)), *[slab(w) for w in ffn_w]],
        out_shape=[jax.ShapeDtypeStruct((n_rows, ATT_W), BF16),
                   *[jax.ShapeDtypeStruct(w.shape, BF16) for w in ffn_w]],
        scratch_shapes=[
            pltpu.VMEM((2, N_KV, 2 * WINDOW, 4 * WINDOW), F32),
            pltpu.VMEM((rows + WINDOW, 2 * N_KV * LANES), BF16),
        ],
        compiler_params=_params(("arbitrary",)),
        name="attn_prompt",
    )(slopes, sinks, q, kv, kv, g_att, *ffn_w)


def _attn_sample_kernel(q_ref, kn_ref, vn_ref, ck_ref, cv_ref, slope_ref, sink_ref, g_ref, perm_ref,
                        o_ref, ko_ref, vo_ref, att_scr, *, n_seq):
    q_sw = _dot(q_ref[...], perm_ref[...])
    r16 = lax.broadcasted_iota(jnp.int32, (N_HEADS, ATT_W), 0)
    c16 = lax.broadcasted_iota(jnp.int32, (N_HEADS, ATT_W), 1)
    own_head = (c16 // HEAD_DIM) == r16
    newest = lax.broadcasted_iota(jnp.int32, (KV_W, WINDOW), 1) == WINDOW - 1
    jj = lax.broadcasted_iota(jnp.int32, (N_HEADS, WINDOW), 1)
    bias = -(slope_ref[...] * (WINDOW - 1 - jj).astype(F32))
    pad = jnp.zeros((LANES - n_seq, KV_W), F32)
    kn_t = jnp.concatenate([kn_ref[...], pad], axis=0).T
    vn_t = jnp.concatenate([vn_ref[...], pad], axis=0).T

    def fold4(v):
        return (v[:, 0:KV_W] + v[:, KV_W:2 * KV_W]) + (v[:, 2 * KV_W:3 * KV_W] + v[:, 3 * KV_W:])

    scores, values = [], []
    for b in range(n_seq):
        kw = jnp.where(newest, jnp.broadcast_to(kn_t[:, b:b + 1], (KV_W, WINDOW)),
                       pltpu.roll(ck_ref[b], WINDOW - 1, 1))
        vw = jnp.where(newest, jnp.broadcast_to(vn_t[:, b:b + 1], (KV_W, WINDOW)),
                       pltpu.roll(cv_ref[b], WINDOW - 1, 1))
        ko_ref[b] = kw
        vo_ref[b] = vw
        qb = jnp.broadcast_to(q_sw[b:b + 1, :], (N_HEADS, ATT_W))
        qbd = fold4(jnp.where(own_head, qb, 0.0))
        scores.append(_dot(qbd.astype(BF16), kw.astype(BF16)))
        values.append(vw.astype(BF16))
    st = jnp.concatenate(scores, axis=0) + jnp.concatenate([bias] * n_seq, axis=0)
    sink = jnp.concatenate([sink_ref[...]] * n_seq, axis=0)
    m = jnp.maximum(jnp.max(st, axis=-1, keepdims=True), sink)
    p = jnp.exp(st - m)
    inv_l = 1.0 / (jnp.sum(p, axis=-1, keepdims=True) + jnp.exp(sink - m))
    pb = p.astype(BF16)
    for b in range(n_seq):
        hs = slice(b * N_HEADS, (b + 1) * N_HEADS)
        o = _dot_nt(pb[hs, :], values[b]) * inv_l[hs, :]
        o4 = jnp.concatenate([o, o, o, o], axis=1)
        att_scr[b:b + 1, :] = jnp.sum(jnp.where(own_head, o4, 0.0), axis=0, keepdims=True)
    normed = _rms(att_scr[...], g_ref[...]).astype(BF16)
    o_ref[...] = _dot_nt(normed, perm_ref[...]).astype(BF16)


def _attn_sample(q, kn, vn, ck, cv, slope_col, sink_col, g_att, perm):
    n = q.shape[0]
    nb = SEQ_BLK
    cache_spec = pl.BlockSpec((nb, KV_W, WINDOW), lambda i: (i, 0, 0))
    row_spec = lambda w: pl.BlockSpec((nb, w), lambda i: (i, 0))
    full = lambda a: pl.BlockSpec(a.shape, lambda i: (0,) * a.ndim)
    return pl.pallas_call(
        functools.partial(_attn_sample_kernel, n_seq=nb),
        grid=(n // nb,),
        in_specs=[row_spec(ATT_W), row_spec(KV_W), row_spec(KV_W), cache_spec, cache_spec,
                  full(slope_col), full(sink_col), full(g_att), full(perm)],
        out_specs=[row_spec(ATT_W), cache_spec, cache_spec],
        out_shape=[
            jax.ShapeDtypeStruct((n, ATT_W), BF16),
            jax.ShapeDtypeStruct(ck.shape, F32),
            jax.ShapeDtypeStruct(cv.shape, F32),
        ],
        scratch_shapes=[pltpu.VMEM((nb, ATT_W), F32)],
        compiler_params=_params(("parallel",)),
        name="attn_sample",
    )(q, kn, vn, ck, cv, slope_col, sink_col, g_att, perm)


def _ssd_prompt_kernel(xb_ref, z_ref, dtb_ref, alog_ref, dskip_ref, g_ref,
                       y_ref, hout_ref, ht_scr, e_scr, tri_scr, yd_scr, *, rows):
    i = pl.program_id(0)
    T = CHUNK

    @pl.when(i == 0)
    def _():
        ht_scr[...] = jnp.zeros_like(ht_scr)
        e_scr[...] = _head_expand_matrix()
        l = lax.broadcasted_iota(jnp.int32, (T, T), 0)
        s = lax.broadcasted_iota(jnp.int32, (T, T), 1)
        tri_scr[...] = jnp.where(s <= l, 1.0, 0.0).astype(BF16)

    a_log2 = -jnp.exp(alog_ref[...]) * LOG2E
    e_mat = e_scr[...]
    tri = tri_scr[...]
    causal = lax.broadcasted_iota(jnp.int32, (T, T), 0) >= lax.broadcasted_iota(jnp.int32, (T, T), 1)
    lo = lax.broadcasted_iota(jnp.int32, (T, LANES), 1) < SSM_P
    gw = SSM_W // SSM_G

    for c in range(rows // T):
        rs = slice(c * T, (c + 1) * T)
        dt = _softplus(xb_ref[rs, CONV_DIM:XBCDT_W] + dtb_ref[...])
        dta = dt * a_log2
        hi, mid, lw = _split3(dta)
        acum = _dot(tri, hi) + _dot(tri, mid) + _dot(tri, lw)
        acum_e = _expand_heads(acum, e_mat)
        dt_e = _expand_heads(dt, e_mat)
        acum_t = acum.T
        xs = xb_ref[rs, 0:SSM_W]
        bm = xb_ref[rs, SSM_W:SSM_W + SSM_G * D_STATE]
        cm = xb_ref[rs, SSM_W + SSM_G * D_STATE:CONV_DIM]
        xdt = xs * dt_e
        last = acum_e[T - 1:T, :]
        xdt_b = xdt.astype(BF16)
        xdec_b = (xdt * jnp.exp2(last - acum_e)).astype(BF16)
        bm_b = bm.astype(BF16)
        cm_b = cm.astype(BF16)
        cbs = [_dot_nt(cm_b[:, g * D_STATE:(g + 1) * D_STATE], bm_b[:, g * D_STATE:(g + 1) * D_STATE])
               for g in range(SSM_G)]
        for pr in range(SSM_H // 2):
            g = (2 * pr) // (SSM_H // SSM_G)
            ws = []
            for t in range(2):
                h = 2 * pr + t
                seg = jnp.broadcast_to(acum[:, h:h + 1], (T, T)) - acum_t[h:h + 1, :]
                ws.append((cbs[g] * jnp.exp2(jnp.where(causal, seg, NEG_INF))).astype(BF16))
            xsl = xdt_b[:, pr * LANES:(pr + 1) * LANES]
            yd_scr[:, pr * LANES:(pr + 1) * LANES] = _dot(jnp.concatenate(ws, axis=1),
                                                          _pair_blockdiag(xsl, xsl, lo, ~lo))
        yoff = []
        for g in range(SSM_G):
            gs = slice(g * gw, (g + 1) * gw)
            ht_g = ht_scr[:, gs]
            yoff.append(_dot(cm_b[:, g * D_STATE:(g + 1) * D_STATE], ht_g.astype(BF16)))
            cst = _dot_tn(bm_b[:, g * D_STATE:(g + 1) * D_STATE], xdec_b[:, gs])
            ht_scr[:, gs] = ht_g * jnp.exp2(last[:, gs]) + cst
        y = yd_scr[...] + jnp.concatenate(yoff, axis=1) * jnp.exp2(acum_e)
        y = y + dskip_ref[...] * xs
        y = y * z_ref[rs, :]
        y_ref[rs, :] = _rms(y, g_ref[...]).astype(BF16)

    @pl.when(i == pl.num_programs(0) - 1)
    def _():
        hout_ref[...] = ht_scr[...].T


def _ssd_prompt(xb, z, dtb, alog, dskip_e, g_ssm):
    n_rows = xb.shape[0]
    rows = ROWS_SSD
    full = lambda a: pl.BlockSpec(a.shape, lambda i: (0,) * a.ndim)
    return pl.pallas_call(
        functools.partial(_ssd_prompt_kernel, rows=rows),
        grid=(n_rows // rows,),
        in_specs=[
            pl.BlockSpec((rows, XBCDT_W), lambda i: (i, 0)),
            pl.BlockSpec((rows, SSM_W), lambda i: (i, 0)),
            full(dtb), full(alog), full(dskip_e), full(g_ssm),
        ],
        out_specs=[
            pl.BlockSpec((rows, SSM_W), lambda i: (i, 0)),
            pl.BlockSpec((SSM_W, D_STATE), lambda i: (0, 0)),
        ],
        out_shape=[
            jax.ShapeDtypeStruct((n_rows, SSM_W), BF16),
            jax.ShapeDtypeStruct((SSM_W, D_STATE), F32),
        ],
        scratch_shapes=[
            pltpu.VMEM((D_STATE, SSM_W), F32),
            pltpu.VMEM((LANES, SSM_W), BF16),
            pltpu.VMEM((CHUNK, CHUNK), BF16),
            pltpu.VMEM((CHUNK, SSM_W), F32),
        ],
        compiler_params=_params(("arbitrary",)),
        name="ssd_prompt",
    )(xb, z, dtb, alog, dskip_e, g_ssm)


def _ssd_sample_kernel(xb_ref, sconv_ref, z_ref, h0_ref, cw_ref, cb_ref, dtb_ref, alog_ref, dskip_ref, g_ref,
                       y_ref, conv_ref, hout_ref, yoff_scr, *, n_seq):
    gw = SSM_W // SSM_G
    gn = SSM_G * D_STATE
    x_new = xb_ref[:, 0:CONV_DIM]
    taps = [sconv_ref[:, k * CONV_DIM:(k + 1) * CONV_DIM] for k in range(CONV_W - 1)] + [x_new]
    acc = cb_ref[...]
    for k in range(CONV_W):
        acc = acc + taps[k] * cw_ref[k:k + 1, :]
    for k in range(1, CONV_W):
        conv_ref[:, (k - 1) * CONV_DIM:k * CONV_DIM] = taps[k]
    xc = _silu(acc)
    xs = xc[:, 0:SSM_W]
    bm = xc[:, SSM_W:SSM_W + gn]
    cm = xc[:, SSM_W + gn:CONV_DIM]

    e_mat = _head_expand_matrix()
    dt = _softplus(xb_ref[:, CONV_DIM:XBCDT_W] + dtb_ref[...])
    dt_e = _expand_heads(dt, e_mat)
    dec_e = jnp.exp(_expand_heads(dt * (-jnp.exp(alog_ref[...])), e_mat))
    xdt = xs * dt_e

    lane_w = lax.broadcasted_iota(jnp.int32, (n_seq, SSM_W), 1)
    first_grp = lane_w < gw
    cbv = [jnp.sum(cm[:, g * D_STATE:(g + 1) * D_STATE] * bm[:, g * D_STATE:(g + 1) * D_STATE],
                   axis=-1, keepdims=True) for g in range(SSM_G)]
    cb_e = jnp.where(first_grp, cbv[0], cbv[1])

    pad = jnp.zeros((LANES - n_seq, SSM_W), F32)
    xdt_t = jnp.concatenate([xdt, pad], axis=0).T
    dec_t = jnp.concatenate([dec_e, pad], axis=0).T
    sub8 = lax.broadcasted_iota(jnp.int32, (8, D_STATE), 0)
    lane_r = lax.broadcasted_iota(jnp.int32, (8, SSM_W), 1)

    for b in range(n_seq):
        h0 = h0_ref[b]
        c_row = jnp.broadcast_to(cm[b:b + 1, :], (8, gn))
        c8 = jnp.where(sub8 == 0, c_row[:, 0:D_STATE], jnp.where(sub8 == 1, c_row[:, D_STATE:], 0.0))
        r = _dot_nt(c8.astype(BF16), h0.astype(BF16))
        yoff_scr[b:b + 1, :] = jnp.where(lane_r[0:1] < gw, r[0:1, :], r[1:2, :])
        b_row = bm[b:b + 1, :]
        for g in range(SSM_G):
            rs = slice(g * gw, (g + 1) * gw)
            dcol = jnp.broadcast_to(dec_t[rs, b:b + 1], (gw, D_STATE))
            xcol = jnp.broadcast_to(xdt_t[rs, b:b + 1], (gw, D_STATE))
            hout_ref[b, rs, :] = h0[rs, :] * dcol + xcol * b_row[:, g * D_STATE:(g + 1) * D_STATE]

    y = cb_e * xdt + yoff_scr[...] * dec_e
    y = y + dskip_ref[...] * xs
    y = y * _silu(z_ref[...])
    y_ref[...] = _rms(y, g_ref[...]).astype(BF16)


def _ssd_sample(xb, sconv, z, h0, cw8, cb, dtb, alog, dskip_e, g_ssm):
    n = xb.shape[0]
    nb = SEQ_BLK
    row_spec = lambda w: pl.BlockSpec((nb, w), lambda i: (i, 0))
    st_spec = pl.BlockSpec((nb, SSM_W, D_STATE), lambda i: (i, 0, 0))
    full = lambda a: pl.BlockSpec(a.shape, lambda i: (0,) * a.ndim)
    return pl.pallas_call(
        functools.partial(_ssd_sample_kernel, n_seq=nb),
        grid=(n // nb,),
        in_specs=[row_spec(XBCDT_W), row_spec((CONV_W - 1) * CONV_DIM), row_spec(SSM_W), st_spec,
                  full(cw8), full(cb), full(dtb), full(alog), full(dskip_e), full(g_ssm)],
        out_specs=[row_spec(SSM_W), row_spec((CONV_W - 1) * CONV_DIM), st_spec],
        out_shape=[
            jax.ShapeDtypeStruct((n, SSM_W), BF16),
            jax.ShapeDtypeStruct((n, (CONV_W - 1) * CONV_DIM), F32),
            jax.ShapeDtypeStruct(h0.shape, F32),
        ],
        scratch_shapes=[pltpu.VMEM((nb, SSM_W), F32)],
        compiler_params=_params(("parallel",)),
        name="ssd_sample",
    )(xb, sconv, z, h0, cw8, cb, dtb, alog, dskip_e, g_ssm)


def _outproj_kernel(att_ref, ssm_ref, x_ref, woa_ref, wos_ref, gpost_ref, gt1_ref, gpre_ref, sh2_ref, sc2_ref,
                    x1_ref, u2_ref, *, per_row, rows):
    sub = min(SUB_OUT, rows)
    for r0 in range(0, rows, sub):
        rs = slice(r0, r0 + sub)
        mod = lambda ref: ref[rs, :] if per_row else ref[0:1, :]
        mix = _dot(att_ref[rs, :], woa_ref[...]) + _dot(ssm_ref[rs, :], wos_ref[...])
        x1 = x_ref[rs, :] + mod(gt1_ref) * _rms(mix, gpost_ref[...])
        x1_ref[rs, :] = x1
        u2 = _rms(x1, gpre_ref[...]) * (1.0 + mod(sc2_ref)) + mod(sh2_ref)
        u2_ref[rs, :] = u2.astype(BF16)


def _outproj(att, ssm, x, mod, per_row, woa, wos, g_post, g_pre):
    n_rows = x.shape[0]
    rows = min(ROWS_OUT, n_rows)
    row_spec = lambda w: pl.BlockSpec((rows, w), lambda i: (i, 0))
    return pl.pallas_call(
        functools.partial(_outproj_kernel, per_row=per_row, rows=rows),
        grid=(n_rows // rows,),
        in_specs=[
            row_spec(ATT_W), row_spec(SSM_W), row_spec(D_MODEL),
            _const_spec(woa.shape), _const_spec(wos.shape),
            _const_spec((1, D_MODEL)),
            _mod_spec(per_row, rows, 2),
            _const_spec((1, D_MODEL)),
            _mod_spec(per_row, rows, 3),
            _mod_spec(per_row, rows, 4),
        ],
        out_specs=[row_spec(D_MODEL), row_spec(D_MODEL)],
        out_shape=[
            jax.ShapeDtypeStruct((n_rows, D_MODEL), F32),
            jax.ShapeDtypeStruct((n_rows, D_MODEL), BF16),
        ],
        compiler_params=_params(("parallel",)),
        name="outproj",
    )(att, ssm, x, woa, wos, g_post, mod, g_pre, mod, mod)


def _ffn_kernel(u_ref, x1_hbm, wg_ref, wu_ref, wd_ref, gpost_ref, gt2_ref, y_ref, x1_buf, x1_sem,
                *, per_row, rows):
    i = pl.program_id(0)
    j = pl.program_id(1)
    x1_copy = pltpu.make_async_copy(x1_hbm.at[pl.ds(i * rows, rows), :], x1_buf, x1_sem)
    sub = min(SUB_FFN, rows)

    last = pl.num_programs(1) - 1

    def d_ff_slice(first, final):
        for r0 in range(0, rows, sub):
            rs = slice(r0, r0 + sub)
            u = u_ref[rs, :]
            hid = (_silu(_dot(u, wg_ref[...])) * _dot(u, wu_ref[...])).astype(BF16)
            acc = _dot(hid, wd_ref[...])
            if not first:
                acc = y_ref[rs, :] + acc
            if final:
                gt2 = gt2_ref[rs, :] if per_row else gt2_ref[0:1, :]
                acc = x1_buf[rs, :] + gt2 * _rms(acc, gpost_ref[...])
            y_ref[rs, :] = acc

    @pl.when(j == 0)
    def _():
        x1_copy.start()
        d_ff_slice(True, False)

    @pl.when(jnp.logical_and(j > 0, j < last))
    def _():
        d_ff_slice(False, False)

    @pl.when(j == last)
    def _():
        x1_copy.wait()
        d_ff_slice(False, True)


def _ffn(u2, x1, mod, per_row, wg, wu, wd, g_post):
    n_rows = x1.shape[0]
    rows = min(ROWS_FFN, n_rows)
    fb = FF_BLK
    return pl.pallas_call(
        functools.partial(_ffn_kernel, per_row=per_row, rows=rows),
        grid=(n_rows // rows, D_FF // fb),
        in_specs=[
            pl.BlockSpec((rows, D_MODEL), lambda i, j: (i, 0)),
            pl.BlockSpec(memory_space=pl.ANY),
            pl.BlockSpec((D_MODEL, fb), lambda i, j: (0, j)),
            pl.BlockSpec((D_MODEL, fb), lambda i, j: (0, j)),
            pl.BlockSpec((fb, D_MODEL), lambda i, j: (j, 0)),
            pl.BlockSpec((1, D_MODEL), lambda i, j: (0, 0)),
            _mod_spec(per_row, rows, 5),
        ],
        out_specs=pl.BlockSpec((rows, D_MODEL), lambda i, j: (i, 0)),
        out_shape=jax.ShapeDtypeStruct((n_rows, D_MODEL), F32),
        scratch_shapes=[pltpu.VMEM((rows, D_MODEL), F32), pltpu.SemaphoreType.DMA],
        compiler_params=_params(("arbitrary", "arbitrary"), VMEM_LIMIT_FFN),
        name="ffn",
    )(u2, x1, wg, wu, wd, g_post, mod)


def _alibi_slopes():
    return (2.0 ** (-8.0 * np.arange(1, N_HEADS + 1) / N_HEADS)).astype(np.float32)


def kernel(x_prompt, x_sample, cache_k, cache_v, state_conv, state_ssm, c_prompt, c_sample, w_ada, b_ada, g_pre_mix, g_post_mix, w_in, attn_sinks, g_attn_out, conv_w, conv_b, dt_bias, a_log, d_skip, g_ssm_out, w_out, g_pre_ffn, g_post_ffn, w_gate, w_up, w_down):
    assert w_ada.shape[0] == 1, "one layer"
    n_s = x_sample.shape[0]
    row = lambda v: v.reshape(1, -1)

    w_all = _wsplit(jnp.transpose(w_in[0]))
    woa = w_out[0, :ATT_W].astype(BF16)
    wos = w_out[0, ATT_W:].astype(BF16)
    g_att = row(g_attn_out[0])
    swapped_cols = np.arange(ATT_W).reshape(N_KV, Q_PER_KV, HEAD_DIM).swapaxes(0, 1).reshape(-1)
    perm = np.zeros((ATT_W, ATT_W), np.float32)
    perm[swapped_cols, np.arange(ATT_W)] = 1.0
    perm = jnp.asarray(perm, BF16)
    g_att_s = row(g_attn_out[0].reshape(N_KV, Q_PER_KV, HEAD_DIM).swapaxes(0, 1))
    slopes_np = _alibi_slopes()
    slopes = jnp.asarray(slopes_np)
    slopes_s = jnp.asarray(slopes_np.reshape(N_KV, Q_PER_KV).T.reshape(N_HEADS, 1))
    sinks = attn_sinks[0]
    sinks_s = sinks.reshape(N_KV, Q_PER_KV).T.reshape(N_HEADS, 1)
    cw8 = jnp.pad(conv_w[0], ((0, 8 - CONV_W), (0, 0)))
    cb = row(conv_b[0])
    dtb = jnp.pad(row(dt_bias[0]), ((0, 0), (0, LANES - SSM_H)))
    alog = jnp.pad(row(a_log[0]), ((0, 0), (0, LANES - SSM_H)))
    dskip_e = row(jnp.repeat(d_skip[0], SSM_P))
    g_ssm = row(g_ssm_out[0])

    c_p8 = jnp.pad(c_prompt, ((0, 8 - c_prompt.shape[0]), (0, 0)))
    mod_s, mod_p = _ada(c_sample, c_p8, w_ada[0], row(b_ada[0]))

    xp = x_prompt[0]
    q, kv, z, xb, kv_last, x_tail = _inproj(xp, row(g_pre_mix[0]), mod_p, False, w_all, cw8, cb, True)
    att, wg, wu, wd = _attn_prompt(q, kv, slopes, sinks, g_att, w_gate[0], w_up[0], w_down[0])
    ssm, h_p = _ssd_prompt(xb, z, dtb, alog, dskip_e, g_ssm)
    x1, u2 = _outproj(att, ssm, xp, mod_p, False, woa, wos, row(g_post_mix[0]), row(g_pre_ffn[0]))
    y_p = _ffn(u2, x1, mod_p, False, wg, wu, wd, row(g_post_ffn[0]))

    keys_minor = lambda c: jnp.transpose(c, (0, 2, 3, 1)).reshape(n_s, KV_W, WINDOW)
    keys_major = lambda c: jnp.transpose(c.reshape(n_s, N_KV, HEAD_DIM, WINDOW), (0, 3, 1, 2))[None]
    xs_ = x_sample[:, 0, :]
    q_s, _, z_s, xb_s, kv_new, _ = _inproj(xs_, row(g_pre_mix[0]), mod_s, True, w_all, cw8, cb, False)
    att_s, k_s, v_s = _attn_sample(
        q_s, kv_new[:, :KV_W], kv_new[:, KV_W:],
        keys_minor(cache_k[0]), keys_minor(cache_v[0]), slopes_s, sinks_s, g_att_s, perm)
    ssm_s, conv_s, h_s = _ssd_sample(
        xb_s, state_conv[0].reshape(n_s, (CONV_W - 1) * CONV_DIM), z_s,
        state_ssm[0].reshape(n_s, SSM_W, D_STATE), cw8, cb, dtb, alog, dskip_e, g_ssm)
    x1_s, u2_s = _outproj(att_s, ssm_s, xs_, mod_s, True, woa, wos, row(g_post_mix[0]), row(g_pre_ffn[0]))
    y_s = _ffn(u2_s, x1_s, mod_s, True, wg, wu, wd, row(g_post_ffn[0]))

    return (
        y_p[None],
        y_s[:, None, :],
        kv_last[:, :KV_W].reshape(1, 1, WINDOW, N_KV, HEAD_DIM),
        kv_last[:, KV_W:].reshape(1, 1, WINDOW, N_KV, HEAD_DIM),
        x_tail[8 - (CONV_W - 1):].reshape(1, 1, CONV_W - 1, CONV_DIM),
        h_p.reshape(1, 1, SSM_H, SSM_P, D_STATE),
        keys_major(k_s),
        keys_major(v_s),
        conv_s.reshape(1, n_s, CONV_W - 1, CONV_DIM),
        h_s.reshape(1, n_s, SSM_H, SSM_P, D_STATE),
    )
```

```python
import functools

import numpy as np
import jax
import jax.numpy as jnp
from jax import lax
from jax.experimental import pallas as pl
from jax.experimental.pallas import tpu as pltpu

F32 = jnp.float32
BF16 = jnp.bfloat16

D_MODEL = 2048
ATT_W = 1024
HEAD_DIM = 64
N_HEADS = 16
N_KV = 4
Q_PER_KV = 4
KV_W = N_KV * HEAD_DIM
WINDOW = 128
SSM_W = 1024
SSM_P = 64
SSM_H = 16
SSM_G = 2
D_STATE = 128
CONV_W = 4
CONV_DIM = SSM_W + 2 * SSM_G * D_STATE
XBCDT_W = CONV_DIM + 128
D_FF = 5632
EPS = 1e-6
CHUNK = 128
NEG_INF = float("-inf")
LOG2E = 1.4426950408889634

VMEM_LIMIT = 56 * 1024 * 1024
VMEM_LIMIT_FFN = 60 * 1024 * 1024
LANES = 128

ROWS_IN = 512
ROWS_ATT = 512
ROWS_SSD = 512
ROWS_OUT = 512
SUB_OUT = 128
ROWS_FFN = 1024
SUB_FFN = 512
FF_BLK = 512
SEQ_BLK = 8


def _params(sem, vmem=VMEM_LIMIT):
    return pltpu.CompilerParams(dimension_semantics=sem, vmem_limit_bytes=vmem)


def _silu(v):
    h = 0.5 * v
    return h + h * jnp.tanh(h)


def _softplus(v):
    return jnp.maximum(v, 0.0) + jnp.log1p(jnp.exp(-jnp.abs(v)))


def _rms(v, g):
    return v * lax.rsqrt(jnp.mean(v * v, axis=-1, keepdims=True) + EPS) * g


def _split3(v):
    hi = v.astype(BF16)
    r1 = v - hi.astype(F32)
    mid = r1.astype(BF16)
    lo = (r1 - mid.astype(F32)).astype(BF16)
    return hi, mid, lo


def _dot(a, b):
    return jnp.dot(a, b, preferred_element_type=F32)


def _dot_nt(a, b):
    return lax.dot_general(a, b, (((1,), (1,)), ((), ())), preferred_element_type=F32)


def _dot_tn(a, b):
    return lax.dot_general(a, b, (((0,), (0,)), ((), ())), preferred_element_type=F32)


def _expand_heads(v, e_mat):
    hi, mid, lo = _split3(v)
    return _dot(hi, e_mat) + _dot(mid, e_mat) + _dot(lo, e_mat)


def _head_expand_matrix():
    k = lax.broadcasted_iota(jnp.int32, (LANES, SSM_W), 0)
    c = lax.broadcasted_iota(jnp.int32, (LANES, SSM_W), 1)
    return jnp.where((c // SSM_P) == k, 1.0, 0.0).astype(BF16)


def _ada_kernel(cs_ref, cp_ref, w_ref, b_ref, ms_ref, mp_ref):
    w = w_ref[...].astype(BF16)
    b = b_ref[...]
    ms_ref[...] = _dot(_silu(cs_ref[...]).astype(BF16), w) + b
    mp_ref[...] = _dot(_silu(cp_ref[...]).astype(BF16), w) + b


def _ada(c_s, c_p8, w_ada, b_ada):
    n = w_ada.shape[1]
    bn = 1024
    return pl.pallas_call(
        _ada_kernel,
        grid=(n // bn,),
        in_specs=[
            pl.BlockSpec(c_s.shape, lambda j: (0, 0)),
            pl.BlockSpec(c_p8.shape, lambda j: (0, 0)),
            pl.BlockSpec((D_MODEL, bn), lambda j: (0, j)),
            pl.BlockSpec((1, bn), lambda j: (0, j)),
        ],
        out_specs=[
            pl.BlockSpec((c_s.shape[0], bn), lambda j: (0, j)),
            pl.BlockSpec((c_p8.shape[0], bn), lambda j: (0, j)),
        ],
        out_shape=[
            jax.ShapeDtypeStruct((c_s.shape[0], n), F32),
            jax.ShapeDtypeStruct((c_p8.shape[0], n), F32),
        ],
        compiler_params=_params(("parallel",)),
        name="ada",
    )(c_s, c_p8, w_ada, b_ada)


def _mod_spec(per_row, rows, col):
    if per_row:
        return pl.BlockSpec((rows, D_MODEL), lambda i, *_: (i, col))
    return pl.BlockSpec((8, D_MODEL), lambda i, *_: (0, col))


def _mod_rows(ref, per_row):
    return ref[...] if per_row else ref[0:1, :]


WS_BLK = 256
W_ALL_COLS = 4608
COL_X, COL_Q, COL_Z, COL_KV = 0, 2048, 3072, 4096
N_WS_SRC = -(-(ATT_W + 2 * KV_W + SSM_W + CONV_DIM + SSM_H) // WS_BLK)


def _ws_dst_block(s):
    q_end = ATT_W // WS_BLK
    kv_end = q_end + 2 * KV_W // WS_BLK
    z_end = kv_end + SSM_W // WS_BLK
    return jnp.where(s < q_end, COL_Q // WS_BLK + s,
                     jnp.where(s < kv_end, COL_KV // WS_BLK + s - q_end,
                               jnp.where(s < z_end, COL_Z // WS_BLK + s - kv_end,
                                         COL_X // WS_BLK + s - z_end)))


def _wsplit_kernel(wt_ref, o_ref):
    s = pl.program_id(0)
    n_valid = jnp.where(s < N_WS_SRC - 1, WS_BLK, jnp.where(s == N_WS_SRC - 1, SSM_H, 0))
    lane = lax.broadcasted_iota(jnp.int32, o_ref.shape, 1)
    o_ref[...] = jnp.where(lane < n_valid, wt_ref[...].T, 0.0).astype(BF16)


def _wsplit(w_in_t):
    k_dim = w_in_t.shape[1]
    return pl.pallas_call(
        _wsplit_kernel,
        grid=(N_WS_SRC + 1,),
        in_specs=[pl.BlockSpec((WS_BLK, k_dim), lambda s: (jnp.minimum(s, N_WS_SRC - 1), 0))],
        out_specs=pl.BlockSpec((k_dim, WS_BLK), lambda s: (0, _ws_dst_block(s))),
        out_shape=jax.ShapeDtypeStruct((k_dim, W_ALL_COLS), BF16),
        compiler_params=_params(("arbitrary",)),
        name="wsplit",
    )(w_in_t)


def _inproj_kernel(x_ref, g_ref, sh_ref, sc_ref, wq_ref, wkv_ref, wz_ref, wx_ref, cw_ref, cb_ref,
                   q_ref, kv_ref, z_ref, xb_ref, kvlast_ref, xtail_ref, xp_scr, *, per_row, rows, fuse_conv):
    tile = 2 * LANES
    if fuse_conv:
        @pl.when(pl.program_id(0) == 0)
        def _():
            xp_scr[:, rows:rows + 8, :] = jnp.zeros((CONV_DIM // LANES, 8, LANES), F32)

    u = _rms(x_ref[...], g_ref[...]) * (1.0 + _mod_rows(sc_ref, per_row)) + _mod_rows(sh_ref, per_row)
    u = u.astype(BF16)

    def conv_slab(s, xs):
        ls = slice(s * LANES, (s + 1) * LANES)
        xtail_ref[:, ls] = xs[rows - 8:, :]
        xp_scr[s, 0:8, :] = xp_scr[s, rows:rows + 8, :]
        xp_scr[s, 8:, :] = xs
        acc = cb_ref[:, ls] + xp_scr[s, pl.ds(8 - (CONV_W - 1), rows), :] * cw_ref[0:1, ls]
        for k in range(1, CONV_W):
            acc = acc + xp_scr[s, pl.ds(8 - (CONV_W - 1) + k, rows), :] * cw_ref[k:k + 1, ls]
        xb_ref[:, ls] = _silu(acc)

    def post_z(c0, c1):
        def post(r):
            z_ref[:, c0:c1] = _silu(r) if fuse_conv else r
        return post

    def post_q(c0, c1):
        def post(r):
            q_ref[:, c0:c1] = (r * (HEAD_DIM ** -0.5)).astype(BF16)
        return post

    def post_kv(c0, c1):
        def post(r):
            kv_ref[:, c0:c1] = r.astype(BF16)
            kvlast_ref[:, c0:c1] = r[rows - WINDOW:, :]
        return post

    xb = _dot(u, wx_ref[...])
    xb_ref[:, CONV_DIM:] = xb[:, CONV_DIM:]
    if fuse_conv:
        slabs = [functools.partial(conv_slab, s, xb[:, s * LANES:(s + 1) * LANES]) for s in range(CONV_DIM // LANES)]
    else:
        xb_ref[:, 0:CONV_DIM] = xb[:, 0:CONV_DIM]
        xtail_ref[...] = xb[rows - 8:, 0:CONV_DIM]
        slabs = []

    jobs = []
    for w_ref, width, post_of in ((wz_ref, SSM_W, post_z), (wq_ref, ATT_W, post_q), (wkv_ref, 2 * KV_W, post_kv)):
        for c0 in range(0, width, tile):
            jobs.append((w_ref, c0, c0 + tile, post_of(c0, c0 + tile)))
    pending = None
    for n, (w_ref, c0, c1, post) in enumerate(jobs):
        r = _dot(u, w_ref[:, c0:c1])
        for slab in slabs[n * len(slabs) // len(jobs):(n + 1) * len(slabs) // len(jobs)]:
            slab()
        if pending is not None:
            pending[0](pending[1])
        pending = (post, r)
    pending[0](pending[1])


def _const_spec(shape):
    return pl.BlockSpec(shape, lambda i, *_: (0,) * len(shape), pipeline_mode=pl.Buffered(1))


def _inproj(x, g, mod, per_row, w_all, cw8, cb, fuse_conv):
    n_rows = x.shape[0]
    rows = min(ROWS_IN, n_rows)
    row_spec = lambda w: pl.BlockSpec((rows, w), lambda i: (i, 0))
    seg_spec = lambda width, col: pl.BlockSpec((D_MODEL, width), lambda i: (0, col // width),
                                               pipeline_mode=pl.Buffered(1))
    return pl.pallas_call(
        functools.partial(_inproj_kernel, per_row=per_row, rows=rows, fuse_conv=fuse_conv),
        grid=(n_rows // rows,),
        in_specs=[
            row_spec(D_MODEL),
            _const_spec((1, D_MODEL)),
            _mod_spec(per_row, rows, 0),
            _mod_spec(per_row, rows, 1),
            seg_spec(ATT_W, COL_Q), seg_spec(2 * KV_W, COL_KV), seg_spec(SSM_W, COL_Z), seg_spec(XBCDT_W, COL_X),
            _const_spec(cw8.shape), _const_spec(cb.shape),
        ],
        out_specs=[
            row_spec(ATT_W), row_spec(2 * KV_W), row_spec(SSM_W), row_spec(XBCDT_W),
            pl.BlockSpec((WINDOW, 2 * KV_W), lambda i: (0, 0)),
            pl.BlockSpec((8, CONV_DIM), lambda i: (0, 0)),
        ],
        out_shape=[
            jax.ShapeDtypeStruct((n_rows, ATT_W), BF16),
            jax.ShapeDtypeStruct((n_rows, 2 * KV_W), BF16),
            jax.ShapeDtypeStruct((n_rows, SSM_W), F32),
            jax.ShapeDtypeStruct((n_rows, XBCDT_W), F32),
            jax.ShapeDtypeStruct((WINDOW, 2 * KV_W), F32),
            jax.ShapeDtypeStruct((8, CONV_DIM), F32),
        ],
        scratch_shapes=[pltpu.VMEM((CONV_DIM // LANES, rows + 8, LANES), F32)],
        compiler_params=_params(("arbitrary",)),
        name="inproj",
    )(x, g, mod, mod, w_all, w_all, w_all, w_all, cw8, cb)


def _pair_blockdiag(lo_src, hi_src, keep_lo, keep_hi):
    zero = jnp.zeros_like(lo_src)
    return jnp.concatenate([jnp.where(keep_lo, lo_src, zero), jnp.where(keep_hi, hi_src, zero)], axis=0)


def _attn_prompt_kernel(slopes_ref, sinks_ref, q_ref, kvc_ref, kvp_ref, g_ref, wg_ref, wu_ref, wd_ref,
                        o_ref, wg_o, wu_o, wd_o, bias_scr, kvdup_scr, *, n_sub):
    i = pl.program_id(0)
    blk = WINDOW

    wg_o[...] = wg_ref[...].astype(BF16)
    wu_o[...] = wu_ref[...].astype(BF16)
    wd_o[...] = wd_ref[...].astype(BF16)

    @pl.when(i == 0)
    def _():
        a = lax.broadcasted_iota(jnp.int32, (blk, 2 * blk), 0)
        j = lax.broadcasted_iota(jnp.int32, (blk, 2 * blk), 1)
        dist = a + blk - j
        valid = (dist >= 0) & (dist < WINDOW)
        distf = dist.astype(F32)
        for h in range(N_HEADS):
            kvh, g = divmod(h, Q_PER_KV)
            pr, t = divmod(g, 2)
            b = jnp.where(valid, -(slopes_ref[h] * distf), NEG_INF)
            b = jnp.where(j == 0, sinks_ref[h], b)
            rs = slice(pr * blk, (pr + 1) * blk)
            cs = slice(t * 2 * blk, (t + 1) * 2 * blk)
            bias_scr[1, kvh, rs, cs] = b
            bias_scr[0, kvh, rs, cs] = jnp.where((j >= blk) | (j == 0), b, NEG_INF)

    for src_ref, r_lo, r_hi in ((kvp_ref, 0, blk), (kvc_ref, blk, kvdup_scr.shape[0])):
        lo_half = lax.broadcasted_iota(jnp.int32, (r_hi - r_lo, LANES), 1) < HEAD_DIM
        for part in range(2):
            for s in range(2):
                c_in = part * KV_W + s * LANES
                c_out = part * N_KV * LANES + 2 * s * LANES
                x = src_ref[:, c_in:c_in + LANES].astype(F32)
                xr = pltpu.roll(x, HEAD_DIM, 1)
                kvdup_scr[r_lo:r_hi, c_out:c_out + LANES] = jnp.where(lo_half, x, xr).astype(BF16)
                kvdup_scr[r_lo:r_hi, c_out + LANES:c_out + 2 * LANES] = jnp.where(lo_half, xr, x).astype(BF16)

    lane = lax.broadcasted_iota(jnp.int32, (2 * blk, LANES), 1)
    not_sink = lax.broadcasted_iota(jnp.int32, (2 * blk, LANES), 0) != 0
    keep_lo = (lane < HEAD_DIM) & not_sink
    keep_hi = (lane >= HEAD_DIM) & not_sink
    ones_cols = jnp.concatenate([jnp.where(lane < HEAD_DIM, 1.0, 0.0),
                                 jnp.where(lane >= HEAD_DIM, 1.0, 0.0)], axis=0).astype(BF16)
    n_chunk = 4
    rc = 2 * blk // n_chunk

    def body(b, carry):
        r0 = pl.multiple_of(b * blk, blk)
        variant = jnp.where(jnp.logical_and(i == 0, b == 0), 0, 1)
        vbds, scores = [], []
        for kvh in range(N_KV):
            c0 = kvh * Q_PER_KV * HEAD_DIM
            kd = kvdup_scr[pl.ds(r0, 2 * blk), kvh * LANES:(kvh + 1) * LANES]
            vd = kvdup_scr[pl.ds(r0, 2 * blk), (N_KV + kvh) * LANES:(N_KV + kvh + 1) * LANES]
            kbd = _pair_blockdiag(kd, kd, keep_lo, keep_hi)
            vbds.append(jnp.concatenate([_pair_blockdiag(vd, vd, keep_lo, keep_hi), ones_cols], axis=1))
            q2 = jnp.concatenate([q_ref[pl.ds(r0, blk), c0:c0 + LANES],
                                  q_ref[pl.ds(r0, blk), c0 + LANES:c0 + 2 * LANES]], axis=0)
            scores.append(_dot_nt(q2, kbd) + bias_scr[variant, kvh])
        probs = []
        for kvh in range(N_KV):
            chunks = []
            for c in range(n_chunk):
                st = scores[kvh][c * rc:(c + 1) * rc, :]
                sl = st[:, 0:2 * blk]
                sr = st[:, 2 * blk:]
                p_l = jnp.exp(sl - jnp.max(sl, axis=-1, keepdims=True))
                p_r = jnp.exp(sr - jnp.max(sr, axis=-1, keepdims=True))
                chunks.append(jnp.concatenate([p_l, p_r], axis=1).astype(BF16))
            probs.append(jnp.concatenate(chunks, axis=0))
        outs = []
        for kvh in range(N_KV):
            o2 = _dot(probs[kvh], vbds[kvh])
            o = o2[:, 0:LANES] * (1.0 / o2[:, LANES:])
            outs += [o[0:blk], o[blk:]]
        att = jnp.concatenate(outs, axis=1)
        o_ref[pl.ds(r0, blk), :] = _rms(att, g_ref[...]).astype(BF16)
        return carry

    lax.fori_loop(0, n_sub, body, 0, unroll=2)


def _attn_prompt(q, kv, slopes, sinks, g_att, w_gate, w_up, w_down):
    n_rows = q.shape[0]
    rows = ROWS_ATT
    n_sub = rows // WINDOW
    n_steps = n_rows // rows
    smem = pl.BlockSpec(memory_space=pltpu.SMEM)
    slab = lambda w: pl.BlockSpec((w.shape[0] // n_steps, w.shape[1]), lambda i: (i, 0))
    ffn_w = (w_gate, w_up, w_down)
    assert all(w.shape[0] % (16 * n_steps) == 0 for w in ffn_w)
    return pl.pallas_call(
        functools.partial(_attn_prompt_kernel, n_sub=n_sub),
        grid=(n_rows // rows,),
        in_specs=[
            smem, smem,
            pl.BlockSpec((rows, ATT_W), lambda i: (i, 0)),
            pl.BlockSpec((rows, 2 * KV_W), lambda i: (i, 0)),
            pl.BlockSpec((WINDOW, 2 * KV_W), lambda i: (jnp.maximum(i * n_sub - 1, 0), 0)),
            pl.BlockSpec((1, ATT_W), lambda i: (0, 0)),
            *[slab(w) for w in ffn_w],
        ],
        out_specs=[pl.BlockSpec((rows, ATT_W), lambda i: (i, 0)), *[slab(w) for w in ffn_w]],
        out_shape=[jax.ShapeDtypeStruct((n_rows, ATT_W), BF16),
                   *[jax.ShapeDtypeStruct(w.shape, BF16) for w in ffn_w]],
        scratch_shapes=[
            pltpu.VMEM((2, N_KV, 2 * WINDOW, 4 * WINDOW), F32),
            pltpu.VMEM((rows + WINDOW, 2 * N_KV * LANES), BF16),
        ],
        compiler_params=_params(("arbitrary",)),
        name="attn_prompt",
    )(slopes, sinks, q, kv, kv, g_att, *ffn_w)


def _attn_sample_kernel(q_ref, kn_ref, vn_ref, ck_ref, cv_ref, slope_ref, sink_ref, g_ref, perm_ref,
                        o_ref, ko_ref, vo_ref, att_scr, *, n_seq):
    q_sw = _dot(q_ref[...], perm_ref[...])
    r16 = lax.broadcasted_iota(jnp.int32, (N_HEADS, ATT_W), 0)
    c16 = lax.broadcasted_iota(jnp.int32, (N_HEADS, ATT_W), 1)
    own_head = (c16 // HEAD_DIM) == r16
    newest = lax.broadcasted_iota(jnp.int32, (KV_W, WINDOW), 1) == WINDOW - 1
    jj = lax.broadcasted_iota(jnp.int32, (N_HEADS, WINDOW), 1)
    bias = -(slope_ref[...] * (WINDOW - 1 - jj).astype(F32))
    pad = jnp.zeros((LANES - n_seq, KV_W), F32)
    kn_t = jnp.concatenate([kn_ref[...], pad], axis=0).T
    vn_t = jnp.concatenate([vn_ref[...], pad], axis=0).T

    def fold4(v):
        return (v[:, 0:KV_W] + v[:, KV_W:2 * KV_W]) + (v[:, 2 * KV_W:3 * KV_W] + v[:, 3 * KV_W:])

    scores, values = [], []
    for b in range(n_seq):
        kw = jnp.where(newest, jnp.broadcast_to(kn_t[:, b:b + 1], (KV_W, WINDOW)),
                       pltpu.roll(ck_ref[b], WINDOW - 1, 1))
        vw = jnp.where(newest, jnp.broadcast_to(vn_t[:, b:b + 1], (KV_W, WINDOW)),
                       pltpu.roll(cv_ref[b], WINDOW - 1, 1))
        ko_ref[b] = kw
        vo_ref[b] = vw
        qb = jnp.broadcast_to(q_sw[b:b + 1, :], (N_HEADS, ATT_W))
        qbd = fold4(jnp.where(own_head, qb, 0.0))
        scores.append(_dot(qbd.astype(BF16), kw.astype(BF16)))
        values.append(vw.astype(BF16))
    st = jnp.concatenate(scores, axis=0) + jnp.concatenate([bias] * n_seq, axis=0)
    sink = jnp.concatenate([sink_ref[...]] * n_seq, axis=0)
    m = jnp.maximum(jnp.max(st, axis=-1, keepdims=True), sink)
    p = jnp.exp(st - m)
    inv_l = 1.0 / (jnp.sum(p, axis=-1, keepdims=True) + jnp.exp(sink - m))
    pb = p.astype(BF16)
    for b in range(n_seq):
        hs = slice(b * N_HEADS, (b + 1) * N_HEADS)
        o = _dot_nt(pb[hs, :], values[b]) * inv_l[hs, :]
        o4 = jnp.concatenate([o, o, o, o], axis=1)
        att_scr[b:b + 1, :] = jnp.sum(jnp.where(own_head, o4, 0.0), axis=0, keepdims=True)
    normed = _rms(att_scr[...], g_ref[...]).astype(BF16)
    o_ref[...] = _dot_nt(normed, perm_ref[...]).astype(BF16)


def _attn_sample(q, kn, vn, ck, cv, slope_col, sink_col, g_att, perm):
    n = q.shape[0]
    nb = SEQ_BLK
    cache_spec = pl.BlockSpec((nb, KV_W, WINDOW), lambda i: (i, 0, 0))
    row_spec = lambda w: pl.BlockSpec((nb, w), lambda i: (i, 0))
    full = lambda a: pl.BlockSpec(a.shape, lambda i: (0,) * a.ndim)
    return pl.pallas_call(
        functools.partial(_attn_sample_kernel, n_seq=nb),
        grid=(n // nb,),
        in_specs=[row_spec(ATT_W), row_spec(KV_W), row_spec(KV_W), cache_spec, cache_spec,
                  full(slope_col), full(sink_col), full(g_att), full(perm)],
        out_specs=[row_spec(ATT_W), cache_spec, cache_spec],
        out_shape=[
            jax.ShapeDtypeStruct((n, ATT_W), BF16),
            jax.ShapeDtypeStruct(ck.shape, F32),
            jax.ShapeDtypeStruct(cv.shape, F32),
        ],
        scratch_shapes=[pltpu.VMEM((nb, ATT_W), F32)],
        compiler_params=_params(("parallel",)),
        name="attn_sample",
    )(q, kn, vn, ck, cv, slope_col, sink_col, g_att, perm)


def _ssd_prompt_kernel(xb_ref, z_ref, dtb_ref, alog_ref, dskip_ref, g_ref,
                       y_ref, hout_ref, ht_scr, e_scr, tri_scr, yd_scr, *, rows):
    i = pl.program_id(0)
    T = CHUNK

    @pl.when(i == 0)
    def _():
        ht_scr[...] = jnp.zeros_like(ht_scr)
        e_scr[...] = _head_expand_matrix()
        l = lax.broadcasted_iota(jnp.int32, (T, T), 0)
        s = lax.broadcasted_iota(jnp.int32, (T, T), 1)
        tri_scr[...] = jnp.where(s <= l, 1.0, 0.0).astype(BF16)

    a_log2 = -jnp.exp(alog_ref[...]) * LOG2E
    e_mat = e_scr[...]
    tri = tri_scr[...]
    causal = lax.broadcasted_iota(jnp.int32, (T, T), 0) >= lax.broadcasted_iota(jnp.int32, (T, T), 1)
    lo = lax.broadcasted_iota(jnp.int32, (T, LANES), 1) < SSM_P
    gw = SSM_W // SSM_G

    for c in range(rows // T):
        rs = slice(c * T, (c + 1) * T)
        dt = _softplus(xb_ref[rs, CONV_DIM:XBCDT_W] + dtb_ref[...])
        dta = dt * a_log2
        hi, mid, lw = _split3(dta)
        acum = _dot(tri, hi) + _dot(tri, mid) + _dot(tri, lw)
        acum_e = _expand_heads(acum, e_mat)
        dt_e = _expand_heads(dt, e_mat)
        acum_t = acum.T
        xs = xb_ref[rs, 0:SSM_W]
        bm = xb_ref[rs, SSM_W:SSM_W + SSM_G * D_STATE]
        cm = xb_ref[rs, SSM_W + SSM_G * D_STATE:CONV_DIM]
        xdt = xs * dt_e
        last = acum_e[T - 1:T, :]
        xdt_b = xdt.astype(BF16)
        xdec_b = (xdt * jnp.exp2(last - acum_e)).astype(BF16)
        bm_b = bm.astype(BF16)
        cm_b = cm.astype(BF16)
        cbs = [_dot_nt(cm_b[:, g * D_STATE:(g + 1) * D_STATE], bm_b[:, g * D_STATE:(g + 1) * D_STATE])
               for g in range(SSM_G)]
        for pr in range(SSM_H // 2):
            g = (2 * pr) // (SSM_H // SSM_G)
            ws = []
            for t in range(2):
                h = 2 * pr + t
                seg = jnp.broadcast_to(acum[:, h:h + 1], (T, T)) - acum_t[h:h + 1, :]
                ws.append((cbs[g] * jnp.exp2(jnp.where(causal, seg, NEG_INF))).astype(BF16))
            xsl = xdt_b[:, pr * LANES:(pr + 1) * LANES]
            yd_scr[:, pr * LANES:(pr + 1) * LANES] = _dot(jnp.concatenate(ws, axis=1),
                                                          _pair_blockdiag(xsl, xsl, lo, ~lo))
        yoff = []
        for g in range(SSM_G):
            gs = slice(g * gw, (g + 1) * gw)
            ht_g = ht_scr[:, gs]
            yoff.append(_dot(cm_b[:, g * D_STATE:(g + 1) * D_STATE], ht_g.astype(BF16)))
            cst = _dot_tn(bm_b[:, g * D_STATE:(g + 1) * D_STATE], xdec_b[:, gs])
            ht_scr[:, gs] = ht_g * jnp.exp2(last[:, gs]) + cst
        y = yd_scr[...] + jnp.concatenate(yoff, axis=1) * jnp.exp2(acum_e)
        y = y + dskip_ref[...] * xs
        y = y * z_ref[rs, :]
        y_ref[rs, :] = _rms(y, g_ref[...]).astype(BF16)

    @pl.when(i == pl.num_programs(0) - 1)
    def _():
        hout_ref[...] = ht_scr[...].T


def _ssd_prompt(xb, z, dtb, alog, dskip_e, g_ssm):
    n_rows = xb.shape[0]
    rows = ROWS_SSD
    full = lambda a: pl.BlockSpec(a.shape, lambda i: (0,) * a.ndim)
    return pl.pallas_call(
        functools.partial(_ssd_prompt_kernel, rows=rows),
        grid=(n_rows // rows,),
        in_specs=[
            pl.BlockSpec((rows, XBCDT_W), lambda i: (i, 0)),
            pl.BlockSpec((rows, SSM_W), lambda i: (i, 0)),
            full(dtb), full(alog), full(dskip_e), full(g_ssm),
        ],
        out_specs=[
            pl.BlockSpec((rows, SSM_W), lambda i: (i, 0)),
            pl.BlockSpec((SSM_W, D_STATE), lambda i: (0, 0)),
        ],
        out_shape=[
            jax.ShapeDtypeStruct((n_rows, SSM_W), BF16),
            jax.ShapeDtypeStruct((SSM_W, D_STATE), F32),
        ],
        scratch_shapes=[
            pltpu.VMEM((D_STATE, SSM_W), F32),
            pltpu.VMEM((LANES, SSM_W), BF16),
            pltpu.VMEM((CHUNK, CHUNK), BF16),
            pltpu.VMEM((CHUNK, SSM_W), F32),
        ],
        compiler_params=_params(("arbitrary",)),
        name="ssd_prompt",
    )(xb, z, dtb, alog, dskip_e, g_ssm)


def _ssd_sample_kernel(xb_ref, sconv_ref, z_ref, h0_ref, cw_ref, cb_ref, dtb_ref, alog_ref, dskip_ref, g_ref,
                       y_ref, conv_ref, hout_ref, yoff_scr, *, n_seq):
    gw = SSM_W // SSM_G
    gn = SSM_G * D_STATE
    x_new = xb_ref[:, 0:CONV_DIM]
    taps = [sconv_ref[:, k * CONV_DIM:(k + 1) * CONV_DIM] for k in range(CONV_W - 1)] + [x_new]
    acc = cb_ref[...]
    for k in range(CONV_W):
        acc = acc + taps[k] * cw_ref[k:k + 1, :]
    for k in range(1, CONV_W):
        conv_ref[:, (k - 1) * CONV_DIM:k * CONV_DIM] = taps[k]
    xc = _silu(acc)
    xs = xc[:, 0:SSM_W]
    bm = xc[:, SSM_W:SSM_W + gn]
    cm = xc[:, SSM_W + gn:CONV_DIM]

    e_mat = _head_expand_matrix()
    dt = _softplus(xb_ref[:, CONV_DIM:XBCDT_W] + dtb_ref[...])
    dt_e = _expand_heads(dt, e_mat)
    dec_e = jnp.exp(_expand_heads(dt * (-jnp.exp(alog_ref[...])), e_mat))
    xdt = xs * dt_e

    lane_w = lax.broadcasted_iota(jnp.int32, (n_seq, SSM_W), 1)
    first_grp = lane_w < gw
    cbv = [jnp.sum(cm[:, g * D_STATE:(g + 1) * D_STATE] * bm[:, g * D_STATE:(g + 1) * D_STATE],
                   axis=-1, keepdims=True) for g in range(SSM_G)]
    cb_e = jnp.where(first_grp, cbv[0], cbv[1])

    pad = jnp.zeros((LANES - n_seq, SSM_W), F32)
    xdt_t = jnp.concatenate([xdt, pad], axis=0).T
    dec_t = jnp.concatenate([dec_e, pad], axis=0).T
    sub8 = lax.broadcasted_iota(jnp.int32, (8, D_STATE), 0)
    lane_r = lax.broadcasted_iota(jnp.int32, (8, SSM_W), 1)

    for b in range(n_seq):
        h0 = h0_ref[b]
        c_row = jnp.broadcast_to(cm[b:b + 1, :], (8, gn))
        c8 = jnp.where(sub8 == 0, c_row[:, 0:D_STATE], jnp.where(sub8 == 1, c_row[:, D_STATE:], 0.0))
        r = _dot_nt(c8.astype(BF16), h0.astype(BF16))
        yoff_scr[b:b + 1, :] = jnp.where(lane_r[0:1] < gw, r[0:1, :], r[1:2, :])
        b_row = bm[b:b + 1, :]
        for g in range(SSM_G):
            rs = slice(g * gw, (g + 1) * gw)
            dcol = jnp.broadcast_to(dec_t[rs, b:b + 1], (gw, D_STATE))
            xcol = jnp.broadcast_to(xdt_t[rs, b:b + 1], (gw, D_STATE))
            hout_ref[b, rs, :] = h0[rs, :] * dcol + xcol * b_row[:, g * D_STATE:(g + 1) * D_STATE]

    y = cb_e * xdt + yoff_scr[...] * dec_e
    y = y + dskip_ref[...] * xs
    y = y * _silu(z_ref[...])
    y_ref[...] = _rms(y, g_ref[...]).astype(BF16)


def _ssd_sample(xb, sconv, z, h0, cw8, cb, dtb, alog, dskip_e, g_ssm):
    n = xb.shape[0]
    nb = SEQ_BLK
    row_spec = lambda w: pl.BlockSpec((nb, w), lambda i: (i, 0))
    st_spec = pl.BlockSpec((nb, SSM_W, D_STATE), lambda i: (i, 0, 0))
    full = lambda a: pl.BlockSpec(a.shape, lambda i: (0,) * a.ndim)
    return pl.pallas_call(
        functools.partial(_ssd_sample_kernel, n_seq=nb),
        grid=(n // nb,),
        in_specs=[row_spec(XBCDT_W), row_spec((CONV_W - 1) * CONV_DIM), row_spec(SSM_W), st_spec,
                  full(cw8), full(cb), full(dtb), full(alog), full(dskip_e), full(g_ssm)],
        out_specs=[row_spec(SSM_W), row_spec((CONV_W - 1) * CONV_DIM), st_spec],
        out_shape=[
            jax.ShapeDtypeStruct((n, SSM_W), BF16),
            jax.ShapeDtypeStruct((n, (CONV_W - 1) * CONV_DIM), F32),
            jax.ShapeDtypeStruct(h0.shape, F32),
        ],
        scratch_shapes=[pltpu.VMEM((nb, SSM_W), F32)],
        compiler_params=_params(("parallel",)),
        name="ssd_sample",
    )(xb, sconv, z, h0, cw8, cb, dtb, alog, dskip_e, g_ssm)


def _outproj_kernel(att_ref, ssm_ref, x_ref, woa_ref, wos_ref, gpost_ref, gt1_ref, gpre_ref, sh2_ref, sc2_ref,
                    x1_ref, u2_ref, *, per_row, rows):
    sub = min(SUB_OUT, rows)
    for r0 in range(0, rows, sub):
        rs = slice(r0, r0 + sub)
        mod = lambda ref: ref[rs, :] if per_row else ref[0:1, :]
        mix = _dot(att_ref[rs, :], woa_ref[...]) + _dot(ssm_ref[rs, :], wos_ref[...])
        x1 = x_ref[rs, :] + _rms(mix, gpost_ref[...] * mod(gt1_ref))
        x1_ref[rs, :] = x1
        u2 = _rms(x1, gpre_ref[...] * (1.0 + mod(sc2_ref))) + mod(sh2_ref)
        u2_ref[rs, :] = u2.astype(BF16)


def _outproj(att, ssm, x, mod, per_row, woa, wos, g_post, g_pre):
    n_rows = x.shape[0]
    rows = min(ROWS_OUT, n_rows)
    row_spec = lambda w: pl.BlockSpec((rows, w), lambda i: (i, 0))
    return pl.pallas_call(
        functools.partial(_outproj_kernel, per_row=per_row, rows=rows),
        grid=(n_rows // rows,),
        in_specs=[
            row_spec(ATT_W), row_spec(SSM_W), row_spec(D_MODEL),
            _const_spec(woa.shape), _const_spec(wos.shape),
            _const_spec((1, D_MODEL)),
            _mod_spec(per_row, rows, 2),
            _const_spec((1, D_MODEL)),
            _mod_spec(per_row, rows, 3),
            _mod_spec(per_row, rows, 4),
        ],
        out_specs=[row_spec(D_MODEL), row_spec(D_MODEL)],
        out_shape=[
            jax.ShapeDtypeStruct((n_rows, D_MODEL), F32),
            jax.ShapeDtypeStruct((n_rows, D_MODEL), BF16),
        ],
        compiler_params=_params(("parallel",)),
        name="outproj",
    )(att, ssm, x, woa, wos, g_post, mod, g_pre, mod, mod)


def _ffn_kernel(u_ref, x1_hbm, wg_ref, wu_ref, wd_ref, gpost_ref, gt2_ref, us_ref, x1s_ref, gt2s_ref,
                y_ref, ys_ref, x1_buf, x1_sem, *, rows):
    i = pl.program_id(0)
    j = pl.program_id(1)
    x1_copy = pltpu.make_async_copy(x1_hbm.at[pl.ds(i * rows, rows), :], x1_buf, x1_sem)
    last = pl.num_programs(1) - 1

    def ffn_rows(u, first, final, acc_prev, resid, gt2):
        hid = (_silu(_dot(u, wg_ref[...])) * _dot(u, wu_ref[...])).astype(BF16)
        acc = _dot(hid, wd_ref[...])
        if not first:
            acc = acc_prev() + acc
        if final:
            acc = resid() + gt2() * _rms(acc, gpost_ref[...])
        return acc

    def d_ff_slice(first, final):
        sub = min(SUB_FFN, rows)
        for r0 in range(0, rows, sub):
            rs = slice(r0, r0 + sub)
            y_ref[rs, :] = ffn_rows(u_ref[rs, :], first, final, lambda: y_ref[rs, :],
                                    lambda: x1_buf[rs, :], lambda: gt2_ref[0:1, :])

        @pl.when(i == pl.num_programs(0) - 1)
        def _():
            ys_ref[...] = ffn_rows(us_ref[...], first, final, lambda: ys_ref[...],
                                   lambda: x1s_ref[...], lambda: gt2s_ref[...])

    @pl.when(j == 0)
    def _():
        x1_copy.start()
        d_ff_slice(True, False)

    @pl.when(jnp.logical_and(j > 0, j < last))
    def _():
        d_ff_slice(False, False)

    @pl.when(j == last)
    def _():
        x1_copy.wait()
        d_ff_slice(False, True)


def _ffn(u2, x1, mod_p, u2_s, x1_s, mod_s, wg, wu, wd, g_post):
    n_rows = x1.shape[0]
    n_s = x1_s.shape[0]
    rows = ROWS_FFN
    fb = FF_BLK
    assert D_FF // fb >= 2 and n_rows % rows == 0
    whole = lambda r, c: pl.BlockSpec((r, D_MODEL), lambda i, j: (0, c))
    return pl.pallas_call(
        functools.partial(_ffn_kernel, rows=rows),
        grid=(n_rows // rows, D_FF // fb),
        in_specs=[
            pl.BlockSpec((rows, D_MODEL), lambda i, j: (i, 0)),
            pl.BlockSpec(memory_space=pl.ANY),
            pl.BlockSpec((D_MODEL, fb), lambda i, j: (0, j)),
            pl.BlockSpec((D_MODEL, fb), lambda i, j: (0, j)),
            pl.BlockSpec((fb, D_MODEL), lambda i, j: (j, 0)),
            whole(1, 0),
            _mod_spec(False, rows, 5),
            whole(n_s, 0), whole(n_s, 0), whole(n_s, 5),
        ],
        out_specs=[pl.BlockSpec((rows, D_MODEL), lambda i, j: (i, 0)), whole(n_s, 0)],
        out_shape=[jax.ShapeDtypeStruct((n_rows, D_MODEL), F32), jax.ShapeDtypeStruct((n_s, D_MODEL), F32)],
        scratch_shapes=[pltpu.VMEM((rows, D_MODEL), F32), pltpu.SemaphoreType.DMA],
        compiler_params=_params(("arbitrary", "arbitrary"), VMEM_LIMIT_FFN),
        name="ffn",
    )(u2, x1, wg, wu, wd, g_post, mod_p, u2_s, x1_s, mod_s)


def _alibi_slopes():
    return (2.0 ** (-8.0 * np.arange(1, N_HEADS + 1) / N_HEADS)).astype(np.float32)


def kernel(x_prompt, x_sample, cache_k, cache_v, state_conv, state_ssm, c_prompt, c_sample, w_ada, b_ada, g_pre_mix, g_post_mix, w_in, attn_sinks, g_attn_out, conv_w, conv_b, dt_bias, a_log, d_skip, g_ssm_out, w_out, g_pre_ffn, g_post_ffn, w_gate, w_up, w_down):
    assert w_ada.shape[0] == 1, "one layer"
    n_s = x_sample.shape[0]
    row = lambda v: v.reshape(1, -1)

    w_all = _wsplit(jnp.transpose(w_in[0]))
    woa = w_out[0, :ATT_W].astype(BF16)
    wos = w_out[0, ATT_W:].astype(BF16)
    g_att = row(g_attn_out[0])
    swapped_cols = np.arange(ATT_W).reshape(N_KV, Q_PER_KV, HEAD_DIM).swapaxes(0, 1).reshape(-1)
    perm = np.zeros((ATT_W, ATT_W), np.float32)
    perm[swapped_cols, np.arange(ATT_W)] = 1.0
    perm = jnp.asarray(perm, BF16)
    g_att_s = row(g_attn_out[0].reshape(N_KV, Q_PER_KV, HEAD_DIM).swapaxes(0, 1))
    slopes_np = _alibi_slopes()
    slopes = jnp.asarray(slopes_np)
    slopes_s = jnp.asarray(slopes_np.reshape(N_KV, Q_PER_KV).T.reshape(N_HEADS, 1))
    sinks = attn_sinks[0]
    sinks_s = sinks.reshape(N_KV, Q_PER_KV).T.reshape(N_HEADS, 1)
    cw8 = jnp.pad(conv_w[0], ((0, 8 - CONV_W), (0, 0)))
    cb = row(conv_b[0])
    dtb = jnp.pad(row(dt_bias[0]), ((0, 0), (0, LANES - SSM_H)))
    alog = jnp.pad(row(a_log[0]), ((0, 0), (0, LANES - SSM_H)))
    dskip_e = row(jnp.repeat(d_skip[0], SSM_P))
    g_ssm = row(g_ssm_out[0])

    c_p8 = jnp.pad(c_prompt, ((0, 8 - c_prompt.shape[0]), (0, 0)))
    mod_s, mod_p = _ada(c_sample, c_p8, w_ada[0], row(b_ada[0]))

    xp = x_prompt[0]
    q, kv, z, xb, kv_last, x_tail = _inproj(xp, row(g_pre_mix[0]), mod_p, False, w_all, cw8, cb, True)
    att, wg, wu, wd = _attn_prompt(q, kv, slopes, sinks, g_att, w_gate[0], w_up[0], w_down[0])
    ssm, h_p = _ssd_prompt(xb, z, dtb, alog, dskip_e, g_ssm)
    x1, u2 = _outproj(att, ssm, xp, mod_p, False, woa, wos, row(g_post_mix[0]), row(g_pre_ffn[0]))

    keys_minor = lambda c: jnp.transpose(c, (0, 2, 3, 1)).reshape(n_s, KV_W, WINDOW)
    keys_major = lambda c: jnp.transpose(c.reshape(n_s, N_KV, HEAD_DIM, WINDOW), (0, 3, 1, 2))[None]
    xs_ = x_sample[:, 0, :]
    q_s, _, z_s, xb_s, kv_new, _ = _inproj(xs_, row(g_pre_mix[0]), mod_s, True, w_all, cw8, cb, False)
    att_s, k_s, v_s = _attn_sample(
        q_s, kv_new[:, :KV_W], kv_new[:, KV_W:],
        keys_minor(cache_k[0]), keys_minor(cache_v[0]), slopes_s, sinks_s, g_att_s, perm)
    ssm_s, conv_s, h_s = _ssd_sample(
        xb_s, state_conv[0].reshape(n_s, (CONV_W - 1) * CONV_DIM), z_s,
        state_ssm[0].reshape(n_s, SSM_W, D_STATE), cw8, cb, dtb, alog, dskip_e, g_ssm)
    x1_s, u2_s = _outproj(att_s, ssm_s, xs_, mod_s, True, woa, wos, row(g_post_mix[0]), row(g_pre_ffn[0]))
    y_p, y_s = _ffn(u2, x1, mod_p, u2_s, x1_s, mod_s, wg, wu, wd, row(g_post_ffn[0]))

    return (
        y_p[None],
        y_s[:, None, :],
        kv_last[:, :KV_W].reshape(1, 1, WINDOW, N_KV, HEAD_DIM),
        kv_last[:, KV_W:].reshape(1, 1, WINDOW, N_KV, HEAD_DIM),
        x_tail[8 - (CONV_W - 1):].reshape(1, 1, CONV_W - 1, CONV_DIM),
        h_p.reshape(1, 1, SSM_H, SSM_P, D_STATE),
        keys_major(k_s),
        keys_major(v_s),
        conv_s.reshape(1, n_s, CONV_W - 1, CONV_DIM),
        h_s.reshape(1, n_s, SSM_H, SSM_P, D_STATE),
    )
```

```python
import functools

import numpy as np
import jax
import jax.numpy as jnp
from jax import lax
from jax.experimental import pallas as pl
from jax.experimental.pallas import tpu as pltpu

F32 = jnp.float32
BF16 = jnp.bfloat16

D_MODEL = 2048
ATT_W = 1024
HEAD_DIM = 64
N_HEADS = 16
N_KV = 4
Q_PER_KV = 4
KV_W = N_KV * HEAD_DIM
WINDOW = 128
SSM_W = 1024
SSM_P = 64
SSM_H = 16
SSM_G = 2
D_STATE = 128
CONV_W = 4
CONV_DIM = SSM_W + 2 * SSM_G * D_STATE
XBCDT_W = CONV_DIM + 128
D_FF = 5632
EPS = 1e-6
CHUNK = 128
NEG_INF = float("-inf")
LOG2E = 1.4426950408889634

VMEM_LIMIT = 56 * 1024 * 1024
VMEM_LIMIT_FFN = 60 * 1024 * 1024
LANES = 128

ROWS_IN = 512
ROWS_ATT = 512
ROWS_SSD = 512
ROWS_OUT = 512
SUB_OUT = 128
ROWS_FFN = 1024
SUB_FFN = 512
FF_BLK = 512
SEQ_BLK = 8


def _params(sem, vmem=VMEM_LIMIT):
    return pltpu.CompilerParams(dimension_semantics=sem, vmem_limit_bytes=vmem)


def _silu(v):
    h = 0.5 * v
    return h + h * jnp.tanh(h)


def _softplus(v):
    return jnp.maximum(v, 0.0) + jnp.log1p(jnp.exp(-jnp.abs(v)))


def _rms(v, g):
    return v * lax.rsqrt(jnp.mean(v * v, axis=-1, keepdims=True) + EPS) * g


def _split3(v):
    hi = v.astype(BF16)
    r1 = v - hi.astype(F32)
    mid = r1.astype(BF16)
    lo = (r1 - mid.astype(F32)).astype(BF16)
    return hi, mid, lo


def _dot(a, b):
    return jnp.dot(a, b, preferred_element_type=F32)


def _dot_nt(a, b):
    return lax.dot_general(a, b, (((1,), (1,)), ((), ())), preferred_element_type=F32)


def _dot_tn(a, b):
    return lax.dot_general(a, b, (((0,), (0,)), ((), ())), preferred_element_type=F32)


def _expand_heads(v, e_mat):
    hi, mid, lo = _split3(v)
    return _dot(hi, e_mat) + _dot(mid, e_mat) + _dot(lo, e_mat)


def _head_expand_matrix():
    k = lax.broadcasted_iota(jnp.int32, (LANES, SSM_W), 0)
    c = lax.broadcasted_iota(jnp.int32, (LANES, SSM_W), 1)
    return jnp.where((c // SSM_P) == k, 1.0, 0.0).astype(BF16)


def _mod_spec(per_row, rows, col):
    if per_row:
        return pl.BlockSpec((rows, D_MODEL), lambda i, *_: (i, col))
    return pl.BlockSpec((8, D_MODEL), lambda i, *_: (0, col))


def _mod_rows(ref, per_row):
    return ref[...] if per_row else ref[0:1, :]


WS_BLK = 256
W_ALL_COLS = 4608
COL_X, COL_Q, COL_Z, COL_KV = 0, 2048, 3072, 4096
N_WS_SRC = -(-(ATT_W + 2 * KV_W + SSM_W + CONV_DIM + SSM_H) // WS_BLK)


def _ws_dst_block(s):
    q_end = ATT_W // WS_BLK
    kv_end = q_end + 2 * KV_W // WS_BLK
    z_end = kv_end + SSM_W // WS_BLK
    return jnp.where(s < q_end, COL_Q // WS_BLK + s,
                     jnp.where(s < kv_end, COL_KV // WS_BLK + s - q_end,
                               jnp.where(s < z_end, COL_Z // WS_BLK + s - kv_end,
                                         COL_X // WS_BLK + s - z_end)))


ADA_BLK = 768


def _prep_kernel(cs_ref, cp_ref, wa_ref, b_ref, wt_ref, ms_ref, mp_ref, wall_ref):
    s = pl.program_id(0)
    w = wa_ref[...].astype(BF16)
    b = b_ref[...]
    ms_ref[...] = _dot(_silu(cs_ref[...]).astype(BF16), w) + b
    mp_ref[...] = _dot(_silu(cp_ref[...]).astype(BF16), w) + b
    n_valid = jnp.where(s < N_WS_SRC - 1, WS_BLK, jnp.where(s == N_WS_SRC - 1, SSM_H, 0))
    lane = lax.broadcasted_iota(jnp.int32, wall_ref.shape, 1)
    wall_ref[...] = jnp.where(lane < n_valid, wt_ref[...].T, 0.0).astype(BF16)


def _prep(c_s, c_p8, w_ada, b_ada, w_in_t):
    k_dim = w_in_t.shape[1]
    n = w_ada.shape[1]
    n_ada = n // ADA_BLK
    assert n % ADA_BLK == 0 and n_ada <= N_WS_SRC + 1
    ada_col = lambda s: (0, jnp.minimum(s, n_ada - 1))
    return pl.pallas_call(
        _prep_kernel,
        grid=(N_WS_SRC + 1,),
        in_specs=[
            pl.BlockSpec(c_s.shape, lambda s: (0, 0)),
            pl.BlockSpec(c_p8.shape, lambda s: (0, 0)),
            pl.BlockSpec((D_MODEL, ADA_BLK), ada_col),
            pl.BlockSpec((1, ADA_BLK), ada_col),
            pl.BlockSpec((WS_BLK, k_dim), lambda s: (jnp.minimum(s, N_WS_SRC - 1), 0)),
        ],
        out_specs=[
            pl.BlockSpec((c_s.shape[0], ADA_BLK), ada_col),
            pl.BlockSpec((c_p8.shape[0], ADA_BLK), ada_col),
            pl.BlockSpec((k_dim, WS_BLK), lambda s: (0, _ws_dst_block(s))),
        ],
        out_shape=[
            jax.ShapeDtypeStruct((c_s.shape[0], n), F32),
            jax.ShapeDtypeStruct((c_p8.shape[0], n), F32),
            jax.ShapeDtypeStruct((k_dim, W_ALL_COLS), BF16),
        ],
        compiler_params=_params(("arbitrary",)),
        name="prep",
    )(c_s, c_p8, w_ada, b_ada, w_in_t)


def _inproj_kernel(x_ref, g_ref, sh_ref, sc_ref, wq_ref, wkv_ref, wz_ref, wx_ref, cw_ref, cb_ref,
                   q_ref, kv_ref, z_ref, xb_ref, kvlast_ref, xtail_ref, xp_scr, *, per_row, rows, fuse_conv):
    tile = 2 * LANES
    if fuse_conv:
        @pl.when(pl.program_id(0) == 0)
        def _():
            xp_scr[:, rows:rows + 8, :] = jnp.zeros((CONV_DIM // LANES, 8, LANES), F32)

    u = _rms(x_ref[...], g_ref[...]) * (1.0 + _mod_rows(sc_ref, per_row)) + _mod_rows(sh_ref, per_row)
    u = u.astype(BF16)

    def conv_slab(s, xs):
        ls = slice(s * LANES, (s + 1) * LANES)
        xtail_ref[:, ls] = xs[rows - 8:, :]
        xp_scr[s, 0:8, :] = xp_scr[s, rows:rows + 8, :]
        xp_scr[s, 8:, :] = xs
        acc = cb_ref[:, ls] + xp_scr[s, pl.ds(8 - (CONV_W - 1), rows), :] * cw_ref[0:1, ls]
        for k in range(1, CONV_W):
            acc = acc + xp_scr[s, pl.ds(8 - (CONV_W - 1) + k, rows), :] * cw_ref[k:k + 1, ls]
        xb_ref[:, ls] = _silu(acc)

    def post_z(c0, c1):
        def post(r):
            z_ref[:, c0:c1] = _silu(r) if fuse_conv else r
        return post

    def post_q(c0, c1):
        def post(r):
            q_ref[:, c0:c1] = (r * (HEAD_DIM ** -0.5)).astype(BF16)
        return post

    def post_kv(c0, c1):
        def post(r):
            kv_ref[:, c0:c1] = r.astype(BF16)
            kvlast_ref[:, c0:c1] = r[rows - WINDOW:, :]
        return post

    xb = _dot(u, wx_ref[...])
    xb_ref[:, CONV_DIM:] = xb[:, CONV_DIM:]
    if fuse_conv:
        slabs = [functools.partial(conv_slab, s, xb[:, s * LANES:(s + 1) * LANES]) for s in range(CONV_DIM // LANES)]
    else:
        xb_ref[:, 0:CONV_DIM] = xb[:, 0:CONV_DIM]
        xtail_ref[...] = xb[rows - 8:, 0:CONV_DIM]
        slabs = []

    jobs = []
    for w_ref, width, post_of in ((wz_ref, SSM_W, post_z), (wq_ref, ATT_W, post_q), (wkv_ref, 2 * KV_W, post_kv)):
        for c0 in range(0, width, tile):
            jobs.append((w_ref, c0, c0 + tile, post_of(c0, c0 + tile)))
    pending = None
    for n, (w_ref, c0, c1, post) in enumerate(jobs):
        r = _dot(u, w_ref[:, c0:c1])
        for slab in slabs[n * len(slabs) // len(jobs):(n + 1) * len(slabs) // len(jobs)]:
            slab()
        if pending is not None:
            pending[0](pending[1])
        pending = (post, r)
    pending[0](pending[1])


def _const_spec(shape):
    return pl.BlockSpec(shape, lambda i, *_: (0,) * len(shape), pipeline_mode=pl.Buffered(1))


def _inproj(x, g, mod, per_row, w_all, cw8, cb, fuse_conv):
    n_rows = x.shape[0]
    rows = min(ROWS_IN, n_rows)
    row_spec = lambda w: pl.BlockSpec((rows, w), lambda i: (i, 0))
    seg_spec = lambda width, col: pl.BlockSpec((D_MODEL, width), lambda i: (0, col // width),
                                               pipeline_mode=pl.Buffered(1))
    return pl.pallas_call(
        functools.partial(_inproj_kernel, per_row=per_row, rows=rows, fuse_conv=fuse_conv),
        grid=(n_rows // rows,),
        in_specs=[
            row_spec(D_MODEL),
            _const_spec((1, D_MODEL)),
            _mod_spec(per_row, rows, 0),
            _mod_spec(per_row, rows, 1),
            seg_spec(ATT_W, COL_Q), seg_spec(2 * KV_W, COL_KV), seg_spec(SSM_W, COL_Z), seg_spec(XBCDT_W, COL_X),
            _const_spec(cw8.shape), _const_spec(cb.shape),
        ],
        out_specs=[
            row_spec(ATT_W), row_spec(2 * KV_W), row_spec(SSM_W), row_spec(XBCDT_W),
            pl.BlockSpec((WINDOW, 2 * KV_W), lambda i: (0, 0)),
            pl.BlockSpec((8, CONV_DIM), lambda i: (0, 0)),
        ],
        out_shape=[
            jax.ShapeDtypeStruct((n_rows, ATT_W), BF16),
            jax.ShapeDtypeStruct((n_rows, 2 * KV_W), BF16),
            jax.ShapeDtypeStruct((n_rows, SSM_W), F32),
            jax.ShapeDtypeStruct((n_rows, XBCDT_W), F32),
            jax.ShapeDtypeStruct((WINDOW, 2 * KV_W), F32),
            jax.ShapeDtypeStruct((8, CONV_DIM), F32),
        ],
        scratch_shapes=[pltpu.VMEM((CONV_DIM // LANES, rows + 8, LANES), F32)],
        compiler_params=_params(("arbitrary",)),
        name="inproj",
    )(x, g, mod, mod, w_all, w_all, w_all, w_all, cw8, cb)


def _pair_blockdiag(lo_src, hi_src, keep_lo, keep_hi):
    zero = jnp.zeros_like(lo_src)
    return jnp.concatenate([jnp.where(keep_lo, lo_src, zero), jnp.where(keep_hi, hi_src, zero)], axis=0)


def _attn_prompt_kernel(slopes_ref, sinks_ref, q_ref, kvc_ref, kvp_ref, g_ref, wg_ref, wu_ref, wd_ref, wo_ref,
                        o_ref, wg_o, wu_o, wd_o, wo_o, bias_scr, kvdup_scr, *, n_sub):
    i = pl.program_id(0)
    blk = WINDOW

    wg_o[...] = wg_ref[...].astype(BF16)
    wu_o[...] = wu_ref[...].astype(BF16)
    wd_o[...] = wd_ref[...].astype(BF16)
    wo_o[...] = wo_ref[...].astype(BF16)

    @pl.when(i == 0)
    def _():
        a = lax.broadcasted_iota(jnp.int32, (blk, 2 * blk), 0)
        j = lax.broadcasted_iota(jnp.int32, (blk, 2 * blk), 1)
        dist = a + blk - j
        valid = (dist >= 0) & (dist < WINDOW)
        distf = dist.astype(F32)
        for h in range(N_HEADS):
            kvh, g = divmod(h, Q_PER_KV)
            pr, t = divmod(g, 2)
            b = jnp.where(valid, -(slopes_ref[h] * distf), NEG_INF)
            b = jnp.where(j == 0, sinks_ref[h], b)
            rs = slice(pr * blk, (pr + 1) * blk)
            cs = slice(t * 2 * blk, (t + 1) * 2 * blk)
            bias_scr[1, kvh, rs, cs] = b
            bias_scr[0, kvh, rs, cs] = jnp.where((j >= blk) | (j == 0), b, NEG_INF)

    for src_ref, r_lo, r_hi in ((kvp_ref, 0, blk), (kvc_ref, blk, kvdup_scr.shape[0])):
        lo_half = lax.broadcasted_iota(jnp.int32, (r_hi - r_lo, LANES), 1) < HEAD_DIM
        for part in range(2):
            for s in range(2):
                c_in = part * KV_W + s * LANES
                c_out = part * N_KV * LANES + 2 * s * LANES
                x = src_ref[:, c_in:c_in + LANES].astype(F32)
                xr = pltpu.roll(x, HEAD_DIM, 1)
                kvdup_scr[r_lo:r_hi, c_out:c_out + LANES] = jnp.where(lo_half, x, xr).astype(BF16)
                kvdup_scr[r_lo:r_hi, c_out + LANES:c_out + 2 * LANES] = jnp.where(lo_half, xr, x).astype(BF16)

    lane = lax.broadcasted_iota(jnp.int32, (2 * blk, LANES), 1)
    not_sink = lax.broadcasted_iota(jnp.int32, (2 * blk, LANES), 0) != 0
    keep_lo = (lane < HEAD_DIM) & not_sink
    keep_hi = (lane >= HEAD_DIM) & not_sink
    ones_cols = jnp.concatenate([jnp.where(lane < HEAD_DIM, 1.0, 0.0),
                                 jnp.where(lane >= HEAD_DIM, 1.0, 0.0)], axis=0).astype(BF16)
    n_chunk = 4
    rc = 2 * blk // n_chunk

    def body(b, carry):
        r0 = pl.multiple_of(b * blk, blk)
        variant = jnp.where(jnp.logical_and(i == 0, b == 0), 0, 1)
        vbds, scores = [], []
        for kvh in range(N_KV):
            c0 = kvh * Q_PER_KV * HEAD_DIM
            kd = kvdup_scr[pl.ds(r0, 2 * blk), kvh * LANES:(kvh + 1) * LANES]
            vd = kvdup_scr[pl.ds(r0, 2 * blk), (N_KV + kvh) * LANES:(N_KV + kvh + 1) * LANES]
            kbd = _pair_blockdiag(kd, kd, keep_lo, keep_hi)
            vbds.append(jnp.concatenate([_pair_blockdiag(vd, vd, keep_lo, keep_hi), ones_cols], axis=1))
            q2 = jnp.concatenate([q_ref[pl.ds(r0, blk), c0:c0 + LANES],
                                  q_ref[pl.ds(r0, blk), c0 + LANES:c0 + 2 * LANES]], axis=0)
            scores.append(_dot_nt(q2, kbd) + bias_scr[variant, kvh])
        probs = []
        for kvh in range(N_KV):
            chunks = []
            for c in range(n_chunk):
                st = scores[kvh][c * rc:(c + 1) * rc, :]
                sl = st[:, 0:2 * blk]
                sr = st[:, 2 * blk:]
                p_l = jnp.exp(sl - jnp.max(sl, axis=-1, keepdims=True))
                p_r = jnp.exp(sr - jnp.max(sr, axis=-1, keepdims=True))
                chunks.append(jnp.concatenate([p_l, p_r], axis=1).astype(BF16))
            probs.append(jnp.concatenate(chunks, axis=0))
        outs = []
        for kvh in range(N_KV):
            o2 = _dot(probs[kvh], vbds[kvh])
            o = o2[:, 0:LANES] * (1.0 / o2[:, LANES:])
            outs += [o[0:blk], o[blk:]]
        att = jnp.concatenate(outs, axis=1)
        o_ref[pl.ds(r0, blk), :] = _rms(att, g_ref[...]).astype(BF16)
        return carry

    lax.fori_loop(0, n_sub, body, 0, unroll=2)


def _attn_prompt(q, kv, slopes, sinks, g_att, w_gate, w_up, w_down, w_out):
    n_rows = q.shape[0]
    rows = ROWS_ATT
    n_sub = rows // WINDOW
    n_steps = n_rows // rows
    smem = pl.BlockSpec(memory_space=pltpu.SMEM)
    slab = lambda w: pl.BlockSpec((w.shape[0] // n_steps, w.shape[1]), lambda i: (i, 0))
    ffn_w = (w_gate, w_up, w_down, w_out)
    assert all(w.shape[0] % (16 * n_steps) == 0 for w in ffn_w)
    return pl.pallas_call(
        functools.partial(_attn_prompt_kernel, n_sub=n_sub),
        grid=(n_rows // rows,),
        in_specs=[
            smem, smem,
            pl.BlockSpec((rows, ATT_W), lambda i: (i, 0)),
            pl.BlockSpec((rows, 2 * KV_W), lambda i: (i, 0)),
            pl.BlockSpec((WINDOW, 2 * KV_W), lambda i: (jnp.maximum(i * n_sub - 1, 0), 0)),
            pl.BlockSpec((1, ATT_W), lambda i: (0, 0)),
            *[slab(w) for w in ffn_w],
        ],
        out_specs=[pl.BlockSpec((rows, ATT_W), lambda i: (i, 0)), *[slab(w) for w in ffn_w]],
        out_shape=[jax.ShapeDtypeStruct((n_rows, ATT_W), BF16),
                   *[jax.ShapeDtypeStruct(w.shape, BF16) for w in ffn_w]],
        scratch_shapes=[
            pltpu.VMEM((2, N_KV, 2 * WINDOW, 4 * WINDOW), F32),
            pltpu.VMEM((rows + WINDOW, 2 * N_KV * LANES), BF16),
        ],
        compiler_params=_params(("arbitrary",)),
        name="attn_prompt",
    )(slopes, sinks, q, kv, kv, g_att, *ffn_w)


def _attn_sample_kernel(q_ref, kn_ref, vn_ref, ck_ref, cv_ref, slope_ref, sink_ref, g_ref, perm_ref,
                        o_ref, ko_ref, vo_ref, att_scr, *, n_seq):
    q_sw = _dot(q_ref[...], perm_ref[...])
    r16 = lax.broadcasted_iota(jnp.int32, (N_HEADS, ATT_W), 0)
    c16 = lax.broadcasted_iota(jnp.int32, (N_HEADS, ATT_W), 1)
    own_head = (c16 // HEAD_DIM) == r16
    newest = lax.broadcasted_iota(jnp.int32, (KV_W, WINDOW), 1) == WINDOW - 1
    jj = lax.broadcasted_iota(jnp.int32, (N_HEADS, WINDOW), 1)
    bias = -(slope_ref[...] * (WINDOW - 1 - jj).astype(F32))
    pad = jnp.zeros((LANES - n_seq, KV_W), F32)
    kn_t = jnp.concatenate([kn_ref[...], pad], axis=0).T
    vn_t = jnp.concatenate([vn_ref[...], pad], axis=0).T

    def fold4(v):
        return (v[:, 0:KV_W] + v[:, KV_W:2 * KV_W]) + (v[:, 2 * KV_W:3 * KV_W] + v[:, 3 * KV_W:])

    scores, values = [], []
    for b in range(n_seq):
        kw = jnp.where(newest, jnp.broadcast_to(kn_t[:, b:b + 1], (KV_W, WINDOW)),
                       pltpu.roll(ck_ref[b], WINDOW - 1, 1))
        vw = jnp.where(newest, jnp.broadcast_to(vn_t[:, b:b + 1], (KV_W, WINDOW)),
                       pltpu.roll(cv_ref[b], WINDOW - 1, 1))
        ko_ref[b] = kw
        vo_ref[b] = vw
        qb = jnp.broadcast_to(q_sw[b:b + 1, :], (N_HEADS, ATT_W))
        qbd = fold4(jnp.where(own_head, qb, 0.0))
        scores.append(_dot(qbd.astype(BF16), kw.astype(BF16)))
        values.append(vw.astype(BF16))
    st = jnp.concatenate(scores, axis=0) + jnp.concatenate([bias] * n_seq, axis=0)
    sink = jnp.concatenate([sink_ref[...]] * n_seq, axis=0)
    m = jnp.maximum(jnp.max(st, axis=-1, keepdims=True), sink)
    p = jnp.exp(st - m)
    inv_l = 1.0 / (jnp.sum(p, axis=-1, keepdims=True) + jnp.exp(sink - m))
    pb = p.astype(BF16)
    for b in range(n_seq):
        hs = slice(b * N_HEADS, (b + 1) * N_HEADS)
        o = _dot_nt(pb[hs, :], values[b]) * inv_l[hs, :]
        o4 = jnp.concatenate([o, o, o, o], axis=1)
        att_scr[b:b + 1, :] = jnp.sum(jnp.where(own_head, o4, 0.0), axis=0, keepdims=True)
    normed = _rms(att_scr[...], g_ref[...]).astype(BF16)
    o_ref[...] = _dot_nt(normed, perm_ref[...]).astype(BF16)


def _attn_sample(q, kn, vn, ck, cv, slope_col, sink_col, g_att, perm):
    n = q.shape[0]
    nb = SEQ_BLK
    cache_spec = pl.BlockSpec((nb, KV_W, WINDOW), lambda i: (i, 0, 0))
    row_spec = lambda w: pl.BlockSpec((nb, w), lambda i: (i, 0))
    full = lambda a: pl.BlockSpec(a.shape, lambda i: (0,) * a.ndim)
    return pl.pallas_call(
        functools.partial(_attn_sample_kernel, n_seq=nb),
        grid=(n // nb,),
        in_specs=[row_spec(ATT_W), row_spec(KV_W), row_spec(KV_W), cache_spec, cache_spec,
                  full(slope_col), full(sink_col), full(g_att), full(perm)],
        out_specs=[row_spec(ATT_W), cache_spec, cache_spec],
        out_shape=[
            jax.ShapeDtypeStruct((n, ATT_W), BF16),
            jax.ShapeDtypeStruct(ck.shape, F32),
            jax.ShapeDtypeStruct(cv.shape, F32),
        ],
        scratch_shapes=[pltpu.VMEM((nb, ATT_W), F32)],
        compiler_params=_params(("parallel",)),
        name="attn_sample",
    )(q, kn, vn, ck, cv, slope_col, sink_col, g_att, perm)


def _ssd_prompt_kernel(xb_ref, z_ref, dtb_ref, alog_ref, dskip_ref, g_ref,
                       y_ref, hout_ref, ht_scr, e_scr, tri_scr, yd_scr, *, rows):
    i = pl.program_id(0)
    T = CHUNK

    @pl.when(i == 0)
    def _():
        ht_scr[...] = jnp.zeros_like(ht_scr)
        e_scr[...] = _head_expand_matrix()
        l = lax.broadcasted_iota(jnp.int32, (T, T), 0)
        s = lax.broadcasted_iota(jnp.int32, (T, T), 1)
        tri_scr[...] = jnp.where(s <= l, 1.0, 0.0).astype(BF16)

    a_log2 = -jnp.exp(alog_ref[...]) * LOG2E
    e_mat = e_scr[...]
    tri = tri_scr[...]
    causal = lax.broadcasted_iota(jnp.int32, (T, T), 0) >= lax.broadcasted_iota(jnp.int32, (T, T), 1)
    lo = lax.broadcasted_iota(jnp.int32, (T, LANES), 1) < SSM_P
    gw = SSM_W // SSM_G

    for c in range(rows // T):
        rs = slice(c * T, (c + 1) * T)
        dt = _softplus(xb_ref[rs, CONV_DIM:XBCDT_W] + dtb_ref[...])
        dta = dt * a_log2
        hi, mid, lw = _split3(dta)
        acum = _dot(tri, hi) + _dot(tri, mid) + _dot(tri, lw)
        acum_e = _expand_heads(acum, e_mat)
        dt_e = _expand_heads(dt, e_mat)
        acum_t = acum.T
        xs = xb_ref[rs, 0:SSM_W]
        bm = xb_ref[rs, SSM_W:SSM_W + SSM_G * D_STATE]
        cm = xb_ref[rs, SSM_W + SSM_G * D_STATE:CONV_DIM]
        xdt = xs * dt_e
        last = acum_e[T - 1:T, :]
        xdt_b = xdt.astype(BF16)
        xdec_b = (xdt * jnp.exp2(last - acum_e)).astype(BF16)
        bm_b = bm.astype(BF16)
        cm_b = cm.astype(BF16)
        cbs = [_dot_nt(cm_b[:, g * D_STATE:(g + 1) * D_STATE], bm_b[:, g * D_STATE:(g + 1) * D_STATE])
               for g in range(SSM_G)]
        for pr in range(SSM_H // 2):
            g = (2 * pr) // (SSM_H // SSM_G)
            ws = []
            for t in range(2):
                h = 2 * pr + t
                seg = jnp.broadcast_to(acum[:, h:h + 1], (T, T)) - acum_t[h:h + 1, :]
                ws.append((cbs[g] * jnp.exp2(jnp.where(causal, seg, NEG_INF))).astype(BF16))
            xsl = xdt_b[:, pr * LANES:(pr + 1) * LANES]
            yd_scr[:, pr * LANES:(pr + 1) * LANES] = _dot(jnp.concatenate(ws, axis=1),
                                                          _pair_blockdiag(xsl, xsl, lo, ~lo))
        yoff = []
        for g in range(SSM_G):
            gs = slice(g * gw, (g + 1) * gw)
            ht_g = ht_scr[:, gs]
            yoff.append(_dot(cm_b[:, g * D_STATE:(g + 1) * D_STATE], ht_g.astype(BF16)))
            cst = _dot_tn(bm_b[:, g * D_STATE:(g + 1) * D_STATE], xdec_b[:, gs])
            ht_scr[:, gs] = ht_g * jnp.exp2(last[:, gs]) + cst
        y = yd_scr[...] + jnp.concatenate(yoff, axis=1) * jnp.exp2(acum_e)
        y = y + dskip_ref[...] * xs
        y = y * z_ref[rs, :]
        y_ref[rs, :] = _rms(y, g_ref[...]).astype(BF16)

    @pl.when(i == pl.num_programs(0) - 1)
    def _():
        hout_ref[...] = ht_scr[...].T


def _ssd_prompt(xb, z, dtb, alog, dskip_e, g_ssm):
    n_rows = xb.shape[0]
    rows = ROWS_SSD
    full = lambda a: pl.BlockSpec(a.shape, lambda i: (0,) * a.ndim)
    return pl.pallas_call(
        functools.partial(_ssd_prompt_kernel, rows=rows),
        grid=(n_rows // rows,),
        in_specs=[
            pl.BlockSpec((rows, XBCDT_W), lambda i: (i, 0)),
            pl.BlockSpec((rows, SSM_W), lambda i: (i, 0)),
            full(dtb), full(alog), full(dskip_e), full(g_ssm),
        ],
        out_specs=[
            pl.BlockSpec((rows, SSM_W), lambda i: (i, 0)),
            pl.BlockSpec((SSM_W, D_STATE), lambda i: (0, 0)),
        ],
        out_shape=[
            jax.ShapeDtypeStruct((n_rows, SSM_W), BF16),
            jax.ShapeDtypeStruct((SSM_W, D_STATE), F32),
        ],
        scratch_shapes=[
            pltpu.VMEM((D_STATE, SSM_W), F32),
            pltpu.VMEM((LANES, SSM_W), BF16),
            pltpu.VMEM((CHUNK, CHUNK), BF16),
            pltpu.VMEM((CHUNK, SSM_W), F32),
        ],
        compiler_params=_params(("arbitrary",)),
        name="ssd_prompt",
    )(xb, z, dtb, alog, dskip_e, g_ssm)


def _ssd_sample_kernel(xb_ref, sconv_ref, z_ref, h0_ref, cw_ref, cb_ref, dtb_ref, alog_ref, dskip_ref, g_ref,
                       y_ref, conv_ref, hout_ref, yoff_scr, *, n_seq):
    gw = SSM_W // SSM_G
    gn = SSM_G * D_STATE
    x_new = xb_ref[:, 0:CONV_DIM]
    taps = [sconv_ref[:, k * CONV_DIM:(k + 1) * CONV_DIM] for k in range(CONV_W - 1)] + [x_new]
    acc = cb_ref[...]
    for k in range(CONV_W):
        acc = acc + taps[k] * cw_ref[k:k + 1, :]
    for k in range(1, CONV_W):
        conv_ref[:, (k - 1) * CONV_DIM:k * CONV_DIM] = taps[k]
    xc = _silu(acc)
    xs = xc[:, 0:SSM_W]
    bm = xc[:, SSM_W:SSM_W + gn]
    cm = xc[:, SSM_W + gn:CONV_DIM]

    e_mat = _head_expand_matrix()
    dt = _softplus(xb_ref[:, CONV_DIM:XBCDT_W] + dtb_ref[...])
    dt_e = _expand_heads(dt, e_mat)
    dec_e = jnp.exp(_expand_heads(dt * (-jnp.exp(alog_ref[...])), e_mat))
    xdt = xs * dt_e

    lane_w = lax.broadcasted_iota(jnp.int32, (n_seq, SSM_W), 1)
    first_grp = lane_w < gw
    cbv = [jnp.sum(cm[:, g * D_STATE:(g + 1) * D_STATE] * bm[:, g * D_STATE:(g + 1) * D_STATE],
                   axis=-1, keepdims=True) for g in range(SSM_G)]
    cb_e = jnp.where(first_grp, cbv[0], cbv[1])

    pad = jnp.zeros((LANES - n_seq, SSM_W), F32)
    xdt_t = jnp.concatenate([xdt, pad], axis=0).T
    dec_t = jnp.concatenate([dec_e, pad], axis=0).T
    sub8 = lax.broadcasted_iota(jnp.int32, (8, D_STATE), 0)
    lane_r = lax.broadcasted_iota(jnp.int32, (8, SSM_W), 1)

    for b in range(n_seq):
        h0 = h0_ref[b]
        c_row = jnp.broadcast_to(cm[b:b + 1, :], (8, gn))
        c8 = jnp.where(sub8 == 0, c_row[:, 0:D_STATE], jnp.where(sub8 == 1, c_row[:, D_STATE:], 0.0))
        r = _dot_nt(c8.astype(BF16), h0.astype(BF16))
        yoff_scr[b:b + 1, :] = jnp.where(lane_r[0:1] < gw, r[0:1, :], r[1:2, :])
        b_row = bm[b:b + 1, :]
        for g in range(SSM_G):
            rs = slice(g * gw, (g + 1) * gw)
            dcol = jnp.broadcast_to(dec_t[rs, b:b + 1], (gw, D_STATE))
            xcol = jnp.broadcast_to(xdt_t[rs, b:b + 1], (gw, D_STATE))
            hout_ref[b, rs, :] = h0[rs, :] * dcol + xcol * b_row[:, g * D_STATE:(g + 1) * D_STATE]

    y = cb_e * xdt + yoff_scr[...] * dec_e
    y = y + dskip_ref[...] * xs
    y = y * _silu(z_ref[...])
    y_ref[...] = _rms(y, g_ref[...]).astype(BF16)


def _ssd_sample(xb, sconv, z, h0, cw8, cb, dtb, alog, dskip_e, g_ssm):
    n = xb.shape[0]
    nb = SEQ_BLK
    row_spec = lambda w: pl.BlockSpec((nb, w), lambda i: (i, 0))
    st_spec = pl.BlockSpec((nb, SSM_W, D_STATE), lambda i: (i, 0, 0))
    full = lambda a: pl.BlockSpec(a.shape, lambda i: (0,) * a.ndim)
    return pl.pallas_call(
        functools.partial(_ssd_sample_kernel, n_seq=nb),
        grid=(n // nb,),
        in_specs=[row_spec(XBCDT_W), row_spec((CONV_W - 1) * CONV_DIM), row_spec(SSM_W), st_spec,
                  full(cw8), full(cb), full(dtb), full(alog), full(dskip_e), full(g_ssm)],
        out_specs=[row_spec(SSM_W), row_spec((CONV_W - 1) * CONV_DIM), st_spec],
        out_shape=[
            jax.ShapeDtypeStruct((n, SSM_W), BF16),
            jax.ShapeDtypeStruct((n, (CONV_W - 1) * CONV_DIM), F32),
            jax.ShapeDtypeStruct(h0.shape, F32),
        ],
        scratch_shapes=[pltpu.VMEM((nb, SSM_W), F32)],
        compiler_params=_params(("parallel",)),
        name="ssd_sample",
    )(xb, sconv, z, h0, cw8, cb, dtb, alog, dskip_e, g_ssm)


def _outproj_kernel(att_ref, ssm_ref, x_ref, woa_ref, wos_ref, gpost_ref, gt1_ref, gpre_ref, sh2_ref, sc2_ref,
                    x1_ref, u2_ref, *, per_row, rows):
    sub = min(SUB_OUT, rows)
    for r0 in range(0, rows, sub):
        rs = slice(r0, r0 + sub)
        mod = lambda ref: ref[rs, :] if per_row else ref[0:1, :]
        mix = _dot(att_ref[rs, :], woa_ref[...]) + _dot(ssm_ref[rs, :], wos_ref[...])
        x1 = x_ref[rs, :] + _rms(mix, gpost_ref[...] * mod(gt1_ref))
        x1_ref[rs, :] = x1
        u2 = _rms(x1, gpre_ref[...] * (1.0 + mod(sc2_ref))) + mod(sh2_ref)
        u2_ref[rs, :] = u2.astype(BF16)


def _outproj(att, ssm, x, mod, per_row, w_out, g_post, g_pre):
    half = lambda r: pl.BlockSpec((ATT_W, D_MODEL), lambda i: (r, 0), pipeline_mode=pl.Buffered(1))
    n_rows = x.shape[0]
    rows = min(ROWS_OUT, n_rows)
    row_spec = lambda w: pl.BlockSpec((rows, w), lambda i: (i, 0))
    return pl.pallas_call(
        functools.partial(_outproj_kernel, per_row=per_row, rows=rows),
        grid=(n_rows // rows,),
        in_specs=[
            row_spec(ATT_W), row_spec(SSM_W), row_spec(D_MODEL),
            half(0), half(1),
            _const_spec((1, D_MODEL)),
            _mod_spec(per_row, rows, 2),
            _const_spec((1, D_MODEL)),
            _mod_spec(per_row, rows, 3),
            _mod_spec(per_row, rows, 4),
        ],
        out_specs=[row_spec(D_MODEL), row_spec(D_MODEL)],
        out_shape=[
            jax.ShapeDtypeStruct((n_rows, D_MODEL), F32),
            jax.ShapeDtypeStruct((n_rows, D_MODEL), BF16),
        ],
        compiler_params=_params(("parallel",)),
        name="outproj",
    )(att, ssm, x, w_out, w_out, g_post, mod, g_pre, mod, mod)


def _ffn_kernel(u_ref, x1_hbm, wg_ref, wu_ref, wd_ref, gpost_ref, gt2_ref, us_ref, x1s_ref, gt2s_ref,
                y_ref, ys_ref, x1_buf, x1_sem, *, rows):
    i = pl.program_id(0)
    j = pl.program_id(1)
    x1_copy = pltpu.make_async_copy(x1_hbm.at[pl.ds(i * rows, rows), :], x1_buf, x1_sem)
    last = pl.num_programs(1) - 1

    def ffn_rows(u, first, final, acc_prev, resid, gt2):
        hid = (_silu(_dot(u, wg_ref[...])) * _dot(u, wu_ref[...])).astype(BF16)
        acc = _dot(hid, wd_ref[...])
        if not first:
            acc = acc_prev() + acc
        if final:
            acc = resid() + gt2() * _rms(acc, gpost_ref[...])
        return acc

    def d_ff_slice(first, final):
        sub = min(SUB_FFN, rows)
        for r0 in range(0, rows, sub):
            rs = slice(r0, r0 + sub)
            y_ref[rs, :] = ffn_rows(u_ref[rs, :], first, final, lambda: y_ref[rs, :],
                                    lambda: x1_buf[rs, :], lambda: gt2_ref[0:1, :])

        @pl.when(i == pl.num_programs(0) - 1)
        def _():
            ys_ref[...] = ffn_rows(us_ref[...], first, final, lambda: ys_ref[...],
                                   lambda: x1s_ref[...], lambda: gt2s_ref[...])

    @pl.when(j == 0)
    def _():
        x1_copy.start()
        d_ff_slice(True, False)

    @pl.when(jnp.logical_and(j > 0, j < last))
    def _():
        d_ff_slice(False, False)

    @pl.when(j == last)
    def _():
        x1_copy.wait()
        d_ff_slice(False, True)


def _ffn(u2, x1, mod_p, u2_s, x1_s, mod_s, wg, wu, wd, g_post):
    n_rows = x1.shape[0]
    n_s = x1_s.shape[0]
    rows = ROWS_FFN
    fb = FF_BLK
    assert D_FF // fb >= 2 and n_rows % rows == 0
    whole = lambda r, c: pl.BlockSpec((r, D_MODEL), lambda i, j: (0, c))
    return pl.pallas_call(
        functools.partial(_ffn_kernel, rows=rows),
        grid=(n_rows // rows, D_FF // fb),
        in_specs=[
            pl.BlockSpec((rows, D_MODEL), lambda i, j: (i, 0)),
            pl.BlockSpec(memory_space=pl.ANY),
            pl.BlockSpec((D_MODEL, fb), lambda i, j: (0, j)),
            pl.BlockSpec((D_MODEL, fb), lambda i, j: (0, j)),
            pl.BlockSpec((fb, D_MODEL), lambda i, j: (j, 0)),
            whole(1, 0),
            _mod_spec(False, rows, 5),
            whole(n_s, 0), whole(n_s, 0), whole(n_s, 5),
        ],
        out_specs=[pl.BlockSpec((rows, D_MODEL), lambda i, j: (i, 0)), whole(n_s, 0)],
        out_shape=[jax.ShapeDtypeStruct((n_rows, D_MODEL), F32), jax.ShapeDtypeStruct((n_s, D_MODEL), F32)],
        scratch_shapes=[pltpu.VMEM((rows, D_MODEL), F32), pltpu.SemaphoreType.DMA],
        compiler_params=_params(("arbitrary", "arbitrary"), VMEM_LIMIT_FFN),
        name="ffn",
    )(u2, x1, wg, wu, wd, g_post, mod_p, u2_s, x1_s, mod_s)


def _alibi_slopes():
    return (2.0 ** (-8.0 * np.arange(1, N_HEADS + 1) / N_HEADS)).astype(np.float32)


def kernel(x_prompt, x_sample, cache_k, cache_v, state_conv, state_ssm, c_prompt, c_sample, w_ada, b_ada, g_pre_mix, g_post_mix, w_in, attn_sinks, g_attn_out, conv_w, conv_b, dt_bias, a_log, d_skip, g_ssm_out, w_out, g_pre_ffn, g_post_ffn, w_gate, w_up, w_down):
    assert w_ada.shape[0] == 1, "one layer"
    n_s = x_sample.shape[0]
    row = lambda v: v.reshape(1, -1)

    g_att = row(g_attn_out[0])
    swapped_cols = np.arange(ATT_W).reshape(N_KV, Q_PER_KV, HEAD_DIM).swapaxes(0, 1).reshape(-1)
    perm = np.zeros((ATT_W, ATT_W), np.float32)
    perm[swapped_cols, np.arange(ATT_W)] = 1.0
    perm = jnp.asarray(perm, BF16)
    g_att_s = row(g_attn_out[0].reshape(N_KV, Q_PER_KV, HEAD_DIM).swapaxes(0, 1))
    slopes_np = _alibi_slopes()
    slopes = jnp.asarray(slopes_np)
    slopes_s = jnp.asarray(slopes_np.reshape(N_KV, Q_PER_KV).T.reshape(N_HEADS, 1))
    sinks = attn_sinks[0]
    sinks_s = sinks.reshape(N_KV, Q_PER_KV).T.reshape(N_HEADS, 1)
    cw8 = jnp.pad(conv_w[0], ((0, 8 - CONV_W), (0, 0)))
    cb = row(conv_b[0])
    dtb = jnp.pad(row(dt_bias[0]), ((0, 0), (0, LANES - SSM_H)))
    alog = jnp.pad(row(a_log[0]), ((0, 0), (0, LANES - SSM_H)))
    dskip_e = row(jnp.repeat(d_skip[0], SSM_P))
    g_ssm = row(g_ssm_out[0])

    c_p8 = jnp.pad(c_prompt, ((0, 8 - c_prompt.shape[0]), (0, 0)))
    mod_s, mod_p, w_all = _prep(c_sample, c_p8, w_ada[0], row(b_ada[0]), jnp.transpose(w_in[0]))

    xp = x_prompt[0]
    q, kv, z, xb, kv_last, x_tail = _inproj(xp, row(g_pre_mix[0]), mod_p, False, w_all, cw8, cb, True)
    att, wg, wu, wd, wo = _attn_prompt(q, kv, slopes, sinks, g_att, w_gate[0], w_up[0], w_down[0], w_out[0])
    ssm, h_p = _ssd_prompt(xb, z, dtb, alog, dskip_e, g_ssm)
    x1, u2 = _outproj(att, ssm, xp, mod_p, False, wo, row(g_post_mix[0]), row(g_pre_ffn[0]))

    keys_minor = lambda c: jnp.transpose(c, (0, 2, 3, 1)).reshape(n_s, KV_W, WINDOW)
    keys_major = lambda c: jnp.transpose(c.reshape(n_s, N_KV, HEAD_DIM, WINDOW), (0, 3, 1, 2))[None]
    xs_ = x_sample[:, 0, :]
    q_s, _, z_s, xb_s, kv_new, _ = _inproj(xs_, row(g_pre_mix[0]), mod_s, True, w_all, cw8, cb, False)
    att_s, k_s, v_s = _attn_sample(
        q_s, kv_new[:, :KV_W], kv_new[:, KV_W:],
        keys_minor(cache_k[0]), keys_minor(cache_v[0]), slopes_s, sinks_s, g_att_s, perm)
    ssm_s, conv_s, h_s = _ssd_sample(
        xb_s, state_conv[0].reshape(n_s, (CONV_W - 1) * CONV_DIM), z_s,
        state_ssm[0].reshape(n_s, SSM_W, D_STATE), cw8, cb, dtb, alog, dskip_e, g_ssm)
    x1_s, u2_s = _outproj(att_s, ssm_s, xs_, mod_s, True, wo, row(g_post_mix[0]), row(g_pre_ffn[0]))
    y_p, y_s = _ffn(u2, x1, mod_p, u2_s, x1_s, mod_s, wg, wu, wd, row(g_post_ffn[0]))

    return (
        y_p[None],
        y_s[:, None, :],
        kv_last[:, :KV_W].reshape(1, 1, WINDOW, N_KV, HEAD_DIM),
        kv_last[:, KV_W:].reshape(1, 1, WINDOW, N_KV, HEAD_DIM),
        x_tail[8 - (CONV_W - 1):].reshape(1, 1, CONV_W - 1, CONV_DIM),
        h_p.reshape(1, 1, SSM_H, SSM_P, D_STATE),
        keys_major(k_s),
        keys_major(v_s),
        conv_s.reshape(1, n_s, CONV_W - 1, CONV_DIM),
        h_s.reshape(1, n_s, SSM_H, SSM_P, D_STATE),
    )
```

```python
import functools

import numpy as np
import jax
import jax.numpy as jnp
from jax import lax
from jax.experimental import pallas as pl
from jax.experimental.pallas import tpu as pltpu

F32 = jnp.float32
BF16 = jnp.bfloat16

D_MODEL = 2048
ATT_W = 1024
HEAD_DIM = 64
N_HEADS = 16
N_KV = 4
Q_PER_KV = 4
KV_W = N_KV * HEAD_DIM
WINDOW = 128
SSM_W = 1024
SSM_P = 64
SSM_H = 16
SSM_G = 2
D_STATE = 128
CONV_W = 4
CONV_DIM = SSM_W + 2 * SSM_G * D_STATE
XBCDT_W = CONV_DIM + 128
D_FF = 5632
EPS = 1e-6
CHUNK = 128
NEG_INF = float("-inf")
LOG2E = 1.4426950408889634

VMEM_LIMIT = 56 * 1024 * 1024
VMEM_LIMIT_FFN = 60 * 1024 * 1024
LANES = 128

ROWS_IN = 512
ROWS_ATT = 512
ROWS_SSD = 512
ROWS_OUT = 512
SUB_OUT = 128
ROWS_FFN = 1024
SUB_FFN = 512
FF_BLK = 512
SEQ_BLK = 8


def _params(sem, vmem=VMEM_LIMIT):
    return pltpu.CompilerParams(dimension_semantics=sem, vmem_limit_bytes=vmem)


def _silu(v):
    h = 0.5 * v
    return h + h * jnp.tanh(h)


def _softplus(v):
    return jnp.maximum(v, 0.0) + jnp.log1p(jnp.exp(-jnp.abs(v)))


def _rms(v, g):
    return v * lax.rsqrt(jnp.mean(v * v, axis=-1, keepdims=True) + EPS) * g


def _split3(v):
    hi = v.astype(BF16)
    r1 = v - hi.astype(F32)
    mid = r1.astype(BF16)
    lo = (r1 - mid.astype(F32)).astype(BF16)
    return hi, mid, lo


def _dot(a, b):
    return jnp.dot(a, b, preferred_element_type=F32)


def _dot_nt(a, b):
    return lax.dot_general(a, b, (((1,), (1,)), ((), ())), preferred_element_type=F32)


def _dot_tn(a, b):
    return lax.dot_general(a, b, (((0,), (0,)), ((), ())), preferred_element_type=F32)


def _expand_heads(v, e_mat):
    hi, mid, lo = _split3(v)
    return _dot(hi, e_mat) + _dot(mid, e_mat) + _dot(lo, e_mat)


def _head_expand_matrix():
    k = lax.broadcasted_iota(jnp.int32, (LANES, SSM_W), 0)
    c = lax.broadcasted_iota(jnp.int32, (LANES, SSM_W), 1)
    return jnp.where((c // SSM_P) == k, 1.0, 0.0).astype(BF16)


def _mod_spec(col):
    return pl.BlockSpec((8, D_MODEL), lambda i, *_: (0, col))


WS_BLK = 256
W_ALL_COLS = 4608
COL_X, COL_Q, COL_Z, COL_KV = 0, 2048, 3072, 4096
N_WS_SRC = -(-(ATT_W + 2 * KV_W + SSM_W + CONV_DIM + SSM_H) // WS_BLK)


def _ws_dst_block(s):
    q_end = ATT_W // WS_BLK
    kv_end = q_end + 2 * KV_W // WS_BLK
    z_end = kv_end + SSM_W // WS_BLK
    return jnp.where(s < q_end, COL_Q // WS_BLK + s,
                     jnp.where(s < kv_end, COL_KV // WS_BLK + s - q_end,
                               jnp.where(s < z_end, COL_Z // WS_BLK + s - kv_end,
                                         COL_X // WS_BLK + s - z_end)))


ADA_BLK = 768


def _prep_kernel(cs_ref, cp_ref, wa_ref, b_ref, wt_ref, ms_ref, mp_ref, wall_ref):
    s = pl.program_id(0)
    w = wa_ref[...].astype(BF16)
    b = b_ref[...]
    ms_ref[...] = _dot(_silu(cs_ref[...]).astype(BF16), w) + b
    mp_ref[...] = _dot(_silu(cp_ref[...]).astype(BF16), w) + b
    n_valid = jnp.where(s < N_WS_SRC - 1, WS_BLK, jnp.where(s == N_WS_SRC - 1, SSM_H, 0))
    lane = lax.broadcasted_iota(jnp.int32, wall_ref.shape, 1)
    wall_ref[...] = jnp.where(lane < n_valid, wt_ref[...].T, 0.0).astype(BF16)


def _prep(c_s, c_p8, w_ada, b_ada, w_in_t):
    k_dim = w_in_t.shape[1]
    n = w_ada.shape[1]
    n_ada = n // ADA_BLK
    assert n % ADA_BLK == 0 and n_ada <= N_WS_SRC + 1
    ada_col = lambda s: (0, jnp.minimum(s, n_ada - 1))
    return pl.pallas_call(
        _prep_kernel,
        grid=(N_WS_SRC + 1,),
        in_specs=[
            pl.BlockSpec(c_s.shape, lambda s: (0, 0)),
            pl.BlockSpec(c_p8.shape, lambda s: (0, 0)),
            pl.BlockSpec((D_MODEL, ADA_BLK), ada_col),
            pl.BlockSpec((1, ADA_BLK), ada_col),
            pl.BlockSpec((WS_BLK, k_dim), lambda s: (jnp.minimum(s, N_WS_SRC - 1), 0)),
        ],
        out_specs=[
            pl.BlockSpec((c_s.shape[0], ADA_BLK), ada_col),
            pl.BlockSpec((c_p8.shape[0], ADA_BLK), ada_col),
            pl.BlockSpec((k_dim, WS_BLK), lambda s: (0, _ws_dst_block(s))),
        ],
        out_shape=[
            jax.ShapeDtypeStruct((c_s.shape[0], n), F32),
            jax.ShapeDtypeStruct((c_p8.shape[0], n), F32),
            jax.ShapeDtypeStruct((k_dim, W_ALL_COLS), BF16),
        ],
        compiler_params=_params(("arbitrary",)),
        name="prep",
    )(c_s, c_p8, w_ada, b_ada, w_in_t)


def _modulated_norm(x, g, shift, scale):
    return (_rms(x, g) * (1.0 + scale) + shift).astype(BF16)


def _inproj_kernel(x_ref, g_ref, sh_ref, sc_ref, wq_ref, wkv_ref, wz_ref, wx_ref, cw_ref, cb_ref,
                   xs_ref, shs_ref, scs_ref,
                   q_ref, kv_ref, z_ref, xb_ref, kvlast_ref, xtail_ref, qs_ref, kvs_ref, zs_ref, xbs_ref,
                   xp_scr, *, rows):
    i = pl.program_id(0)
    tile = 2 * LANES

    @pl.when(i == 0)
    def _():
        xp_scr[:, rows:rows + 8, :] = jnp.zeros((CONV_DIM // LANES, 8, LANES), F32)

    u = _modulated_norm(x_ref[...], g_ref[...], sh_ref[0:1, :], sc_ref[0:1, :])

    def conv_slab(s, xs):
        ls = slice(s * LANES, (s + 1) * LANES)
        xtail_ref[:, ls] = xs[rows - 8:, :]
        xp_scr[s, 0:8, :] = xp_scr[s, rows:rows + 8, :]
        xp_scr[s, 8:, :] = xs
        acc = cb_ref[:, ls] + xp_scr[s, pl.ds(8 - (CONV_W - 1), rows), :] * cw_ref[0:1, ls]
        for k in range(1, CONV_W):
            acc = acc + xp_scr[s, pl.ds(8 - (CONV_W - 1) + k, rows), :] * cw_ref[k:k + 1, ls]
        xb_ref[:, ls] = _silu(acc)

    def post_z(c0, c1):
        def post(r):
            z_ref[:, c0:c1] = _silu(r)
        return post

    def post_q(c0, c1):
        def post(r):
            q_ref[:, c0:c1] = (r * (HEAD_DIM ** -0.5)).astype(BF16)
        return post

    def post_kv(c0, c1):
        def post(r):
            kv_ref[:, c0:c1] = r.astype(BF16)
            kvlast_ref[:, c0:c1] = r[rows - WINDOW:, :]
        return post

    xb = _dot(u, wx_ref[...])
    xb_ref[:, CONV_DIM:] = xb[:, CONV_DIM:]
    slabs = [functools.partial(conv_slab, s, xb[:, s * LANES:(s + 1) * LANES]) for s in range(CONV_DIM // LANES)]
    jobs = []
    for w_ref, width, post_of in ((wz_ref, SSM_W, post_z), (wq_ref, ATT_W, post_q), (wkv_ref, 2 * KV_W, post_kv)):
        for c0 in range(0, width, tile):
            jobs.append((w_ref, c0, c0 + tile, post_of(c0, c0 + tile)))
    pending = None
    for n, (w_ref, c0, c1, post) in enumerate(jobs):
        r = _dot(u, w_ref[:, c0:c1])
        for slab in slabs[n * len(slabs) // len(jobs):(n + 1) * len(slabs) // len(jobs)]:
            slab()
        if pending is not None:
            pending[0](pending[1])
        pending = (post, r)
    pending[0](pending[1])

    @pl.when(i == pl.num_programs(0) - 1)
    def _():
        us = _modulated_norm(xs_ref[...], g_ref[...], shs_ref[...], scs_ref[...])
        qs_ref[...] = (_dot(us, wq_ref[...]) * (HEAD_DIM ** -0.5)).astype(BF16)
        kvs_ref[...] = _dot(us, wkv_ref[...])
        zs_ref[...] = _dot(us, wz_ref[...])
        xbs_ref[...] = _dot(us, wx_ref[...])


def _const_spec(shape):
    return pl.BlockSpec(shape, lambda i, *_: (0,) * len(shape), pipeline_mode=pl.Buffered(1))


def _inproj(x, x_s, g, mod_p, mod_s, w_all, cw8, cb):
    n_rows = x.shape[0]
    n_s = x_s.shape[0]
    rows = ROWS_IN
    row_spec = lambda w: pl.BlockSpec((rows, w), lambda i: (i, 0))
    seg_spec = lambda width, col: pl.BlockSpec((D_MODEL, width), lambda i: (0, col // width),
                                               pipeline_mode=pl.Buffered(1))
    sample_in = lambda col: pl.BlockSpec((n_s, D_MODEL), lambda i: (0, col), pipeline_mode=pl.Buffered(1))
    sample_out = lambda w: pl.BlockSpec((n_s, w), lambda i: (0, 0))
    return pl.pallas_call(
        functools.partial(_inproj_kernel, rows=rows),
        grid=(n_rows // rows,),
        in_specs=[
            row_spec(D_MODEL),
            _const_spec((1, D_MODEL)),
            _mod_spec(0),
            _mod_spec(1),
            seg_spec(ATT_W, COL_Q), seg_spec(2 * KV_W, COL_KV), seg_spec(SSM_W, COL_Z), seg_spec(XBCDT_W, COL_X),
            _const_spec(cw8.shape), _const_spec(cb.shape),
            sample_in(0), sample_in(0), sample_in(1),
        ],
        out_specs=[
            row_spec(ATT_W), row_spec(2 * KV_W), row_spec(SSM_W), row_spec(XBCDT_W),
            pl.BlockSpec((WINDOW, 2 * KV_W), lambda i: (0, 0)),
            pl.BlockSpec((8, CONV_DIM), lambda i: (0, 0)),
            sample_out(ATT_W), sample_out(2 * KV_W), sample_out(SSM_W), sample_out(XBCDT_W),
        ],
        out_shape=[
            jax.ShapeDtypeStruct((n_rows, ATT_W), BF16),
            jax.ShapeDtypeStruct((n_rows, 2 * KV_W), BF16),
            jax.ShapeDtypeStruct((n_rows, SSM_W), F32),
            jax.ShapeDtypeStruct((n_rows, XBCDT_W), F32),
            jax.ShapeDtypeStruct((WINDOW, 2 * KV_W), F32),
            jax.ShapeDtypeStruct((8, CONV_DIM), F32),
            jax.ShapeDtypeStruct((n_s, ATT_W), BF16),
            jax.ShapeDtypeStruct((n_s, 2 * KV_W), F32),
            jax.ShapeDtypeStruct((n_s, SSM_W), F32),
            jax.ShapeDtypeStruct((n_s, XBCDT_W), F32),
        ],
        scratch_shapes=[pltpu.VMEM((CONV_DIM // LANES, rows + 8, LANES), F32)],
        compiler_params=_params(("arbitrary",), VMEM_LIMIT_FFN),
        name="inproj",
    )(x, g, mod_p, mod_p, w_all, w_all, w_all, w_all, cw8, cb, x_s, mod_s, mod_s)


def _pair_blockdiag(lo_src, hi_src, keep_lo, keep_hi):
    zero = jnp.zeros_like(lo_src)
    return jnp.concatenate([jnp.where(keep_lo, lo_src, zero), jnp.where(keep_hi, hi_src, zero)], axis=0)


def _attn_prompt_kernel(slopes_ref, sinks_ref, q_ref, kvc_ref, kvp_ref, g_ref, wg_ref, wu_ref, wd_ref, wo_ref,
                        o_ref, wg_o, wu_o, wd_o, wo_o, bias_scr, kvdup_scr, *, n_sub):
    i = pl.program_id(0)
    blk = WINDOW

    wg_o[...] = wg_ref[...].astype(BF16)
    wu_o[...] = wu_ref[...].astype(BF16)
    wd_o[...] = wd_ref[...].astype(BF16)
    wo_o[...] = wo_ref[...].astype(BF16)

    @pl.when(i == 0)
    def _():
        a = lax.broadcasted_iota(jnp.int32, (blk, 2 * blk), 0)
        j = lax.broadcasted_iota(jnp.int32, (blk, 2 * blk), 1)
        dist = a + blk - j
        valid = (dist >= 0) & (dist < WINDOW)
        distf = dist.astype(F32)
        for h in range(N_HEADS):
            kvh, g = divmod(h, Q_PER_KV)
            pr, t = divmod(g, 2)
            b = jnp.where(valid, -(slopes_ref[h] * distf), NEG_INF)
            b = jnp.where(j == 0, sinks_ref[h], b)
            rs = slice(pr * blk, (pr + 1) * blk)
            cs = slice(t * 2 * blk, (t + 1) * 2 * blk)
            bias_scr[1, kvh, rs, cs] = b
            bias_scr[0, kvh, rs, cs] = jnp.where((j >= blk) | (j == 0), b, NEG_INF)

    for src_ref, r_lo, r_hi in ((kvp_ref, 0, blk), (kvc_ref, blk, kvdup_scr.shape[0])):
        lo_half = lax.broadcasted_iota(jnp.int32, (r_hi - r_lo, LANES), 1) < HEAD_DIM
        for part in range(2):
            for s in range(2):
                c_in = part * KV_W + s * LANES
                c_out = part * N_KV * LANES + 2 * s * LANES
                x = src_ref[:, c_in:c_in + LANES].astype(F32)
                xr = pltpu.roll(x, HEAD_DIM, 1)
                kvdup_scr[r_lo:r_hi, c_out:c_out + LANES] = jnp.where(lo_half, x, xr).astype(BF16)
                kvdup_scr[r_lo:r_hi, c_out + LANES:c_out + 2 * LANES] = jnp.where(lo_half, xr, x).astype(BF16)

    lane = lax.broadcasted_iota(jnp.int32, (2 * blk, LANES), 1)
    not_sink = lax.broadcasted_iota(jnp.int32, (2 * blk, LANES), 0) != 0
    keep_lo = (lane < HEAD_DIM) & not_sink
    keep_hi = (lane >= HEAD_DIM) & not_sink
    ones_cols = jnp.concatenate([jnp.where(lane < HEAD_DIM, 1.0, 0.0),
                                 jnp.where(lane >= HEAD_DIM, 1.0, 0.0)], axis=0).astype(BF16)
    n_chunk = 4
    rc = 2 * blk // n_chunk

    def body(b, carry):
        r0 = pl.multiple_of(b * blk, blk)
        variant = jnp.where(jnp.logical_and(i == 0, b == 0), 0, 1)
        vbds, scores = [], []
        for kvh in range(N_KV):
            c0 = kvh * Q_PER_KV * HEAD_DIM
            kd = kvdup_scr[pl.ds(r0, 2 * blk), kvh * LANES:(kvh + 1) * LANES]
            vd = kvdup_scr[pl.ds(r0, 2 * blk), (N_KV + kvh) * LANES:(N_KV + kvh + 1) * LANES]
            kbd = _pair_blockdiag(kd, kd, keep_lo, keep_hi)
            vbds.append(jnp.concatenate([_pair_blockdiag(vd, vd, keep_lo, keep_hi), ones_cols], axis=1))
            q2 = jnp.concatenate([q_ref[pl.ds(r0, blk), c0:c0 + LANES],
                                  q_ref[pl.ds(r0, blk), c0 + LANES:c0 + 2 * LANES]], axis=0)
            scores.append(_dot_nt(q2, kbd) + bias_scr[variant, kvh])
        probs = []
        for kvh in range(N_KV):
            chunks = []
            for c in range(n_chunk):
                st = scores[kvh][c * rc:(c + 1) * rc, :]
                sl = st[:, 0:2 * blk]
                sr = st[:, 2 * blk:]
                p_l = jnp.exp(sl - jnp.max(sl, axis=-1, keepdims=True))
                p_r = jnp.exp(sr - jnp.max(sr, axis=-1, keepdims=True))
                chunks.append(jnp.concatenate([p_l, p_r], axis=1).astype(BF16))
            probs.append(jnp.concatenate(chunks, axis=0))
        outs = []
        for kvh in range(N_KV):
            o2 = _dot(probs[kvh], vbds[kvh])
            o = o2[:, 0:LANES] * (1.0 / o2[:, LANES:])
            outs += [o[0:blk], o[blk:]]
        att = jnp.concatenate(outs, axis=1)
        o_ref[pl.ds(r0, blk), :] = _rms(att, g_ref[...]).astype(BF16)
        return carry

    lax.fori_loop(0, n_sub, body, 0, unroll=2)


def _attn_prompt(q, kv, slopes, sinks, g_att, w_gate, w_up, w_down, w_out):
    n_rows = q.shape[0]
    rows = ROWS_ATT
    n_sub = rows // WINDOW
    n_steps = n_rows // rows
    smem = pl.BlockSpec(memory_space=pltpu.SMEM)
    slab = lambda w: pl.BlockSpec((w.shape[0] // n_steps, w.shape[1]), lambda i: (i, 0))
    ffn_w = (w_gate, w_up, w_down, w_out)
    assert all(w.shape[0] % (16 * n_steps) == 0 for w in ffn_w)
    return pl.pallas_call(
        functools.partial(_attn_prompt_kernel, n_sub=n_sub),
        grid=(n_rows // rows,),
        in_specs=[
            smem, smem,
            pl.BlockSpec((rows, ATT_W), lambda i: (i, 0)),
            pl.BlockSpec((rows, 2 * KV_W), lambda i: (i, 0)),
            pl.BlockSpec((WINDOW, 2 * KV_W), lambda i: (jnp.maximum(i * n_sub - 1, 0), 0)),
            pl.BlockSpec((1, ATT_W), lambda i: (0, 0)),
            *[slab(w) for w in ffn_w],
        ],
        out_specs=[pl.BlockSpec((rows, ATT_W), lambda i: (i, 0)), *[slab(w) for w in ffn_w]],
        out_shape=[jax.ShapeDtypeStruct((n_rows, ATT_W), BF16),
                   *[jax.ShapeDtypeStruct(w.shape, BF16) for w in ffn_w]],
        scratch_shapes=[
            pltpu.VMEM((2, N_KV, 2 * WINDOW, 4 * WINDOW), F32),
            pltpu.VMEM((rows + WINDOW, 2 * N_KV * LANES), BF16),
        ],
        compiler_params=_params(("arbitrary",)),
        name="attn_prompt",
    )(slopes, sinks, q, kv, kv, g_att, *ffn_w)


def _attn_sample_kernel(q_ref, kn_ref, vn_ref, ck_ref, cv_ref, slope_ref, sink_ref, g_ref, perm_ref,
                        o_ref, ko_ref, vo_ref, att_scr, *, n_seq):
    q_sw = _dot(q_ref[...], perm_ref[...])
    r16 = lax.broadcasted_iota(jnp.int32, (N_HEADS, ATT_W), 0)
    c16 = lax.broadcasted_iota(jnp.int32, (N_HEADS, ATT_W), 1)
    own_head = (c16 // HEAD_DIM) == r16
    newest = lax.broadcasted_iota(jnp.int32, (KV_W, WINDOW), 1) == WINDOW - 1
    jj = lax.broadcasted_iota(jnp.int32, (N_HEADS, WINDOW), 1)
    bias = -(slope_ref[...] * (WINDOW - 1 - jj).astype(F32))
    pad = jnp.zeros((LANES - n_seq, KV_W), F32)
    kn_t = jnp.concatenate([kn_ref[...], pad], axis=0).T
    vn_t = jnp.concatenate([vn_ref[...], pad], axis=0).T

    def fold4(v):
        return (v[:, 0:KV_W] + v[:, KV_W:2 * KV_W]) + (v[:, 2 * KV_W:3 * KV_W] + v[:, 3 * KV_W:])

    scores, values = [], []
    for b in range(n_seq):
        kw = jnp.where(newest, jnp.broadcast_to(kn_t[:, b:b + 1], (KV_W, WINDOW)),
                       pltpu.roll(ck_ref[b], WINDOW - 1, 1))
        vw = jnp.where(newest, jnp.broadcast_to(vn_t[:, b:b + 1], (KV_W, WINDOW)),
                       pltpu.roll(cv_ref[b], WINDOW - 1, 1))
        ko_ref[b] = kw
        vo_ref[b] = vw
        qb = jnp.broadcast_to(q_sw[b:b + 1, :], (N_HEADS, ATT_W))
        qbd = fold4(jnp.where(own_head, qb, 0.0))
        scores.append(_dot(qbd.astype(BF16), kw.astype(BF16)))
        values.append(vw.astype(BF16))
    st = jnp.concatenate(scores, axis=0) + jnp.concatenate([bias] * n_seq, axis=0)
    sink = jnp.concatenate([sink_ref[...]] * n_seq, axis=0)
    m = jnp.maximum(jnp.max(st, axis=-1, keepdims=True), sink)
    p = jnp.exp(st - m)
    inv_l = 1.0 / (jnp.sum(p, axis=-1, keepdims=True) + jnp.exp(sink - m))
    pb = p.astype(BF16)
    for b in range(n_seq):
        hs = slice(b * N_HEADS, (b + 1) * N_HEADS)
        o = _dot_nt(pb[hs, :], values[b]) * inv_l[hs, :]
        o4 = jnp.concatenate([o, o, o, o], axis=1)
        att_scr[b:b + 1, :] = jnp.sum(jnp.where(own_head, o4, 0.0), axis=0, keepdims=True)
    normed = _rms(att_scr[...], g_ref[...]).astype(BF16)
    o_ref[...] = _dot_nt(normed, perm_ref[...]).astype(BF16)


def _attn_sample(q, kn, vn, ck, cv, slope_col, sink_col, g_att, perm):
    n = q.shape[0]
    nb = SEQ_BLK
    cache_spec = pl.BlockSpec((nb, KV_W, WINDOW), lambda i: (i, 0, 0))
    row_spec = lambda w: pl.BlockSpec((nb, w), lambda i: (i, 0))
    full = lambda a: pl.BlockSpec(a.shape, lambda i: (0,) * a.ndim)
    return pl.pallas_call(
        functools.partial(_attn_sample_kernel, n_seq=nb),
        grid=(n // nb,),
        in_specs=[row_spec(ATT_W), row_spec(KV_W), row_spec(KV_W), cache_spec, cache_spec,
                  full(slope_col), full(sink_col), full(g_att), full(perm)],
        out_specs=[row_spec(ATT_W), cache_spec, cache_spec],
        out_shape=[
            jax.ShapeDtypeStruct((n, ATT_W), BF16),
            jax.ShapeDtypeStruct(ck.shape, F32),
            jax.ShapeDtypeStruct(cv.shape, F32),
        ],
        scratch_shapes=[pltpu.VMEM((nb, ATT_W), F32)],
        compiler_params=_params(("parallel",)),
        name="attn_sample",
    )(q, kn, vn, ck, cv, slope_col, sink_col, g_att, perm)


def _ssd_prompt_kernel(xb_ref, z_ref, dtb_ref, alog_ref, dskip_ref, g_ref,
                       y_ref, hout_ref, ht_scr, e_scr, tri_scr, yd_scr, *, rows):
    i = pl.program_id(0)
    T = CHUNK

    @pl.when(i == 0)
    def _():
        ht_scr[...] = jnp.zeros_like(ht_scr)
        e_scr[...] = _head_expand_matrix()
        l = lax.broadcasted_iota(jnp.int32, (T, T), 0)
        s = lax.broadcasted_iota(jnp.int32, (T, T), 1)
        tri_scr[...] = jnp.where(s <= l, 1.0, 0.0).astype(BF16)

    a_log2 = -jnp.exp(alog_ref[...]) * LOG2E
    e_mat = e_scr[...]
    tri = tri_scr[...]
    causal = lax.broadcasted_iota(jnp.int32, (T, T), 0) >= lax.broadcasted_iota(jnp.int32, (T, T), 1)
    lo = lax.broadcasted_iota(jnp.int32, (T, LANES), 1) < SSM_P
    gw = SSM_W // SSM_G

    for c in range(rows // T):
        rs = slice(c * T, (c + 1) * T)
        dt = _softplus(xb_ref[rs, CONV_DIM:XBCDT_W] + dtb_ref[...])
        dta = dt * a_log2
        hi, mid, lw = _split3(dta)
        acum = _dot(tri, hi) + _dot(tri, mid) + _dot(tri, lw)
        acum_e = _expand_heads(acum, e_mat)
        dt_e = _expand_heads(dt, e_mat)
        acum_t = acum.T
        xs = xb_ref[rs, 0:SSM_W]
        bm = xb_ref[rs, SSM_W:SSM_W + SSM_G * D_STATE]
        cm = xb_ref[rs, SSM_W + SSM_G * D_STATE:CONV_DIM]
        xdt = xs * dt_e
        last = acum_e[T - 1:T, :]
        xdt_b = xdt.astype(BF16)
        xdec_b = (xdt * jnp.exp2(last - acum_e)).astype(BF16)
        bm_b = bm.astype(BF16)
        cm_b = cm.astype(BF16)
        cbs = [_dot_nt(cm_b[:, g * D_STATE:(g + 1) * D_STATE], bm_b[:, g * D_STATE:(g + 1) * D_STATE])
               for g in range(SSM_G)]
        for pr in range(SSM_H // 2):
            g = (2 * pr) // (SSM_H // SSM_G)
            ws = []
            for t in range(2):
                h = 2 * pr + t
                seg = jnp.broadcast_to(acum[:, h:h + 1], (T, T)) - acum_t[h:h + 1, :]
                ws.append((cbs[g] * jnp.exp2(jnp.where(causal, seg, NEG_INF))).astype(BF16))
            xsl = xdt_b[:, pr * LANES:(pr + 1) * LANES]
            yd_scr[:, pr * LANES:(pr + 1) * LANES] = _dot(jnp.concatenate(ws, axis=1),
                                                          _pair_blockdiag(xsl, xsl, lo, ~lo))
        yoff = []
        for g in range(SSM_G):
            gs = slice(g * gw, (g + 1) * gw)
            ht_g = ht_scr[:, gs]
            yoff.append(_dot(cm_b[:, g * D_STATE:(g + 1) * D_STATE], ht_g.astype(BF16)))
            cst = _dot_tn(bm_b[:, g * D_STATE:(g + 1) * D_STATE], xdec_b[:, gs])
            ht_scr[:, gs] = ht_g * jnp.exp2(last[:, gs]) + cst
        y = yd_scr[...] + jnp.concatenate(yoff, axis=1) * jnp.exp2(acum_e)
        y = y + dskip_ref[...] * xs
        y = y * z_ref[rs, :]
        y_ref[rs, :] = _rms(y, g_ref[...]).astype(BF16)

    @pl.when(i == pl.num_programs(0) - 1)
    def _():
        hout_ref[...] = ht_scr[...].T


def _ssd_prompt(xb, z, dtb, alog, dskip_e, g_ssm):
    n_rows = xb.shape[0]
    rows = ROWS_SSD
    full = lambda a: pl.BlockSpec(a.shape, lambda i: (0,) * a.ndim)
    return pl.pallas_call(
        functools.partial(_ssd_prompt_kernel, rows=rows),
        grid=(n_rows // rows,),
        in_specs=[
            pl.BlockSpec((rows, XBCDT_W), lambda i: (i, 0)),
            pl.BlockSpec((rows, SSM_W), lambda i: (i, 0)),
            full(dtb), full(alog), full(dskip_e), full(g_ssm),
        ],
        out_specs=[
            pl.BlockSpec((rows, SSM_W), lambda i: (i, 0)),
            pl.BlockSpec((SSM_W, D_STATE), lambda i: (0, 0)),
        ],
        out_shape=[
            jax.ShapeDtypeStruct((n_rows, SSM_W), BF16),
            jax.ShapeDtypeStruct((SSM_W, D_STATE), F32),
        ],
        scratch_shapes=[
            pltpu.VMEM((D_STATE, SSM_W), F32),
            pltpu.VMEM((LANES, SSM_W), BF16),
            pltpu.VMEM((CHUNK, CHUNK), BF16),
            pltpu.VMEM((CHUNK, SSM_W), F32),
        ],
        compiler_params=_params(("arbitrary",)),
        name="ssd_prompt",
    )(xb, z, dtb, alog, dskip_e, g_ssm)


def _ssd_sample_kernel(xb_ref, sconv_ref, z_ref, h0_ref, cw_ref, cb_ref, dtb_ref, alog_ref, dskip_ref, g_ref,
                       y_ref, conv_ref, hout_ref, yoff_scr, *, n_seq):
    gw = SSM_W // SSM_G
    gn = SSM_G * D_STATE
    x_new = xb_ref[:, 0:CONV_DIM]
    taps = [sconv_ref[:, k * CONV_DIM:(k + 1) * CONV_DIM] for k in range(CONV_W - 1)] + [x_new]
    acc = cb_ref[...]
    for k in range(CONV_W):
        acc = acc + taps[k] * cw_ref[k:k + 1, :]
    for k in range(1, CONV_W):
        conv_ref[:, (k - 1) * CONV_DIM:k * CONV_DIM] = taps[k]
    xc = _silu(acc)
    xs = xc[:, 0:SSM_W]
    bm = xc[:, SSM_W:SSM_W + gn]
    cm = xc[:, SSM_W + gn:CONV_DIM]

    e_mat = _head_expand_matrix()
    dt = _softplus(xb_ref[:, CONV_DIM:XBCDT_W] + dtb_ref[...])
    dt_e = _expand_heads(dt, e_mat)
    dec_e = jnp.exp(_expand_heads(dt * (-jnp.exp(alog_ref[...])), e_mat))
    xdt = xs * dt_e

    lane_w = lax.broadcasted_iota(jnp.int32, (n_seq, SSM_W), 1)
    first_grp = lane_w < gw
    cbv = [jnp.sum(cm[:, g * D_STATE:(g + 1) * D_STATE] * bm[:, g * D_STATE:(g + 1) * D_STATE],
                   axis=-1, keepdims=True) for g in range(SSM_G)]
    cb_e = jnp.where(first_grp, cbv[0], cbv[1])

    pad = jnp.zeros((LANES - n_seq, SSM_W), F32)
    xdt_t = jnp.concatenate([xdt, pad], axis=0).T
    dec_t = jnp.concatenate([dec_e, pad], axis=0).T
    sub8 = lax.broadcasted_iota(jnp.int32, (8, D_STATE), 0)
    lane_r = lax.broadcasted_iota(jnp.int32, (8, SSM_W), 1)

    for b in range(n_seq):
        h0 = h0_ref[b]
        c_row = jnp.broadcast_to(cm[b:b + 1, :], (8, gn))
        c8 = jnp.where(sub8 == 0, c_row[:, 0:D_STATE], jnp.where(sub8 == 1, c_row[:, D_STATE:], 0.0))
        r = _dot_nt(c8.astype(BF16), h0.astype(BF16))
        yoff_scr[b:b + 1, :] = jnp.where(lane_r[0:1] < gw, r[0:1, :], r[1:2, :])
        b_row = bm[b:b + 1, :]
        for g in range(SSM_G):
            rs = slice(g * gw, (g + 1) * gw)
            dcol = jnp.broadcast_to(dec_t[rs, b:b + 1], (gw, D_STATE))
            xcol = jnp.broadcast_to(xdt_t[rs, b:b + 1], (gw, D_STATE))
            hout_ref[b, rs, :] = h0[rs, :] * dcol + xcol * b_row[:, g * D_STATE:(g + 1) * D_STATE]

    y = cb_e * xdt + yoff_scr[...] * dec_e
    y = y + dskip_ref[...] * xs
    y = y * _silu(z_ref[...])
    y_ref[...] = _rms(y, g_ref[...]).astype(BF16)


def _ssd_sample(xb, sconv, z, h0, cw8, cb, dtb, alog, dskip_e, g_ssm):
    n = xb.shape[0]
    nb = SEQ_BLK
    row_spec = lambda w: pl.BlockSpec((nb, w), lambda i: (i, 0))
    st_spec = pl.BlockSpec((nb, SSM_W, D_STATE), lambda i: (i, 0, 0))
    full = lambda a: pl.BlockSpec(a.shape, lambda i: (0,) * a.ndim)
    return pl.pallas_call(
        functools.partial(_ssd_sample_kernel, n_seq=nb),
        grid=(n // nb,),
        in_specs=[row_spec(XBCDT_W), row_spec((CONV_W - 1) * CONV_DIM), row_spec(SSM_W), st_spec,
                  full(cw8), full(cb), full(dtb), full(alog), full(dskip_e), full(g_ssm)],
        out_specs=[row_spec(SSM_W), row_spec((CONV_W - 1) * CONV_DIM), st_spec],
        out_shape=[
            jax.ShapeDtypeStruct((n, SSM_W), BF16),
            jax.ShapeDtypeStruct((n, (CONV_W - 1) * CONV_DIM), F32),
            jax.ShapeDtypeStruct(h0.shape, F32),
        ],
        scratch_shapes=[pltpu.VMEM((nb, SSM_W), F32)],
        compiler_params=_params(("parallel",)),
        name="ssd_sample",
    )(xb, sconv, z, h0, cw8, cb, dtb, alog, dskip_e, g_ssm)


def _outproj_rows(att, ssm, x, woa_ref, wos_ref, gpost, gt1, gpre, sh2, sc2):
    mix = _dot(att, woa_ref[...]) + _dot(ssm, wos_ref[...])
    x1 = x + _rms(mix, gpost * gt1)
    u2 = _rms(x1, gpre * (1.0 + sc2)) + sh2
    return x1, u2.astype(BF16)


def _outproj_kernel(att_ref, ssm_ref, x_ref, woa_ref, wos_ref, gpost_ref, gt1_ref, gpre_ref, sh2_ref, sc2_ref,
                    atts_ref, ssms_ref, xs_ref, gt1s_ref, sh2s_ref, sc2s_ref,
                    x1_ref, u2_ref, x1s_ref, u2s_ref, *, rows):
    for r0 in range(0, rows, SUB_OUT):
        rs = slice(r0, r0 + SUB_OUT)
        x1_ref[rs, :], u2_ref[rs, :] = _outproj_rows(
            att_ref[rs, :], ssm_ref[rs, :], x_ref[rs, :], woa_ref, wos_ref, gpost_ref[...], gt1_ref[0:1, :],
            gpre_ref[...], sh2_ref[0:1, :], sc2_ref[0:1, :])

    @pl.when(pl.program_id(0) == pl.num_programs(0) - 1)
    def _():
        x1s_ref[...], u2s_ref[...] = _outproj_rows(
            atts_ref[...], ssms_ref[...], xs_ref[...], woa_ref, wos_ref, gpost_ref[...], gt1s_ref[...],
            gpre_ref[...], sh2s_ref[...], sc2s_ref[...])


def _outproj(att, ssm, x, att_s, ssm_s, x_s, mod_p, mod_s, w_out, g_post, g_pre):
    n_rows = x.shape[0]
    n_s = x_s.shape[0]
    rows = ROWS_OUT
    assert rows % SUB_OUT == 0 and n_rows % rows == 0
    half = lambda r: pl.BlockSpec((ATT_W, D_MODEL), lambda i: (r, 0), pipeline_mode=pl.Buffered(1))
    row_spec = lambda w: pl.BlockSpec((rows, w), lambda i: (i, 0))
    sample = lambda w, col=0: pl.BlockSpec((n_s, w), lambda i: (0, col))
    return pl.pallas_call(
        functools.partial(_outproj_kernel, rows=rows),
        grid=(n_rows // rows,),
        in_specs=[
            row_spec(ATT_W), row_spec(SSM_W), row_spec(D_MODEL),
            half(0), half(1),
            _const_spec((1, D_MODEL)),
            _mod_spec(2),
            _const_spec((1, D_MODEL)),
            _mod_spec(3),
            _mod_spec(4),
            sample(ATT_W), sample(SSM_W), sample(D_MODEL),
            sample(D_MODEL, 2), sample(D_MODEL, 3), sample(D_MODEL, 4),
        ],
        out_specs=[row_spec(D_MODEL), row_spec(D_MODEL), sample(D_MODEL), sample(D_MODEL)],
        out_shape=[
            jax.ShapeDtypeStruct((n_rows, D_MODEL), F32),
            jax.ShapeDtypeStruct((n_rows, D_MODEL), BF16),
            jax.ShapeDtypeStruct((n_s, D_MODEL), F32),
            jax.ShapeDtypeStruct((n_s, D_MODEL), BF16),
        ],
        compiler_params=_params(("arbitrary",)),
        name="outproj",
    )(att, ssm, x, w_out, w_out, g_post, mod_p, g_pre, mod_p, mod_p, att_s, ssm_s, x_s, mod_s, mod_s, mod_s)


def _ffn_kernel(u_ref, x1_hbm, wg_ref, wu_ref, wd_ref, gpost_ref, gt2_ref, us_ref, x1s_ref, gt2s_ref,
                y_ref, ys_ref, x1_buf, x1_sem, *, rows):
    i = pl.program_id(0)
    j = pl.program_id(1)
    x1_copy = pltpu.make_async_copy(x1_hbm.at[pl.ds(i * rows, rows), :], x1_buf, x1_sem)
    last = pl.num_programs(1) - 1

    def ffn_rows(u, first, final, acc_prev, resid, gt2):
        hid = (_silu(_dot(u, wg_ref[...])) * _dot(u, wu_ref[...])).astype(BF16)
        acc = _dot(hid, wd_ref[...])
        if not first:
            acc = acc_prev() + acc
        if final:
            acc = resid() + gt2() * _rms(acc, gpost_ref[...])
        return acc

    def d_ff_slice(first, final):
        sub = min(SUB_FFN, rows)
        for r0 in range(0, rows, sub):
            rs = slice(r0, r0 + sub)
            y_ref[rs, :] = ffn_rows(u_ref[rs, :], first, final, lambda: y_ref[rs, :],
                                    lambda: x1_buf[rs, :], lambda: gt2_ref[0:1, :])

        @pl.when(i == pl.num_programs(0) - 1)
        def _():
            ys_ref[...] = ffn_rows(us_ref[...], first, final, lambda: ys_ref[...],
                                   lambda: x1s_ref[...], lambda: gt2s_ref[...])

    @pl.when(j == 0)
    def _():
        x1_copy.start()
        d_ff_slice(True, False)

    @pl.when(jnp.logical_and(j > 0, j < last))
    def _():
        d_ff_slice(False, False)

    @pl.when(j == last)
    def _():
        x1_copy.wait()
        d_ff_slice(False, True)


def _ffn(u2, x1, mod_p, u2_s, x1_s, mod_s, wg, wu, wd, g_post):
    n_rows = x1.shape[0]
    n_s = x1_s.shape[0]
    rows = ROWS_FFN
    fb = FF_BLK
    assert D_FF // fb >= 2 and n_rows % rows == 0
    whole = lambda r, c: pl.BlockSpec((r, D_MODEL), lambda i, j: (0, c))
    return pl.pallas_call(
        functools.partial(_ffn_kernel, rows=rows),
        grid=(n_rows // rows, D_FF // fb),
        in_specs=[
            pl.BlockSpec((rows, D_MODEL), lambda i, j: (i, 0)),
            pl.BlockSpec(memory_space=pl.ANY),
            pl.BlockSpec((D_MODEL, fb), lambda i, j: (0, j)),
            pl.BlockSpec((D_MODEL, fb), lambda i, j: (0, j)),
            pl.BlockSpec((fb, D_MODEL), lambda i, j: (j, 0)),
            whole(1, 0),
            _mod_spec(5),
            whole(n_s, 0), whole(n_s, 0), whole(n_s, 5),
        ],
        out_specs=[pl.BlockSpec((rows, D_MODEL), lambda i, j: (i, 0)), whole(n_s, 0)],
        out_shape=[jax.ShapeDtypeStruct((n_rows, D_MODEL), F32), jax.ShapeDtypeStruct((n_s, D_MODEL), F32)],
        scratch_shapes=[pltpu.VMEM((rows, D_MODEL), F32), pltpu.SemaphoreType.DMA],
        compiler_params=_params(("arbitrary", "arbitrary"), VMEM_LIMIT_FFN),
        name="ffn",
    )(u2, x1, wg, wu, wd, g_post, mod_p, u2_s, x1_s, mod_s)


def _alibi_slopes():
    return (2.0 ** (-8.0 * np.arange(1, N_HEADS + 1) / N_HEADS)).astype(np.float32)


def kernel(x_prompt, x_sample, cache_k, cache_v, state_conv, state_ssm, c_prompt, c_sample, w_ada, b_ada, g_pre_mix, g_post_mix, w_in, attn_sinks, g_attn_out, conv_w, conv_b, dt_bias, a_log, d_skip, g_ssm_out, w_out, g_pre_ffn, g_post_ffn, w_gate, w_up, w_down):
    assert w_ada.shape[0] == 1, "one layer"
    n_s = x_sample.shape[0]
    row = lambda v: v.reshape(1, -1)

    g_att = row(g_attn_out[0])
    swapped_cols = np.arange(ATT_W).reshape(N_KV, Q_PER_KV, HEAD_DIM).swapaxes(0, 1).reshape(-1)
    perm = np.zeros((ATT_W, ATT_W), np.float32)
    perm[swapped_cols, np.arange(ATT_W)] = 1.0
    perm = jnp.asarray(perm, BF16)
    g_att_s = row(g_attn_out[0].reshape(N_KV, Q_PER_KV, HEAD_DIM).swapaxes(0, 1))
    slopes_np = _alibi_slopes()
    slopes = jnp.asarray(slopes_np)
    slopes_s = jnp.asarray(slopes_np.reshape(N_KV, Q_PER_KV).T.reshape(N_HEADS, 1))
    sinks = attn_sinks[0]
    sinks_s = sinks.reshape(N_KV, Q_PER_KV).T.reshape(N_HEADS, 1)
    cw8 = jnp.pad(conv_w[0], ((0, 8 - CONV_W), (0, 0)))
    cb = row(conv_b[0])
    dtb = jnp.pad(row(dt_bias[0]), ((0, 0), (0, LANES - SSM_H)))
    alog = jnp.pad(row(a_log[0]), ((0, 0), (0, LANES - SSM_H)))
    dskip_e = row(jnp.repeat(d_skip[0], SSM_P))
    g_ssm = row(g_ssm_out[0])

    c_p8 = jnp.pad(c_prompt, ((0, 8 - c_prompt.shape[0]), (0, 0)))
    mod_s, mod_p, w_all = _prep(c_sample, c_p8, w_ada[0], row(b_ada[0]), jnp.transpose(w_in[0]))

    xp = x_prompt[0]
    xs_ = x_sample[:, 0, :]
    q, kv, z, xb, kv_last, x_tail, q_s, kv_new, z_s, xb_s = _inproj(
        xp, xs_, row(g_pre_mix[0]), mod_p, mod_s, w_all, cw8, cb)

    att, wg, wu, wd, wo = _attn_prompt(q, kv, slopes, sinks, g_att, w_gate[0], w_up[0], w_down[0], w_out[0])
    ssm, h_p = _ssd_prompt(xb, z, dtb, alog, dskip_e, g_ssm)
    keys_minor = lambda c: jnp.transpose(c, (0, 2, 3, 1)).reshape(n_s, KV_W, WINDOW)
    keys_major = lambda c: jnp.transpose(c.reshape(n_s, N_KV, HEAD_DIM, WINDOW), (0, 3, 1, 2))[None]
    att_s, k_s, v_s = _attn_sample(
        q_s, kv_new[:, :KV_W], kv_new[:, KV_W:],
        keys_minor(cache_k[0]), keys_minor(cache_v[0]), slopes_s, sinks_s, g_att_s, perm)
    ssm_s, conv_s, h_s = _ssd_sample(
        xb_s, state_conv[0].reshape(n_s, (CONV_W - 1) * CONV_DIM), z_s,
        state_ssm[0].reshape(n_s, SSM_W, D_STATE), cw8, cb, dtb, alog, dskip_e, g_ssm)

    x1, u2, x1_s, u2_s = _outproj(att, ssm, xp, att_s, ssm_s, xs_, mod_p, mod_s, wo,
                                  row(g_post_mix[0]), row(g_pre_ffn[0]))
    y_p, y_s = _ffn(u2, x1, mod_p, u2_s, x1_s, mod_s, wg, wu, wd, row(g_post_ffn[0]))

    return (
        y_p[None],
        y_s[:, None, :],
        kv_last[:, :KV_W].reshape(1, 1, WINDOW, N_KV, HEAD_DIM),
        kv_last[:, KV_W:].reshape(1, 1, WINDOW, N_KV, HEAD_DIM),
        x_tail[8 - (CONV_W - 1):].reshape(1, 1, CONV_W - 1, CONV_DIM),
        h_p.reshape(1, 1, SSM_H, SSM_P, D_STATE),
        keys_major(k_s),
        keys_major(v_s),
        conv_s.reshape(1, n_s, CONV_W - 1, CONV_DIM),
        h_s.reshape(1, n_s, SSM_H, SSM_P, D_STATE),
    )
```

```python
import functools

import numpy as np
import jax
import jax.numpy as jnp
from jax import lax
from jax.experimental import pallas as pl
from jax.experimental.pallas import tpu as pltpu

F32 = jnp.float32
BF16 = jnp.bfloat16

D_MODEL = 2048
ATT_W = 1024
HEAD_DIM = 64
N_HEADS = 16
N_KV = 4
Q_PER_KV = 4
KV_W = N_KV * HEAD_DIM
WINDOW = 128
SSM_W = 1024
SSM_P = 64
SSM_H = 16
SSM_G = 2
D_STATE = 128
CONV_W = 4
CONV_DIM = SSM_W + 2 * SSM_G * D_STATE
XBCDT_W = CONV_DIM + 128
D_FF = 5632
EPS = 1e-6
CHUNK = 128
NEG_INF = float("-inf")
LOG2E = 1.4426950408889634

VMEM_LIMIT = 56 * 1024 * 1024
VMEM_LIMIT_FFN = 60 * 1024 * 1024
LANES = 128

ROWS_IN = 512
ROWS_ATT = 1024
ROWS_SSD = 1024
ROWS_OUT = 512
SUB_OUT = 128
ROWS_FFN = 1024
SUB_FFN = 512
FF_BLK = 512
SEQ_BLK = 8


def _params(sem, vmem=VMEM_LIMIT):
    return pltpu.CompilerParams(dimension_semantics=sem, vmem_limit_bytes=vmem)


def _silu(v):
    h = 0.5 * v
    return h + h * jnp.tanh(h)


def _softplus(v):
    return jnp.maximum(v, 0.0) + jnp.log1p(jnp.exp(-jnp.abs(v)))


def _rms(v, g):
    return v * lax.rsqrt(jnp.mean(v * v, axis=-1, keepdims=True) + EPS) * g


def _split3(v):
    hi = v.astype(BF16)
    r1 = v - hi.astype(F32)
    mid = r1.astype(BF16)
    lo = (r1 - mid.astype(F32)).astype(BF16)
    return hi, mid, lo


def _dot(a, b):
    return jnp.dot(a, b, preferred_element_type=F32)


def _dot_nt(a, b):
    return lax.dot_general(a, b, (((1,), (1,)), ((), ())), preferred_element_type=F32)


def _dot_tn(a, b):
    return lax.dot_general(a, b, (((0,), (0,)), ((), ())), preferred_element_type=F32)


def _expand_heads(v, e_mat):
    hi, mid, lo = _split3(v)
    return _dot(hi, e_mat) + _dot(mid, e_mat) + _dot(lo, e_mat)


def _head_expand_matrix():
    k = lax.broadcasted_iota(jnp.int32, (LANES, SSM_W), 0)
    c = lax.broadcasted_iota(jnp.int32, (LANES, SSM_W), 1)
    return jnp.where((c // SSM_P) == k, 1.0, 0.0).astype(BF16)


def _mod_spec(col):
    return pl.BlockSpec((8, D_MODEL), lambda i, *_: (0, col))


WS_BLK = 256
W_ALL_COLS = 4608
COL_X, COL_Q, COL_Z, COL_KV = 0, 2048, 3072, 4096
N_WS_SRC = -(-(ATT_W + 2 * KV_W + SSM_W + CONV_DIM + SSM_H) // WS_BLK)


def _ws_dst_block(s):
    q_end = ATT_W // WS_BLK
    kv_end = q_end + 2 * KV_W // WS_BLK
    z_end = kv_end + SSM_W // WS_BLK
    return jnp.where(s < q_end, COL_Q // WS_BLK + s,
                     jnp.where(s < kv_end, COL_KV // WS_BLK + s - q_end,
                               jnp.where(s < z_end, COL_Z // WS_BLK + s - kv_end,
                                         COL_X // WS_BLK + s - z_end)))


ADA_BLK = 768


def _prep_kernel(cs_ref, cp_ref, wa_ref, b_ref, wt_ref, ms_ref, mp_ref, wall_ref):
    s = pl.program_id(0)
    w = wa_ref[...].astype(BF16)
    b = b_ref[...]
    ms_ref[...] = _dot(_silu(cs_ref[...]).astype(BF16), w) + b
    mp_ref[...] = _dot(_silu(cp_ref[...]).astype(BF16), w) + b
    n_valid = jnp.where(s < N_WS_SRC - 1, WS_BLK, jnp.where(s == N_WS_SRC - 1, SSM_H, 0))
    lane = lax.broadcasted_iota(jnp.int32, wall_ref.shape, 1)
    wall_ref[...] = jnp.where(lane < n_valid, wt_ref[...].T, 0.0).astype(BF16)


def _prep(c_s, c_p8, w_ada, b_ada, w_in_t):
    k_dim = w_in_t.shape[1]
    n = w_ada.shape[1]
    n_ada = n // ADA_BLK
    assert n % ADA_BLK == 0 and n_ada <= N_WS_SRC + 1
    ada_col = lambda s: (0, jnp.minimum(s, n_ada - 1))
    return pl.pallas_call(
        _prep_kernel,
        grid=(N_WS_SRC + 1,),
        in_specs=[
            pl.BlockSpec(c_s.shape, lambda s: (0, 0)),
            pl.BlockSpec(c_p8.shape, lambda s: (0, 0)),
            pl.BlockSpec((D_MODEL, ADA_BLK), ada_col),
            pl.BlockSpec((1, ADA_BLK), ada_col),
            pl.BlockSpec((WS_BLK, k_dim), lambda s: (jnp.minimum(s, N_WS_SRC - 1), 0)),
        ],
        out_specs=[
            pl.BlockSpec((c_s.shape[0], ADA_BLK), ada_col),
            pl.BlockSpec((c_p8.shape[0], ADA_BLK), ada_col),
            pl.BlockSpec((k_dim, WS_BLK), lambda s: (0, _ws_dst_block(s))),
        ],
        out_shape=[
            jax.ShapeDtypeStruct((c_s.shape[0], n), F32),
            jax.ShapeDtypeStruct((c_p8.shape[0], n), F32),
            jax.ShapeDtypeStruct((k_dim, W_ALL_COLS), BF16),
        ],
        compiler_params=_params(("arbitrary",)),
        name="prep",
    )(c_s, c_p8, w_ada, b_ada, w_in_t)


def _modulated_norm(x, g, shift, scale):
    return (_rms(x, g) * (1.0 + scale) + shift).astype(BF16)


def _inproj_kernel(x_ref, g_ref, sh_ref, sc_ref, wq_ref, wkv_ref, wz_ref, wx_ref, cw_ref, cb_ref,
                   xs_ref, shs_ref, scs_ref, perm_ref,
                   q_ref, kv_ref, z_ref, xb_ref, kvlast_ref, xtail_ref, qs_ref, kvs_ref, zs_ref, xbs_ref,
                   xp_scr, *, rows):
    i = pl.program_id(0)
    tile = 2 * LANES

    @pl.when(i == 0)
    def _():
        xp_scr[:, rows:rows + 8, :] = jnp.zeros((CONV_DIM // LANES, 8, LANES), F32)

    u = _modulated_norm(x_ref[...], g_ref[...], sh_ref[0:1, :], sc_ref[0:1, :])

    def conv_slab(s, xs):
        ls = slice(s * LANES, (s + 1) * LANES)
        xtail_ref[:, ls] = xs[rows - 8:, :]
        xp_scr[s, 0:8, :] = xp_scr[s, rows:rows + 8, :]
        xp_scr[s, 8:, :] = xs
        acc = cb_ref[:, ls] + xp_scr[s, pl.ds(8 - (CONV_W - 1), rows), :] * cw_ref[0:1, ls]
        for k in range(1, CONV_W):
            acc = acc + xp_scr[s, pl.ds(8 - (CONV_W - 1) + k, rows), :] * cw_ref[k:k + 1, ls]
        xb_ref[:, ls] = _silu(acc)

    def post_z(c0, c1):
        def post(r):
            z_ref[:, c0:c1] = _silu(r)
        return post

    def post_q(c0, c1):
        def post(r):
            q_ref[:, c0:c1] = (r * (HEAD_DIM ** -0.5)).astype(BF16)
        return post

    def post_kv(c0, c1):
        def post(r):
            kv_ref[:, c0:c1] = r.astype(BF16)
            kvlast_ref[:, c0:c1] = r[rows - WINDOW:, :]
        return post

    xb = _dot(u, wx_ref[...])
    xb_ref[:, CONV_DIM:] = xb[:, CONV_DIM:]
    slabs = [functools.partial(conv_slab, s, xb[:, s * LANES:(s + 1) * LANES]) for s in range(CONV_DIM // LANES)]
    jobs = []
    for w_ref, width, post_of in ((wz_ref, SSM_W, post_z), (wq_ref, ATT_W, post_q), (wkv_ref, 2 * KV_W, post_kv)):
        for c0 in range(0, width, tile):
            jobs.append((w_ref, c0, c0 + tile, post_of(c0, c0 + tile)))
    pending = None
    for n, (w_ref, c0, c1, post) in enumerate(jobs):
        r = _dot(u, w_ref[:, c0:c1])
        for slab in slabs[n * len(slabs) // len(jobs):(n + 1) * len(slabs) // len(jobs)]:
            slab()
        if pending is not None:
            pending[0](pending[1])
        pending = (post, r)
    pending[0](pending[1])

    @pl.when(i == pl.num_programs(0) - 1)
    def _():
        us = _modulated_norm(xs_ref[...], g_ref[...], shs_ref[...], scs_ref[...])
        qs = (_dot(us, wq_ref[...]) * (HEAD_DIM ** -0.5)).astype(BF16)
        qs_ref[...] = _dot(qs, perm_ref[...]).astype(BF16)
        kvs_ref[...] = _dot(us, wkv_ref[...])
        zs_ref[...] = _dot(us, wz_ref[...])
        xbs_ref[...] = _dot(us, wx_ref[...])


def _const_spec(shape):
    return pl.BlockSpec(shape, lambda i, *_: (0,) * len(shape), pipeline_mode=pl.Buffered(1))


def _inproj(x, x_s, g, mod_p, mod_s, w_all, cw8, cb, perm):
    n_rows = x.shape[0]
    n_s = x_s.shape[0]
    rows = ROWS_IN
    row_spec = lambda w: pl.BlockSpec((rows, w), lambda i: (i, 0))
    seg_spec = lambda width, col: pl.BlockSpec((D_MODEL, width), lambda i: (0, col // width),
                                               pipeline_mode=pl.Buffered(1))
    sample_in = lambda col: pl.BlockSpec((n_s, D_MODEL), lambda i: (0, col), pipeline_mode=pl.Buffered(1))
    sample_out = lambda w: pl.BlockSpec((n_s, w), lambda i: (0, 0))
    return pl.pallas_call(
        functools.partial(_inproj_kernel, rows=rows),
        grid=(n_rows // rows,),
        in_specs=[
            row_spec(D_MODEL),
            _const_spec((1, D_MODEL)),
            _mod_spec(0),
            _mod_spec(1),
            seg_spec(ATT_W, COL_Q), seg_spec(2 * KV_W, COL_KV), seg_spec(SSM_W, COL_Z), seg_spec(XBCDT_W, COL_X),
            _const_spec(cw8.shape), _const_spec(cb.shape),
            sample_in(0), sample_in(0), sample_in(1), _const_spec(perm.shape),
        ],
        out_specs=[
            row_spec(ATT_W), row_spec(2 * KV_W), row_spec(SSM_W), row_spec(XBCDT_W),
            pl.BlockSpec((WINDOW, 2 * KV_W), lambda i: (0, 0)),
            pl.BlockSpec((8, CONV_DIM), lambda i: (0, 0)),
            sample_out(ATT_W), sample_out(2 * KV_W), sample_out(SSM_W), sample_out(XBCDT_W),
        ],
        out_shape=[
            jax.ShapeDtypeStruct((n_rows, ATT_W), BF16),
            jax.ShapeDtypeStruct((n_rows, 2 * KV_W), BF16),
            jax.ShapeDtypeStruct((n_rows, SSM_W), F32),
            jax.ShapeDtypeStruct((n_rows, XBCDT_W), F32),
            jax.ShapeDtypeStruct((WINDOW, 2 * KV_W), F32),
            jax.ShapeDtypeStruct((8, CONV_DIM), F32),
            jax.ShapeDtypeStruct((n_s, ATT_W), BF16),
            jax.ShapeDtypeStruct((n_s, 2 * KV_W), F32),
            jax.ShapeDtypeStruct((n_s, SSM_W), F32),
            jax.ShapeDtypeStruct((n_s, XBCDT_W), F32),
        ],
        scratch_shapes=[pltpu.VMEM((CONV_DIM // LANES, rows + 8, LANES), F32)],
        compiler_params=_params(("arbitrary",), VMEM_LIMIT_FFN),
        name="inproj",
    )(x, g, mod_p, mod_p, w_all, w_all, w_all, w_all, cw8, cb, x_s, mod_s, mod_s, perm)


def _pair_blockdiag(lo_src, hi_src, keep_lo, keep_hi):
    zero = jnp.zeros_like(lo_src)
    return jnp.concatenate([jnp.where(keep_lo, lo_src, zero), jnp.where(keep_hi, hi_src, zero)], axis=0)


def _attn_prompt_kernel(slopes_ref, sinks_ref, q_ref, kvc_ref, kvp_ref, g_ref, wg_ref, wu_ref, wd_ref, wo_ref,
                        o_ref, wg_o, wu_o, wd_o, wo_o, bias_scr, kvdup_scr, *, n_sub):
    i = pl.program_id(0)
    blk = WINDOW

    wg_o[...] = wg_ref[...].astype(BF16)
    wu_o[...] = wu_ref[...].astype(BF16)
    wd_o[...] = wd_ref[...].astype(BF16)
    wo_o[...] = wo_ref[...].astype(BF16)

    @pl.when(i == 0)
    def _():
        a = lax.broadcasted_iota(jnp.int32, (blk, 2 * blk), 0)
        j = lax.broadcasted_iota(jnp.int32, (blk, 2 * blk), 1)
        dist = a + blk - j
        valid = (dist >= 0) & (dist < WINDOW)
        distf = dist.astype(F32)
        for h in range(N_HEADS):
            kvh, g = divmod(h, Q_PER_KV)
            pr, t = divmod(g, 2)
            b = jnp.where(valid, -(slopes_ref[h] * distf), NEG_INF)
            b = jnp.where(j == 0, sinks_ref[h], b)
            rs = slice(pr * blk, (pr + 1) * blk)
            cs = slice(t * 2 * blk, (t + 1) * 2 * blk)
            bias_scr[1, kvh, rs, cs] = b
            bias_scr[0, kvh, rs, cs] = jnp.where((j >= blk) | (j == 0), b, NEG_INF)

    for src_ref, r_lo, r_hi in ((kvp_ref, 0, blk), (kvc_ref, blk, kvdup_scr.shape[0])):
        lo_half = lax.broadcasted_iota(jnp.int32, (r_hi - r_lo, LANES), 1) < HEAD_DIM
        for part in range(2):
            for s in range(2):
                c_in = part * KV_W + s * LANES
                c_out = part * N_KV * LANES + 2 * s * LANES
                x = src_ref[:, c_in:c_in + LANES].astype(F32)
                xr = pltpu.roll(x, HEAD_DIM, 1)
                kvdup_scr[r_lo:r_hi, c_out:c_out + LANES] = jnp.where(lo_half, x, xr).astype(BF16)
                kvdup_scr[r_lo:r_hi, c_out + LANES:c_out + 2 * LANES] = jnp.where(lo_half, xr, x).astype(BF16)

    lane = lax.broadcasted_iota(jnp.int32, (2 * blk, LANES), 1)
    not_sink = lax.broadcasted_iota(jnp.int32, (2 * blk, LANES), 0) != 0
    keep_lo = (lane < HEAD_DIM) & not_sink
    keep_hi = (lane >= HEAD_DIM) & not_sink
    ones_cols = jnp.concatenate([jnp.where(lane < HEAD_DIM, 1.0, 0.0),
                                 jnp.where(lane >= HEAD_DIM, 1.0, 0.0)], axis=0).astype(BF16)
    n_chunk = 4
    rc = 2 * blk // n_chunk

    def body(b, carry):
        r0 = pl.multiple_of(b * blk, blk)
        variant = jnp.where(jnp.logical_and(i == 0, b == 0), 0, 1)
        vbds, scores = [], []
        for kvh in range(N_KV):
            c0 = kvh * Q_PER_KV * HEAD_DIM
            kd = kvdup_scr[pl.ds(r0, 2 * blk), kvh * LANES:(kvh + 1) * LANES]
            vd = kvdup_scr[pl.ds(r0, 2 * blk), (N_KV + kvh) * LANES:(N_KV + kvh + 1) * LANES]
            kbd = _pair_blockdiag(kd, kd, keep_lo, keep_hi)
            vbds.append(jnp.concatenate([_pair_blockdiag(vd, vd, keep_lo, keep_hi), ones_cols], axis=1))
            q2 = jnp.concatenate([q_ref[pl.ds(r0, blk), c0:c0 + LANES],
                                  q_ref[pl.ds(r0, blk), c0 + LANES:c0 + 2 * LANES]], axis=0)
            scores.append(_dot_nt(q2, kbd) + bias_scr[variant, kvh])
        probs = []
        for kvh in range(N_KV):
            chunks = []
            for c in range(n_chunk):
                st = scores[kvh][c * rc:(c + 1) * rc, :]
                sl = st[:, 0:2 * blk]
                sr = st[:, 2 * blk:]
                p_l = jnp.exp(sl - jnp.max(sl, axis=-1, keepdims=True))
                p_r = jnp.exp(sr - jnp.max(sr, axis=-1, keepdims=True))
                chunks.append(jnp.concatenate([p_l, p_r], axis=1).astype(BF16))
            probs.append(jnp.concatenate(chunks, axis=0))
        outs = []
        for kvh in range(N_KV):
            o2 = _dot(probs[kvh], vbds[kvh])
            o = o2[:, 0:LANES] * (1.0 / o2[:, LANES:])
            outs += [o[0:blk], o[blk:]]
        att = jnp.concatenate(outs, axis=1)
        o_ref[pl.ds(r0, blk), :] = _rms(att, g_ref[...]).astype(BF16)
        return carry

    lax.fori_loop(0, n_sub, body, 0, unroll=2)


def _attn_prompt(q, kv, slopes, sinks, g_att, w_gate, w_up, w_down, w_out):
    n_rows = q.shape[0]
    rows = ROWS_ATT
    n_sub = rows // WINDOW
    n_steps = n_rows // rows
    smem = pl.BlockSpec(memory_space=pltpu.SMEM)
    slab = lambda w: pl.BlockSpec((w.shape[0] // n_steps, w.shape[1]), lambda i: (i, 0))
    ffn_w = (w_gate, w_up, w_down, w_out)
    assert all(w.shape[0] % (16 * n_steps) == 0 for w in ffn_w)
    return pl.pallas_call(
        functools.partial(_attn_prompt_kernel, n_sub=n_sub),
        grid=(n_rows // rows,),
        in_specs=[
            smem, smem,
            pl.BlockSpec((rows, ATT_W), lambda i: (i, 0)),
            pl.BlockSpec((rows, 2 * KV_W), lambda i: (i, 0)),
            pl.BlockSpec((WINDOW, 2 * KV_W), lambda i: (jnp.maximum(i * n_sub - 1, 0), 0)),
            pl.BlockSpec((1, ATT_W), lambda i: (0, 0)),
            *[slab(w) for w in ffn_w],
        ],
        out_specs=[pl.BlockSpec((rows, ATT_W), lambda i: (i, 0)), *[slab(w) for w in ffn_w]],
        out_shape=[jax.ShapeDtypeStruct((n_rows, ATT_W), BF16),
                   *[jax.ShapeDtypeStruct(w.shape, BF16) for w in ffn_w]],
        scratch_shapes=[
            pltpu.VMEM((2, N_KV, 2 * WINDOW, 4 * WINDOW), F32),
            pltpu.VMEM((rows + WINDOW, 2 * N_KV * LANES), BF16),
        ],
        compiler_params=_params(("arbitrary",)),
        name="attn_prompt",
    )(slopes, sinks, q, kv, kv, g_att, *ffn_w)


def _attn_sample_kernel(q_ref, kn_ref, vn_ref, ck_ref, cv_ref, slope_ref, sink_ref, g_ref,
                        o_ref, ko_ref, vo_ref, att_scr, *, n_seq):
    r16 = lax.broadcasted_iota(jnp.int32, (N_HEADS, ATT_W), 0)
    c16 = lax.broadcasted_iota(jnp.int32, (N_HEADS, ATT_W), 1)
    own_head = (c16 // HEAD_DIM) == r16
    newest = lax.broadcasted_iota(jnp.int32, (KV_W, WINDOW), 1) == WINDOW - 1
    jj = lax.broadcasted_iota(jnp.int32, (N_HEADS, WINDOW), 1)
    bias = -(slope_ref[...] * (WINDOW - 1 - jj).astype(F32))
    pad = jnp.zeros((LANES - n_seq, KV_W), F32)
    kn_t = jnp.concatenate([kn_ref[...], pad], axis=0).T
    vn_t = jnp.concatenate([vn_ref[...], pad], axis=0).T

    def fold4(v):
        return (v[:, 0:KV_W] + v[:, KV_W:2 * KV_W]) + (v[:, 2 * KV_W:3 * KV_W] + v[:, 3 * KV_W:])

    scores, values = [], []
    for b in range(n_seq):
        kw = jnp.where(newest, jnp.broadcast_to(kn_t[:, b:b + 1], (KV_W, WINDOW)),
                       pltpu.roll(ck_ref[b], WINDOW - 1, 1))
        vw = jnp.where(newest, jnp.broadcast_to(vn_t[:, b:b + 1], (KV_W, WINDOW)),
                       pltpu.roll(cv_ref[b], WINDOW - 1, 1))
        ko_ref[b] = kw
        vo_ref[b] = vw
        qb = jnp.broadcast_to(q_ref[b:b + 1, :].astype(F32), (N_HEADS, ATT_W))
        qbd = fold4(jnp.where(own_head, qb, 0.0))
        scores.append(_dot(qbd.astype(BF16), kw.astype(BF16)))
        values.append(vw.astype(BF16))
    st = jnp.concatenate(scores, axis=0) + jnp.concatenate([bias] * n_seq, axis=0)
    sink = jnp.concatenate([sink_ref[...]] * n_seq, axis=0)
    m = jnp.maximum(jnp.max(st, axis=-1, keepdims=True), sink)
    p = jnp.exp(st - m)
    inv_l = 1.0 / (jnp.sum(p, axis=-1, keepdims=True) + jnp.exp(sink - m))
    pb = p.astype(BF16)
    for b in range(n_seq):
        hs = slice(b * N_HEADS, (b + 1) * N_HEADS)
        o = _dot_nt(pb[hs, :], values[b]) * inv_l[hs, :]
        o4 = jnp.concatenate([o, o, o, o], axis=1)
        att_scr[b:b + 1, :] = jnp.sum(jnp.where(own_head, o4, 0.0), axis=0, keepdims=True)
    o_ref[...] = _rms(att_scr[...], g_ref[...]).astype(BF16)


def _attn_sample(q, kn, vn, ck, cv, slope_col, sink_col, g_att):
    n = q.shape[0]
    nb = SEQ_BLK
    cache_spec = pl.BlockSpec((nb, KV_W, WINDOW), lambda i: (i, 0, 0))
    row_spec = lambda w: pl.BlockSpec((nb, w), lambda i: (i, 0))
    full = lambda a: pl.BlockSpec(a.shape, lambda i: (0,) * a.ndim)
    return pl.pallas_call(
        functools.partial(_attn_sample_kernel, n_seq=nb),
        grid=(n // nb,),
        in_specs=[row_spec(ATT_W), row_spec(KV_W), row_spec(KV_W), cache_spec, cache_spec,
                  full(slope_col), full(sink_col), full(g_att)],
        out_specs=[row_spec(ATT_W), cache_spec, cache_spec],
        out_shape=[
            jax.ShapeDtypeStruct((n, ATT_W), BF16),
            jax.ShapeDtypeStruct(ck.shape, F32),
            jax.ShapeDtypeStruct(cv.shape, F32),
        ],
        scratch_shapes=[pltpu.VMEM((nb, ATT_W), F32)],
        compiler_params=_params(("parallel",)),
        name="attn_sample",
    )(q, kn, vn, ck, cv, slope_col, sink_col, g_att)


def _ssd_prompt_kernel(xb_ref, z_ref, dtb_ref, alog_ref, dskip_ref, g_ref,
                       y_ref, hout_ref, ht_scr, e_scr, tri_scr, yd_scr, *, rows):
    i = pl.program_id(0)
    T = CHUNK

    @pl.when(i == 0)
    def _():
        ht_scr[...] = jnp.zeros_like(ht_scr)
        e_scr[...] = _head_expand_matrix()
        l = lax.broadcasted_iota(jnp.int32, (T, T), 0)
        s = lax.broadcasted_iota(jnp.int32, (T, T), 1)
        tri_scr[...] = jnp.where(s <= l, 1.0, 0.0).astype(BF16)

    a_log2 = -jnp.exp(alog_ref[...]) * LOG2E
    e_mat = e_scr[...]
    tri = tri_scr[...]
    causal = lax.broadcasted_iota(jnp.int32, (T, T), 0) >= lax.broadcasted_iota(jnp.int32, (T, T), 1)
    lo = lax.broadcasted_iota(jnp.int32, (T, LANES), 1) < SSM_P
    gw = SSM_W // SSM_G

    for c in range(rows // T):
        rs = slice(c * T, (c + 1) * T)
        dt = _softplus(xb_ref[rs, CONV_DIM:XBCDT_W] + dtb_ref[...])
        dta = dt * a_log2
        hi, mid, lw = _split3(dta)
        acum = _dot(tri, hi) + _dot(tri, mid) + _dot(tri, lw)
        acum_e = _expand_heads(acum, e_mat)
        dt_e = _expand_heads(dt, e_mat)
        acum_t = acum.T
        xs = xb_ref[rs, 0:SSM_W]
        bm = xb_ref[rs, SSM_W:SSM_W + SSM_G * D_STATE]
        cm = xb_ref[rs, SSM_W + SSM_G * D_STATE:CONV_DIM]
        xdt = xs * dt_e
        last = acum_e[T - 1:T, :]
        xdt_b = xdt.astype(BF16)
        xdec_b = (xdt * jnp.exp2(last - acum_e)).astype(BF16)
        bm_b = bm.astype(BF16)
        cm_b = cm.astype(BF16)
        cbs = [_dot_nt(cm_b[:, g * D_STATE:(g + 1) * D_STATE], bm_b[:, g * D_STATE:(g + 1) * D_STATE])
               for g in range(SSM_G)]
        for pr in range(SSM_H // 2):
            g = (2 * pr) // (SSM_H // SSM_G)
            ws = []
            for t in range(2):
                h = 2 * pr + t
                seg = jnp.broadcast_to(acum[:, h:h + 1], (T, T)) - acum_t[h:h + 1, :]
                ws.append((cbs[g] * jnp.exp2(jnp.where(causal, seg, NEG_INF))).astype(BF16))
            xsl = xdt_b[:, pr * LANES:(pr + 1) * LANES]
            yd_scr[:, pr * LANES:(pr + 1) * LANES] = _dot(jnp.concatenate(ws, axis=1),
                                                          _pair_blockdiag(xsl, xsl, lo, ~lo))
        yoff = []
        for g in range(SSM_G):
            gs = slice(g * gw, (g + 1) * gw)
            ht_g = ht_scr[:, gs]
            yoff.append(_dot(cm_b[:, g * D_STATE:(g + 1) * D_STATE], ht_g.astype(BF16)))
            cst = _dot_tn(bm_b[:, g * D_STATE:(g + 1) * D_STATE], xdec_b[:, gs])
            ht_scr[:, gs] = ht_g * jnp.exp2(last[:, gs]) + cst
        y = yd_scr[...] + jnp.concatenate(yoff, axis=1) * jnp.exp2(acum_e)
        y = y + dskip_ref[...] * xs
        y = y * z_ref[rs, :]
        y_ref[rs, :] = _rms(y, g_ref[...]).astype(BF16)

    @pl.when(i == pl.num_programs(0) - 1)
    def _():
        hout_ref[...] = ht_scr[...].T


def _ssd_prompt(xb, z, dtb, alog, dskip_e, g_ssm):
    n_rows = xb.shape[0]
    rows = ROWS_SSD
    full = lambda a: pl.BlockSpec(a.shape, lambda i: (0,) * a.ndim)
    return pl.pallas_call(
        functools.partial(_ssd_prompt_kernel, rows=rows),
        grid=(n_rows // rows,),
        in_specs=[
            pl.BlockSpec((rows, XBCDT_W), lambda i: (i, 0)),
            pl.BlockSpec((rows, SSM_W), lambda i: (i, 0)),
            full(dtb), full(alog), full(dskip_e), full(g_ssm),
        ],
        out_specs=[
            pl.BlockSpec((rows, SSM_W), lambda i: (i, 0)),
            pl.BlockSpec((SSM_W, D_STATE), lambda i: (0, 0)),
        ],
        out_shape=[
            jax.ShapeDtypeStruct((n_rows, SSM_W), BF16),
            jax.ShapeDtypeStruct((SSM_W, D_STATE), F32),
        ],
        scratch_shapes=[
            pltpu.VMEM((D_STATE, SSM_W), F32),
            pltpu.VMEM((LANES, SSM_W), BF16),
            pltpu.VMEM((CHUNK, CHUNK), BF16),
            pltpu.VMEM((CHUNK, SSM_W), F32),
        ],
        compiler_params=_params(("arbitrary",)),
        name="ssd_prompt",
    )(xb, z, dtb, alog, dskip_e, g_ssm)


def _ssd_sample_kernel(xb_ref, sconv_ref, z_ref, h0_ref, cw_ref, cb_ref, dtb_ref, alog_ref, dskip_ref, g_ref,
                       y_ref, conv_ref, hout_ref, yoff_scr, *, n_seq):
    gw = SSM_W // SSM_G
    gn = SSM_G * D_STATE
    x_new = xb_ref[:, 0:CONV_DIM]
    taps = [sconv_ref[:, k * CONV_DIM:(k + 1) * CONV_DIM] for k in range(CONV_W - 1)] + [x_new]
    acc = cb_ref[...]
    for k in range(CONV_W):
        acc = acc + taps[k] * cw_ref[k:k + 1, :]
    for k in range(1, CONV_W):
        conv_ref[:, (k - 1) * CONV_DIM:k * CONV_DIM] = taps[k]
    xc = _silu(acc)
    xs = xc[:, 0:SSM_W]
    bm = xc[:, SSM_W:SSM_W + gn]
    cm = xc[:, SSM_W + gn:CONV_DIM]

    e_mat = _head_expand_matrix()
    dt = _softplus(xb_ref[:, CONV_DIM:XBCDT_W] + dtb_ref[...])
    dt_e = _expand_heads(dt, e_mat)
    dec_e = jnp.exp(_expand_heads(dt * (-jnp.exp(alog_ref[...])), e_mat))
    xdt = xs * dt_e

    lane_w = lax.broadcasted_iota(jnp.int32, (n_seq, SSM_W), 1)
    first_grp = lane_w < gw
    cbv = [jnp.sum(cm[:, g * D_STATE:(g + 1) * D_STATE] * bm[:, g * D_STATE:(g + 1) * D_STATE],
                   axis=-1, keepdims=True) for g in range(SSM_G)]
    cb_e = jnp.where(first_grp, cbv[0], cbv[1])

    pad = jnp.zeros((LANES - n_seq, SSM_W), F32)
    xdt_t = jnp.concatenate([xdt, pad], axis=0).T
    dec_t = jnp.concatenate([dec_e, pad], axis=0).T
    sub8 = lax.broadcasted_iota(jnp.int32, (8, D_STATE), 0)
    lane_r = lax.broadcasted_iota(jnp.int32, (8, SSM_W), 1)

    for b in range(n_seq):
        h0 = h0_ref[b]
        c_row = jnp.broadcast_to(cm[b:b + 1, :], (8, gn))
        c8 = jnp.where(sub8 == 0, c_row[:, 0:D_STATE], jnp.where(sub8 == 1, c_row[:, D_STATE:], 0.0))
        r = _dot_nt(c8.astype(BF16), h0.astype(BF16))
        yoff_scr[b:b + 1, :] = jnp.where(lane_r[0:1] < gw, r[0:1, :], r[1:2, :])
        b_row = bm[b:b + 1, :]
        for g in range(SSM_G):
            rs = slice(g * gw, (g + 1) * gw)
            dcol = jnp.broadcast_to(dec_t[rs, b:b + 1], (gw, D_STATE))
            xcol = jnp.broadcast_to(xdt_t[rs, b:b + 1], (gw, D_STATE))
            hout_ref[b, rs, :] = h0[rs, :] * dcol + xcol * b_row[:, g * D_STATE:(g + 1) * D_STATE]

    y = cb_e * xdt + yoff_scr[...] * dec_e
    y = y + dskip_ref[...] * xs
    y = y * _silu(z_ref[...])
    y_ref[...] = _rms(y, g_ref[...]).astype(BF16)


def _ssd_sample(xb, sconv, z, h0, cw8, cb, dtb, alog, dskip_e, g_ssm):
    n = xb.shape[0]
    nb = SEQ_BLK
    row_spec = lambda w: pl.BlockSpec((nb, w), lambda i: (i, 0))
    st_spec = pl.BlockSpec((nb, SSM_W, D_STATE), lambda i: (i, 0, 0))
    full = lambda a: pl.BlockSpec(a.shape, lambda i: (0,) * a.ndim)
    return pl.pallas_call(
        functools.partial(_ssd_sample_kernel, n_seq=nb),
        grid=(n // nb,),
        in_specs=[row_spec(XBCDT_W), row_spec((CONV_W - 1) * CONV_DIM), row_spec(SSM_W), st_spec,
                  full(cw8), full(cb), full(dtb), full(alog), full(dskip_e), full(g_ssm)],
        out_specs=[row_spec(SSM_W), row_spec((CONV_W - 1) * CONV_DIM), st_spec],
        out_shape=[
            jax.ShapeDtypeStruct((n, SSM_W), BF16),
            jax.ShapeDtypeStruct((n, (CONV_W - 1) * CONV_DIM), F32),
            jax.ShapeDtypeStruct(h0.shape, F32),
        ],
        scratch_shapes=[pltpu.VMEM((nb, SSM_W), F32)],
        compiler_params=_params(("parallel",)),
        name="ssd_sample",
    )(xb, sconv, z, h0, cw8, cb, dtb, alog, dskip_e, g_ssm)


def _outproj_rows(att, ssm, x, woa_ref, wos_ref, gpost, gt1, gpre, sh2, sc2):
    mix = _dot(att, woa_ref[...]) + _dot(ssm, wos_ref[...])
    x1 = x + _rms(mix, gpost * gt1)
    u2 = _rms(x1, gpre * (1.0 + sc2)) + sh2
    return x1, u2.astype(BF16)


def _outproj_kernel(att_ref, ssm_ref, x_ref, woa_ref, wos_ref, gpost_ref, gt1_ref, gpre_ref, sh2_ref, sc2_ref,
                    atts_ref, ssms_ref, xs_ref, gt1s_ref, sh2s_ref, sc2s_ref, perm_ref,
                    x1_ref, u2_ref, x1s_ref, u2s_ref, *, rows):
    for r0 in range(0, rows, SUB_OUT):
        rs = slice(r0, r0 + SUB_OUT)
        x1_ref[rs, :], u2_ref[rs, :] = _outproj_rows(
            att_ref[rs, :], ssm_ref[rs, :], x_ref[rs, :], woa_ref, wos_ref, gpost_ref[...], gt1_ref[0:1, :],
            gpre_ref[...], sh2_ref[0:1, :], sc2_ref[0:1, :])

    @pl.when(pl.program_id(0) == pl.num_programs(0) - 1)
    def _():
        att_s = _dot_nt(atts_ref[...], perm_ref[...]).astype(BF16)
        x1s_ref[...], u2s_ref[...] = _outproj_rows(
            att_s, ssms_ref[...], xs_ref[...], woa_ref, wos_ref, gpost_ref[...], gt1s_ref[...],
            gpre_ref[...], sh2s_ref[...], sc2s_ref[...])


def _outproj(att, ssm, x, att_s, ssm_s, x_s, mod_p, mod_s, w_out, g_post, g_pre, perm):
    n_rows = x.shape[0]
    n_s = x_s.shape[0]
    rows = ROWS_OUT
    assert rows % SUB_OUT == 0 and n_rows % rows == 0
    half = lambda r: pl.BlockSpec((ATT_W, D_MODEL), lambda i: (r, 0), pipeline_mode=pl.Buffered(1))
    row_spec = lambda w: pl.BlockSpec((rows, w), lambda i: (i, 0))
    sample = lambda w, col=0: pl.BlockSpec((n_s, w), lambda i: (0, col))
    return pl.pallas_call(
        functools.partial(_outproj_kernel, rows=rows),
        grid=(n_rows // rows,),
        in_specs=[
            row_spec(ATT_W), row_spec(SSM_W), row_spec(D_MODEL),
            half(0), half(1),
            _const_spec((1, D_MODEL)),
            _mod_spec(2),
            _const_spec((1, D_MODEL)),
            _mod_spec(3),
            _mod_spec(4),
            sample(ATT_W), sample(SSM_W), sample(D_MODEL),
            sample(D_MODEL, 2), sample(D_MODEL, 3), sample(D_MODEL, 4), _const_spec(perm.shape),
        ],
        out_specs=[row_spec(D_MODEL), row_spec(D_MODEL), sample(D_MODEL), sample(D_MODEL)],
        out_shape=[
            jax.ShapeDtypeStruct((n_rows, D_MODEL), F32),
            jax.ShapeDtypeStruct((n_rows, D_MODEL), BF16),
            jax.ShapeDtypeStruct((n_s, D_MODEL), F32),
            jax.ShapeDtypeStruct((n_s, D_MODEL), BF16),
        ],
        compiler_params=_params(("arbitrary",)),
        name="outproj",
    )(att, ssm, x, w_out, w_out, g_post, mod_p, g_pre, mod_p, mod_p, att_s, ssm_s, x_s, mod_s, mod_s, mod_s, perm)


def _ffn_kernel(u_ref, x1_hbm, wg_ref, wu_ref, wd_ref, gpost_ref, gt2_ref, us_ref, x1s_ref, gt2s_ref,
                y_ref, ys_ref, x1_buf, x1_sem, *, rows):
    i = pl.program_id(0)
    j = pl.program_id(1)
    x1_copy = pltpu.make_async_copy(x1_hbm.at[pl.ds(i * rows, rows), :], x1_buf, x1_sem)
    last = pl.num_programs(1) - 1

    def ffn_rows(u, first, final, acc_prev, resid, gt2):
        hid = (_silu(_dot(u, wg_ref[...])) * _dot(u, wu_ref[...])).astype(BF16)
        acc = _dot(hid, wd_ref[...])
        if not first:
            acc = acc_prev() + acc
        if final:
            acc = resid() + gt2() * _rms(acc, gpost_ref[...])
        return acc

    def d_ff_slice(first, final):
        sub = min(SUB_FFN, rows)
        for r0 in range(0, rows, sub):
            rs = slice(r0, r0 + sub)
            y_ref[rs, :] = ffn_rows(u_ref[rs, :], first, final, lambda: y_ref[rs, :],
                                    lambda: x1_buf[rs, :], lambda: gt2_ref[0:1, :])

        @pl.when(i == pl.num_programs(0) - 1)
        def _():
            ys_ref[...] = ffn_rows(us_ref[...], first, final, lambda: ys_ref[...],
                                   lambda: x1s_ref[...], lambda: gt2s_ref[...])

    @pl.when(j == 0)
    def _():
        x1_copy.start()
        d_ff_slice(True, False)

    @pl.when(jnp.logical_and(j > 0, j < last))
    def _():
        d_ff_slice(False, False)

    @pl.when(j == last)
    def _():
        x1_copy.wait()
        d_ff_slice(False, True)


def _ffn(u2, x1, mod_p, u2_s, x1_s, mod_s, wg, wu, wd, g_post):
    n_rows = x1.shape[0]
    n_s = x1_s.shape[0]
    rows = ROWS_FFN
    fb = FF_BLK
    assert D_FF // fb >= 2 and n_rows % rows == 0
    whole = lambda r, c: pl.BlockSpec((r, D_MODEL), lambda i, j: (0, c))
    return pl.pallas_call(
        functools.partial(_ffn_kernel, rows=rows),
        grid=(n_rows // rows, D_FF // fb),
        in_specs=[
            pl.BlockSpec((rows, D_MODEL), lambda i, j: (i, 0)),
            pl.BlockSpec(memory_space=pl.ANY),
            pl.BlockSpec((D_MODEL, fb), lambda i, j: (0, j)),
            pl.BlockSpec((D_MODEL, fb), lambda i, j: (0, j)),
            pl.BlockSpec((fb, D_MODEL), lambda i, j: (j, 0)),
            whole(1, 0),
            _mod_spec(5),
            whole(n_s, 0), whole(n_s, 0), whole(n_s, 5),
        ],
        out_specs=[pl.BlockSpec((rows, D_MODEL), lambda i, j: (i, 0)), whole(n_s, 0)],
        out_shape=[jax.ShapeDtypeStruct((n_rows, D_MODEL), F32), jax.ShapeDtypeStruct((n_s, D_MODEL), F32)],
        scratch_shapes=[pltpu.VMEM((rows, D_MODEL), F32), pltpu.SemaphoreType.DMA],
        compiler_params=_params(("arbitrary", "arbitrary"), VMEM_LIMIT_FFN),
        name="ffn",
    )(u2, x1, wg, wu, wd, g_post, mod_p, u2_s, x1_s, mod_s)


def _alibi_slopes():
    return (2.0 ** (-8.0 * np.arange(1, N_HEADS + 1) / N_HEADS)).astype(np.float32)


def kernel(x_prompt, x_sample, cache_k, cache_v, state_conv, state_ssm, c_prompt, c_sample, w_ada, b_ada, g_pre_mix, g_post_mix, w_in, attn_sinks, g_attn_out, conv_w, conv_b, dt_bias, a_log, d_skip, g_ssm_out, w_out, g_pre_ffn, g_post_ffn, w_gate, w_up, w_down):
    assert w_ada.shape[0] == 1, "one layer"
    n_s = x_sample.shape[0]
    row = lambda v: v.reshape(1, -1)

    g_att = row(g_attn_out[0])
    swapped_cols = np.arange(ATT_W).reshape(N_KV, Q_PER_KV, HEAD_DIM).swapaxes(0, 1).reshape(-1)
    perm = np.zeros((ATT_W, ATT_W), np.float32)
    perm[swapped_cols, np.arange(ATT_W)] = 1.0
    perm = jnp.asarray(perm, BF16)
    g_att_s = row(g_attn_out[0].reshape(N_KV, Q_PER_KV, HEAD_DIM).swapaxes(0, 1))
    slopes_np = _alibi_slopes()
    slopes = jnp.asarray(slopes_np)
    slopes_s = jnp.asarray(slopes_np.reshape(N_KV, Q_PER_KV).T.reshape(N_HEADS, 1))
    sinks = attn_sinks[0]
    sinks_s = sinks.reshape(N_KV, Q_PER_KV).T.reshape(N_HEADS, 1)
    cw8 = jnp.pad(conv_w[0], ((0, 8 - CONV_W), (0, 0)))
    cb = row(conv_b[0])
    dtb = jnp.pad(row(dt_bias[0]), ((0, 0), (0, LANES - SSM_H)))
    alog = jnp.pad(row(a_log[0]), ((0, 0), (0, LANES - SSM_H)))
    dskip_e = row(jnp.repeat(d_skip[0], SSM_P))
    g_ssm = row(g_ssm_out[0])

    c_p8 = jnp.pad(c_prompt, ((0, 8 - c_prompt.shape[0]), (0, 0)))
    mod_s, mod_p, w_all = _prep(c_sample, c_p8, w_ada[0], row(b_ada[0]), jnp.transpose(w_in[0]))

    xp = x_prompt[0]
    xs_ = x_sample[:, 0, :]
    q, kv, z, xb, kv_last, x_tail, q_s, kv_new, z_s, xb_s = _inproj(
        xp, xs_, row(g_pre_mix[0]), mod_p, mod_s, w_all, cw8, cb, perm)

    att, wg, wu, wd, wo = _attn_prompt(q, kv, slopes, sinks, g_att, w_gate[0], w_up[0], w_down[0], w_out[0])
    ssm, h_p = _ssd_prompt(xb, z, dtb, alog, dskip_e, g_ssm)
    keys_minor = lambda c: jnp.transpose(c, (0, 2, 3, 1)).reshape(n_s, KV_W, WINDOW)
    keys_major = lambda c: jnp.transpose(c.reshape(n_s, N_KV, HEAD_DIM, WINDOW), (0, 3, 1, 2))[None]
    att_s, k_s, v_s = _attn_sample(
        q_s, kv_new[:, :KV_W], kv_new[:, KV_W:],
        keys_minor(cache_k[0]), keys_minor(cache_v[0]), slopes_s, sinks_s, g_att_s)
    ssm_s, conv_s, h_s = _ssd_sample(
        xb_s, state_conv[0].reshape(n_s, (CONV_W - 1) * CONV_DIM), z_s,
        state_ssm[0].reshape(n_s, SSM_W, D_STATE), cw8, cb, dtb, alog, dskip_e, g_ssm)

    x1, u2, x1_s, u2_s = _outproj(att, ssm, xp, att_s, ssm_s, xs_, mod_p, mod_s, wo,
                                  row(g_post_mix[0]), row(g_pre_ffn[0]), perm)
    y_p, y_s = _ffn(u2, x1, mod_p, u2_s, x1_s, mod_s, wg, wu, wd, row(g_post_ffn[0]))

    return (
        y_p[None],
        y_s[:, None, :],
        kv_last[:, :KV_W].reshape(1, 1, WINDOW, N_KV, HEAD_DIM),
        kv_last[:, KV_W:].reshape(1, 1, WINDOW, N_KV, HEAD_DIM),
        x_tail[8 - (CONV_W - 1):].reshape(1, 1, CONV_W - 1, CONV_DIM),
        h_p.reshape(1, 1, SSM_H, SSM_P, D_STATE),
        keys_major(k_s),
        keys_major(v_s),
        conv_s.reshape(1, n_s, CONV_W - 1, CONV_DIM),
        h_s.reshape(1, n_s, SSM_H, SSM_P, D_STATE),
    )
```

```python
import functools

import numpy as np
import jax
import jax.numpy as jnp
from jax import lax
from jax.experimental import pallas as pl
from jax.experimental.pallas import tpu as pltpu

F32 = jnp.float32
BF16 = jnp.bfloat16

D_MODEL = 2048
ATT_W = 1024
HEAD_DIM = 64
N_HEADS = 16
N_KV = 4
Q_PER_KV = 4
KV_W = N_KV * HEAD_DIM
WINDOW = 128
SSM_W = 1024
SSM_P = 64
SSM_H = 16
SSM_G = 2
D_STATE = 128
CONV_W = 4
CONV_DIM = SSM_W + 2 * SSM_G * D_STATE
XBCDT_W = CONV_DIM + 128
D_FF = 5632
EPS = 1e-6
CHUNK = 128
NEG_INF = float("-inf")
LOG2E = 1.4426950408889634

VMEM_LIMIT = 56 * 1024 * 1024
VMEM_LIMIT_FFN = 60 * 1024 * 1024
LANES = 128

ROWS_IN = 512
ROWS_ATT = 512
ROWS_SSD = 1024
ROWS_OUT = 512
SUB_OUT = 128
ROWS_FFN = 1024
SUB_FFN = 512
FF_BLK = 512
SEQ_BLK = 8


def _params(sem, vmem=VMEM_LIMIT):
    return pltpu.CompilerParams(dimension_semantics=sem, vmem_limit_bytes=vmem)


def _silu(v):
    h = 0.5 * v
    return h + h * jnp.tanh(h)


def _softplus(v):
    return jnp.maximum(v, 0.0) + jnp.log1p(jnp.exp(-jnp.abs(v)))


def _rms(v, g):
    return v * lax.rsqrt(jnp.mean(v * v, axis=-1, keepdims=True) + EPS) * g


def _split3(v):
    hi = v.astype(BF16)
    r1 = v - hi.astype(F32)
    mid = r1.astype(BF16)
    lo = (r1 - mid.astype(F32)).astype(BF16)
    return hi, mid, lo


def _dot(a, b):
    return jnp.dot(a, b, preferred_element_type=F32)


def _dot_nt(a, b):
    return lax.dot_general(a, b, (((1,), (1,)), ((), ())), preferred_element_type=F32)


def _dot_tn(a, b):
    return lax.dot_general(a, b, (((0,), (0,)), ((), ())), preferred_element_type=F32)


def _expand_heads(v, e_mat):
    hi, mid, lo = _split3(v)
    return _dot(hi, e_mat) + _dot(mid, e_mat) + _dot(lo, e_mat)


def _head_expand_matrix():
    k = lax.broadcasted_iota(jnp.int32, (LANES, SSM_W), 0)
    c = lax.broadcasted_iota(jnp.int32, (LANES, SSM_W), 1)
    return jnp.where((c // SSM_P) == k, 1.0, 0.0).astype(BF16)


def _mod_spec(col):
    return pl.BlockSpec((8, D_MODEL), lambda i, *_: (0, col))


WS_BLK = 256
W_ALL_COLS = 4608
COL_X, COL_Q, COL_Z, COL_KV = 0, 2048, 3072, 4096
N_WS_SRC = -(-(ATT_W + 2 * KV_W + SSM_W + CONV_DIM + SSM_H) // WS_BLK)


def _ws_dst_block(s):
    q_end = ATT_W // WS_BLK
    kv_end = q_end + 2 * KV_W // WS_BLK
    z_end = kv_end + SSM_W // WS_BLK
    return jnp.where(s < q_end, COL_Q // WS_BLK + s,
                     jnp.where(s < kv_end, COL_KV // WS_BLK + s - q_end,
                               jnp.where(s < z_end, COL_Z // WS_BLK + s - kv_end,
                                         COL_X // WS_BLK + s - z_end)))


ADA_BLK = 768


def _prep_kernel(cs_ref, cp_ref, wa_ref, b_ref, wt_ref, ms_ref, mp_ref, wall_ref):
    s = pl.program_id(0)
    w = wa_ref[...].astype(BF16)
    b = b_ref[...]
    ms_ref[...] = _dot(_silu(cs_ref[...]).astype(BF16), w) + b
    mp_ref[...] = _dot(_silu(cp_ref[...]).astype(BF16), w) + b
    n_valid = jnp.where(s < N_WS_SRC - 1, WS_BLK, jnp.where(s == N_WS_SRC - 1, SSM_H, 0))
    lane = lax.broadcasted_iota(jnp.int32, wall_ref.shape, 1)
    wall_ref[...] = jnp.where(lane < n_valid, wt_ref[...].T, 0.0).astype(BF16)


def _prep(c_s, c_p8, w_ada, b_ada, w_in_t):
    k_dim = w_in_t.shape[1]
    n = w_ada.shape[1]
    n_ada = n // ADA_BLK
    assert n % ADA_BLK == 0 and n_ada <= N_WS_SRC + 1
    ada_col = lambda s: (0, jnp.minimum(s, n_ada - 1))
    return pl.pallas_call(
        _prep_kernel,
        grid=(N_WS_SRC + 1,),
        in_specs=[
            pl.BlockSpec(c_s.shape, lambda s: (0, 0)),
            pl.BlockSpec(c_p8.shape, lambda s: (0, 0)),
            pl.BlockSpec((D_MODEL, ADA_BLK), ada_col),
            pl.BlockSpec((1, ADA_BLK), ada_col),
            pl.BlockSpec((WS_BLK, k_dim), lambda s: (jnp.minimum(s, N_WS_SRC - 1), 0)),
        ],
        out_specs=[
            pl.BlockSpec((c_s.shape[0], ADA_BLK), ada_col),
            pl.BlockSpec((c_p8.shape[0], ADA_BLK), ada_col),
            pl.BlockSpec((k_dim, WS_BLK), lambda s: (0, _ws_dst_block(s))),
        ],
        out_shape=[
            jax.ShapeDtypeStruct((c_s.shape[0], n), F32),
            jax.ShapeDtypeStruct((c_p8.shape[0], n), F32),
            jax.ShapeDtypeStruct((k_dim, W_ALL_COLS), BF16),
        ],
        compiler_params=_params(("arbitrary",)),
        name="prep",
    )(c_s, c_p8, w_ada, b_ada, w_in_t)


def _modulated_norm(x, g, shift, scale):
    return (_rms(x, g) * (1.0 + scale) + shift).astype(BF16)


def _inproj_kernel(x_ref, g_ref, sh_ref, sc_ref, wq_ref, wkv_ref, wz_ref, wx_ref, cw_ref, cb_ref,
                   xs_ref, shs_ref, scs_ref, perm_ref,
                   q_ref, kv_ref, z_ref, xb_ref, kvlast_ref, xtail_ref, qs_ref, kvs_ref, zs_ref, xbs_ref,
                   xp_scr, *, rows):
    i = pl.program_id(0)
    tile = 2 * LANES

    @pl.when(i == 0)
    def _():
        xp_scr[:, rows:rows + 8, :] = jnp.zeros((CONV_DIM // LANES, 8, LANES), F32)

    u = _modulated_norm(x_ref[...], g_ref[...], sh_ref[0:1, :], sc_ref[0:1, :])

    def conv_slab(s, xs):
        ls = slice(s * LANES, (s + 1) * LANES)
        xtail_ref[:, ls] = xs[rows - 8:, :]
        xp_scr[s, 0:8, :] = xp_scr[s, rows:rows + 8, :]
        xp_scr[s, 8:, :] = xs
        acc = cb_ref[:, ls] + xp_scr[s, pl.ds(8 - (CONV_W - 1), rows), :] * cw_ref[0:1, ls]
        for k in range(1, CONV_W):
            acc = acc + xp_scr[s, pl.ds(8 - (CONV_W - 1) + k, rows), :] * cw_ref[k:k + 1, ls]
        xb_ref[:, ls] = _silu(acc)

    def post_z(c0, c1):
        def post(r):
            z_ref[:, c0:c1] = _silu(r)
        return post

    def post_q(c0, c1):
        def post(r):
            q_ref[:, c0:c1] = (r * (HEAD_DIM ** -0.5)).astype(BF16)
        return post

    def post_kv(c0, c1):
        def post(r):
            kv_ref[:, c0:c1] = r.astype(BF16)
            kvlast_ref[:, c0:c1] = r[rows - WINDOW:, :]
        return post

    xb = _dot(u, wx_ref[...])
    xb_ref[:, CONV_DIM:] = xb[:, CONV_DIM:]
    slabs = [functools.partial(conv_slab, s, xb[:, s * LANES:(s + 1) * LANES]) for s in range(CONV_DIM // LANES)]
    jobs = []
    for w_ref, width, post_of in ((wz_ref, SSM_W, post_z), (wq_ref, ATT_W, post_q), (wkv_ref, 2 * KV_W, post_kv)):
        for c0 in range(0, width, tile):
            jobs.append((w_ref, c0, c0 + tile, post_of(c0, c0 + tile)))
    pending = None
    for n, (w_ref, c0, c1, post) in enumerate(jobs):
        r = _dot(u, w_ref[:, c0:c1])
        for slab in slabs[n * len(slabs) // len(jobs):(n + 1) * len(slabs) // len(jobs)]:
            slab()
        if pending is not None:
            pending[0](pending[1])
        pending = (post, r)
    pending[0](pending[1])

    @pl.when(i == pl.num_programs(0) - 1)
    def _():
        us = _modulated_norm(xs_ref[...], g_ref[...], shs_ref[...], scs_ref[...])
        qs = (_dot(us, wq_ref[...]) * (HEAD_DIM ** -0.5)).astype(BF16)
        qs_ref[...] = _dot(qs, perm_ref[...]).astype(BF16)
        kvs_ref[...] = _dot(us, wkv_ref[...])
        zs_ref[...] = _dot(us, wz_ref[...])
        xbs_ref[...] = _dot(us, wx_ref[...])


def _const_spec(shape):
    return pl.BlockSpec(shape, lambda i, *_: (0,) * len(shape), pipeline_mode=pl.Buffered(1))


def _inproj(x, x_s, g, mod_p, mod_s, w_all, cw8, cb, perm):
    n_rows = x.shape[0]
    n_s = x_s.shape[0]
    rows = ROWS_IN
    row_spec = lambda w: pl.BlockSpec((rows, w), lambda i: (i, 0))
    seg_spec = lambda width, col: pl.BlockSpec((D_MODEL, width), lambda i: (0, col // width),
                                               pipeline_mode=pl.Buffered(1))
    sample_in = lambda col: pl.BlockSpec((n_s, D_MODEL), lambda i: (0, col), pipeline_mode=pl.Buffered(1))
    sample_out = lambda w: pl.BlockSpec((n_s, w), lambda i: (0, 0))
    return pl.pallas_call(
        functools.partial(_inproj_kernel, rows=rows),
        grid=(n_rows // rows,),
        in_specs=[
            row_spec(D_MODEL),
            _const_spec((1, D_MODEL)),
            _mod_spec(0),
            _mod_spec(1),
            seg_spec(ATT_W, COL_Q), seg_spec(2 * KV_W, COL_KV), seg_spec(SSM_W, COL_Z), seg_spec(XBCDT_W, COL_X),
            _const_spec(cw8.shape), _const_spec(cb.shape),
            sample_in(0), sample_in(0), sample_in(1), _const_spec(perm.shape),
        ],
        out_specs=[
            row_spec(ATT_W), row_spec(2 * KV_W), row_spec(SSM_W), row_spec(XBCDT_W),
            pl.BlockSpec((WINDOW, 2 * KV_W), lambda i: (0, 0)),
            pl.BlockSpec((8, CONV_DIM), lambda i: (0, 0)),
            sample_out(ATT_W), sample_out(2 * KV_W), sample_out(SSM_W), sample_out(XBCDT_W),
        ],
        out_shape=[
            jax.ShapeDtypeStruct((n_rows, ATT_W), BF16),
            jax.ShapeDtypeStruct((n_rows, 2 * KV_W), BF16),
            jax.ShapeDtypeStruct((n_rows, SSM_W), F32),
            jax.ShapeDtypeStruct((n_rows, XBCDT_W), F32),
            jax.ShapeDtypeStruct((WINDOW, 2 * KV_W), F32),
            jax.ShapeDtypeStruct((8, CONV_DIM), F32),
            jax.ShapeDtypeStruct((n_s, ATT_W), BF16),
            jax.ShapeDtypeStruct((n_s, 2 * KV_W), F32),
            jax.ShapeDtypeStruct((n_s, SSM_W), F32),
            jax.ShapeDtypeStruct((n_s, XBCDT_W), F32),
        ],
        scratch_shapes=[pltpu.VMEM((CONV_DIM // LANES, rows + 8, LANES), F32)],
        compiler_params=_params(("arbitrary",), VMEM_LIMIT_FFN),
        name="inproj",
    )(x, g, mod_p, mod_p, w_all, w_all, w_all, w_all, cw8, cb, x_s, mod_s, mod_s, perm)


def _pair_blockdiag(lo_src, hi_src, keep_lo, keep_hi):
    zero = jnp.zeros_like(lo_src)
    return jnp.concatenate([jnp.where(keep_lo, lo_src, zero), jnp.where(keep_hi, hi_src, zero)], axis=0)


def _attn_prompt_kernel(slopes_ref, sinks_ref, q_ref, kvc_ref, kvp_ref, g_ref, wg_ref, wu_ref, wd_ref, wo_ref,
                        o_ref, wg_o, wu_o, wd_o, wo_o, bias_scr, kvdup_scr, *, n_sub):
    i = pl.program_id(0)
    blk = WINDOW

    wg_o[...] = wg_ref[...].astype(BF16)
    wu_o[...] = wu_ref[...].astype(BF16)
    wd_o[...] = wd_ref[...].astype(BF16)
    wo_o[...] = wo_ref[...].astype(BF16)

    @pl.when(i == 0)
    def _():
        a = lax.broadcasted_iota(jnp.int32, (blk, 2 * blk), 0)
        j = lax.broadcasted_iota(jnp.int32, (blk, 2 * blk), 1)
        dist = a + blk - j
        valid = (dist >= 0) & (dist < WINDOW)
        distf = dist.astype(F32)
        for h in range(N_HEADS):
            kvh, g = divmod(h, Q_PER_KV)
            pr, t = divmod(g, 2)
            b = jnp.where(valid, -(slopes_ref[h] * distf), NEG_INF)
            b = jnp.where(j == 0, sinks_ref[h], b)
            rs = slice(pr * blk, (pr + 1) * blk)
            cs = slice(t * 2 * blk, (t + 1) * 2 * blk)
            bias_scr[1, kvh, rs, cs] = b
            bias_scr[0, kvh, rs, cs] = jnp.where((j >= blk) | (j == 0), b, NEG_INF)

    for src_ref, r_lo, r_hi in ((kvp_ref, 0, blk), (kvc_ref, blk, kvdup_scr.shape[0])):
        lo_half = lax.broadcasted_iota(jnp.int32, (r_hi - r_lo, LANES), 1) < HEAD_DIM
        for part in range(2):
            for s in range(2):
                c_in = part * KV_W + s * LANES
                c_out = part * N_KV * LANES + 2 * s * LANES
                x = src_ref[:, c_in:c_in + LANES].astype(F32)
                xr = pltpu.roll(x, HEAD_DIM, 1)
                kvdup_scr[r_lo:r_hi, c_out:c_out + LANES] = jnp.where(lo_half, x, xr).astype(BF16)
                kvdup_scr[r_lo:r_hi, c_out + LANES:c_out + 2 * LANES] = jnp.where(lo_half, xr, x).astype(BF16)

    lane = lax.broadcasted_iota(jnp.int32, (2 * blk, LANES), 1)
    not_sink = lax.broadcasted_iota(jnp.int32, (2 * blk, LANES), 0) != 0
    keep_lo = (lane < HEAD_DIM) & not_sink
    keep_hi = (lane >= HEAD_DIM) & not_sink
    ones_cols = jnp.concatenate([jnp.where(lane < HEAD_DIM, 1.0, 0.0),
                                 jnp.where(lane >= HEAD_DIM, 1.0, 0.0)], axis=0).astype(BF16)
    n_chunk = 4
    rc = 2 * blk // n_chunk

    def body(b, carry):
        r0 = pl.multiple_of(b * blk, blk)
        variant = jnp.where(jnp.logical_and(i == 0, b == 0), 0, 1)
        vbds, scores = [], []
        for kvh in range(N_KV):
            c0 = kvh * Q_PER_KV * HEAD_DIM
            kd = kvdup_scr[pl.ds(r0, 2 * blk), kvh * LANES:(kvh + 1) * LANES]
            vd = kvdup_scr[pl.ds(r0, 2 * blk), (N_KV + kvh) * LANES:(N_KV + kvh + 1) * LANES]
            kbd = _pair_blockdiag(kd, kd, keep_lo, keep_hi)
            vbds.append(jnp.concatenate([_pair_blockdiag(vd, vd, keep_lo, keep_hi), ones_cols], axis=1))
            q2 = jnp.concatenate([q_ref[pl.ds(r0, blk), c0:c0 + LANES],
                                  q_ref[pl.ds(r0, blk), c0 + LANES:c0 + 2 * LANES]], axis=0)
            scores.append(_dot_nt(q2, kbd) + bias_scr[variant, kvh])
        probs = []
        for kvh in range(N_KV):
            chunks = []
            for c in range(n_chunk):
                st = scores[kvh][c * rc:(c + 1) * rc, :]
                sl = st[:, 0:2 * blk]
                sr = st[:, 2 * blk:]
                p_l = jnp.exp(sl - jnp.max(sl, axis=-1, keepdims=True))
                p_r = jnp.exp(sr - jnp.max(sr, axis=-1, keepdims=True))
                chunks.append(jnp.concatenate([p_l, p_r], axis=1).astype(BF16))
            probs.append(jnp.concatenate(chunks, axis=0))
        outs = []
        for kvh in range(N_KV):
            o2 = _dot(probs[kvh], vbds[kvh])
            o = o2[:, 0:LANES] * (1.0 / o2[:, LANES:])
            outs += [o[0:blk], o[blk:]]
        att = jnp.concatenate(outs, axis=1)
        o_ref[pl.ds(r0, blk), :] = _rms(att, g_ref[...]).astype(BF16)
        return carry

    lax.fori_loop(0, n_sub, body, 0, unroll=2)


def _attn_prompt(q, kv, slopes, sinks, g_att, w_gate, w_up, w_down, w_out):
    n_rows = q.shape[0]
    rows = ROWS_ATT
    n_sub = rows // WINDOW
    n_steps = n_rows // rows
    smem = pl.BlockSpec(memory_space=pltpu.SMEM)
    slab = lambda w: pl.BlockSpec((w.shape[0] // n_steps, w.shape[1]), lambda i: (i, 0))
    ffn_w = (w_gate, w_up, w_down, w_out)
    assert all(w.shape[0] % (16 * n_steps) == 0 for w in ffn_w)
    return pl.pallas_call(
        functools.partial(_attn_prompt_kernel, n_sub=n_sub),
        grid=(n_rows // rows,),
        in_specs=[
            smem, smem,
            pl.BlockSpec((rows, ATT_W), lambda i: (i, 0)),
            pl.BlockSpec((rows, 2 * KV_W), lambda i: (i, 0)),
            pl.BlockSpec((WINDOW, 2 * KV_W), lambda i: (jnp.maximum(i * n_sub - 1, 0), 0)),
            pl.BlockSpec((1, ATT_W), lambda i: (0, 0)),
            *[slab(w) for w in ffn_w],
        ],
        out_specs=[pl.BlockSpec((rows, ATT_W), lambda i: (i, 0)), *[slab(w) for w in ffn_w]],
        out_shape=[jax.ShapeDtypeStruct((n_rows, ATT_W), BF16),
                   *[jax.ShapeDtypeStruct(w.shape, BF16) for w in ffn_w]],
        scratch_shapes=[
            pltpu.VMEM((2, N_KV, 2 * WINDOW, 4 * WINDOW), F32),
            pltpu.VMEM((rows + WINDOW, 2 * N_KV * LANES), BF16),
        ],
        compiler_params=_params(("arbitrary",)),
        name="attn_prompt",
    )(slopes, sinks, q, kv, kv, g_att, *ffn_w)


def _attn_sample_kernel(q_ref, kn_ref, vn_ref, ck_ref, cv_ref, slope_ref, sink_ref, g_ref,
                        o_ref, ko_ref, vo_ref, att_scr, *, n_seq):
    r16 = lax.broadcasted_iota(jnp.int32, (N_HEADS, ATT_W), 0)
    c16 = lax.broadcasted_iota(jnp.int32, (N_HEADS, ATT_W), 1)
    own_head = (c16 // HEAD_DIM) == r16
    newest = lax.broadcasted_iota(jnp.int32, (KV_W, WINDOW), 1) == WINDOW - 1
    jj = lax.broadcasted_iota(jnp.int32, (N_HEADS, WINDOW), 1)
    bias = -(slope_ref[...] * (WINDOW - 1 - jj).astype(F32))
    pad = jnp.zeros((LANES - n_seq, KV_W), F32)
    kn_t = jnp.concatenate([kn_ref[...], pad], axis=0).T
    vn_t = jnp.concatenate([vn_ref[...], pad], axis=0).T

    def fold4(v):
        return (v[:, 0:KV_W] + v[:, KV_W:2 * KV_W]) + (v[:, 2 * KV_W:3 * KV_W] + v[:, 3 * KV_W:])

    scores, values = [], []
    for b in range(n_seq):
        kw = jnp.where(newest, jnp.broadcast_to(kn_t[:, b:b + 1], (KV_W, WINDOW)),
                       pltpu.roll(ck_ref[b], WINDOW - 1, 1))
        vw = jnp.where(newest, jnp.broadcast_to(vn_t[:, b:b + 1], (KV_W, WINDOW)),
                       pltpu.roll(cv_ref[b], WINDOW - 1, 1))
        ko_ref[b] = kw
        vo_ref[b] = vw
        qb = jnp.broadcast_to(q_ref[b:b + 1, :].astype(F32), (N_HEADS, ATT_W))
        qbd = fold4(jnp.where(own_head, qb, 0.0))
        scores.append(_dot(qbd.astype(BF16), kw.astype(BF16)))
        values.append(vw.astype(BF16))
    st = jnp.concatenate(scores, axis=0) + jnp.concatenate([bias] * n_seq, axis=0)
    sink = jnp.concatenate([sink_ref[...]] * n_seq, axis=0)
    m = jnp.maximum(jnp.max(st, axis=-1, keepdims=True), sink)
    p = jnp.exp(st - m)
    inv_l = 1.0 / (jnp.sum(p, axis=-1, keepdims=True) + jnp.exp(sink - m))
    pb = p.astype(BF16)
    for b in range(n_seq):
        hs = slice(b * N_HEADS, (b + 1) * N_HEADS)
        o = _dot_nt(pb[hs, :], values[b]) * inv_l[hs, :]
        o4 = jnp.concatenate([o, o, o, o], axis=1)
        att_scr[b:b + 1, :] = jnp.sum(jnp.where(own_head, o4, 0.0), axis=0, keepdims=True)
    o_ref[...] = _rms(att_scr[...], g_ref[...]).astype(BF16)


def _attn_sample(q, kn, vn, ck, cv, slope_col, sink_col, g_att):
    n = q.shape[0]
    nb = SEQ_BLK
    cache_spec = pl.BlockSpec((nb, KV_W, WINDOW), lambda i: (i, 0, 0))
    row_spec = lambda w: pl.BlockSpec((nb, w), lambda i: (i, 0))
    full = lambda a: pl.BlockSpec(a.shape, lambda i: (0,) * a.ndim)
    return pl.pallas_call(
        functools.partial(_attn_sample_kernel, n_seq=nb),
        grid=(n // nb,),
        in_specs=[row_spec(ATT_W), row_spec(KV_W), row_spec(KV_W), cache_spec, cache_spec,
                  full(slope_col), full(sink_col), full(g_att)],
        out_specs=[row_spec(ATT_W), cache_spec, cache_spec],
        out_shape=[
            jax.ShapeDtypeStruct((n, ATT_W), BF16),
            jax.ShapeDtypeStruct(ck.shape, F32),
            jax.ShapeDtypeStruct(cv.shape, F32),
        ],
        scratch_shapes=[pltpu.VMEM((nb, ATT_W), F32)],
        compiler_params=_params(("parallel",)),
        name="attn_sample",
    )(q, kn, vn, ck, cv, slope_col, sink_col, g_att)


def _ssd_prompt_kernel(xb_ref, z_ref, dtb_ref, alog_ref, dskip_ref, g_ref,
                       y_ref, hout_ref, ht_scr, e_scr, tri_scr, yd_scr, *, rows):
    i = pl.program_id(0)
    T = CHUNK

    @pl.when(i == 0)
    def _():
        ht_scr[...] = jnp.zeros_like(ht_scr)
        e_scr[...] = _head_expand_matrix()
        l = lax.broadcasted_iota(jnp.int32, (T, T), 0)
        s = lax.broadcasted_iota(jnp.int32, (T, T), 1)
        tri_scr[...] = jnp.where(s <= l, 1.0, 0.0).astype(BF16)

    a_log2 = -jnp.exp(alog_ref[...]) * LOG2E
    e_mat = e_scr[...]
    tri = tri_scr[...]
    causal = lax.broadcasted_iota(jnp.int32, (T, T), 0) >= lax.broadcasted_iota(jnp.int32, (T, T), 1)
    lo = lax.broadcasted_iota(jnp.int32, (T, LANES), 1) < SSM_P
    gw = SSM_W // SSM_G

    for c in range(rows // T):
        rs = slice(c * T, (c + 1) * T)
        dt = _softplus(xb_ref[rs, CONV_DIM:XBCDT_W] + dtb_ref[...])
        dta = dt * a_log2
        hi, mid, lw = _split3(dta)
        acum = _dot(tri, hi) + _dot(tri, mid) + _dot(tri, lw)
        acum_e = _expand_heads(acum, e_mat)
        dt_e = _expand_heads(dt, e_mat)
        acum_t = acum.T
        xs = xb_ref[rs, 0:SSM_W]
        bm = xb_ref[rs, SSM_W:SSM_W + SSM_G * D_STATE]
        cm = xb_ref[rs, SSM_W + SSM_G * D_STATE:CONV_DIM]
        xdt = xs * dt_e
        last = acum_e[T - 1:T, :]
        xdt_b = xdt.astype(BF16)
        xdec_b = (xdt * jnp.exp2(last - acum_e)).astype(BF16)
        bm_b = bm.astype(BF16)
        cm_b = cm.astype(BF16)
        cbs = [_dot_nt(cm_b[:, g * D_STATE:(g + 1) * D_STATE], bm_b[:, g * D_STATE:(g + 1) * D_STATE])
               for g in range(SSM_G)]
        for pr in range(SSM_H // 2):
            g = (2 * pr) // (SSM_H // SSM_G)
            ws = []
            for t in range(2):
                h = 2 * pr + t
                seg = jnp.broadcast_to(acum[:, h:h + 1], (T, T)) - acum_t[h:h + 1, :]
                ws.append((cbs[g] * jnp.exp2(jnp.where(causal, seg, NEG_INF))).astype(BF16))
            xsl = xdt_b[:, pr * LANES:(pr + 1) * LANES]
            yd_scr[:, pr * LANES:(pr + 1) * LANES] = _dot(jnp.concatenate(ws, axis=1),
                                                          _pair_blockdiag(xsl, xsl, lo, ~lo))
        yoff = []
        for g in range(SSM_G):
            gs = slice(g * gw, (g + 1) * gw)
            ht_g = ht_scr[:, gs]
            yoff.append(_dot(cm_b[:, g * D_STATE:(g + 1) * D_STATE], ht_g.astype(BF16)))
            cst = _dot_tn(bm_b[:, g * D_STATE:(g + 1) * D_STATE], xdec_b[:, gs])
            ht_scr[:, gs] = ht_g * jnp.exp2(last[:, gs]) + cst
        y = yd_scr[...] + jnp.concatenate(yoff, axis=1) * jnp.exp2(acum_e)
        y = y + dskip_ref[...] * xs
        y = y * z_ref[rs, :]
        y_ref[rs, :] = _rms(y, g_ref[...]).astype(BF16)

    @pl.when(i == pl.num_programs(0) - 1)
    def _():
        hout_ref[...] = ht_scr[...].T


def _ssd_prompt(xb, z, dtb, alog, dskip_e, g_ssm):
    n_rows = xb.shape[0]
    rows = ROWS_SSD
    full = lambda a: pl.BlockSpec(a.shape, lambda i: (0,) * a.ndim)
    return pl.pallas_call(
        functools.partial(_ssd_prompt_kernel, rows=rows),
        grid=(n_rows // rows,),
        in_specs=[
            pl.BlockSpec((rows, XBCDT_W), lambda i: (i, 0)),
            pl.BlockSpec((rows, SSM_W), lambda i: (i, 0)),
            full(dtb), full(alog), full(dskip_e), full(g_ssm),
        ],
        out_specs=[
            pl.BlockSpec((rows, SSM_W), lambda i: (i, 0)),
            pl.BlockSpec((SSM_W, D_STATE), lambda i: (0, 0)),
        ],
        out_shape=[
            jax.ShapeDtypeStruct((n_rows, SSM_W), BF16),
            jax.ShapeDtypeStruct((SSM_W, D_STATE), F32),
        ],
        scratch_shapes=[
            pltpu.VMEM((D_STATE, SSM_W), F32),
            pltpu.VMEM((LANES, SSM_W), BF16),
            pltpu.VMEM((CHUNK, CHUNK), BF16),
            pltpu.VMEM((CHUNK, SSM_W), F32),
        ],
        compiler_params=_params(("arbitrary",)),
        name="ssd_prompt",
    )(xb, z, dtb, alog, dskip_e, g_ssm)


def _ssd_sample_kernel(xb_ref, sconv_ref, z_ref, h0_ref, cw_ref, cb_ref, dtb_ref, alog_ref, dskip_ref, g_ref,
                       y_ref, conv_ref, hout_ref, yoff_scr, *, n_seq):
    gw = SSM_W // SSM_G
    gn = SSM_G * D_STATE
    x_new = xb_ref[:, 0:CONV_DIM]
    taps = [sconv_ref[:, k * CONV_DIM:(k + 1) * CONV_DIM] for k in range(CONV_W - 1)] + [x_new]
    acc = cb_ref[...]
    for k in range(CONV_W):
        acc = acc + taps[k] * cw_ref[k:k + 1, :]
    for k in range(1, CONV_W):
        conv_ref[:, (k - 1) * CONV_DIM:k * CONV_DIM] = taps[k]
    xc = _silu(acc)
    xs = xc[:, 0:SSM_W]
    bm = xc[:, SSM_W:SSM_W + gn]
    cm = xc[:, SSM_W + gn:CONV_DIM]

    e_mat = _head_expand_matrix()
    dt = _softplus(xb_ref[:, CONV_DIM:XBCDT_W] + dtb_ref[...])
    dt_e = _expand_heads(dt, e_mat)
    dec_e = jnp.exp(_expand_heads(dt * (-jnp.exp(alog_ref[...])), e_mat))
    xdt = xs * dt_e

    lane_w = lax.broadcasted_iota(jnp.int32, (n_seq, SSM_W), 1)
    first_grp = lane_w < gw
    cbv = [jnp.sum(cm[:, g * D_STATE:(g + 1) * D_STATE] * bm[:, g * D_STATE:(g + 1) * D_STATE],
                   axis=-1, keepdims=True) for g in range(SSM_G)]
    cb_e = jnp.where(first_grp, cbv[0], cbv[1])

    pad = jnp.zeros((LANES - n_seq, SSM_W), F32)
    xdt_t = jnp.concatenate([xdt, pad], axis=0).T
    dec_t = jnp.concatenate([dec_e, pad], axis=0).T
    sub8 = lax.broadcasted_iota(jnp.int32, (8, D_STATE), 0)
    lane_r = lax.broadcasted_iota(jnp.int32, (8, SSM_W), 1)

    for b in range(n_seq):
        h0 = h0_ref[b]
        c_row = jnp.broadcast_to(cm[b:b + 1, :], (8, gn))
        c8 = jnp.where(sub8 == 0, c_row[:, 0:D_STATE], jnp.where(sub8 == 1, c_row[:, D_STATE:], 0.0))
        r = _dot_nt(c8.astype(BF16), h0.astype(BF16))
        yoff_scr[b:b + 1, :] = jnp.where(lane_r[0:1] < gw, r[0:1, :], r[1:2, :])
        b_row = bm[b:b + 1, :]
        for g in range(SSM_G):
            rs = slice(g * gw, (g + 1) * gw)
            dcol = jnp.broadcast_to(dec_t[rs, b:b + 1], (gw, D_STATE))
            xcol = jnp.broadcast_to(xdt_t[rs, b:b + 1], (gw, D_STATE))
            hout_ref[b, rs, :] = h0[rs, :] * dcol + xcol * b_row[:, g * D_STATE:(g + 1) * D_STATE]

    y = cb_e * xdt + yoff_scr[...] * dec_e
    y = y + dskip_ref[...] * xs
    y = y * _silu(z_ref[...])
    y_ref[...] = _rms(y, g_ref[...]).astype(BF16)


def _ssd_sample(xb, sconv, z, h0, cw8, cb, dtb, alog, dskip_e, g_ssm):
    n = xb.shape[0]
    nb = SEQ_BLK
    row_spec = lambda w: pl.BlockSpec((nb, w), lambda i: (i, 0))
    st_spec = pl.BlockSpec((nb, SSM_W, D_STATE), lambda i: (i, 0, 0))
    full = lambda a: pl.BlockSpec(a.shape, lambda i: (0,) * a.ndim)
    return pl.pallas_call(
        functools.partial(_ssd_sample_kernel, n_seq=nb),
        grid=(n // nb,),
        in_specs=[row_spec(XBCDT_W), row_spec((CONV_W - 1) * CONV_DIM), row_spec(SSM_W), st_spec,
                  full(cw8), full(cb), full(dtb), full(alog), full(dskip_e), full(g_ssm)],
        out_specs=[row_spec(SSM_W), row_spec((CONV_W - 1) * CONV_DIM), st_spec],
        out_shape=[
            jax.ShapeDtypeStruct((n, SSM_W), BF16),
            jax.ShapeDtypeStruct((n, (CONV_W - 1) * CONV_DIM), F32),
            jax.ShapeDtypeStruct(h0.shape, F32),
        ],
        scratch_shapes=[pltpu.VMEM((nb, SSM_W), F32)],
        compiler_params=_params(("parallel",)),
        name="ssd_sample",
    )(xb, sconv, z, h0, cw8, cb, dtb, alog, dskip_e, g_ssm)


def _outproj_rows(att, ssm, x, woa_ref, wos_ref, gpost, gt1, gpre, sh2, sc2):
    mix = _dot(att, woa_ref[...]) + _dot(ssm, wos_ref[...])
    x1 = x + _rms(mix, gpost * gt1)
    u2 = _rms(x1, gpre * (1.0 + sc2)) + sh2
    return x1, u2.astype(BF16)


def _outproj_kernel(att_ref, ssm_ref, x_ref, woa_ref, wos_ref, gpost_ref, gt1_ref, gpre_ref, sh2_ref, sc2_ref,
                    atts_ref, ssms_ref, xs_ref, gt1s_ref, sh2s_ref, sc2s_ref, perm_ref,
                    x1_ref, u2_ref, x1s_ref, u2s_ref, *, rows):
    for r0 in range(0, rows, SUB_OUT):
        rs = slice(r0, r0 + SUB_OUT)
        x1_ref[rs, :], u2_ref[rs, :] = _outproj_rows(
            att_ref[rs, :], ssm_ref[rs, :], x_ref[rs, :], woa_ref, wos_ref, gpost_ref[...], gt1_ref[0:1, :],
            gpre_ref[...], sh2_ref[0:1, :], sc2_ref[0:1, :])

    @pl.when(pl.program_id(0) == pl.num_programs(0) - 1)
    def _():
        att_s = _dot_nt(atts_ref[...], perm_ref[...]).astype(BF16)
        x1s_ref[...], u2s_ref[...] = _outproj_rows(
            att_s, ssms_ref[...], xs_ref[...], woa_ref, wos_ref, gpost_ref[...], gt1s_ref[...],
            gpre_ref[...], sh2s_ref[...], sc2s_ref[...])


def _outproj(att, ssm, x, att_s, ssm_s, x_s, mod_p, mod_s, w_out, g_post, g_pre, perm):
    n_rows = x.shape[0]
    n_s = x_s.shape[0]
    rows = ROWS_OUT
    assert rows % SUB_OUT == 0 and n_rows % rows == 0
    half = lambda r: pl.BlockSpec((ATT_W, D_MODEL), lambda i: (r, 0), pipeline_mode=pl.Buffered(1))
    row_spec = lambda w: pl.BlockSpec((rows, w), lambda i: (i, 0))
    sample = lambda w, col=0: pl.BlockSpec((n_s, w), lambda i: (0, col))
    return pl.pallas_call(
        functools.partial(_outproj_kernel, rows=rows),
        grid=(n_rows // rows,),
        in_specs=[
            row_spec(ATT_W), row_spec(SSM_W), row_spec(D_MODEL),
            half(0), half(1),
            _const_spec((1, D_MODEL)),
            _mod_spec(2),
            _const_spec((1, D_MODEL)),
            _mod_spec(3),
            _mod_spec(4),
            sample(ATT_W), sample(SSM_W), sample(D_MODEL),
            sample(D_MODEL, 2), sample(D_MODEL, 3), sample(D_MODEL, 4), _const_spec(perm.shape),
        ],
        out_specs=[row_spec(D_MODEL), row_spec(D_MODEL), sample(D_MODEL), sample(D_MODEL)],
        out_shape=[
            jax.ShapeDtypeStruct((n_rows, D_MODEL), F32),
            jax.ShapeDtypeStruct((n_rows, D_MODEL), BF16),
            jax.ShapeDtypeStruct((n_s, D_MODEL), F32),
            jax.ShapeDtypeStruct((n_s, D_MODEL), BF16),
        ],
        compiler_params=_params(("arbitrary",)),
        name="outproj",
    )(att, ssm, x, w_out, w_out, g_post, mod_p, g_pre, mod_p, mod_p, att_s, ssm_s, x_s, mod_s, mod_s, mod_s, perm)


def _ffn_kernel(u_ref, x1_hbm, wg_ref, wu_ref, wd_ref, gpost_ref, gt2_ref, us_ref, x1s_ref, gt2s_ref,
                y_ref, ys_ref, x1_buf, x1_sem, *, rows):
    i = pl.program_id(0)
    j = pl.program_id(1)
    x1_copy = pltpu.make_async_copy(x1_hbm.at[pl.ds(i * rows, rows), :], x1_buf, x1_sem)
    last = pl.num_programs(1) - 1

    def ffn_rows(u, first, final, acc_prev, resid, gt2):
        hid = (_silu(_dot(u, wg_ref[...])) * _dot(u, wu_ref[...])).astype(BF16)
        acc = _dot(hid, wd_ref[...])
        if not first:
            acc = acc_prev() + acc
        if final:
            acc = resid() + gt2() * _rms(acc, gpost_ref[...])
        return acc

    def d_ff_slice(first, final):
        sub = min(SUB_FFN, rows)
        for r0 in range(0, rows, sub):
            rs = slice(r0, r0 + sub)
            y_ref[rs, :] = ffn_rows(u_ref[rs, :], first, final, lambda: y_ref[rs, :],
                                    lambda: x1_buf[rs, :], lambda: gt2_ref[0:1, :])

        @pl.when(i == pl.num_programs(0) - 1)
        def _():
            ys_ref[...] = ffn_rows(us_ref[...], first, final, lambda: ys_ref[...],
                                   lambda: x1s_ref[...], lambda: gt2s_ref[...])

    @pl.when(j == 0)
    def _():
        x1_copy.start()
        d_ff_slice(True, False)

    @pl.when(jnp.logical_and(j > 0, j < last))
    def _():
        d_ff_slice(False, False)

    @pl.when(j == last)
    def _():
        x1_copy.wait()
        d_ff_slice(False, True)


def _ffn(u2, x1, mod_p, u2_s, x1_s, mod_s, wg, wu, wd, g_post):
    n_rows = x1.shape[0]
    n_s = x1_s.shape[0]
    rows = ROWS_FFN
    fb = FF_BLK
    assert D_FF // fb >= 2 and n_rows % rows == 0
    whole = lambda r, c: pl.BlockSpec((r, D_MODEL), lambda i, j: (0, c))
    return pl.pallas_call(
        functools.partial(_ffn_kernel, rows=rows),
        grid=(n_rows // rows, D_FF // fb),
        in_specs=[
            pl.BlockSpec((rows, D_MODEL), lambda i, j: (i, 0)),
            pl.BlockSpec(memory_space=pl.ANY),
            pl.BlockSpec((D_MODEL, fb), lambda i, j: (0, j)),
            pl.BlockSpec((D_MODEL, fb), lambda i, j: (0, j)),
            pl.BlockSpec((fb, D_MODEL), lambda i, j: (j, 0)),
            whole(1, 0),
            _mod_spec(5),
            whole(n_s, 0), whole(n_s, 0), whole(n_s, 5),
        ],
        out_specs=[pl.BlockSpec((rows, D_MODEL), lambda i, j: (i, 0)), whole(n_s, 0)],
        out_shape=[jax.ShapeDtypeStruct((n_rows, D_MODEL), F32), jax.ShapeDtypeStruct((n_s, D_MODEL), F32)],
        scratch_shapes=[pltpu.VMEM((rows, D_MODEL), F32), pltpu.SemaphoreType.DMA],
        compiler_params=_params(("arbitrary", "arbitrary"), VMEM_LIMIT_FFN),
        name="ffn",
    )(u2, x1, wg, wu, wd, g_post, mod_p, u2_s, x1_s, mod_s)


def _alibi_slopes():
    return (2.0 ** (-8.0 * np.arange(1, N_HEADS + 1) / N_HEADS)).astype(np.float32)


def kernel(x_prompt, x_sample, cache_k, cache_v, state_conv, state_ssm, c_prompt, c_sample, w_ada, b_ada, g_pre_mix, g_post_mix, w_in, attn_sinks, g_attn_out, conv_w, conv_b, dt_bias, a_log, d_skip, g_ssm_out, w_out, g_pre_ffn, g_post_ffn, w_gate, w_up, w_down):
    assert w_ada.shape[0] == 1, "one layer"
    n_s = x_sample.shape[0]
    row = lambda v: v.reshape(1, -1)

    g_att = row(g_attn_out[0])
    swapped_cols = np.arange(ATT_W).reshape(N_KV, Q_PER_KV, HEAD_DIM).swapaxes(0, 1).reshape(-1)
    perm = np.zeros((ATT_W, ATT_W), np.float32)
    perm[swapped_cols, np.arange(ATT_W)] = 1.0
    perm = jnp.asarray(perm, BF16)
    g_att_s = row(g_attn_out[0].reshape(N_KV, Q_PER_KV, HEAD_DIM).swapaxes(0, 1))
    slopes_np = _alibi_slopes()
    slopes = jnp.asarray(slopes_np)
    slopes_s = jnp.asarray(slopes_np.reshape(N_KV, Q_PER_KV).T.reshape(N_HEADS, 1))
    sinks = attn_sinks[0]
    sinks_s = sinks.reshape(N_KV, Q_PER_KV).T.reshape(N_HEADS, 1)
    cw8 = jnp.pad(conv_w[0], ((0, 8 - CONV_W), (0, 0)))
    cb = row(conv_b[0])
    dtb = jnp.pad(row(dt_bias[0]), ((0, 0), (0, LANES - SSM_H)))
    alog = jnp.pad(row(a_log[0]), ((0, 0), (0, LANES - SSM_H)))
    dskip_e = row(jnp.repeat(d_skip[0], SSM_P))
    g_ssm = row(g_ssm_out[0])

    c_p8 = jnp.pad(c_prompt, ((0, 8 - c_prompt.shape[0]), (0, 0)))
    mod_s, mod_p, w_all = _prep(c_sample, c_p8, w_ada[0], row(b_ada[0]), jnp.transpose(w_in[0]))

    xp = x_prompt[0]
    xs_ = x_sample[:, 0, :]
    q, kv, z, xb, kv_last, x_tail, q_s, kv_new, z_s, xb_s = _inproj(
        xp, xs_, row(g_pre_mix[0]), mod_p, mod_s, w_all, cw8, cb, perm)

    att, wg, wu, wd, wo = _attn_prompt(q, kv, slopes, sinks, g_att, w_gate[0], w_up[0], w_down[0], w_out[0])
    ssm, h_p = _ssd_prompt(xb, z, dtb, alog, dskip_e, g_ssm)
    keys_minor = lambda c: jnp.transpose(c, (0, 2, 3, 1)).reshape(n_s, KV_W, WINDOW)
    keys_major = lambda c: jnp.transpose(c.reshape(n_s, N_KV, HEAD_DIM, WINDOW), (0, 3, 1, 2))[None]
    att_s, k_s, v_s = _attn_sample(
        q_s, kv_new[:, :KV_W], kv_new[:, KV_W:],
        keys_minor(cache_k[0]), keys_minor(cache_v[0]), slopes_s, sinks_s, g_att_s)
    ssm_s, conv_s, h_s = _ssd_sample(
        xb_s, state_conv[0].reshape(n_s, (CONV_W - 1) * CONV_DIM), z_s,
        state_ssm[0].reshape(n_s, SSM_W, D_STATE), cw8, cb, dtb, alog, dskip_e, g_ssm)

    x1, u2, x1_s, u2_s = _outproj(att, ssm, xp, att_s, ssm_s, xs_, mod_p, mod_s, wo,
                                  row(g_post_mix[0]), row(g_pre_ffn[0]), perm)
    y_p, y_s = _ffn(u2, x1, mod_p, u2_s, x1_s, mod_s, wg, wu, wd, row(g_post_ffn[0]))

    return (
        y_p[None],
        y_s[:, None, :],
        kv_last[:, :KV_W].reshape(1, 1, WINDOW, N_KV, HEAD_DIM),
        kv_last[:, KV_W:].reshape(1, 1, WINDOW, N_KV, HEAD_DIM),
        x_tail[8 - (CONV_W - 1):].reshape(1, 1, CONV_W - 1, CONV_DIM),
        h_p.reshape(1, 1, SSM_H, SSM_P, D_STATE),
        keys_major(k_s),
        keys_major(v_s),
        conv_s.reshape(1, n_s, CONV_W - 1, CONV_DIM),
        h_s.reshape(1, n_s, SSM_H, SSM_P, D_STATE),
    )
```

```python
import functools

import numpy as np
import jax
import jax.numpy as jnp
from jax import lax
from jax.experimental import pallas as pl
from jax.experimental.pallas import tpu as pltpu

F32 = jnp.float32
BF16 = jnp.bfloat16

D_MODEL = 2048
ATT_W = 1024
HEAD_DIM = 64
N_HEADS = 16
N_KV = 4
Q_PER_KV = 4
KV_W = N_KV * HEAD_DIM
WINDOW = 128
SSM_W = 1024
SSM_P = 64
SSM_H = 16
SSM_G = 2
D_STATE = 128
CONV_W = 4
CONV_DIM = SSM_W + 2 * SSM_G * D_STATE
XBCDT_W = CONV_DIM + 128
D_FF = 5632
EPS = 1e-6
CHUNK = 128
NEG_INF = float("-inf")
LOG2E = 1.4426950408889634

VMEM_LIMIT = 56 * 1024 * 1024
VMEM_LIMIT_FFN = 60 * 1024 * 1024
LANES = 128

ROWS_IN = 512
ROWS_ATT = 512
ROWS_SSD = 1024
ROWS_OUT = 512
SUB_OUT = 128
ROWS_FFN = 1024
SUB_FFN = 512
FF_BLK = 512
SEQ_BLK = 16


def _params(sem, vmem=VMEM_LIMIT):
    return pltpu.CompilerParams(dimension_semantics=sem, vmem_limit_bytes=vmem)


def _silu(v):
    h = 0.5 * v
    return h + h * jnp.tanh(h)


def _softplus(v):
    return jnp.maximum(v, 0.0) + jnp.log1p(jnp.exp(-jnp.abs(v)))


def _rms(v, g):
    return v * lax.rsqrt(jnp.mean(v * v, axis=-1, keepdims=True) + EPS) * g


def _split3(v):
    hi = v.astype(BF16)
    r1 = v - hi.astype(F32)
    mid = r1.astype(BF16)
    lo = (r1 - mid.astype(F32)).astype(BF16)
    return hi, mid, lo


def _dot(a, b):
    return jnp.dot(a, b, preferred_element_type=F32)


def _dot_nt(a, b):
    return lax.dot_general(a, b, (((1,), (1,)), ((), ())), preferred_element_type=F32)


def _dot_tn(a, b):
    return lax.dot_general(a, b, (((0,), (0,)), ((), ())), preferred_element_type=F32)


def _expand_heads(v, e_mat):
    hi, mid, lo = _split3(v)
    return _dot(hi, e_mat) + _dot(mid, e_mat) + _dot(lo, e_mat)


def _head_expand_matrix():
    k = lax.broadcasted_iota(jnp.int32, (LANES, SSM_W), 0)
    c = lax.broadcasted_iota(jnp.int32, (LANES, SSM_W), 1)
    return jnp.where((c // SSM_P) == k, 1.0, 0.0).astype(BF16)


def _mod_spec(col):
    return pl.BlockSpec((8, D_MODEL), lambda i, *_: (0, col))


WS_BLK = 256
W_ALL_COLS = 4608
COL_X, COL_Q, COL_Z, COL_KV = 0, 2048, 3072, 4096
N_WS_SRC = -(-(ATT_W + 2 * KV_W + SSM_W + CONV_DIM + SSM_H) // WS_BLK)


def _ws_dst_block(s):
    q_end = ATT_W // WS_BLK
    kv_end = q_end + 2 * KV_W // WS_BLK
    z_end = kv_end + SSM_W // WS_BLK
    return jnp.where(s < q_end, COL_Q // WS_BLK + s,
                     jnp.where(s < kv_end, COL_KV // WS_BLK + s - q_end,
                               jnp.where(s < z_end, COL_Z // WS_BLK + s - kv_end,
                                         COL_X // WS_BLK + s - z_end)))


ADA_BLK = 768


def _prep_kernel(cs_ref, cp_ref, wa_ref, b_ref, wt_ref, ms_ref, mp_ref, wall_ref):
    s = pl.program_id(0)
    w = wa_ref[...].astype(BF16)
    b = b_ref[...]
    ms_ref[...] = _dot(_silu(cs_ref[...]).astype(BF16), w) + b
    mp_ref[...] = _dot(_silu(cp_ref[...]).astype(BF16), w) + b
    n_valid = jnp.where(s < N_WS_SRC - 1, WS_BLK, jnp.where(s == N_WS_SRC - 1, SSM_H, 0))
    lane = lax.broadcasted_iota(jnp.int32, wall_ref.shape, 1)
    wall_ref[...] = jnp.where(lane < n_valid, wt_ref[...].T, 0.0).astype(BF16)


def _prep(c_s, c_p8, w_ada, b_ada, w_in_t):
    k_dim = w_in_t.shape[1]
    n = w_ada.shape[1]
    n_ada = n // ADA_BLK
    assert n % ADA_BLK == 0 and n_ada <= N_WS_SRC + 1
    ada_col = lambda s: (0, jnp.minimum(s, n_ada - 1))
    return pl.pallas_call(
        _prep_kernel,
        grid=(N_WS_SRC + 1,),
        in_specs=[
            pl.BlockSpec(c_s.shape, lambda s: (0, 0)),
            pl.BlockSpec(c_p8.shape, lambda s: (0, 0)),
            pl.BlockSpec((D_MODEL, ADA_BLK), ada_col),
            pl.BlockSpec((1, ADA_BLK), ada_col),
            pl.BlockSpec((WS_BLK, k_dim), lambda s: (jnp.minimum(s, N_WS_SRC - 1), 0)),
        ],
        out_specs=[
            pl.BlockSpec((c_s.shape[0], ADA_BLK), ada_col),
            pl.BlockSpec((c_p8.shape[0], ADA_BLK), ada_col),
            pl.BlockSpec((k_dim, WS_BLK), lambda s: (0, _ws_dst_block(s))),
        ],
        out_shape=[
            jax.ShapeDtypeStruct((c_s.shape[0], n), F32),
            jax.ShapeDtypeStruct((c_p8.shape[0], n), F32),
            jax.ShapeDtypeStruct((k_dim, W_ALL_COLS), BF16),
        ],
        compiler_params=_params(("arbitrary",)),
        name="prep",
    )(c_s, c_p8, w_ada, b_ada, w_in_t)


def _modulated_norm(x, g, shift, scale):
    return (_rms(x, g) * (1.0 + scale) + shift).astype(BF16)


def _inproj_kernel(x_ref, g_ref, sh_ref, sc_ref, wq_ref, wkv_ref, wz_ref, wx_ref, cw_ref, cb_ref,
                   xs_ref, shs_ref, scs_ref, perm_ref,
                   q_ref, kv_ref, z_ref, xb_ref, kvlast_ref, xtail_ref, qs_ref, kvs_ref, zs_ref, xbs_ref,
                   xp_scr, *, rows):
    i = pl.program_id(0)
    tile = 2 * LANES

    @pl.when(i == 0)
    def _():
        xp_scr[:, rows:rows + 8, :] = jnp.zeros((CONV_DIM // LANES, 8, LANES), F32)

    u = _modulated_norm(x_ref[...], g_ref[...], sh_ref[0:1, :], sc_ref[0:1, :])

    def conv_slab(s, xs):
        ls = slice(s * LANES, (s + 1) * LANES)
        xtail_ref[:, ls] = xs[rows - 8:, :]
        xp_scr[s, 0:8, :] = xp_scr[s, rows:rows + 8, :]
        xp_scr[s, 8:, :] = xs
        acc = cb_ref[:, ls] + xp_scr[s, pl.ds(8 - (CONV_W - 1), rows), :] * cw_ref[0:1, ls]
        for k in range(1, CONV_W):
            acc = acc + xp_scr[s, pl.ds(8 - (CONV_W - 1) + k, rows), :] * cw_ref[k:k + 1, ls]
        xb_ref[:, ls] = _silu(acc)

    def post_z(c0, c1):
        def post(r):
            z_ref[:, c0:c1] = _silu(r)
        return post

    def post_q(c0, c1):
        def post(r):
            q_ref[:, c0:c1] = (r * (HEAD_DIM ** -0.5)).astype(BF16)
        return post

    def post_kv(c0, c1):
        def post(r):
            kv_ref[:, c0:c1] = r.astype(BF16)
            kvlast_ref[:, c0:c1] = r[rows - WINDOW:, :]
        return post

    xb = _dot(u, wx_ref[...])
    xb_ref[:, CONV_DIM:] = xb[:, CONV_DIM:]
    slabs = [functools.partial(conv_slab, s, xb[:, s * LANES:(s + 1) * LANES]) for s in range(CONV_DIM // LANES)]
    jobs = []
    for w_ref, width, post_of in ((wz_ref, SSM_W, post_z), (wq_ref, ATT_W, post_q), (wkv_ref, 2 * KV_W, post_kv)):
        for c0 in range(0, width, tile):
            jobs.append((w_ref, c0, c0 + tile, post_of(c0, c0 + tile)))
    pending = None
    for n, (w_ref, c0, c1, post) in enumerate(jobs):
        r = _dot(u, w_ref[:, c0:c1])
        for slab in slabs[n * len(slabs) // len(jobs):(n + 1) * len(slabs) // len(jobs)]:
            slab()
        if pending is not None:
            pending[0](pending[1])
        pending = (post, r)
    pending[0](pending[1])

    @pl.when(i == pl.num_programs(0) - 1)
    def _():
        us = _modulated_norm(xs_ref[...], g_ref[...], shs_ref[...], scs_ref[...])
        qs = (_dot(us, wq_ref[...]) * (HEAD_DIM ** -0.5)).astype(BF16)
        qs_ref[...] = _dot(qs, perm_ref[...]).astype(BF16)
        kvs_ref[...] = _dot(us, wkv_ref[...])
        zs_ref[...] = _dot(us, wz_ref[...])
        xbs_ref[...] = _dot(us, wx_ref[...])


def _const_spec(shape):
    return pl.BlockSpec(shape, lambda i, *_: (0,) * len(shape), pipeline_mode=pl.Buffered(1))


def _inproj(x, x_s, g, mod_p, mod_s, w_all, cw8, cb, perm):
    n_rows = x.shape[0]
    n_s = x_s.shape[0]
    rows = ROWS_IN
    row_spec = lambda w: pl.BlockSpec((rows, w), lambda i: (i, 0))
    seg_spec = lambda width, col: pl.BlockSpec((D_MODEL, width), lambda i: (0, col // width),
                                               pipeline_mode=pl.Buffered(1))
    sample_in = lambda col: pl.BlockSpec((n_s, D_MODEL), lambda i: (0, col), pipeline_mode=pl.Buffered(1))
    sample_out = lambda w: pl.BlockSpec((n_s, w), lambda i: (0, 0))
    return pl.pallas_call(
        functools.partial(_inproj_kernel, rows=rows),
        grid=(n_rows // rows,),
        in_specs=[
            row_spec(D_MODEL),
            _const_spec((1, D_MODEL)),
            _mod_spec(0),
            _mod_spec(1),
            seg_spec(ATT_W, COL_Q), seg_spec(2 * KV_W, COL_KV), seg_spec(SSM_W, COL_Z), seg_spec(XBCDT_W, COL_X),
            _const_spec(cw8.shape), _const_spec(cb.shape),
            sample_in(0), sample_in(0), sample_in(1), _const_spec(perm.shape),
        ],
        out_specs=[
            row_spec(ATT_W), row_spec(2 * KV_W), row_spec(SSM_W), row_spec(XBCDT_W),
            pl.BlockSpec((WINDOW, 2 * KV_W), lambda i: (0, 0)),
            pl.BlockSpec((8, CONV_DIM), lambda i: (0, 0)),
            sample_out(ATT_W), sample_out(2 * KV_W), sample_out(SSM_W), sample_out(XBCDT_W),
        ],
        out_shape=[
            jax.ShapeDtypeStruct((n_rows, ATT_W), BF16),
            jax.ShapeDtypeStruct((n_rows, 2 * KV_W), BF16),
            jax.ShapeDtypeStruct((n_rows, SSM_W), F32),
            jax.ShapeDtypeStruct((n_rows, XBCDT_W), F32),
            jax.ShapeDtypeStruct((WINDOW, 2 * KV_W), F32),
            jax.ShapeDtypeStruct((8, CONV_DIM), F32),
            jax.ShapeDtypeStruct((n_s, ATT_W), BF16),
            jax.ShapeDtypeStruct((n_s, 2 * KV_W), F32),
            jax.ShapeDtypeStruct((n_s, SSM_W), F32),
            jax.ShapeDtypeStruct((n_s, XBCDT_W), F32),
        ],
        scratch_shapes=[pltpu.VMEM((CONV_DIM // LANES, rows + 8, LANES), F32)],
        compiler_params=_params(("arbitrary",), VMEM_LIMIT_FFN),
        name="inproj",
    )(x, g, mod_p, mod_p, w_all, w_all, w_all, w_all, cw8, cb, x_s, mod_s, mod_s, perm)


def _pair_blockdiag(lo_src, hi_src, keep_lo, keep_hi):
    zero = jnp.zeros_like(lo_src)
    return jnp.concatenate([jnp.where(keep_lo, lo_src, zero), jnp.where(keep_hi, hi_src, zero)], axis=0)


def _attn_prompt_kernel(slopes_ref, sinks_ref, q_ref, kvc_ref, kvp_ref, g_ref, wg_ref, wu_ref, wd_ref, wo_ref,
                        o_ref, wg_o, wu_o, wd_o, wo_o, bias_scr, kvdup_scr, *, n_sub):
    i = pl.program_id(0)
    blk = WINDOW

    wg_o[...] = wg_ref[...].astype(BF16)
    wu_o[...] = wu_ref[...].astype(BF16)
    wd_o[...] = wd_ref[...].astype(BF16)
    wo_o[...] = wo_ref[...].astype(BF16)

    @pl.when(i == 0)
    def _():
        a = lax.broadcasted_iota(jnp.int32, (blk, 2 * blk), 0)
        j = lax.broadcasted_iota(jnp.int32, (blk, 2 * blk), 1)
        dist = a + blk - j
        valid = (dist >= 0) & (dist < WINDOW)
        distf = dist.astype(F32)
        for h in range(N_HEADS):
            kvh, g = divmod(h, Q_PER_KV)
            pr, t = divmod(g, 2)
            b = jnp.where(valid, -(slopes_ref[h] * distf), NEG_INF)
            b = jnp.where(j == 0, sinks_ref[h], b)
            rs = slice(pr * blk, (pr + 1) * blk)
            cs = slice(t * 2 * blk, (t + 1) * 2 * blk)
            bias_scr[1, kvh, rs, cs] = b
            bias_scr[0, kvh, rs, cs] = jnp.where((j >= blk) | (j == 0), b, NEG_INF)

    for src_ref, r_lo, r_hi in ((kvp_ref, 0, blk), (kvc_ref, blk, kvdup_scr.shape[0])):
        lo_half = lax.broadcasted_iota(jnp.int32, (r_hi - r_lo, LANES), 1) < HEAD_DIM
        for part in range(2):
            for s in range(2):
                c_in = part * KV_W + s * LANES
                c_out = part * N_KV * LANES + 2 * s * LANES
                x = src_ref[:, c_in:c_in + LANES].astype(F32)
                xr = pltpu.roll(x, HEAD_DIM, 1)
                kvdup_scr[r_lo:r_hi, c_out:c_out + LANES] = jnp.where(lo_half, x, xr).astype(BF16)
                kvdup_scr[r_lo:r_hi, c_out + LANES:c_out + 2 * LANES] = jnp.where(lo_half, xr, x).astype(BF16)

    lane = lax.broadcasted_iota(jnp.int32, (2 * blk, LANES), 1)
    not_sink = lax.broadcasted_iota(jnp.int32, (2 * blk, LANES), 0) != 0
    keep_lo = (lane < HEAD_DIM) & not_sink
    keep_hi = (lane >= HEAD_DIM) & not_sink
    ones_cols = jnp.concatenate([jnp.where(lane < HEAD_DIM, 1.0, 0.0),
                                 jnp.where(lane >= HEAD_DIM, 1.0, 0.0)], axis=0).astype(BF16)
    n_chunk = 4
    rc = 2 * blk // n_chunk

    def body(b, carry):
        r0 = pl.multiple_of(b * blk, blk)
        variant = jnp.where(jnp.logical_and(i == 0, b == 0), 0, 1)
        vbds, scores = [], []
        for kvh in range(N_KV):
            c0 = kvh * Q_PER_KV * HEAD_DIM
            kd = kvdup_scr[pl.ds(r0, 2 * blk), kvh * LANES:(kvh + 1) * LANES]
            vd = kvdup_scr[pl.ds(r0, 2 * blk), (N_KV + kvh) * LANES:(N_KV + kvh + 1) * LANES]
            kbd = _pair_blockdiag(kd, kd, keep_lo, keep_hi)
            vbds.append(jnp.concatenate([_pair_blockdiag(vd, vd, keep_lo, keep_hi), ones_cols], axis=1))
            q2 = jnp.concatenate([q_ref[pl.ds(r0, blk), c0:c0 + LANES],
                                  q_ref[pl.ds(r0, blk), c0 + LANES:c0 + 2 * LANES]], axis=0)
            scores.append(_dot_nt(q2, kbd) + bias_scr[variant, kvh])
        probs = []
        for kvh in range(N_KV):
            chunks = []
            for c in range(n_chunk):
                st = scores[kvh][c * rc:(c + 1) * rc, :]
                sl = st[:, 0:2 * blk]
                sr = st[:, 2 * blk:]
                p_l = jnp.exp(sl - jnp.max(sl, axis=-1, keepdims=True))
                p_r = jnp.exp(sr - jnp.max(sr, axis=-1, keepdims=True))
                chunks.append(jnp.concatenate([p_l, p_r], axis=1).astype(BF16))
            probs.append(jnp.concatenate(chunks, axis=0))
        outs = []
        for kvh in range(N_KV):
            o2 = _dot(probs[kvh], vbds[kvh])
            o = o2[:, 0:LANES] * (1.0 / o2[:, LANES:])
            outs += [o[0:blk], o[blk:]]
        att = jnp.concatenate(outs, axis=1)
        o_ref[pl.ds(r0, blk), :] = _rms(att, g_ref[...]).astype(BF16)
        return carry

    lax.fori_loop(0, n_sub, body, 0, unroll=2)


def _attn_prompt(q, kv, slopes, sinks, g_att, w_gate, w_up, w_down, w_out):
    n_rows = q.shape[0]
    rows = ROWS_ATT
    n_sub = rows // WINDOW
    n_steps = n_rows // rows
    smem = pl.BlockSpec(memory_space=pltpu.SMEM)
    slab = lambda w: pl.BlockSpec((w.shape[0] // n_steps, w.shape[1]), lambda i: (i, 0))
    ffn_w = (w_gate, w_up, w_down, w_out)
    assert all(w.shape[0] % (16 * n_steps) == 0 for w in ffn_w)
    return pl.pallas_call(
        functools.partial(_attn_prompt_kernel, n_sub=n_sub),
        grid=(n_rows // rows,),
        in_specs=[
            smem, smem,
            pl.BlockSpec((rows, ATT_W), lambda i: (i, 0)),
            pl.BlockSpec((rows, 2 * KV_W), lambda i: (i, 0)),
            pl.BlockSpec((WINDOW, 2 * KV_W), lambda i: (jnp.maximum(i * n_sub - 1, 0), 0)),
            pl.BlockSpec((1, ATT_W), lambda i: (0, 0)),
            *[slab(w) for w in ffn_w],
        ],
        out_specs=[pl.BlockSpec((rows, ATT_W), lambda i: (i, 0)), *[slab(w) for w in ffn_w]],
        out_shape=[jax.ShapeDtypeStruct((n_rows, ATT_W), BF16),
                   *[jax.ShapeDtypeStruct(w.shape, BF16) for w in ffn_w]],
        scratch_shapes=[
            pltpu.VMEM((2, N_KV, 2 * WINDOW, 4 * WINDOW), F32),
            pltpu.VMEM((rows + WINDOW, 2 * N_KV * LANES), BF16),
        ],
        compiler_params=_params(("arbitrary",)),
        name="attn_prompt",
    )(slopes, sinks, q, kv, kv, g_att, *ffn_w)


def _attn_sample_kernel(q_ref, kn_ref, vn_ref, ck_ref, cv_ref, slope_ref, sink_ref, g_ref,
                        o_ref, ko_ref, vo_ref, att_scr, *, n_seq):
    r16 = lax.broadcasted_iota(jnp.int32, (N_HEADS, ATT_W), 0)
    c16 = lax.broadcasted_iota(jnp.int32, (N_HEADS, ATT_W), 1)
    own_head = (c16 // HEAD_DIM) == r16
    newest = lax.broadcasted_iota(jnp.int32, (KV_W, WINDOW), 1) == WINDOW - 1
    jj = lax.broadcasted_iota(jnp.int32, (N_HEADS, WINDOW), 1)
    bias = -(slope_ref[...] * (WINDOW - 1 - jj).astype(F32))
    pad = jnp.zeros((LANES - n_seq, KV_W), F32)
    kn_t = jnp.concatenate([kn_ref[...], pad], axis=0).T
    vn_t = jnp.concatenate([vn_ref[...], pad], axis=0).T

    def fold4(v):
        return (v[:, 0:KV_W] + v[:, KV_W:2 * KV_W]) + (v[:, 2 * KV_W:3 * KV_W] + v[:, 3 * KV_W:])

    scores, values = [], []
    for b in range(n_seq):
        kw = jnp.where(newest, jnp.broadcast_to(kn_t[:, b:b + 1], (KV_W, WINDOW)),
                       pltpu.roll(ck_ref[b], WINDOW - 1, 1))
        vw = jnp.where(newest, jnp.broadcast_to(vn_t[:, b:b + 1], (KV_W, WINDOW)),
                       pltpu.roll(cv_ref[b], WINDOW - 1, 1))
        ko_ref[b] = kw
        vo_ref[b] = vw
        qb = jnp.broadcast_to(q_ref[b:b + 1, :].astype(F32), (N_HEADS, ATT_W))
        qbd = fold4(jnp.where(own_head, qb, 0.0))
        scores.append(_dot(qbd.astype(BF16), kw.astype(BF16)))
        values.append(vw.astype(BF16))
    st = jnp.concatenate(scores, axis=0) + jnp.concatenate([bias] * n_seq, axis=0)
    sink = jnp.concatenate([sink_ref[...]] * n_seq, axis=0)
    m = jnp.maximum(jnp.max(st, axis=-1, keepdims=True), sink)
    p = jnp.exp(st - m)
    inv_l = 1.0 / (jnp.sum(p, axis=-1, keepdims=True) + jnp.exp(sink - m))
    pb = p.astype(BF16)
    for b in range(n_seq):
        hs = slice(b * N_HEADS, (b + 1) * N_HEADS)
        o = _dot_nt(pb[hs, :], values[b]) * inv_l[hs, :]
        o4 = jnp.concatenate([o, o, o, o], axis=1)
        att_scr[b:b + 1, :] = jnp.sum(jnp.where(own_head, o4, 0.0), axis=0, keepdims=True)
    o_ref[...] = _rms(att_scr[...], g_ref[...]).astype(BF16)


def _attn_sample(q, kn, vn, ck, cv, slope_col, sink_col, g_att):
    n = q.shape[0]
    nb = SEQ_BLK
    cache_spec = pl.BlockSpec((nb, KV_W, WINDOW), lambda i: (i, 0, 0))
    row_spec = lambda w: pl.BlockSpec((nb, w), lambda i: (i, 0))
    full = lambda a: pl.BlockSpec(a.shape, lambda i: (0,) * a.ndim)
    return pl.pallas_call(
        functools.partial(_attn_sample_kernel, n_seq=nb),
        grid=(n // nb,),
        in_specs=[row_spec(ATT_W), row_spec(KV_W), row_spec(KV_W), cache_spec, cache_spec,
                  full(slope_col), full(sink_col), full(g_att)],
        out_specs=[row_spec(ATT_W), cache_spec, cache_spec],
        out_shape=[
            jax.ShapeDtypeStruct((n, ATT_W), BF16),
            jax.ShapeDtypeStruct(ck.shape, F32),
            jax.ShapeDtypeStruct(cv.shape, F32),
        ],
        scratch_shapes=[pltpu.VMEM((nb, ATT_W), F32)],
        compiler_params=_params(("parallel",)),
        name="attn_sample",
    )(q, kn, vn, ck, cv, slope_col, sink_col, g_att)


def _ssd_prompt_kernel(xb_ref, z_ref, dtb_ref, alog_ref, dskip_ref, g_ref,
                       y_ref, hout_ref, ht_scr, e_scr, tri_scr, yd_scr, *, rows):
    i = pl.program_id(0)
    T = CHUNK

    @pl.when(i == 0)
    def _():
        ht_scr[...] = jnp.zeros_like(ht_scr)
        e_scr[...] = _head_expand_matrix()
        l = lax.broadcasted_iota(jnp.int32, (T, T), 0)
        s = lax.broadcasted_iota(jnp.int32, (T, T), 1)
        tri_scr[...] = jnp.where(s <= l, 1.0, 0.0).astype(BF16)

    a_log2 = -jnp.exp(alog_ref[...]) * LOG2E
    e_mat = e_scr[...]
    tri = tri_scr[...]
    causal = lax.broadcasted_iota(jnp.int32, (T, T), 0) >= lax.broadcasted_iota(jnp.int32, (T, T), 1)
    lo = lax.broadcasted_iota(jnp.int32, (T, LANES), 1) < SSM_P
    gw = SSM_W // SSM_G

    for c in range(rows // T):
        rs = slice(c * T, (c + 1) * T)
        dt = _softplus(xb_ref[rs, CONV_DIM:XBCDT_W] + dtb_ref[...])
        dta = dt * a_log2
        hi, mid, lw = _split3(dta)
        acum = _dot(tri, hi) + _dot(tri, mid) + _dot(tri, lw)
        acum_e = _expand_heads(acum, e_mat)
        dt_e = _expand_heads(dt, e_mat)
        acum_t = acum.T
        xs = xb_ref[rs, 0:SSM_W]
        bm = xb_ref[rs, SSM_W:SSM_W + SSM_G * D_STATE]
        cm = xb_ref[rs, SSM_W + SSM_G * D_STATE:CONV_DIM]
        xdt = xs * dt_e
        last = acum_e[T - 1:T, :]
        xdt_b = xdt.astype(BF16)
        xdec_b = (xdt * jnp.exp2(last - acum_e)).astype(BF16)
        bm_b = bm.astype(BF16)
        cm_b = cm.astype(BF16)
        cbs = [_dot_nt(cm_b[:, g * D_STATE:(g + 1) * D_STATE], bm_b[:, g * D_STATE:(g + 1) * D_STATE])
               for g in range(SSM_G)]
        for pr in range(SSM_H // 2):
            g = (2 * pr) // (SSM_H // SSM_G)
            ws = []
            for t in range(2):
                h = 2 * pr + t
                seg = jnp.broadcast_to(acum[:, h:h + 1], (T, T)) - acum_t[h:h + 1, :]
                ws.append((cbs[g] * jnp.exp2(jnp.where(causal, seg, NEG_INF))).astype(BF16))
            xsl = xdt_b[:, pr * LANES:(pr + 1) * LANES]
            yd_scr[:, pr * LANES:(pr + 1) * LANES] = _dot(jnp.concatenate(ws, axis=1),
                                                          _pair_blockdiag(xsl, xsl, lo, ~lo))
        yoff = []
        for g in range(SSM_G):
            gs = slice(g * gw, (g + 1) * gw)
            ht_g = ht_scr[:, gs]
            yoff.append(_dot(cm_b[:, g * D_STATE:(g + 1) * D_STATE], ht_g.astype(BF16)))
            cst = _dot_tn(bm_b[:, g * D_STATE:(g + 1) * D_STATE], xdec_b[:, gs])
            ht_scr[:, gs] = ht_g * jnp.exp2(last[:, gs]) + cst
        y = yd_scr[...] + jnp.concatenate(yoff, axis=1) * jnp.exp2(acum_e)
        y = y + dskip_ref[...] * xs
        y = y * z_ref[rs, :]
        y_ref[rs, :] = _rms(y, g_ref[...]).astype(BF16)

    @pl.when(i == pl.num_programs(0) - 1)
    def _():
        hout_ref[...] = ht_scr[...].T


def _ssd_prompt(xb, z, dtb, alog, dskip_e, g_ssm):
    n_rows = xb.shape[0]
    rows = ROWS_SSD
    full = lambda a: pl.BlockSpec(a.shape, lambda i: (0,) * a.ndim)
    return pl.pallas_call(
        functools.partial(_ssd_prompt_kernel, rows=rows),
        grid=(n_rows // rows,),
        in_specs=[
            pl.BlockSpec((rows, XBCDT_W), lambda i: (i, 0)),
            pl.BlockSpec((rows, SSM_W), lambda i: (i, 0)),
            full(dtb), full(alog), full(dskip_e), full(g_ssm),
        ],
        out_specs=[
            pl.BlockSpec((rows, SSM_W), lambda i: (i, 0)),
            pl.BlockSpec((SSM_W, D_STATE), lambda i: (0, 0)),
        ],
        out_shape=[
            jax.ShapeDtypeStruct((n_rows, SSM_W), BF16),
            jax.ShapeDtypeStruct((SSM_W, D_STATE), F32),
        ],
        scratch_shapes=[
            pltpu.VMEM((D_STATE, SSM_W), F32),
            pltpu.VMEM((LANES, SSM_W), BF16),
            pltpu.VMEM((CHUNK, CHUNK), BF16),
            pltpu.VMEM((CHUNK, SSM_W), F32),
        ],
        compiler_params=_params(("arbitrary",)),
        name="ssd_prompt",
    )(xb, z, dtb, alog, dskip_e, g_ssm)


def _ssd_sample_kernel(xb_ref, sconv_ref, z_ref, h0_ref, cw_ref, cb_ref, dtb_ref, alog_ref, dskip_ref, g_ref,
                       y_ref, conv_ref, hout_ref, yoff_scr, *, n_seq):
    gw = SSM_W // SSM_G
    gn = SSM_G * D_STATE
    x_new = xb_ref[:, 0:CONV_DIM]
    taps = [sconv_ref[:, k * CONV_DIM:(k + 1) * CONV_DIM] for k in range(CONV_W - 1)] + [x_new]
    acc = cb_ref[...]
    for k in range(CONV_W):
        acc = acc + taps[k] * cw_ref[k:k + 1, :]
    for k in range(1, CONV_W):
        conv_ref[:, (k - 1) * CONV_DIM:k * CONV_DIM] = taps[k]
    xc = _silu(acc)
    xs = xc[:, 0:SSM_W]
    bm = xc[:, SSM_W:SSM_W + gn]
    cm = xc[:, SSM_W + gn:CONV_DIM]

    e_mat = _head_expand_matrix()
    dt = _softplus(xb_ref[:, CONV_DIM:XBCDT_W] + dtb_ref[...])
    dt_e = _expand_heads(dt, e_mat)
    dec_e = jnp.exp(_expand_heads(dt * (-jnp.exp(alog_ref[...])), e_mat))
    xdt = xs * dt_e

    lane_w = lax.broadcasted_iota(jnp.int32, (n_seq, SSM_W), 1)
    first_grp = lane_w < gw
    cbv = [jnp.sum(cm[:, g * D_STATE:(g + 1) * D_STATE] * bm[:, g * D_STATE:(g + 1) * D_STATE],
                   axis=-1, keepdims=True) for g in range(SSM_G)]
    cb_e = jnp.where(first_grp, cbv[0], cbv[1])

    pad = jnp.zeros((LANES - n_seq, SSM_W), F32)
    xdt_t = jnp.concatenate([xdt, pad], axis=0).T
    dec_t = jnp.concatenate([dec_e, pad], axis=0).T
    sub8 = lax.broadcasted_iota(jnp.int32, (8, D_STATE), 0)
    lane_r = lax.broadcasted_iota(jnp.int32, (8, SSM_W), 1)

    for b in range(n_seq):
        h0 = h0_ref[b]
        c_row = jnp.broadcast_to(cm[b:b + 1, :], (8, gn))
        c8 = jnp.where(sub8 == 0, c_row[:, 0:D_STATE], jnp.where(sub8 == 1, c_row[:, D_STATE:], 0.0))
        r = _dot_nt(c8.astype(BF16), h0.astype(BF16))
        yoff_scr[b:b + 1, :] = jnp.where(lane_r[0:1] < gw, r[0:1, :], r[1:2, :])
        b_row = bm[b:b + 1, :]
        for g in range(SSM_G):
            rs = slice(g * gw, (g + 1) * gw)
            dcol = jnp.broadcast_to(dec_t[rs, b:b + 1], (gw, D_STATE))
            xcol = jnp.broadcast_to(xdt_t[rs, b:b + 1], (gw, D_STATE))
            hout_ref[b, rs, :] = h0[rs, :] * dcol + xcol * b_row[:, g * D_STATE:(g + 1) * D_STATE]

    y = cb_e * xdt + yoff_scr[...] * dec_e
    y = y + dskip_ref[...] * xs
    y = y * _silu(z_ref[...])
    y_ref[...] = _rms(y, g_ref[...]).astype(BF16)


def _ssd_sample(xb, sconv, z, h0, cw8, cb, dtb, alog, dskip_e, g_ssm):
    n = xb.shape[0]
    nb = SEQ_BLK
    row_spec = lambda w: pl.BlockSpec((nb, w), lambda i: (i, 0))
    st_spec = pl.BlockSpec((nb, SSM_W, D_STATE), lambda i: (i, 0, 0))
    full = lambda a: pl.BlockSpec(a.shape, lambda i: (0,) * a.ndim)
    return pl.pallas_call(
        functools.partial(_ssd_sample_kernel, n_seq=nb),
        grid=(n // nb,),
        in_specs=[row_spec(XBCDT_W), row_spec((CONV_W - 1) * CONV_DIM), row_spec(SSM_W), st_spec,
                  full(cw8), full(cb), full(dtb), full(alog), full(dskip_e), full(g_ssm)],
        out_specs=[row_spec(SSM_W), row_spec((CONV_W - 1) * CONV_DIM), st_spec],
        out_shape=[
            jax.ShapeDtypeStruct((n, SSM_W), BF16),
            jax.ShapeDtypeStruct((n, (CONV_W - 1) * CONV_DIM), F32),
            jax.ShapeDtypeStruct(h0.shape, F32),
        ],
        scratch_shapes=[pltpu.VMEM((nb, SSM_W), F32)],
        compiler_params=_params(("parallel",)),
        name="ssd_sample",
    )(xb, sconv, z, h0, cw8, cb, dtb, alog, dskip_e, g_ssm)


def _outproj_rows(att, ssm, x, woa_ref, wos_ref, gpost, gt1, gpre, sh2, sc2):
    mix = _dot(att, woa_ref[...]) + _dot(ssm, wos_ref[...])
    x1 = x + _rms(mix, gpost * gt1)
    u2 = _rms(x1, gpre * (1.0 + sc2)) + sh2
    return x1, u2.astype(BF16)


def _outproj_kernel(att_ref, ssm_ref, x_ref, woa_ref, wos_ref, gpost_ref, gt1_ref, gpre_ref, sh2_ref, sc2_ref,
                    atts_ref, ssms_ref, xs_ref, gt1s_ref, sh2s_ref, sc2s_ref, perm_ref,
                    x1_ref, u2_ref, x1s_ref, u2s_ref, *, rows):
    for r0 in range(0, rows, SUB_OUT):
        rs = slice(r0, r0 + SUB_OUT)
        x1_ref[rs, :], u2_ref[rs, :] = _outproj_rows(
            att_ref[rs, :], ssm_ref[rs, :], x_ref[rs, :], woa_ref, wos_ref, gpost_ref[...], gt1_ref[0:1, :],
            gpre_ref[...], sh2_ref[0:1, :], sc2_ref[0:1, :])

    @pl.when(pl.program_id(0) == pl.num_programs(0) - 1)
    def _():
        att_s = _dot_nt(atts_ref[...], perm_ref[...]).astype(BF16)
        x1s_ref[...], u2s_ref[...] = _outproj_rows(
            att_s, ssms_ref[...], xs_ref[...], woa_ref, wos_ref, gpost_ref[...], gt1s_ref[...],
            gpre_ref[...], sh2s_ref[...], sc2s_ref[...])


def _outproj(att, ssm, x, att_s, ssm_s, x_s, mod_p, mod_s, w_out, g_post, g_pre, perm):
    n_rows = x.shape[0]
    n_s = x_s.shape[0]
    rows = ROWS_OUT
    assert rows % SUB_OUT == 0 and n_rows % rows == 0
    half = lambda r: pl.BlockSpec((ATT_W, D_MODEL), lambda i: (r, 0), pipeline_mode=pl.Buffered(1))
    row_spec = lambda w: pl.BlockSpec((rows, w), lambda i: (i, 0))
    sample = lambda w, col=0: pl.BlockSpec((n_s, w), lambda i: (0, col))
    return pl.pallas_call(
        functools.partial(_outproj_kernel, rows=rows),
        grid=(n_rows // rows,),
        in_specs=[
            row_spec(ATT_W), row_spec(SSM_W), row_spec(D_MODEL),
            half(0), half(1),
            _const_spec((1, D_MODEL)),
            _mod_spec(2),
            _const_spec((1, D_MODEL)),
            _mod_spec(3),
            _mod_spec(4),
            sample(ATT_W), sample(SSM_W), sample(D_MODEL),
            sample(D_MODEL, 2), sample(D_MODEL, 3), sample(D_MODEL, 4), _const_spec(perm.shape),
        ],
        out_specs=[row_spec(D_MODEL), row_spec(D_MODEL), sample(D_MODEL), sample(D_MODEL)],
        out_shape=[
            jax.ShapeDtypeStruct((n_rows, D_MODEL), F32),
            jax.ShapeDtypeStruct((n_rows, D_MODEL), BF16),
            jax.ShapeDtypeStruct((n_s, D_MODEL), F32),
            jax.ShapeDtypeStruct((n_s, D_MODEL), BF16),
        ],
        compiler_params=_params(("arbitrary",)),
        name="outproj",
    )(att, ssm, x, w_out, w_out, g_post, mod_p, g_pre, mod_p, mod_p, att_s, ssm_s, x_s, mod_s, mod_s, mod_s, perm)


def _ffn_kernel(u_ref, x1_hbm, wg_ref, wu_ref, wd_ref, gpost_ref, gt2_ref, us_ref, x1s_ref, gt2s_ref,
                y_ref, ys_ref, x1_buf, x1_sem, *, rows):
    i = pl.program_id(0)
    j = pl.program_id(1)
    x1_copy = pltpu.make_async_copy(x1_hbm.at[pl.ds(i * rows, rows), :], x1_buf, x1_sem)
    last = pl.num_programs(1) - 1

    def ffn_rows(u, first, final, acc_prev, resid, gt2):
        hid = (_silu(_dot(u, wg_ref[...])) * _dot(u, wu_ref[...])).astype(BF16)
        acc = _dot(hid, wd_ref[...])
        if not first:
            acc = acc_prev() + acc
        if final:
            acc = resid() + gt2() * _rms(acc, gpost_ref[...])
        return acc

    def d_ff_slice(first, final):
        sub = min(SUB_FFN, rows)
        for r0 in range(0, rows, sub):
            rs = slice(r0, r0 + sub)
            y_ref[rs, :] = ffn_rows(u_ref[rs, :], first, final, lambda: y_ref[rs, :],
                                    lambda: x1_buf[rs, :], lambda: gt2_ref[0:1, :])

        @pl.when(i == pl.num_programs(0) - 1)
        def _():
            ys_ref[...] = ffn_rows(us_ref[...], first, final, lambda: ys_ref[...],
                                   lambda: x1s_ref[...], lambda: gt2s_ref[...])

    @pl.when(j == 0)
    def _():
        x1_copy.start()
        d_ff_slice(True, False)

    @pl.when(jnp.logical_and(j > 0, j < last))
    def _():
        d_ff_slice(False, False)

    @pl.when(j == last)
    def _():
        x1_copy.wait()
        d_ff_slice(False, True)


def _ffn(u2, x1, mod_p, u2_s, x1_s, mod_s, wg, wu, wd, g_post):
    n_rows = x1.shape[0]
    n_s = x1_s.shape[0]
    rows = ROWS_FFN
    fb = FF_BLK
    assert D_FF // fb >= 2 and n_rows % rows == 0
    whole = lambda r, c: pl.BlockSpec((r, D_MODEL), lambda i, j: (0, c))
    return pl.pallas_call(
        functools.partial(_ffn_kernel, rows=rows),
        grid=(n_rows // rows, D_FF // fb),
        in_specs=[
            pl.BlockSpec((rows, D_MODEL), lambda i, j: (i, 0)),
            pl.BlockSpec(memory_space=pl.ANY),
            pl.BlockSpec((D_MODEL, fb), lambda i, j: (0, j)),
            pl.BlockSpec((D_MODEL, fb), lambda i, j: (0, j)),
            pl.BlockSpec((fb, D_MODEL), lambda i, j: (j, 0)),
            whole(1, 0),
            _mod_spec(5),
            whole(n_s, 0), whole(n_s, 0), whole(n_s, 5),
        ],
        out_specs=[pl.BlockSpec((rows, D_MODEL), lambda i, j: (i, 0)), whole(n_s, 0)],
        out_shape=[jax.ShapeDtypeStruct((n_rows, D_MODEL), F32), jax.ShapeDtypeStruct((n_s, D_MODEL), F32)],
        scratch_shapes=[pltpu.VMEM((rows, D_MODEL), F32), pltpu.SemaphoreType.DMA],
        compiler_params=_params(("arbitrary", "arbitrary"), VMEM_LIMIT_FFN),
        name="ffn",
    )(u2, x1, wg, wu, wd, g_post, mod_p, u2_s, x1_s, mod_s)


def _alibi_slopes():
    return (2.0 ** (-8.0 * np.arange(1, N_HEADS + 1) / N_HEADS)).astype(np.float32)


def kernel(x_prompt, x_sample, cache_k, cache_v, state_conv, state_ssm, c_prompt, c_sample, w_ada, b_ada, g_pre_mix, g_post_mix, w_in, attn_sinks, g_attn_out, conv_w, conv_b, dt_bias, a_log, d_skip, g_ssm_out, w_out, g_pre_ffn, g_post_ffn, w_gate, w_up, w_down):
    assert w_ada.shape[0] == 1, "one layer"
    n_s = x_sample.shape[0]
    row = lambda v: v.reshape(1, -1)

    g_att = row(g_attn_out[0])
    swapped_cols = np.arange(ATT_W).reshape(N_KV, Q_PER_KV, HEAD_DIM).swapaxes(0, 1).reshape(-1)
    perm = np.zeros((ATT_W, ATT_W), np.float32)
    perm[swapped_cols, np.arange(ATT_W)] = 1.0
    perm = jnp.asarray(perm, BF16)
    g_att_s = row(g_attn_out[0].reshape(N_KV, Q_PER_KV, HEAD_DIM).swapaxes(0, 1))
    slopes_np = _alibi_slopes()
    slopes = jnp.asarray(slopes_np)
    slopes_s = jnp.asarray(slopes_np.reshape(N_KV, Q_PER_KV).T.reshape(N_HEADS, 1))
    sinks = attn_sinks[0]
    sinks_s = sinks.reshape(N_KV, Q_PER_KV).T.reshape(N_HEADS, 1)
    cw8 = jnp.pad(conv_w[0], ((0, 8 - CONV_W), (0, 0)))
    cb = row(conv_b[0])
    dtb = jnp.pad(row(dt_bias[0]), ((0, 0), (0, LANES - SSM_H)))
    alog = jnp.pad(row(a_log[0]), ((0, 0), (0, LANES - SSM_H)))
    dskip_e = row(jnp.repeat(d_skip[0], SSM_P))
    g_ssm = row(g_ssm_out[0])

    c_p8 = jnp.pad(c_prompt, ((0, 8 - c_prompt.shape[0]), (0, 0)))
    mod_s, mod_p, w_all = _prep(c_sample, c_p8, w_ada[0], row(b_ada[0]), jnp.transpose(w_in[0]))

    xp = x_prompt[0]
    xs_ = x_sample[:, 0, :]
    q, kv, z, xb, kv_last, x_tail, q_s, kv_new, z_s, xb_s = _inproj(
        xp, xs_, row(g_pre_mix[0]), mod_p, mod_s, w_all, cw8, cb, perm)

    att, wg, wu, wd, wo = _attn_prompt(q, kv, slopes, sinks, g_att, w_gate[0], w_up[0], w_down[0], w_out[0])
    ssm, h_p = _ssd_prompt(xb, z, dtb, alog, dskip_e, g_ssm)
    keys_minor = lambda c: jnp.transpose(c, (0, 2, 3, 1)).reshape(n_s, KV_W, WINDOW)
    keys_major = lambda c: jnp.transpose(c.reshape(n_s, N_KV, HEAD_DIM, WINDOW), (0, 3, 1, 2))[None]
    att_s, k_s, v_s = _attn_sample(
        q_s, kv_new[:, :KV_W], kv_new[:, KV_W:],
        keys_minor(cache_k[0]), keys_minor(cache_v[0]), slopes_s, sinks_s, g_att_s)
    ssm_s, conv_s, h_s = _ssd_sample(
        xb_s, state_conv[0].reshape(n_s, (CONV_W - 1) * CONV_DIM), z_s,
        state_ssm[0].reshape(n_s, SSM_W, D_STATE), cw8, cb, dtb, alog, dskip_e, g_ssm)

    x1, u2, x1_s, u2_s = _outproj(att, ssm, xp, att_s, ssm_s, xs_, mod_p, mod_s, wo,
                                  row(g_post_mix[0]), row(g_pre_ffn[0]), perm)
    y_p, y_s = _ffn(u2, x1, mod_p, u2_s, x1_s, mod_s, wg, wu, wd, row(g_post_ffn[0]))

    return (
        y_p[None],
        y_s[:, None, :],
        kv_last[:, :KV_W].reshape(1, 1, WINDOW, N_KV, HEAD_DIM),
        kv_last[:, KV_W:].reshape(1, 1, WINDOW, N_KV, HEAD_DIM),
        x_tail[8 - (CONV_W - 1):].reshape(1, 1, CONV_W - 1, CONV_DIM),
        h_p.reshape(1, 1, SSM_H, SSM_P, D_STATE),
        keys_major(k_s),
        keys_major(v_s),
        conv_s.reshape(1, n_s, CONV_W - 1, CONV_DIM),
        h_s.reshape(1, n_s, SSM_H, SSM_P, D_STATE),
    )
```

```python
import functools

import numpy as np
import jax
import jax.numpy as jnp
from jax import lax
from jax.experimental import pallas as pl
from jax.experimental.pallas import tpu as pltpu

F32 = jnp.float32
BF16 = jnp.bfloat16

D_MODEL = 2048
ATT_W = 1024
HEAD_DIM = 64
N_HEADS = 16
N_KV = 4
Q_PER_KV = 4
KV_W = N_KV * HEAD_DIM
WINDOW = 128
SSM_W = 1024
SSM_P = 64
SSM_H = 16
SSM_G = 2
D_STATE = 128
CONV_W = 4
CONV_DIM = SSM_W + 2 * SSM_G * D_STATE
XBCDT_W = CONV_DIM + 128
D_FF = 5632
EPS = 1e-6
CHUNK = 128
NEG_INF = float("-inf")
LOG2E = 1.4426950408889634

VMEM_LIMIT = 56 * 1024 * 1024
VMEM_LIMIT_FFN = 60 * 1024 * 1024
LANES = 128

ROWS_IN = 512
ROWS_ATT = 512
ROWS_SSD = 1024
ROWS_OUT = 512
SUB_OUT = 128
ROWS_FFN = 1024
SUB_FFN = 512
FF_BLK = 512
SEQ_ATT = 16
SEQ_SSD = 8


def _params(sem, vmem=VMEM_LIMIT):
    return pltpu.CompilerParams(dimension_semantics=sem, vmem_limit_bytes=vmem)


def _silu(v):
    h = 0.5 * v
    return h + h * jnp.tanh(h)


def _softplus(v):
    return jnp.maximum(v, 0.0) + jnp.log1p(jnp.exp(-jnp.abs(v)))


def _rms(v, g):
    return v * lax.rsqrt(jnp.mean(v * v, axis=-1, keepdims=True) + EPS) * g


def _split3(v):
    hi = v.astype(BF16)
    r1 = v - hi.astype(F32)
    mid = r1.astype(BF16)
    lo = (r1 - mid.astype(F32)).astype(BF16)
    return hi, mid, lo


def _dot(a, b):
    return jnp.dot(a, b, preferred_element_type=F32)


def _dot_nt(a, b):
    return lax.dot_general(a, b, (((1,), (1,)), ((), ())), preferred_element_type=F32)


def _dot_tn(a, b):
    return lax.dot_general(a, b, (((0,), (0,)), ((), ())), preferred_element_type=F32)


def _expand_heads(v, e_mat):
    hi, mid, lo = _split3(v)
    return _dot(hi, e_mat) + _dot(mid, e_mat) + _dot(lo, e_mat)


def _head_expand_matrix():
    k = lax.broadcasted_iota(jnp.int32, (LANES, SSM_W), 0)
    c = lax.broadcasted_iota(jnp.int32, (LANES, SSM_W), 1)
    return jnp.where((c // SSM_P) == k, 1.0, 0.0).astype(BF16)


def _mod_spec(col):
    return pl.BlockSpec((8, D_MODEL), lambda i, *_: (0, col))


WS_BLK = 256
W_ALL_COLS = 4608
COL_X, COL_Q, COL_Z, COL_KV = 0, 2048, 3072, 4096
N_WS_SRC = -(-(ATT_W + 2 * KV_W + SSM_W + CONV_DIM + SSM_H) // WS_BLK)


def _ws_dst_block(s):
    q_end = ATT_W // WS_BLK
    kv_end = q_end + 2 * KV_W // WS_BLK
    z_end = kv_end + SSM_W // WS_BLK
    return jnp.where(s < q_end, COL_Q // WS_BLK + s,
                     jnp.where(s < kv_end, COL_KV // WS_BLK + s - q_end,
                               jnp.where(s < z_end, COL_Z // WS_BLK + s - kv_end,
                                         COL_X // WS_BLK + s - z_end)))


ADA_BLK = 768


def _prep_kernel(cs_ref, cp_ref, wa_ref, b_ref, wt_ref, ms_ref, mp_ref, wall_ref):
    s = pl.program_id(0)
    w = wa_ref[...].astype(BF16)
    b = b_ref[...]
    ms_ref[...] = _dot(_silu(cs_ref[...]).astype(BF16), w) + b
    mp_ref[...] = _dot(_silu(cp_ref[...]).astype(BF16), w) + b
    n_valid = jnp.where(s < N_WS_SRC - 1, WS_BLK, jnp.where(s == N_WS_SRC - 1, SSM_H, 0))
    lane = lax.broadcasted_iota(jnp.int32, wall_ref.shape, 1)
    wall_ref[...] = jnp.where(lane < n_valid, wt_ref[...].T, 0.0).astype(BF16)


def _prep(c_s, c_p8, w_ada, b_ada, w_in_t):
    k_dim = w_in_t.shape[1]
    n = w_ada.shape[1]
    n_ada = n // ADA_BLK
    assert n % ADA_BLK == 0 and n_ada <= N_WS_SRC + 1
    ada_col = lambda s: (0, jnp.minimum(s, n_ada - 1))
    return pl.pallas_call(
        _prep_kernel,
        grid=(N_WS_SRC + 1,),
        in_specs=[
            pl.BlockSpec(c_s.shape, lambda s: (0, 0)),
            pl.BlockSpec(c_p8.shape, lambda s: (0, 0)),
            pl.BlockSpec((D_MODEL, ADA_BLK), ada_col),
            pl.BlockSpec((1, ADA_BLK), ada_col),
            pl.BlockSpec((WS_BLK, k_dim), lambda s: (jnp.minimum(s, N_WS_SRC - 1), 0)),
        ],
        out_specs=[
            pl.BlockSpec((c_s.shape[0], ADA_BLK), ada_col),
            pl.BlockSpec((c_p8.shape[0], ADA_BLK), ada_col),
            pl.BlockSpec((k_dim, WS_BLK), lambda s: (0, _ws_dst_block(s))),
        ],
        out_shape=[
            jax.ShapeDtypeStruct((c_s.shape[0], n), F32),
            jax.ShapeDtypeStruct((c_p8.shape[0], n), F32),
            jax.ShapeDtypeStruct((k_dim, W_ALL_COLS), BF16),
        ],
        compiler_params=_params(("arbitrary",)),
        name="prep",
    )(c_s, c_p8, w_ada, b_ada, w_in_t)


def _modulated_norm(x, g, shift, scale):
    return (_rms(x, g) * (1.0 + scale) + shift).astype(BF16)


def _inproj_kernel(x_ref, g_ref, sh_ref, sc_ref, wq_ref, wkv_ref, wz_ref, wx_ref, cw_ref, cb_ref,
                   xs_ref, shs_ref, scs_ref, perm_ref,
                   q_ref, kv_ref, z_ref, xb_ref, kvlast_ref, xtail_ref, qs_ref, kvs_ref, zs_ref, xbs_ref,
                   xp_scr, *, rows):
    i = pl.program_id(0)
    tile = 2 * LANES

    @pl.when(i == 0)
    def _():
        xp_scr[:, rows:rows + 8, :] = jnp.zeros((CONV_DIM // LANES, 8, LANES), F32)

    u = _modulated_norm(x_ref[...], g_ref[...], sh_ref[0:1, :], sc_ref[0:1, :])

    def conv_slab(s, xs):
        ls = slice(s * LANES, (s + 1) * LANES)
        xtail_ref[:, ls] = xs[rows - 8:, :]
        xp_scr[s, 0:8, :] = xp_scr[s, rows:rows + 8, :]
        xp_scr[s, 8:, :] = xs
        acc = cb_ref[:, ls] + xp_scr[s, pl.ds(8 - (CONV_W - 1), rows), :] * cw_ref[0:1, ls]
        for k in range(1, CONV_W):
            acc = acc + xp_scr[s, pl.ds(8 - (CONV_W - 1) + k, rows), :] * cw_ref[k:k + 1, ls]
        xb_ref[:, ls] = _silu(acc)

    def post_z(c0, c1):
        def post(r):
            z_ref[:, c0:c1] = _silu(r)
        return post

    def post_q(c0, c1):
        def post(r):
            q_ref[:, c0:c1] = (r * (HEAD_DIM ** -0.5)).astype(BF16)
        return post

    def post_kv(c0, c1):
        def post(r):
            kv_ref[:, c0:c1] = r.astype(BF16)
            kvlast_ref[:, c0:c1] = r[rows - WINDOW:, :]
        return post

    xb = _dot(u, wx_ref[...])
    xb_ref[:, CONV_DIM:] = xb[:, CONV_DIM:]
    slabs = [functools.partial(conv_slab, s, xb[:, s * LANES:(s + 1) * LANES]) for s in range(CONV_DIM // LANES)]
    jobs = []
    for w_ref, width, post_of in ((wz_ref, SSM_W, post_z), (wq_ref, ATT_W, post_q), (wkv_ref, 2 * KV_W, post_kv)):
        for c0 in range(0, width, tile):
            jobs.append((w_ref, c0, c0 + tile, post_of(c0, c0 + tile)))
    pending = None
    for n, (w_ref, c0, c1, post) in enumerate(jobs):
        r = _dot(u, w_ref[:, c0:c1])
        for slab in slabs[n * len(slabs) // len(jobs):(n + 1) * len(slabs) // len(jobs)]:
            slab()
        if pending is not None:
            pending[0](pending[1])
        pending = (post, r)
    pending[0](pending[1])

    @pl.when(i == pl.num_programs(0) - 1)
    def _():
        us = _modulated_norm(xs_ref[...], g_ref[...], shs_ref[...], scs_ref[...])
        qs = (_dot(us, wq_ref[...]) * (HEAD_DIM ** -0.5)).astype(BF16)
        qs_ref[...] = _dot(qs, perm_ref[...]).astype(BF16)
        kvs_ref[...] = _dot(us, wkv_ref[...])
        zs_ref[...] = _dot(us, wz_ref[...])
        xbs_ref[...] = _dot(us, wx_ref[...])


def _const_spec(shape):
    return pl.BlockSpec(shape, lambda i, *_: (0,) * len(shape), pipeline_mode=pl.Buffered(1))


def _inproj(x, x_s, g, mod_p, mod_s, w_all, cw8, cb, perm):
    n_rows = x.shape[0]
    n_s = x_s.shape[0]
    rows = ROWS_IN
    row_spec = lambda w: pl.BlockSpec((rows, w), lambda i: (i, 0))
    seg_spec = lambda width, col: pl.BlockSpec((D_MODEL, width), lambda i: (0, col // width),
                                               pipeline_mode=pl.Buffered(1))
    sample_in = lambda col: pl.BlockSpec((n_s, D_MODEL), lambda i: (0, col), pipeline_mode=pl.Buffered(1))
    sample_out = lambda w: pl.BlockSpec((n_s, w), lambda i: (0, 0))
    return pl.pallas_call(
        functools.partial(_inproj_kernel, rows=rows),
        grid=(n_rows // rows,),
        in_specs=[
            row_spec(D_MODEL),
            _const_spec((1, D_MODEL)),
            _mod_spec(0),
            _mod_spec(1),
            seg_spec(ATT_W, COL_Q), seg_spec(2 * KV_W, COL_KV), seg_spec(SSM_W, COL_Z), seg_spec(XBCDT_W, COL_X),
            _const_spec(cw8.shape), _const_spec(cb.shape),
            sample_in(0), sample_in(0), sample_in(1), _const_spec(perm.shape),
        ],
        out_specs=[
            row_spec(ATT_W), row_spec(2 * KV_W), row_spec(SSM_W), row_spec(XBCDT_W),
            pl.BlockSpec((WINDOW, 2 * KV_W), lambda i: (0, 0)),
            pl.BlockSpec((8, CONV_DIM), lambda i: (0, 0)),
            sample_out(ATT_W), sample_out(2 * KV_W), sample_out(SSM_W), sample_out(XBCDT_W),
        ],
        out_shape=[
            jax.ShapeDtypeStruct((n_rows, ATT_W), BF16),
            jax.ShapeDtypeStruct((n_rows, 2 * KV_W), BF16),
            jax.ShapeDtypeStruct((n_rows, SSM_W), F32),
            jax.ShapeDtypeStruct((n_rows, XBCDT_W), F32),
            jax.ShapeDtypeStruct((WINDOW, 2 * KV_W), F32),
            jax.ShapeDtypeStruct((8, CONV_DIM), F32),
            jax.ShapeDtypeStruct((n_s, ATT_W), BF16),
            jax.ShapeDtypeStruct((n_s, 2 * KV_W), F32),
            jax.ShapeDtypeStruct((n_s, SSM_W), F32),
            jax.ShapeDtypeStruct((n_s, XBCDT_W), F32),
        ],
        scratch_shapes=[pltpu.VMEM((CONV_DIM // LANES, rows + 8, LANES), F32)],
        compiler_params=_params(("arbitrary",), VMEM_LIMIT_FFN),
        name="inproj",
    )(x, g, mod_p, mod_p, w_all, w_all, w_all, w_all, cw8, cb, x_s, mod_s, mod_s, perm)


def _pair_blockdiag(lo_src, hi_src, keep_lo, keep_hi):
    zero = jnp.zeros_like(lo_src)
    return jnp.concatenate([jnp.where(keep_lo, lo_src, zero), jnp.where(keep_hi, hi_src, zero)], axis=0)


def _attn_prompt_kernel(slopes_ref, sinks_ref, q_ref, kvc_ref, kvp_ref, g_ref, wg_ref, wu_ref, wd_ref, wo_ref,
                        o_ref, wg_o, wu_o, wd_o, wo_o, bias_scr, kvdup_scr, *, n_sub):
    i = pl.program_id(0)
    blk = WINDOW

    wg_o[...] = wg_ref[...].astype(BF16)
    wu_o[...] = wu_ref[...].astype(BF16)
    wd_o[...] = wd_ref[...].astype(BF16)
    wo_o[...] = wo_ref[...].astype(BF16)

    @pl.when(i == 0)
    def _():
        a = lax.broadcasted_iota(jnp.int32, (blk, 2 * blk), 0)
        j = lax.broadcasted_iota(jnp.int32, (blk, 2 * blk), 1)
        dist = a + blk - j
        valid = (dist >= 0) & (dist < WINDOW)
        distf = dist.astype(F32)
        for h in range(N_HEADS):
            kvh, g = divmod(h, Q_PER_KV)
            pr, t = divmod(g, 2)
            b = jnp.where(valid, -(slopes_ref[h] * distf), NEG_INF)
            b = jnp.where(j == 0, sinks_ref[h], b)
            rs = slice(pr * blk, (pr + 1) * blk)
            cs = slice(t * 2 * blk, (t + 1) * 2 * blk)
            bias_scr[1, kvh, rs, cs] = b
            bias_scr[0, kvh, rs, cs] = jnp.where((j >= blk) | (j == 0), b, NEG_INF)

    for src_ref, r_lo, r_hi in ((kvp_ref, 0, blk), (kvc_ref, blk, kvdup_scr.shape[0])):
        lo_half = lax.broadcasted_iota(jnp.int32, (r_hi - r_lo, LANES), 1) < HEAD_DIM
        for part in range(2):
            for s in range(2):
                c_in = part * KV_W + s * LANES
                c_out = part * N_KV * LANES + 2 * s * LANES
                x = src_ref[:, c_in:c_in + LANES].astype(F32)
                xr = pltpu.roll(x, HEAD_DIM, 1)
                kvdup_scr[r_lo:r_hi, c_out:c_out + LANES] = jnp.where(lo_half, x, xr).astype(BF16)
                kvdup_scr[r_lo:r_hi, c_out + LANES:c_out + 2 * LANES] = jnp.where(lo_half, xr, x).astype(BF16)

    lane = lax.broadcasted_iota(jnp.int32, (2 * blk, LANES), 1)
    not_sink = lax.broadcasted_iota(jnp.int32, (2 * blk, LANES), 0) != 0
    keep_lo = (lane < HEAD_DIM) & not_sink
    keep_hi = (lane >= HEAD_DIM) & not_sink
    ones_cols = jnp.concatenate([jnp.where(lane < HEAD_DIM, 1.0, 0.0),
                                 jnp.where(lane >= HEAD_DIM, 1.0, 0.0)], axis=0).astype(BF16)
    n_chunk = 4
    rc = 2 * blk // n_chunk

    def body(b, carry):
        r0 = pl.multiple_of(b * blk, blk)
        variant = jnp.where(jnp.logical_and(i == 0, b == 0), 0, 1)
        vbds, scores = [], []
        for kvh in range(N_KV):
            c0 = kvh * Q_PER_KV * HEAD_DIM
            kd = kvdup_scr[pl.ds(r0, 2 * blk), kvh * LANES:(kvh + 1) * LANES]
            vd = kvdup_scr[pl.ds(r0, 2 * blk), (N_KV + kvh) * LANES:(N_KV + kvh + 1) * LANES]
            kbd = _pair_blockdiag(kd, kd, keep_lo, keep_hi)
            vbds.append(jnp.concatenate([_pair_blockdiag(vd, vd, keep_lo, keep_hi), ones_cols], axis=1))
            q2 = jnp.concatenate([q_ref[pl.ds(r0, blk), c0:c0 + LANES],
                                  q_ref[pl.ds(r0, blk), c0 + LANES:c0 + 2 * LANES]], axis=0)
            scores.append(_dot_nt(q2, kbd) + bias_scr[variant, kvh])
        probs = []
        for kvh in range(N_KV):
            chunks = []
            for c in range(n_chunk):
                st = scores[kvh][c * rc:(c + 1) * rc, :]
                sl = st[:, 0:2 * blk]
                sr = st[:, 2 * blk:]
                p_l = jnp.exp(sl - jnp.max(sl, axis=-1, keepdims=True))
                p_r = jnp.exp(sr - jnp.max(sr, axis=-1, keepdims=True))
                chunks.append(jnp.concatenate([p_l, p_r], axis=1).astype(BF16))
            probs.append(jnp.concatenate(chunks, axis=0))
        outs = []
        for kvh in range(N_KV):
            o2 = _dot(probs[kvh], vbds[kvh])
            o = o2[:, 0:LANES] * (1.0 / o2[:, LANES:])
            outs += [o[0:blk], o[blk:]]
        att = jnp.concatenate(outs, axis=1)
        o_ref[pl.ds(r0, blk), :] = _rms(att, g_ref[...]).astype(BF16)
        return carry

    lax.fori_loop(0, n_sub, body, 0, unroll=2)


def _attn_prompt(q, kv, slopes, sinks, g_att, w_gate, w_up, w_down, w_out):
    n_rows = q.shape[0]
    rows = ROWS_ATT
    n_sub = rows // WINDOW
    n_steps = n_rows // rows
    smem = pl.BlockSpec(memory_space=pltpu.SMEM)
    slab = lambda w: pl.BlockSpec((w.shape[0] // n_steps, w.shape[1]), lambda i: (i, 0))
    ffn_w = (w_gate, w_up, w_down, w_out)
    assert all(w.shape[0] % (16 * n_steps) == 0 for w in ffn_w)
    return pl.pallas_call(
        functools.partial(_attn_prompt_kernel, n_sub=n_sub),
        grid=(n_rows // rows,),
        in_specs=[
            smem, smem,
            pl.BlockSpec((rows, ATT_W), lambda i: (i, 0)),
            pl.BlockSpec((rows, 2 * KV_W), lambda i: (i, 0)),
            pl.BlockSpec((WINDOW, 2 * KV_W), lambda i: (jnp.maximum(i * n_sub - 1, 0), 0)),
            pl.BlockSpec((1, ATT_W), lambda i: (0, 0)),
            *[slab(w) for w in ffn_w],
        ],
        out_specs=[pl.BlockSpec((rows, ATT_W), lambda i: (i, 0)), *[slab(w) for w in ffn_w]],
        out_shape=[jax.ShapeDtypeStruct((n_rows, ATT_W), BF16),
                   *[jax.ShapeDtypeStruct(w.shape, BF16) for w in ffn_w]],
        scratch_shapes=[
            pltpu.VMEM((2, N_KV, 2 * WINDOW, 4 * WINDOW), F32),
            pltpu.VMEM((rows + WINDOW, 2 * N_KV * LANES), BF16),
        ],
        compiler_params=_params(("arbitrary",)),
        name="attn_prompt",
    )(slopes, sinks, q, kv, kv, g_att, *ffn_w)


def _attn_sample_kernel(q_ref, kn_ref, vn_ref, ck_ref, cv_ref, slope_ref, sink_ref, g_ref,
                        o_ref, ko_ref, vo_ref, att_scr, *, n_seq):
    r16 = lax.broadcasted_iota(jnp.int32, (N_HEADS, ATT_W), 0)
    c16 = lax.broadcasted_iota(jnp.int32, (N_HEADS, ATT_W), 1)
    own_head = (c16 // HEAD_DIM) == r16
    newest = lax.broadcasted_iota(jnp.int32, (KV_W, WINDOW), 1) == WINDOW - 1
    jj = lax.broadcasted_iota(jnp.int32, (N_HEADS, WINDOW), 1)
    bias = -(slope_ref[...] * (WINDOW - 1 - jj).astype(F32))
    pad = jnp.zeros((LANES - n_seq, KV_W), F32)
    kn_t = jnp.concatenate([kn_ref[...], pad], axis=0).T
    vn_t = jnp.concatenate([vn_ref[...], pad], axis=0).T

    def fold4(v):
        return (v[:, 0:KV_W] + v[:, KV_W:2 * KV_W]) + (v[:, 2 * KV_W:3 * KV_W] + v[:, 3 * KV_W:])

    scores, values = [], []
    for b in range(n_seq):
        kw = jnp.where(newest, jnp.broadcast_to(kn_t[:, b:b + 1], (KV_W, WINDOW)),
                       pltpu.roll(ck_ref[b], WINDOW - 1, 1))
        vw = jnp.where(newest, jnp.broadcast_to(vn_t[:, b:b + 1], (KV_W, WINDOW)),
                       pltpu.roll(cv_ref[b], WINDOW - 1, 1))
        ko_ref[b] = kw
        vo_ref[b] = vw
        qb = jnp.broadcast_to(q_ref[b:b + 1, :].astype(F32), (N_HEADS, ATT_W))
        qbd = fold4(jnp.where(own_head, qb, 0.0))
        scores.append(_dot(qbd.astype(BF16), kw.astype(BF16)))
        values.append(vw.astype(BF16))
    st = jnp.concatenate(scores, axis=0) + jnp.concatenate([bias] * n_seq, axis=0)
    sink = jnp.concatenate([sink_ref[...]] * n_seq, axis=0)
    m = jnp.maximum(jnp.max(st, axis=-1, keepdims=True), sink)
    p = jnp.exp(st - m)
    inv_l = 1.0 / (jnp.sum(p, axis=-1, keepdims=True) + jnp.exp(sink - m))
    pb = p.astype(BF16)
    for b in range(n_seq):
        hs = slice(b * N_HEADS, (b + 1) * N_HEADS)
        o = _dot_nt(pb[hs, :], values[b]) * inv_l[hs, :]
        o4 = jnp.concatenate([o, o, o, o], axis=1)
        att_scr[b:b + 1, :] = jnp.sum(jnp.where(own_head, o4, 0.0), axis=0, keepdims=True)
    o_ref[...] = _rms(att_scr[...], g_ref[...]).astype(BF16)


def _attn_sample(q, kn, vn, ck, cv, slope_col, sink_col, g_att):
    n = q.shape[0]
    nb = SEQ_ATT
    cache_spec = pl.BlockSpec((nb, KV_W, WINDOW), lambda i: (i, 0, 0))
    row_spec = lambda w: pl.BlockSpec((nb, w), lambda i: (i, 0))
    full = lambda a: pl.BlockSpec(a.shape, lambda i: (0,) * a.ndim)
    return pl.pallas_call(
        functools.partial(_attn_sample_kernel, n_seq=nb),
        grid=(n // nb,),
        in_specs=[row_spec(ATT_W), row_spec(KV_W), row_spec(KV_W), cache_spec, cache_spec,
                  full(slope_col), full(sink_col), full(g_att)],
        out_specs=[row_spec(ATT_W), cache_spec, cache_spec],
        out_shape=[
            jax.ShapeDtypeStruct((n, ATT_W), BF16),
            jax.ShapeDtypeStruct(ck.shape, F32),
            jax.ShapeDtypeStruct(cv.shape, F32),
        ],
        scratch_shapes=[pltpu.VMEM((nb, ATT_W), F32)],
        compiler_params=_params(("parallel",)),
        name="attn_sample",
    )(q, kn, vn, ck, cv, slope_col, sink_col, g_att)


def _ssd_prompt_kernel(xb_ref, z_ref, dtb_ref, alog_ref, dskip_ref, g_ref,
                       y_ref, hout_ref, ht_scr, e_scr, tri_scr, yd_scr, *, rows):
    i = pl.program_id(0)
    T = CHUNK

    @pl.when(i == 0)
    def _():
        ht_scr[...] = jnp.zeros_like(ht_scr)
        e_scr[...] = _head_expand_matrix()
        l = lax.broadcasted_iota(jnp.int32, (T, T), 0)
        s = lax.broadcasted_iota(jnp.int32, (T, T), 1)
        tri_scr[...] = jnp.where(s <= l, 1.0, 0.0).astype(BF16)

    a_log2 = -jnp.exp(alog_ref[...]) * LOG2E
    e_mat = e_scr[...]
    tri = tri_scr[...]
    causal = lax.broadcasted_iota(jnp.int32, (T, T), 0) >= lax.broadcasted_iota(jnp.int32, (T, T), 1)
    lo = lax.broadcasted_iota(jnp.int32, (T, LANES), 1) < SSM_P
    gw = SSM_W // SSM_G

    for c in range(rows // T):
        rs = slice(c * T, (c + 1) * T)
        dt = _softplus(xb_ref[rs, CONV_DIM:XBCDT_W] + dtb_ref[...])
        dta = dt * a_log2
        hi, mid, lw = _split3(dta)
        acum = _dot(tri, hi) + _dot(tri, mid) + _dot(tri, lw)
        acum_e = _expand_heads(acum, e_mat)
        dt_e = _expand_heads(dt, e_mat)
        acum_t = acum.T
        xs = xb_ref[rs, 0:SSM_W]
        bm = xb_ref[rs, SSM_W:SSM_W + SSM_G * D_STATE]
        cm = xb_ref[rs, SSM_W + SSM_G * D_STATE:CONV_DIM]
        xdt = xs * dt_e
        last = acum_e[T - 1:T, :]
        xdt_b = xdt.astype(BF16)
        xdec_b = (xdt * jnp.exp2(last - acum_e)).astype(BF16)
        bm_b = bm.astype(BF16)
        cm_b = cm.astype(BF16)
        cbs = [_dot_nt(cm_b[:, g * D_STATE:(g + 1) * D_STATE], bm_b[:, g * D_STATE:(g + 1) * D_STATE])
               for g in range(SSM_G)]
        for pr in range(SSM_H // 2):
            g = (2 * pr) // (SSM_H // SSM_G)
            ws = []
            for t in range(2):
                h = 2 * pr + t
                seg = jnp.broadcast_to(acum[:, h:h + 1], (T, T)) - acum_t[h:h + 1, :]
                ws.append((cbs[g] * jnp.exp2(jnp.where(causal, seg, NEG_INF))).astype(BF16))
            xsl = xdt_b[:, pr * LANES:(pr + 1) * LANES]
            yd_scr[:, pr * LANES:(pr + 1) * LANES] = _dot(jnp.concatenate(ws, axis=1),
                                                          _pair_blockdiag(xsl, xsl, lo, ~lo))
        yoff = []
        for g in range(SSM_G):
            gs = slice(g * gw, (g + 1) * gw)
            ht_g = ht_scr[:, gs]
            yoff.append(_dot(cm_b[:, g * D_STATE:(g + 1) * D_STATE], ht_g.astype(BF16)))
            cst = _dot_tn(bm_b[:, g * D_STATE:(g + 1) * D_STATE], xdec_b[:, gs])
            ht_scr[:, gs] = ht_g * jnp.exp2(last[:, gs]) + cst
        y = yd_scr[...] + jnp.concatenate(yoff, axis=1) * jnp.exp2(acum_e)
        y = y + dskip_ref[...] * xs
        y = y * z_ref[rs, :]
        y_ref[rs, :] = _rms(y, g_ref[...]).astype(BF16)

    @pl.when(i == pl.num_programs(0) - 1)
    def _():
        hout_ref[...] = ht_scr[...].T


def _ssd_prompt(xb, z, dtb, alog, dskip_e, g_ssm):
    n_rows = xb.shape[0]
    rows = ROWS_SSD
    full = lambda a: pl.BlockSpec(a.shape, lambda i: (0,) * a.ndim)
    return pl.pallas_call(
        functools.partial(_ssd_prompt_kernel, rows=rows),
        grid=(n_rows // rows,),
        in_specs=[
            pl.BlockSpec((rows, XBCDT_W), lambda i: (i, 0)),
            pl.BlockSpec((rows, SSM_W), lambda i: (i, 0)),
            full(dtb), full(alog), full(dskip_e), full(g_ssm),
        ],
        out_specs=[
            pl.BlockSpec((rows, SSM_W), lambda i: (i, 0)),
            pl.BlockSpec((SSM_W, D_STATE), lambda i: (0, 0)),
        ],
        out_shape=[
            jax.ShapeDtypeStruct((n_rows, SSM_W), BF16),
            jax.ShapeDtypeStruct((SSM_W, D_STATE), F32),
        ],
        scratch_shapes=[
            pltpu.VMEM((D_STATE, SSM_W), F32),
            pltpu.VMEM((LANES, SSM_W), BF16),
            pltpu.VMEM((CHUNK, CHUNK), BF16),
            pltpu.VMEM((CHUNK, SSM_W), F32),
        ],
        compiler_params=_params(("arbitrary",)),
        name="ssd_prompt",
    )(xb, z, dtb, alog, dskip_e, g_ssm)


def _ssd_sample_kernel(xb_ref, sconv_ref, z_ref, h0_ref, cw_ref, cb_ref, dtb_ref, alog_ref, dskip_ref, g_ref,
                       y_ref, conv_ref, hout_ref, yoff_scr, *, n_seq):
    gw = SSM_W // SSM_G
    gn = SSM_G * D_STATE
    x_new = xb_ref[:, 0:CONV_DIM]
    taps = [sconv_ref[:, k * CONV_DIM:(k + 1) * CONV_DIM] for k in range(CONV_W - 1)] + [x_new]
    acc = cb_ref[...]
    for k in range(CONV_W):
        acc = acc + taps[k] * cw_ref[k:k + 1, :]
    for k in range(1, CONV_W):
        conv_ref[:, (k - 1) * CONV_DIM:k * CONV_DIM] = taps[k]
    xc = _silu(acc)
    xs = xc[:, 0:SSM_W]
    bm = xc[:, SSM_W:SSM_W + gn]
    cm = xc[:, SSM_W + gn:CONV_DIM]

    e_mat = _head_expand_matrix()
    dt = _softplus(xb_ref[:, CONV_DIM:XBCDT_W] + dtb_ref[...])
    dt_e = _expand_heads(dt, e_mat)
    dec_e = jnp.exp(_expand_heads(dt * (-jnp.exp(alog_ref[...])), e_mat))
    xdt = xs * dt_e

    lane_w = lax.broadcasted_iota(jnp.int32, (n_seq, SSM_W), 1)
    first_grp = lane_w < gw
    cbv = [jnp.sum(cm[:, g * D_STATE:(g + 1) * D_STATE] * bm[:, g * D_STATE:(g + 1) * D_STATE],
                   axis=-1, keepdims=True) for g in range(SSM_G)]
    cb_e = jnp.where(first_grp, cbv[0], cbv[1])

    pad = jnp.zeros((LANES - n_seq, SSM_W), F32)
    xdt_t = jnp.concatenate([xdt, pad], axis=0).T
    dec_t = jnp.concatenate([dec_e, pad], axis=0).T
    sub8 = lax.broadcasted_iota(jnp.int32, (8, D_STATE), 0)
    lane_r = lax.broadcasted_iota(jnp.int32, (8, SSM_W), 1)

    for b in range(n_seq):
        h0 = h0_ref[b]
        c_row = jnp.broadcast_to(cm[b:b + 1, :], (8, gn))
        c8 = jnp.where(sub8 == 0, c_row[:, 0:D_STATE], jnp.where(sub8 == 1, c_row[:, D_STATE:], 0.0))
        r = _dot_nt(c8.astype(BF16), h0.astype(BF16))
        yoff_scr[b:b + 1, :] = jnp.where(lane_r[0:1] < gw, r[0:1, :], r[1:2, :])
        b_row = bm[b:b + 1, :]
        for g in range(SSM_G):
            rs = slice(g * gw, (g + 1) * gw)
            dcol = jnp.broadcast_to(dec_t[rs, b:b + 1], (gw, D_STATE))
            xcol = jnp.broadcast_to(xdt_t[rs, b:b + 1], (gw, D_STATE))
            hout_ref[b, rs, :] = h0[rs, :] * dcol + xcol * b_row[:, g * D_STATE:(g + 1) * D_STATE]

    y = cb_e * xdt + yoff_scr[...] * dec_e
    y = y + dskip_ref[...] * xs
    y = y * _silu(z_ref[...])
    y_ref[...] = _rms(y, g_ref[...]).astype(BF16)


def _ssd_sample(xb, sconv, z, h0, cw8, cb, dtb, alog, dskip_e, g_ssm):
    n = xb.shape[0]
    nb = SEQ_SSD
    row_spec = lambda w: pl.BlockSpec((nb, w), lambda i: (i, 0))
    st_spec = pl.BlockSpec((nb, SSM_W, D_STATE), lambda i: (i, 0, 0))
    full = lambda a: pl.BlockSpec(a.shape, lambda i: (0,) * a.ndim)
    return pl.pallas_call(
        functools.partial(_ssd_sample_kernel, n_seq=nb),
        grid=(n // nb,),
        in_specs=[row_spec(XBCDT_W), row_spec((CONV_W - 1) * CONV_DIM), row_spec(SSM_W), st_spec,
                  full(cw8), full(cb), full(dtb), full(alog), full(dskip_e), full(g_ssm)],
        out_specs=[row_spec(SSM_W), row_spec((CONV_W - 1) * CONV_DIM), st_spec],
        out_shape=[
            jax.ShapeDtypeStruct((n, SSM_W), BF16),
            jax.ShapeDtypeStruct((n, (CONV_W - 1) * CONV_DIM), F32),
            jax.ShapeDtypeStruct(h0.shape, F32),
        ],
        scratch_shapes=[pltpu.VMEM((nb, SSM_W), F32)],
        compiler_params=_params(("parallel",)),
        name="ssd_sample",
    )(xb, sconv, z, h0, cw8, cb, dtb, alog, dskip_e, g_ssm)


def _outproj_rows(att, ssm, x, woa_ref, wos_ref, gpost, gt1, gpre, sh2, sc2):
    mix = _dot(att, woa_ref[...]) + _dot(ssm, wos_ref[...])
    x1 = x + _rms(mix, gpost * gt1)
    u2 = _rms(x1, gpre * (1.0 + sc2)) + sh2
    return x1, u2.astype(BF16)


def _outproj_kernel(att_ref, ssm_ref, x_ref, woa_ref, wos_ref, gpost_ref, gt1_ref, gpre_ref, sh2_ref, sc2_ref,
                    atts_ref, ssms_ref, xs_ref, gt1s_ref, sh2s_ref, sc2s_ref, perm_ref,
                    x1_ref, u2_ref, x1s_ref, u2s_ref, *, rows):
    for r0 in range(0, rows, SUB_OUT):
        rs = slice(r0, r0 + SUB_OUT)
        x1_ref[rs, :], u2_ref[rs, :] = _outproj_rows(
            att_ref[rs, :], ssm_ref[rs, :], x_ref[rs, :], woa_ref, wos_ref, gpost_ref[...], gt1_ref[0:1, :],
            gpre_ref[...], sh2_ref[0:1, :], sc2_ref[0:1, :])

    @pl.when(pl.program_id(0) == pl.num_programs(0) - 1)
    def _():
        att_s = _dot_nt(atts_ref[...], perm_ref[...]).astype(BF16)
        x1s_ref[...], u2s_ref[...] = _outproj_rows(
            att_s, ssms_ref[...], xs_ref[...], woa_ref, wos_ref, gpost_ref[...], gt1s_ref[...],
            gpre_ref[...], sh2s_ref[...], sc2s_ref[...])


def _outproj(att, ssm, x, att_s, ssm_s, x_s, mod_p, mod_s, w_out, g_post, g_pre, perm):
    n_rows = x.shape[0]
    n_s = x_s.shape[0]
    rows = ROWS_OUT
    assert rows % SUB_OUT == 0 and n_rows % rows == 0
    half = lambda r: pl.BlockSpec((ATT_W, D_MODEL), lambda i: (r, 0), pipeline_mode=pl.Buffered(1))
    row_spec = lambda w: pl.BlockSpec((rows, w), lambda i: (i, 0))
    sample = lambda w, col=0: pl.BlockSpec((n_s, w), lambda i: (0, col))
    return pl.pallas_call(
        functools.partial(_outproj_kernel, rows=rows),
        grid=(n_rows // rows,),
        in_specs=[
            row_spec(ATT_W), row_spec(SSM_W), row_spec(D_MODEL),
            half(0), half(1),
            _const_spec((1, D_MODEL)),
            _mod_spec(2),
            _const_spec((1, D_MODEL)),
            _mod_spec(3),
            _mod_spec(4),
            sample(ATT_W), sample(SSM_W), sample(D_MODEL),
            sample(D_MODEL, 2), sample(D_MODEL, 3), sample(D_MODEL, 4), _const_spec(perm.shape),
        ],
        out_specs=[row_spec(D_MODEL), row_spec(D_MODEL), sample(D_MODEL), sample(D_MODEL)],
        out_shape=[
            jax.ShapeDtypeStruct((n_rows, D_MODEL), F32),
            jax.ShapeDtypeStruct((n_rows, D_MODEL), BF16),
            jax.ShapeDtypeStruct((n_s, D_MODEL), F32),
            jax.ShapeDtypeStruct((n_s, D_MODEL), BF16),
        ],
        compiler_params=_params(("arbitrary",)),
        name="outproj",
    )(att, ssm, x, w_out, w_out, g_post, mod_p, g_pre, mod_p, mod_p, att_s, ssm_s, x_s, mod_s, mod_s, mod_s, perm)


def _ffn_kernel(u_ref, x1_hbm, wg_ref, wu_ref, wd_ref, gpost_ref, gt2_ref, us_ref, x1s_ref, gt2s_ref,
                y_ref, ys_ref, x1_buf, x1_sem, *, rows):
    i = pl.program_id(0)
    j = pl.program_id(1)
    x1_copy = pltpu.make_async_copy(x1_hbm.at[pl.ds(i * rows, rows), :], x1_buf, x1_sem)
    last = pl.num_programs(1) - 1

    def ffn_rows(u, first, final, acc_prev, resid, gt2):
        hid = (_silu(_dot(u, wg_ref[...])) * _dot(u, wu_ref[...])).astype(BF16)
        acc = _dot(hid, wd_ref[...])
        if not first:
            acc = acc_prev() + acc
        if final:
            acc = resid() + gt2() * _rms(acc, gpost_ref[...])
        return acc

    def d_ff_slice(first, final):
        sub = min(SUB_FFN, rows)
        for r0 in range(0, rows, sub):
            rs = slice(r0, r0 + sub)
            y_ref[rs, :] = ffn_rows(u_ref[rs, :], first, final, lambda: y_ref[rs, :],
                                    lambda: x1_buf[rs, :], lambda: gt2_ref[0:1, :])

        @pl.when(i == pl.num_programs(0) - 1)
        def _():
            ys_ref[...] = ffn_rows(us_ref[...], first, final, lambda: ys_ref[...],
                                   lambda: x1s_ref[...], lambda: gt2s_ref[...])

    @pl.when(j == 0)
    def _():
        x1_copy.start()
        d_ff_slice(True, False)

    @pl.when(jnp.logical_and(j > 0, j < last))
    def _():
        d_ff_slice(False, False)

    @pl.when(j == last)
    def _():
        x1_copy.wait()
        d_ff_slice(False, True)


def _ffn(u2, x1, mod_p, u2_s, x1_s, mod_s, wg, wu, wd, g_post):
    n_rows = x1.shape[0]
    n_s = x1_s.shape[0]
    rows = ROWS_FFN
    fb = FF_BLK
    assert D_FF // fb >= 2 and n_rows % rows == 0
    whole = lambda r, c: pl.BlockSpec((r, D_MODEL), lambda i, j: (0, c))
    return pl.pallas_call(
        functools.partial(_ffn_kernel, rows=rows),
        grid=(n_rows // rows, D_FF // fb),
        in_specs=[
            pl.BlockSpec((rows, D_MODEL), lambda i, j: (i, 0)),
            pl.BlockSpec(memory_space=pl.ANY),
            pl.BlockSpec((D_MODEL, fb), lambda i, j: (0, j)),
            pl.BlockSpec((D_MODEL, fb), lambda i, j: (0, j)),
            pl.BlockSpec((fb, D_MODEL), lambda i, j: (j, 0)),
            whole(1, 0),
            _mod_spec(5),
            whole(n_s, 0), whole(n_s, 0), whole(n_s, 5),
        ],
        out_specs=[pl.BlockSpec((rows, D_MODEL), lambda i, j: (i, 0)), whole(n_s, 0)],
        out_shape=[jax.ShapeDtypeStruct((n_rows, D_MODEL), F32), jax.ShapeDtypeStruct((n_s, D_MODEL), F32)],
        scratch_shapes=[pltpu.VMEM((rows, D_MODEL), F32), pltpu.SemaphoreType.DMA],
        compiler_params=_params(("arbitrary", "arbitrary"), VMEM_LIMIT_FFN),
        name="ffn",
    )(u2, x1, wg, wu, wd, g_post, mod_p, u2_s, x1_s, mod_s)


def _alibi_slopes():
    return (2.0 ** (-8.0 * np.arange(1, N_HEADS + 1) / N_HEADS)).astype(np.float32)


def kernel(x_prompt, x_sample, cache_k, cache_v, state_conv, state_ssm, c_prompt, c_sample, w_ada, b_ada, g_pre_mix, g_post_mix, w_in, attn_sinks, g_attn_out, conv_w, conv_b, dt_bias, a_log, d_skip, g_ssm_out, w_out, g_pre_ffn, g_post_ffn, w_gate, w_up, w_down):
    assert w_ada.shape[0] == 1, "one layer"
    n_s = x_sample.shape[0]
    row = lambda v: v.reshape(1, -1)

    g_att = row(g_attn_out[0])
    swapped_cols = np.arange(ATT_W).reshape(N_KV, Q_PER_KV, HEAD_DIM).swapaxes(0, 1).reshape(-1)
    perm = np.zeros((ATT_W, ATT_W), np.float32)
    perm[swapped_cols, np.arange(ATT_W)] = 1.0
    perm = jnp.asarray(perm, BF16)
    g_att_s = row(g_attn_out[0].reshape(N_KV, Q_PER_KV, HEAD_DIM).swapaxes(0, 1))
    slopes_np = _alibi_slopes()
    slopes = jnp.asarray(slopes_np)
    slopes_s = jnp.asarray(slopes_np.reshape(N_KV, Q_PER_KV).T.reshape(N_HEADS, 1))
    sinks = attn_sinks[0]
    sinks_s = sinks.reshape(N_KV, Q_PER_KV).T.reshape(N_HEADS, 1)
    cw8 = jnp.pad(conv_w[0], ((0, 8 - CONV_W), (0, 0)))
    cb = row(conv_b[0])
    dtb = jnp.pad(row(dt_bias[0]), ((0, 0), (0, LANES - SSM_H)))
    alog = jnp.pad(row(a_log[0]), ((0, 0), (0, LANES - SSM_H)))
    dskip_e = row(jnp.repeat(d_skip[0], SSM_P))
    g_ssm = row(g_ssm_out[0])

    c_p8 = jnp.pad(c_prompt, ((0, 8 - c_prompt.shape[0]), (0, 0)))
    mod_s, mod_p, w_all = _prep(c_sample, c_p8, w_ada[0], row(b_ada[0]), jnp.transpose(w_in[0]))

    xp = x_prompt[0]
    xs_ = x_sample[:, 0, :]
    q, kv, z, xb, kv_last, x_tail, q_s, kv_new, z_s, xb_s = _inproj(
        xp, xs_, row(g_pre_mix[0]), mod_p, mod_s, w_all, cw8, cb, perm)

    att, wg, wu, wd, wo = _attn_prompt(q, kv, slopes, sinks, g_att, w_gate[0], w_up[0], w_down[0], w_out[0])
    ssm, h_p = _ssd_prompt(xb, z, dtb, alog, dskip_e, g_ssm)
    keys_minor = lambda c: jnp.transpose(c, (0, 2, 3, 1)).reshape(n_s, KV_W, WINDOW)
    keys_major = lambda c: jnp.transpose(c.reshape(n_s, N_KV, HEAD_DIM, WINDOW), (0, 3, 1, 2))[None]
    att_s, k_s, v_s = _attn_sample(
        q_s, kv_new[:, :KV_W], kv_new[:, KV_W:],
        keys_minor(cache_k[0]), keys_minor(cache_v[0]), slopes_s, sinks_s, g_att_s)
    ssm_s, conv_s, h_s = _ssd_sample(
        xb_s, state_conv[0].reshape(n_s, (CONV_W - 1) * CONV_DIM), z_s,
        state_ssm[0].reshape(n_s, SSM_W, D_STATE), cw8, cb, dtb, alog, dskip_e, g_ssm)

    x1, u2, x1_s, u2_s = _outproj(att, ssm, xp, att_s, ssm_s, xs_, mod_p, mod_s, wo,
                                  row(g_post_mix[0]), row(g_pre_ffn[0]), perm)
    y_p, y_s = _ffn(u2, x1, mod_p, u2_s, x1_s, mod_s, wg, wu, wd, row(g_post_ffn[0]))

    return (
        y_p[None],
        y_s[:, None, :],
        kv_last[:, :KV_W].reshape(1, 1, WINDOW, N_KV, HEAD_DIM),
        kv_last[:, KV_W:].reshape(1, 1, WINDOW, N_KV, HEAD_DIM),
        x_tail[8 - (CONV_W - 1):].reshape(1, 1, CONV_W - 1, CONV_DIM),
        h_p.reshape(1, 1, SSM_H, SSM_P, D_STATE),
        keys_major(k_s),
        keys_major(v_s),
        conv_s.reshape(1, n_s, CONV_W - 1, CONV_DIM),
        h_s.reshape(1, n_s, SSM_H, SSM_P, D_STATE),
    )
```

```python
import functools

import numpy as np
import jax
import jax.numpy as jnp
from jax import lax
from jax.experimental import pallas as pl
from jax.experimental.pallas import tpu as pltpu

F32 = jnp.float32
BF16 = jnp.bfloat16

D_MODEL = 2048
ATT_W = 1024
HEAD_DIM = 64
N_HEADS = 16
N_KV = 4
Q_PER_KV = 4
KV_W = N_KV * HEAD_DIM
WINDOW = 128
SSM_W = 1024
SSM_P = 64
SSM_H = 16
SSM_G = 2
D_STATE = 128
CONV_W = 4
CONV_DIM = SSM_W + 2 * SSM_G * D_STATE
XBCDT_W = CONV_DIM + 128
D_FF = 5632
EPS = 1e-6
CHUNK = 128
NEG_INF = float("-inf")
LOG2E = 1.4426950408889634
Q_SCALE = HEAD_DIM ** -0.5 * LOG2E

VMEM_LIMIT = 56 * 1024 * 1024
VMEM_LIMIT_FFN = 60 * 1024 * 1024
LANES = 128

ROWS_IN = 512
ROWS_ATT = 512
ROWS_SSD = 1024
ROWS_OUT = 512
SUB_OUT = 128
ROWS_FFN = 1024
SUB_FFN = 512
FF_BLK = 512
SEQ_ATT = 16
SEQ_SSD = 8


def _params(sem, vmem=VMEM_LIMIT):
    return pltpu.CompilerParams(dimension_semantics=sem, vmem_limit_bytes=vmem)


def _silu(v):
    h = 0.5 * v
    return h + h * jnp.tanh(h)


def _softplus(v):
    return jnp.maximum(v, 0.0) + jnp.log1p(jnp.exp(-jnp.abs(v)))


def _rms(v, g):
    return v * lax.rsqrt(jnp.mean(v * v, axis=-1, keepdims=True) + EPS) * g


def _split3(v):
    hi = v.astype(BF16)
    r1 = v - hi.astype(F32)
    mid = r1.astype(BF16)
    lo = (r1 - mid.astype(F32)).astype(BF16)
    return hi, mid, lo


def _dot(a, b):
    return jnp.dot(a, b, preferred_element_type=F32)


def _dot_nt(a, b):
    return lax.dot_general(a, b, (((1,), (1,)), ((), ())), preferred_element_type=F32)


def _dot_tn(a, b):
    return lax.dot_general(a, b, (((0,), (0,)), ((), ())), preferred_element_type=F32)


def _expand_heads(v, e_mat):
    hi, mid, lo = _split3(v)
    return _dot(hi, e_mat) + _dot(mid, e_mat) + _dot(lo, e_mat)


def _head_expand_matrix():
    k = lax.broadcasted_iota(jnp.int32, (LANES, SSM_W), 0)
    c = lax.broadcasted_iota(jnp.int32, (LANES, SSM_W), 1)
    return jnp.where((c // SSM_P) == k, 1.0, 0.0).astype(BF16)


def _mod_spec(col):
    return pl.BlockSpec((8, D_MODEL), lambda i, *_: (0, col))


WS_BLK = 256
W_ALL_COLS = 4608
COL_X, COL_Q, COL_Z, COL_KV = 0, 2048, 3072, 4096
N_WS_SRC = -(-(ATT_W + 2 * KV_W + SSM_W + CONV_DIM + SSM_H) // WS_BLK)


def _ws_dst_block(s):
    q_end = ATT_W // WS_BLK
    kv_end = q_end + 2 * KV_W // WS_BLK
    z_end = kv_end + SSM_W // WS_BLK
    return jnp.where(s < q_end, COL_Q // WS_BLK + s,
                     jnp.where(s < kv_end, COL_KV // WS_BLK + s - q_end,
                               jnp.where(s < z_end, COL_Z // WS_BLK + s - kv_end,
                                         COL_X // WS_BLK + s - z_end)))


ADA_BLK = 768


def _prep_kernel(cs_ref, cp_ref, wa_ref, b_ref, wt_ref, ms_ref, mp_ref, wall_ref):
    s = pl.program_id(0)
    w = wa_ref[...].astype(BF16)
    b = b_ref[...]
    ms_ref[...] = _dot(_silu(cs_ref[...]).astype(BF16), w) + b
    mp_ref[...] = _dot(_silu(cp_ref[...]).astype(BF16), w) + b
    n_valid = jnp.where(s < N_WS_SRC - 1, WS_BLK, jnp.where(s == N_WS_SRC - 1, SSM_H, 0))
    lane = lax.broadcasted_iota(jnp.int32, wall_ref.shape, 1)
    wall_ref[...] = jnp.where(lane < n_valid, wt_ref[...].T, 0.0).astype(BF16)


def _prep(c_s, c_p8, w_ada, b_ada, w_in_t):
    k_dim = w_in_t.shape[1]
    n = w_ada.shape[1]
    n_ada = n // ADA_BLK
    assert n % ADA_BLK == 0 and n_ada <= N_WS_SRC + 1
    ada_col = lambda s: (0, jnp.minimum(s, n_ada - 1))
    return pl.pallas_call(
        _prep_kernel,
        grid=(N_WS_SRC + 1,),
        in_specs=[
            pl.BlockSpec(c_s.shape, lambda s: (0, 0)),
            pl.BlockSpec(c_p8.shape, lambda s: (0, 0)),
            pl.BlockSpec((D_MODEL, ADA_BLK), ada_col),
            pl.BlockSpec((1, ADA_BLK), ada_col),
            pl.BlockSpec((WS_BLK, k_dim), lambda s: (jnp.minimum(s, N_WS_SRC - 1), 0)),
        ],
        out_specs=[
            pl.BlockSpec((c_s.shape[0], ADA_BLK), ada_col),
            pl.BlockSpec((c_p8.shape[0], ADA_BLK), ada_col),
            pl.BlockSpec((k_dim, WS_BLK), lambda s: (0, _ws_dst_block(s))),
        ],
        out_shape=[
            jax.ShapeDtypeStruct((c_s.shape[0], n), F32),
            jax.ShapeDtypeStruct((c_p8.shape[0], n), F32),
            jax.ShapeDtypeStruct((k_dim, W_ALL_COLS), BF16),
        ],
        compiler_params=_params(("arbitrary",)),
        name="prep",
    )(c_s, c_p8, w_ada, b_ada, w_in_t)


def _modulated_norm(x, g, shift, scale):
    return (_rms(x, g) * (1.0 + scale) + shift).astype(BF16)


def _inproj_kernel(x_ref, g_ref, sh_ref, sc_ref, wq_ref, wkv_ref, wz_ref, wx_ref, cw_ref, cb_ref,
                   xs_ref, shs_ref, scs_ref, perm_ref,
                   q_ref, kv_ref, z_ref, xb_ref, kvlast_ref, xtail_ref, qs_ref, kvs_ref, zs_ref, xbs_ref,
                   xp_scr, *, rows):
    i = pl.program_id(0)
    tile = 2 * LANES

    @pl.when(i == 0)
    def _():
        xp_scr[:, rows:rows + 8, :] = jnp.zeros((CONV_DIM // LANES, 8, LANES), F32)

    u = _modulated_norm(x_ref[...], g_ref[...], sh_ref[0:1, :], sc_ref[0:1, :])

    def conv_slab(s, xs):
        ls = slice(s * LANES, (s + 1) * LANES)
        xtail_ref[:, ls] = xs[rows - 8:, :]
        xp_scr[s, 0:8, :] = xp_scr[s, rows:rows + 8, :]
        xp_scr[s, 8:, :] = xs
        acc = cb_ref[:, ls] + xp_scr[s, pl.ds(8 - (CONV_W - 1), rows), :] * cw_ref[0:1, ls]
        for k in range(1, CONV_W):
            acc = acc + xp_scr[s, pl.ds(8 - (CONV_W - 1) + k, rows), :] * cw_ref[k:k + 1, ls]
        xb_ref[:, ls] = _silu(acc)

    def post_z(c0, c1):
        def post(r):
            z_ref[:, c0:c1] = _silu(r)
        return post

    def post_q(c0, c1):
        def post(r):
            q_ref[:, c0:c1] = (r * Q_SCALE).astype(BF16)
        return post

    def post_kv(c0, c1):
        def post(r):
            kv_ref[:, c0:c1] = r.astype(BF16)
            kvlast_ref[:, c0:c1] = r[rows - WINDOW:, :]
        return post

    xb = _dot(u, wx_ref[...])
    xb_ref[:, CONV_DIM:] = xb[:, CONV_DIM:]
    slabs = [functools.partial(conv_slab, s, xb[:, s * LANES:(s + 1) * LANES]) for s in range(CONV_DIM // LANES)]
    jobs = []
    for w_ref, width, post_of in ((wz_ref, SSM_W, post_z), (wq_ref, ATT_W, post_q), (wkv_ref, 2 * KV_W, post_kv)):
        for c0 in range(0, width, tile):
            jobs.append((w_ref, c0, c0 + tile, post_of(c0, c0 + tile)))
    pending = None
    for n, (w_ref, c0, c1, post) in enumerate(jobs):
        r = _dot(u, w_ref[:, c0:c1])
        for slab in slabs[n * len(slabs) // len(jobs):(n + 1) * len(slabs) // len(jobs)]:
            slab()
        if pending is not None:
            pending[0](pending[1])
        pending = (post, r)
    pending[0](pending[1])

    @pl.when(i == pl.num_programs(0) - 1)
    def _():
        us = _modulated_norm(xs_ref[...], g_ref[...], shs_ref[...], scs_ref[...])
        qs = (_dot(us, wq_ref[...]) * Q_SCALE).astype(BF16)
        qs_ref[...] = _dot(qs, perm_ref[...]).astype(BF16)
        kvs_ref[...] = _dot(us, wkv_ref[...])
        zs_ref[...] = _dot(us, wz_ref[...])
        xbs_ref[...] = _dot(us, wx_ref[...])


def _const_spec(shape):
    return pl.BlockSpec(shape, lambda i, *_: (0,) * len(shape), pipeline_mode=pl.Buffered(1))


def _inproj(x, x_s, g, mod_p, mod_s, w_all, cw8, cb, perm):
    n_rows = x.shape[0]
    n_s = x_s.shape[0]
    rows = ROWS_IN
    row_spec = lambda w: pl.BlockSpec((rows, w), lambda i: (i, 0))
    seg_spec = lambda width, col: pl.BlockSpec((D_MODEL, width), lambda i: (0, col // width),
                                               pipeline_mode=pl.Buffered(1))
    sample_in = lambda col: pl.BlockSpec((n_s, D_MODEL), lambda i: (0, col), pipeline_mode=pl.Buffered(1))
    sample_out = lambda w: pl.BlockSpec((n_s, w), lambda i: (0, 0))
    return pl.pallas_call(
        functools.partial(_inproj_kernel, rows=rows),
        grid=(n_rows // rows,),
        in_specs=[
            row_spec(D_MODEL),
            _const_spec((1, D_MODEL)),
            _mod_spec(0),
            _mod_spec(1),
            seg_spec(ATT_W, COL_Q), seg_spec(2 * KV_W, COL_KV), seg_spec(SSM_W, COL_Z), seg_spec(XBCDT_W, COL_X),
            _const_spec(cw8.shape), _const_spec(cb.shape),
            sample_in(0), sample_in(0), sample_in(1), _const_spec(perm.shape),
        ],
        out_specs=[
            row_spec(ATT_W), row_spec(2 * KV_W), row_spec(SSM_W), row_spec(XBCDT_W),
            pl.BlockSpec((WINDOW, 2 * KV_W), lambda i: (0, 0)),
            pl.BlockSpec((8, CONV_DIM), lambda i: (0, 0)),
            sample_out(ATT_W), sample_out(2 * KV_W), sample_out(SSM_W), sample_out(XBCDT_W),
        ],
        out_shape=[
            jax.ShapeDtypeStruct((n_rows, ATT_W), BF16),
            jax.ShapeDtypeStruct((n_rows, 2 * KV_W), BF16),
            jax.ShapeDtypeStruct((n_rows, SSM_W), F32),
            jax.ShapeDtypeStruct((n_rows, XBCDT_W), F32),
            jax.ShapeDtypeStruct((WINDOW, 2 * KV_W), F32),
            jax.ShapeDtypeStruct((8, CONV_DIM), F32),
            jax.ShapeDtypeStruct((n_s, ATT_W), BF16),
            jax.ShapeDtypeStruct((n_s, 2 * KV_W), F32),
            jax.ShapeDtypeStruct((n_s, SSM_W), F32),
            jax.ShapeDtypeStruct((n_s, XBCDT_W), F32),
        ],
        scratch_shapes=[pltpu.VMEM((CONV_DIM // LANES, rows + 8, LANES), F32)],
        compiler_params=_params(("arbitrary",), VMEM_LIMIT_FFN),
        name="inproj",
    )(x, g, mod_p, mod_p, w_all, w_all, w_all, w_all, cw8, cb, x_s, mod_s, mod_s, perm)


def _pair_blockdiag(lo_src, hi_src, keep_lo, keep_hi):
    zero = jnp.zeros_like(lo_src)
    return jnp.concatenate([jnp.where(keep_lo, lo_src, zero), jnp.where(keep_hi, hi_src, zero)], axis=0)


def _attn_prompt_kernel(slopes_ref, sinks_ref, q_ref, kvc_ref, kvp_ref, g_ref, wg_ref, wu_ref, wd_ref, wo_ref,
                        o_ref, wg_o, wu_o, wd_o, wo_o, bias_scr, kvdup_scr, *, n_sub):
    i = pl.program_id(0)
    blk = WINDOW

    wg_o[...] = wg_ref[...].astype(BF16)
    wu_o[...] = wu_ref[...].astype(BF16)
    wd_o[...] = wd_ref[...].astype(BF16)
    wo_o[...] = wo_ref[...].astype(BF16)

    @pl.when(i == 0)
    def _():
        a = lax.broadcasted_iota(jnp.int32, (blk, 2 * blk), 0)
        j = lax.broadcasted_iota(jnp.int32, (blk, 2 * blk), 1)
        dist = a + blk - j
        valid = (dist >= 0) & (dist < WINDOW)
        distf = dist.astype(F32)
        for h in range(N_HEADS):
            kvh, g = divmod(h, Q_PER_KV)
            pr, t = divmod(g, 2)
            b = jnp.where(valid, -(slopes_ref[h] * distf) * LOG2E, NEG_INF)
            b = jnp.where(j == 0, sinks_ref[h] * LOG2E, b)
            rs = slice(pr * blk, (pr + 1) * blk)
            cs = slice(t * 2 * blk, (t + 1) * 2 * blk)
            bias_scr[1, kvh, rs, cs] = b
            bias_scr[0, kvh, rs, cs] = jnp.where((j >= blk) | (j == 0), b, NEG_INF)

    for src_ref, r_lo, r_hi in ((kvp_ref, 0, blk), (kvc_ref, blk, kvdup_scr.shape[0])):
        lo_half = lax.broadcasted_iota(jnp.int32, (r_hi - r_lo, LANES), 1) < HEAD_DIM
        for part in range(2):
            for s in range(2):
                c_in = part * KV_W + s * LANES
                c_out = part * N_KV * LANES + 2 * s * LANES
                x = src_ref[:, c_in:c_in + LANES].astype(F32)
                xr = pltpu.roll(x, HEAD_DIM, 1)
                kvdup_scr[r_lo:r_hi, c_out:c_out + LANES] = jnp.where(lo_half, x, xr).astype(BF16)
                kvdup_scr[r_lo:r_hi, c_out + LANES:c_out + 2 * LANES] = jnp.where(lo_half, xr, x).astype(BF16)

    lane = lax.broadcasted_iota(jnp.int32, (2 * blk, LANES), 1)
    not_sink = lax.broadcasted_iota(jnp.int32, (2 * blk, LANES), 0) != 0
    keep_lo = (lane < HEAD_DIM) & not_sink
    keep_hi = (lane >= HEAD_DIM) & not_sink
    ones_cols = jnp.concatenate([jnp.where(lane < HEAD_DIM, 1.0, 0.0),
                                 jnp.where(lane >= HEAD_DIM, 1.0, 0.0)], axis=0).astype(BF16)
    n_chunk = 4
    rc = 2 * blk // n_chunk

    def body(b, carry):
        r0 = pl.multiple_of(b * blk, blk)
        variant = jnp.where(jnp.logical_and(i == 0, b == 0), 0, 1)
        vbds, scores = [], []
        for kvh in range(N_KV):
            c0 = kvh * Q_PER_KV * HEAD_DIM
            kd = kvdup_scr[pl.ds(r0, 2 * blk), kvh * LANES:(kvh + 1) * LANES]
            vd = kvdup_scr[pl.ds(r0, 2 * blk), (N_KV + kvh) * LANES:(N_KV + kvh + 1) * LANES]
            kbd = _pair_blockdiag(kd, kd, keep_lo, keep_hi)
            vbds.append(jnp.concatenate([_pair_blockdiag(vd, vd, keep_lo, keep_hi), ones_cols], axis=1))
            q2 = jnp.concatenate([q_ref[pl.ds(r0, blk), c0:c0 + LANES],
                                  q_ref[pl.ds(r0, blk), c0 + LANES:c0 + 2 * LANES]], axis=0)
            scores.append(_dot_nt(q2, kbd) + bias_scr[variant, kvh])
        probs = []
        for kvh in range(N_KV):
            chunks = []
            for c in range(n_chunk):
                st = scores[kvh][c * rc:(c + 1) * rc, :]
                sl = st[:, 0:2 * blk]
                sr = st[:, 2 * blk:]
                p_l = jnp.exp2(sl - jnp.max(sl, axis=-1, keepdims=True))
                p_r = jnp.exp2(sr - jnp.max(sr, axis=-1, keepdims=True))
                chunks.append(jnp.concatenate([p_l, p_r], axis=1).astype(BF16))
            probs.append(jnp.concatenate(chunks, axis=0))
        outs = []
        for kvh in range(N_KV):
            o2 = _dot(probs[kvh], vbds[kvh])
            o = o2[:, 0:LANES] * (1.0 / o2[:, LANES:])
            outs += [o[0:blk], o[blk:]]
        att = jnp.concatenate(outs, axis=1)
        o_ref[pl.ds(r0, blk), :] = _rms(att, g_ref[...]).astype(BF16)
        return carry

    lax.fori_loop(0, n_sub, body, 0, unroll=2)


def _attn_prompt(q, kv, slopes, sinks, g_att, w_gate, w_up, w_down, w_out):
    n_rows = q.shape[0]
    rows = ROWS_ATT
    n_sub = rows // WINDOW
    n_steps = n_rows // rows
    smem = pl.BlockSpec(memory_space=pltpu.SMEM)
    slab = lambda w: pl.BlockSpec((w.shape[0] // n_steps, w.shape[1]), lambda i: (i, 0))
    ffn_w = (w_gate, w_up, w_down, w_out)
    assert all(w.shape[0] % (16 * n_steps) == 0 for w in ffn_w)
    return pl.pallas_call(
        functools.partial(_attn_prompt_kernel, n_sub=n_sub),
        grid=(n_rows // rows,),
        in_specs=[
            smem, smem,
            pl.BlockSpec((rows, ATT_W), lambda i: (i, 0)),
            pl.BlockSpec((rows, 2 * KV_W), lambda i: (i, 0)),
            pl.BlockSpec((WINDOW, 2 * KV_W), lambda i: (jnp.maximum(i * n_sub - 1, 0), 0)),
            pl.BlockSpec((1, ATT_W), lambda i: (0, 0)),
            *[slab(w) for w in ffn_w],
        ],
        out_specs=[pl.BlockSpec((rows, ATT_W), lambda i: (i, 0)), *[slab(w) for w in ffn_w]],
        out_shape=[jax.ShapeDtypeStruct((n_rows, ATT_W), BF16),
                   *[jax.ShapeDtypeStruct(w.shape, BF16) for w in ffn_w]],
        scratch_shapes=[
            pltpu.VMEM((2, N_KV, 2 * WINDOW, 4 * WINDOW), F32),
            pltpu.VMEM((rows + WINDOW, 2 * N_KV * LANES), BF16),
        ],
        compiler_params=_params(("arbitrary",)),
        name="attn_prompt",
    )(slopes, sinks, q, kv, kv, g_att, *ffn_w)


def _attn_sample_kernel(q_ref, kn_ref, vn_ref, ck_ref, cv_ref, slope_ref, sink_ref, g_ref,
                        o_ref, ko_ref, vo_ref, att_scr, *, n_seq):
    r16 = lax.broadcasted_iota(jnp.int32, (N_HEADS, ATT_W), 0)
    c16 = lax.broadcasted_iota(jnp.int32, (N_HEADS, ATT_W), 1)
    own_head = (c16 // HEAD_DIM) == r16
    newest = lax.broadcasted_iota(jnp.int32, (KV_W, WINDOW), 1) == WINDOW - 1
    jj = lax.broadcasted_iota(jnp.int32, (N_HEADS, WINDOW), 1)
    bias = -(slope_ref[...] * (WINDOW - 1 - jj).astype(F32)) * LOG2E
    pad = jnp.zeros((LANES - n_seq, KV_W), F32)
    kn_t = jnp.concatenate([kn_ref[...], pad], axis=0).T
    vn_t = jnp.concatenate([vn_ref[...], pad], axis=0).T

    def fold4(v):
        return (v[:, 0:KV_W] + v[:, KV_W:2 * KV_W]) + (v[:, 2 * KV_W:3 * KV_W] + v[:, 3 * KV_W:])

    scores, values = [], []
    for b in range(n_seq):
        kw = jnp.where(newest, jnp.broadcast_to(kn_t[:, b:b + 1], (KV_W, WINDOW)),
                       pltpu.roll(ck_ref[b], WINDOW - 1, 1))
        vw = jnp.where(newest, jnp.broadcast_to(vn_t[:, b:b + 1], (KV_W, WINDOW)),
                       pltpu.roll(cv_ref[b], WINDOW - 1, 1))
        ko_ref[b] = kw
        vo_ref[b] = vw
        qb = jnp.broadcast_to(q_ref[b:b + 1, :].astype(F32), (N_HEADS, ATT_W))
        qbd = fold4(jnp.where(own_head, qb, 0.0))
        scores.append(_dot(qbd.astype(BF16), kw.astype(BF16)))
        values.append(vw.astype(BF16))
    st = jnp.concatenate(scores, axis=0) + jnp.concatenate([bias] * n_seq, axis=0)
    sink = jnp.concatenate([sink_ref[...] * LOG2E] * n_seq, axis=0)
    m = jnp.maximum(jnp.max(st, axis=-1, keepdims=True), sink)
    p = jnp.exp2(st - m)
    inv_l = 1.0 / (jnp.sum(p, axis=-1, keepdims=True) + jnp.exp2(sink - m))
    pb = p.astype(BF16)
    for b in range(n_seq):
        hs = slice(b * N_HEADS, (b + 1) * N_HEADS)
        o = _dot_nt(pb[hs, :], values[b]) * inv_l[hs, :]
        o4 = jnp.concatenate([o, o, o, o], axis=1)
        att_scr[b:b + 1, :] = jnp.sum(jnp.where(own_head, o4, 0.0), axis=0, keepdims=True)
    o_ref[...] = _rms(att_scr[...], g_ref[...]).astype(BF16)


def _attn_sample(q, kn, vn, ck, cv, slope_col, sink_col, g_att):
    n = q.shape[0]
    nb = SEQ_ATT
    cache_spec = pl.BlockSpec((nb, KV_W, WINDOW), lambda i: (i, 0, 0))
    row_spec = lambda w: pl.BlockSpec((nb, w), lambda i: (i, 0))
    full = lambda a: pl.BlockSpec(a.shape, lambda i: (0,) * a.ndim)
    return pl.pallas_call(
        functools.partial(_attn_sample_kernel, n_seq=nb),
        grid=(n // nb,),
        in_specs=[row_spec(ATT_W), row_spec(KV_W), row_spec(KV_W), cache_spec, cache_spec,
                  full(slope_col), full(sink_col), full(g_att)],
        out_specs=[row_spec(ATT_W), cache_spec, cache_spec],
        out_shape=[
            jax.ShapeDtypeStruct((n, ATT_W), BF16),
            jax.ShapeDtypeStruct(ck.shape, F32),
            jax.ShapeDtypeStruct(cv.shape, F32),
        ],
        scratch_shapes=[pltpu.VMEM((nb, ATT_W), F32)],
        compiler_params=_params(("parallel",)),
        name="attn_sample",
    )(q, kn, vn, ck, cv, slope_col, sink_col, g_att)


def _ssd_prompt_kernel(xb_ref, z_ref, dtb_ref, alog_ref, dskip_ref, g_ref,
                       y_ref, hout_ref, ht_scr, e_scr, tri_scr, yd_scr, *, rows):
    i = pl.program_id(0)
    T = CHUNK

    @pl.when(i == 0)
    def _():
        ht_scr[...] = jnp.zeros_like(ht_scr)
        e_scr[...] = _head_expand_matrix()
        l = lax.broadcasted_iota(jnp.int32, (T, T), 0)
        s = lax.broadcasted_iota(jnp.int32, (T, T), 1)
        tri_scr[...] = jnp.where(s <= l, 1.0, 0.0).astype(BF16)

    a_log2 = -jnp.exp(alog_ref[...]) * LOG2E
    e_mat = e_scr[...]
    tri = tri_scr[...]
    causal = lax.broadcasted_iota(jnp.int32, (T, T), 0) >= lax.broadcasted_iota(jnp.int32, (T, T), 1)
    lo = lax.broadcasted_iota(jnp.int32, (T, LANES), 1) < SSM_P
    gw = SSM_W // SSM_G

    for c in range(rows // T):
        rs = slice(c * T, (c + 1) * T)
        dt = _softplus(xb_ref[rs, CONV_DIM:XBCDT_W] + dtb_ref[...])
        dta = dt * a_log2
        hi, mid, lw = _split3(dta)
        acum = _dot(tri, hi) + _dot(tri, mid) + _dot(tri, lw)
        acum_e = _expand_heads(acum, e_mat)
        dt_e = _expand_heads(dt, e_mat)
        acum_t = acum.T
        xs = xb_ref[rs, 0:SSM_W]
        bm = xb_ref[rs, SSM_W:SSM_W + SSM_G * D_STATE]
        cm = xb_ref[rs, SSM_W + SSM_G * D_STATE:CONV_DIM]
        xdt = xs * dt_e
        last = acum_e[T - 1:T, :]
        xdt_b = xdt.astype(BF16)
        xdec_b = (xdt * jnp.exp2(last - acum_e)).astype(BF16)
        bm_b = bm.astype(BF16)
        cm_b = cm.astype(BF16)
        cbs = [_dot_nt(cm_b[:, g * D_STATE:(g + 1) * D_STATE], bm_b[:, g * D_STATE:(g + 1) * D_STATE])
               for g in range(SSM_G)]
        for pr in range(SSM_H // 2):
            g = (2 * pr) // (SSM_H // SSM_G)
            ws = []
            for t in range(2):
                h = 2 * pr + t
                seg = jnp.broadcast_to(acum[:, h:h + 1], (T, T)) - acum_t[h:h + 1, :]
                ws.append((cbs[g] * jnp.exp2(jnp.where(causal, seg, NEG_INF))).astype(BF16))
            xsl = xdt_b[:, pr * LANES:(pr + 1) * LANES]
            yd_scr[:, pr * LANES:(pr + 1) * LANES] = _dot(jnp.concatenate(ws, axis=1),
                                                          _pair_blockdiag(xsl, xsl, lo, ~lo))
        yoff = []
        for g in range(SSM_G):
            gs = slice(g * gw, (g + 1) * gw)
            ht_g = ht_scr[:, gs]
            yoff.append(_dot(cm_b[:, g * D_STATE:(g + 1) * D_STATE], ht_g.astype(BF16)))
            cst = _dot_tn(bm_b[:, g * D_STATE:(g + 1) * D_STATE], xdec_b[:, gs])
            ht_scr[:, gs] = ht_g * jnp.exp2(last[:, gs]) + cst
        y = yd_scr[...] + jnp.concatenate(yoff, axis=1) * jnp.exp2(acum_e)
        y = y + dskip_ref[...] * xs
        y = y * z_ref[rs, :]
        y_ref[rs, :] = _rms(y, g_ref[...]).astype(BF16)

    @pl.when(i == pl.num_programs(0) - 1)
    def _():
        hout_ref[...] = ht_scr[...].T


def _ssd_prompt(xb, z, dtb, alog, dskip_e, g_ssm):
    n_rows = xb.shape[0]
    rows = ROWS_SSD
    full = lambda a: pl.BlockSpec(a.shape, lambda i: (0,) * a.ndim)
    return pl.pallas_call(
        functools.partial(_ssd_prompt_kernel, rows=rows),
        grid=(n_rows // rows,),
        in_specs=[
            pl.BlockSpec((rows, XBCDT_W), lambda i: (i, 0)),
            pl.BlockSpec((rows, SSM_W), lambda i: (i, 0)),
            full(dtb), full(alog), full(dskip_e), full(g_ssm),
        ],
        out_specs=[
            pl.BlockSpec((rows, SSM_W), lambda i: (i, 0)),
            pl.BlockSpec((SSM_W, D_STATE), lambda i: (0, 0)),
        ],
        out_shape=[
            jax.ShapeDtypeStruct((n_rows, SSM_W), BF16),
            jax.ShapeDtypeStruct((SSM_W, D_STATE), F32),
        ],
        scratch_shapes=[
            pltpu.VMEM((D_STATE, SSM_W), F32),
            pltpu.VMEM((LANES, SSM_W), BF16),
            pltpu.VMEM((CHUNK, CHUNK), BF16),
            pltpu.VMEM((CHUNK, SSM_W), F32),
        ],
        compiler_params=_params(("arbitrary",)),
        name="ssd_prompt",
    )(xb, z, dtb, alog, dskip_e, g_ssm)


def _ssd_sample_kernel(xb_ref, sconv_ref, z_ref, h0_ref, cw_ref, cb_ref, dtb_ref, alog_ref, dskip_ref, g_ref,
                       y_ref, conv_ref, hout_ref, yoff_scr, *, n_seq):
    gw = SSM_W // SSM_G
    gn = SSM_G * D_STATE
    x_new = xb_ref[:, 0:CONV_DIM]
    taps = [sconv_ref[:, k * CONV_DIM:(k + 1) * CONV_DIM] for k in range(CONV_W - 1)] + [x_new]
    acc = cb_ref[...]
    for k in range(CONV_W):
        acc = acc + taps[k] * cw_ref[k:k + 1, :]
    for k in range(1, CONV_W):
        conv_ref[:, (k - 1) * CONV_DIM:k * CONV_DIM] = taps[k]
    xc = _silu(acc)
    xs = xc[:, 0:SSM_W]
    bm = xc[:, SSM_W:SSM_W + gn]
    cm = xc[:, SSM_W + gn:CONV_DIM]

    e_mat = _head_expand_matrix()
    dt = _softplus(xb_ref[:, CONV_DIM:XBCDT_W] + dtb_ref[...])
    dt_e = _expand_heads(dt, e_mat)
    dec_e = jnp.exp(_expand_heads(dt * (-jnp.exp(alog_ref[...])), e_mat))
    xdt = xs * dt_e

    lane_w = lax.broadcasted_iota(jnp.int32, (n_seq, SSM_W), 1)
    first_grp = lane_w < gw
    cbv = [jnp.sum(cm[:, g * D_STATE:(g + 1) * D_STATE] * bm[:, g * D_STATE:(g + 1) * D_STATE],
                   axis=-1, keepdims=True) for g in range(SSM_G)]
    cb_e = jnp.where(first_grp, cbv[0], cbv[1])

    pad = jnp.zeros((LANES - n_seq, SSM_W), F32)
    xdt_t = jnp.concatenate([xdt, pad], axis=0).T
    dec_t = jnp.concatenate([dec_e, pad], axis=0).T
    sub8 = lax.broadcasted_iota(jnp.int32, (8, D_STATE), 0)
    lane_r = lax.broadcasted_iota(jnp.int32, (8, SSM_W), 1)

    for b in range(n_seq):
        h0 = h0_ref[b]
        c_row = jnp.broadcast_to(cm[b:b + 1, :], (8, gn))
        c8 = jnp.where(sub8 == 0, c_row[:, 0:D_STATE], jnp.where(sub8 == 1, c_row[:, D_STATE:], 0.0))
        r = _dot_nt(c8.astype(BF16), h0.astype(BF16))
        yoff_scr[b:b + 1, :] = jnp.where(lane_r[0:1] < gw, r[0:1, :], r[1:2, :])
        b_row = bm[b:b + 1, :]
        for g in range(SSM_G):
            rs = slice(g * gw, (g + 1) * gw)
            dcol = jnp.broadcast_to(dec_t[rs, b:b + 1], (gw, D_STATE))
            xcol = jnp.broadcast_to(xdt_t[rs, b:b + 1], (gw, D_STATE))
            hout_ref[b, rs, :] = h0[rs, :] * dcol + xcol * b_row[:, g * D_STATE:(g + 1) * D_STATE]

    y = cb_e * xdt + yoff_scr[...] * dec_e
    y = y + dskip_ref[...] * xs
    y = y * _silu(z_ref[...])
    y_ref[...] = _rms(y, g_ref[...]).astype(BF16)


def _ssd_sample(xb, sconv, z, h0, cw8, cb, dtb, alog, dskip_e, g_ssm):
    n = xb.shape[0]
    nb = SEQ_SSD
    row_spec = lambda w: pl.BlockSpec((nb, w), lambda i: (i, 0))
    st_spec = pl.BlockSpec((nb, SSM_W, D_STATE), lambda i: (i, 0, 0))
    full = lambda a: pl.BlockSpec(a.shape, lambda i: (0,) * a.ndim)
    return pl.pallas_call(
        functools.partial(_ssd_sample_kernel, n_seq=nb),
        grid=(n // nb,),
        in_specs=[row_spec(XBCDT_W), row_spec((CONV_W - 1) * CONV_DIM), row_spec(SSM_W), st_spec,
                  full(cw8), full(cb), full(dtb), full(alog), full(dskip_e), full(g_ssm)],
        out_specs=[row_spec(SSM_W), row_spec((CONV_W - 1) * CONV_DIM), st_spec],
        out_shape=[
            jax.ShapeDtypeStruct((n, SSM_W), BF16),
            jax.ShapeDtypeStruct((n, (CONV_W - 1) * CONV_DIM), F32),
            jax.ShapeDtypeStruct(h0.shape, F32),
        ],
        scratch_shapes=[pltpu.VMEM((nb, SSM_W), F32)],
        compiler_params=_params(("parallel",)),
        name="ssd_sample",
    )(xb, sconv, z, h0, cw8, cb, dtb, alog, dskip_e, g_ssm)


def _outproj_rows(att, ssm, x, woa_ref, wos_ref, gpost, gt1, gpre, sh2, sc2):
    mix = _dot(att, woa_ref[...]) + _dot(ssm, wos_ref[...])
    x1 = x + _rms(mix, gpost * gt1)
    u2 = _rms(x1, gpre * (1.0 + sc2)) + sh2
    return x1, u2.astype(BF16)


def _outproj_kernel(att_ref, ssm_ref, x_ref, woa_ref, wos_ref, gpost_ref, gt1_ref, gpre_ref, sh2_ref, sc2_ref,
                    atts_ref, ssms_ref, xs_ref, gt1s_ref, sh2s_ref, sc2s_ref, perm_ref,
                    x1_ref, u2_ref, x1s_ref, u2s_ref, *, rows):
    for r0 in range(0, rows, SUB_OUT):
        rs = slice(r0, r0 + SUB_OUT)
        x1_ref[rs, :], u2_ref[rs, :] = _outproj_rows(
            att_ref[rs, :], ssm_ref[rs, :], x_ref[rs, :], woa_ref, wos_ref, gpost_ref[...], gt1_ref[0:1, :],
            gpre_ref[...], sh2_ref[0:1, :], sc2_ref[0:1, :])

    @pl.when(pl.program_id(0) == pl.num_programs(0) - 1)
    def _():
        att_s = _dot_nt(atts_ref[...], perm_ref[...]).astype(BF16)
        x1s_ref[...], u2s_ref[...] = _outproj_rows(
            att_s, ssms_ref[...], xs_ref[...], woa_ref, wos_ref, gpost_ref[...], gt1s_ref[...],
            gpre_ref[...], sh2s_ref[...], sc2s_ref[...])


def _outproj(att, ssm, x, att_s, ssm_s, x_s, mod_p, mod_s, w_out, g_post, g_pre, perm):
    n_rows = x.shape[0]
    n_s = x_s.shape[0]
    rows = ROWS_OUT
    assert rows % SUB_OUT == 0 and n_rows % rows == 0
    half = lambda r: pl.BlockSpec((ATT_W, D_MODEL), lambda i: (r, 0), pipeline_mode=pl.Buffered(1))
    row_spec = lambda w: pl.BlockSpec((rows, w), lambda i: (i, 0))
    sample = lambda w, col=0: pl.BlockSpec((n_s, w), lambda i: (0, col))
    return pl.pallas_call(
        functools.partial(_outproj_kernel, rows=rows),
        grid=(n_rows // rows,),
        in_specs=[
            row_spec(ATT_W), row_spec(SSM_W), row_spec(D_MODEL),
            half(0), half(1),
            _const_spec((1, D_MODEL)),
            _mod_spec(2),
            _const_spec((1, D_MODEL)),
            _mod_spec(3),
            _mod_spec(4),
            sample(ATT_W), sample(SSM_W), sample(D_MODEL),
            sample(D_MODEL, 2), sample(D_MODEL, 3), sample(D_MODEL, 4), _const_spec(perm.shape),
        ],
        out_specs=[row_spec(D_MODEL), row_spec(D_MODEL), sample(D_MODEL), sample(D_MODEL)],
        out_shape=[
            jax.ShapeDtypeStruct((n_rows, D_MODEL), F32),
            jax.ShapeDtypeStruct((n_rows, D_MODEL), BF16),
            jax.ShapeDtypeStruct((n_s, D_MODEL), F32),
            jax.ShapeDtypeStruct((n_s, D_MODEL), BF16),
        ],
        compiler_params=_params(("arbitrary",)),
        name="outproj",
    )(att, ssm, x, w_out, w_out, g_post, mod_p, g_pre, mod_p, mod_p, att_s, ssm_s, x_s, mod_s, mod_s, mod_s, perm)


def _ffn_kernel(u_ref, x1_hbm, wg_ref, wu_ref, wd_ref, gpost_ref, gt2_ref, us_ref, x1s_ref, gt2s_ref,
                y_ref, ys_ref, x1_buf, x1_sem, *, rows):
    i = pl.program_id(0)
    j = pl.program_id(1)
    x1_copy = pltpu.make_async_copy(x1_hbm.at[pl.ds(i * rows, rows), :], x1_buf, x1_sem)
    last = pl.num_programs(1) - 1

    def ffn_rows(u, first, final, acc_prev, resid, gt2):
        hid = (_silu(_dot(u, wg_ref[...])) * _dot(u, wu_ref[...])).astype(BF16)
        acc = _dot(hid, wd_ref[...])
        if not first:
            acc = acc_prev() + acc
        if final:
            acc = resid() + gt2() * _rms(acc, gpost_ref[...])
        return acc

    def d_ff_slice(first, final):
        sub = min(SUB_FFN, rows)
        for r0 in range(0, rows, sub):
            rs = slice(r0, r0 + sub)
            y_ref[rs, :] = ffn_rows(u_ref[rs, :], first, final, lambda: y_ref[rs, :],
                                    lambda: x1_buf[rs, :], lambda: gt2_ref[0:1, :])

        @pl.when(i == pl.num_programs(0) - 1)
        def _():
            ys_ref[...] = ffn_rows(us_ref[...], first, final, lambda: ys_ref[...],
                                   lambda: x1s_ref[...], lambda: gt2s_ref[...])

    @pl.when(j == 0)
    def _():
        x1_copy.start()
        d_ff_slice(True, False)

    @pl.when(jnp.logical_and(j > 0, j < last))
    def _():
        d_ff_slice(False, False)

    @pl.when(j == last)
    def _():
        x1_copy.wait()
        d_ff_slice(False, True)


def _ffn(u2, x1, mod_p, u2_s, x1_s, mod_s, wg, wu, wd, g_post):
    n_rows = x1.shape[0]
    n_s = x1_s.shape[0]
    rows = ROWS_FFN
    fb = FF_BLK
    assert D_FF // fb >= 2 and n_rows % rows == 0
    whole = lambda r, c: pl.BlockSpec((r, D_MODEL), lambda i, j: (0, c))
    return pl.pallas_call(
        functools.partial(_ffn_kernel, rows=rows),
        grid=(n_rows // rows, D_FF // fb),
        in_specs=[
            pl.BlockSpec((rows, D_MODEL), lambda i, j: (i, 0)),
            pl.BlockSpec(memory_space=pl.ANY),
            pl.BlockSpec((D_MODEL, fb), lambda i, j: (0, j)),
            pl.BlockSpec((D_MODEL, fb), lambda i, j: (0, j)),
            pl.BlockSpec((fb, D_MODEL), lambda i, j: (j, 0)),
            whole(1, 0),
            _mod_spec(5),
            whole(n_s, 0), whole(n_s, 0), whole(n_s, 5),
        ],
        out_specs=[pl.BlockSpec((rows, D_MODEL), lambda i, j: (i, 0)), whole(n_s, 0)],
        out_shape=[jax.ShapeDtypeStruct((n_rows, D_MODEL), F32), jax.ShapeDtypeStruct((n_s, D_MODEL), F32)],
        scratch_shapes=[pltpu.VMEM((rows, D_MODEL), F32), pltpu.SemaphoreType.DMA],
        compiler_params=_params(("arbitrary", "arbitrary"), VMEM_LIMIT_FFN),
        name="ffn",
    )(u2, x1, wg, wu, wd, g_post, mod_p, u2_s, x1_s, mod_s)


def _alibi_slopes():
    return (2.0 ** (-8.0 * np.arange(1, N_HEADS + 1) / N_HEADS)).astype(np.float32)


def kernel(x_prompt, x_sample, cache_k, cache_v, state_conv, state_ssm, c_prompt, c_sample, w_ada, b_ada, g_pre_mix, g_post_mix, w_in, attn_sinks, g_attn_out, conv_w, conv_b, dt_bias, a_log, d_skip, g_ssm_out, w_out, g_pre_ffn, g_post_ffn, w_gate, w_up, w_down):
    assert w_ada.shape[0] == 1, "one layer"
    n_s = x_sample.shape[0]
    row = lambda v: v.reshape(1, -1)

    g_att = row(g_attn_out[0])
    swapped_cols = np.arange(ATT_W).reshape(N_KV, Q_PER_KV, HEAD_DIM).swapaxes(0, 1).reshape(-1)
    perm = np.zeros((ATT_W, ATT_W), np.float32)
    perm[swapped_cols, np.arange(ATT_W)] = 1.0
    perm = jnp.asarray(perm, BF16)
    g_att_s = row(g_attn_out[0].reshape(N_KV, Q_PER_KV, HEAD_DIM).swapaxes(0, 1))
    slopes_np = _alibi_slopes()
    slopes = jnp.asarray(slopes_np)
    slopes_s = jnp.asarray(slopes_np.reshape(N_KV, Q_PER_KV).T.reshape(N_HEADS, 1))
    sinks = attn_sinks[0]
    sinks_s = sinks.reshape(N_KV, Q_PER_KV).T.reshape(N_HEADS, 1)
    cw8 = jnp.pad(conv_w[0], ((0, 8 - CONV_W), (0, 0)))
    cb = row(conv_b[0])
    dtb = jnp.pad(row(dt_bias[0]), ((0, 0), (0, LANES - SSM_H)))
    alog = jnp.pad(row(a_log[0]), ((0, 0), (0, LANES - SSM_H)))
    dskip_e = row(jnp.repeat(d_skip[0], SSM_P))
    g_ssm = row(g_ssm_out[0])

    c_p8 = jnp.pad(c_prompt, ((0, 8 - c_prompt.shape[0]), (0, 0)))
    mod_s, mod_p, w_all = _prep(c_sample, c_p8, w_ada[0], row(b_ada[0]), jnp.transpose(w_in[0]))

    xp = x_prompt[0]
    xs_ = x_sample[:, 0, :]
    q, kv, z, xb, kv_last, x_tail, q_s, kv_new, z_s, xb_s = _inproj(
        xp, xs_, row(g_pre_mix[0]), mod_p, mod_s, w_all, cw8, cb, perm)

    att, wg, wu, wd, wo = _attn_prompt(q, kv, slopes, sinks, g_att, w_gate[0], w_up[0], w_down[0], w_out[0])
    ssm, h_p = _ssd_prompt(xb, z, dtb, alog, dskip_e, g_ssm)
    keys_minor = lambda c: jnp.transpose(c, (0, 2, 3, 1)).reshape(n_s, KV_W, WINDOW)
    keys_major = lambda c: jnp.transpose(c.reshape(n_s, N_KV, HEAD_DIM, WINDOW), (0, 3, 1, 2))[None]
    att_s, k_s, v_s = _attn_sample(
        q_s, kv_new[:, :KV_W], kv_new[:, KV_W:],
        keys_minor(cache_k[0]), keys_minor(cache_v[0]), slopes_s, sinks_s, g_att_s)
    ssm_s, conv_s, h_s = _ssd_sample(
        xb_s, state_conv[0].reshape(n_s, (CONV_W - 1) * CONV_DIM), z_s,
        state_ssm[0].reshape(n_s, SSM_W, D_STATE), cw8, cb, dtb, alog, dskip_e, g_ssm)

    x1, u2, x1_s, u2_s = _outproj(att, ssm, xp, att_s, ssm_s, xs_, mod_p, mod_s, wo,
                                  row(g_post_mix[0]), row(g_pre_ffn[0]), perm)
    y_p, y_s = _ffn(u2, x1, mod_p, u2_s, x1_s, mod_s, wg, wu, wd, row(g_post_ffn[0]))

    return (
        y_p[None],
        y_s[:, None, :],
        kv_last[:, :KV_W].reshape(1, 1, WINDOW, N_KV, HEAD_DIM),
        kv_last[:, KV_W:].reshape(1, 1, WINDOW, N_KV, HEAD_DIM),
        x_tail[8 - (CONV_W - 1):].reshape(1, 1, CONV_W - 1, CONV_DIM),
        h_p.reshape(1, 1, SSM_H, SSM_P, D_STATE),
        keys_major(k_s),
        keys_major(v_s),
        conv_s.reshape(1, n_s, CONV_W - 1, CONV_DIM),
        h_s.reshape(1, n_s, SSM_H, SSM_P, D_STATE),
    )
```

```python
import functools

import numpy as np
import jax
import jax.numpy as jnp
from jax import lax
from jax.experimental import pallas as pl
from jax.experimental.pallas import tpu as pltpu

F32 = jnp.float32
BF16 = jnp.bfloat16

D_MODEL = 2048
ATT_W = 1024
HEAD_DIM = 64
N_HEADS = 16
N_KV = 4
Q_PER_KV = 4
KV_W = N_KV * HEAD_DIM
WINDOW = 128
SSM_W = 1024
SSM_P = 64
SSM_H = 16
SSM_G = 2
D_STATE = 128
CONV_W = 4
CONV_DIM = SSM_W + 2 * SSM_G * D_STATE
XBCDT_W = CONV_DIM + 128
D_FF = 5632
EPS = 1e-6
CHUNK = 128
NEG_INF = float("-inf")
LOG2E = 1.4426950408889634

VMEM_LIMIT = 56 * 1024 * 1024
VMEM_LIMIT_FFN = 60 * 1024 * 1024
LANES = 128

ROWS_IN = 512
ROWS_ATT = 512
ROWS_SSD = 1024
ROWS_OUT = 512
SUB_OUT = 128
ROWS_FFN = 1024
SUB_FFN = 512
FF_BLK = 512
SEQ_ATT = 16
SEQ_SSD = 8


def _params(sem, vmem=VMEM_LIMIT):
    return pltpu.CompilerParams(dimension_semantics=sem, vmem_limit_bytes=vmem)


def _silu(v):
    h = 0.5 * v
    return h + h * jnp.tanh(h)


def _softplus(v):
    return jnp.maximum(v, 0.0) + jnp.log1p(jnp.exp(-jnp.abs(v)))


def _rms(v, g):
    return v * lax.rsqrt(jnp.mean(v * v, axis=-1, keepdims=True) + EPS) * g


def _split3(v):
    hi = v.astype(BF16)
    r1 = v - hi.astype(F32)
    mid = r1.astype(BF16)
    lo = (r1 - mid.astype(F32)).astype(BF16)
    return hi, mid, lo


def _dot(a, b):
    return jnp.dot(a, b, preferred_element_type=F32)


def _dot_nt(a, b):
    return lax.dot_general(a, b, (((1,), (1,)), ((), ())), preferred_element_type=F32)


def _dot_tn(a, b):
    return lax.dot_general(a, b, (((0,), (0,)), ((), ())), preferred_element_type=F32)


def _expand_heads(v, e_mat):
    hi, mid, lo = _split3(v)
    return _dot(hi, e_mat) + _dot(mid, e_mat) + _dot(lo, e_mat)


def _head_expand_matrix():
    k = lax.broadcasted_iota(jnp.int32, (LANES, SSM_W), 0)
    c = lax.broadcasted_iota(jnp.int32, (LANES, SSM_W), 1)
    return jnp.where((c // SSM_P) == k, 1.0, 0.0).astype(BF16)


def _mod_spec(col):
    return pl.BlockSpec((8, D_MODEL), lambda i, *_: (0, col))


WS_BLK = 256
W_ALL_COLS = 4608
COL_X, COL_Q, COL_Z, COL_KV = 0, 2048, 3072, 4096
N_WS_SRC = -(-(ATT_W + 2 * KV_W + SSM_W + CONV_DIM + SSM_H) // WS_BLK)


def _ws_dst_block(s):
    q_end = ATT_W // WS_BLK
    kv_end = q_end + 2 * KV_W // WS_BLK
    z_end = kv_end + SSM_W // WS_BLK
    return jnp.where(s < q_end, COL_Q // WS_BLK + s,
                     jnp.where(s < kv_end, COL_KV // WS_BLK + s - q_end,
                               jnp.where(s < z_end, COL_Z // WS_BLK + s - kv_end,
                                         COL_X // WS_BLK + s - z_end)))


ADA_BLK = 768


def _prep_kernel(cs_ref, cp_ref, wa_ref, b_ref, wt_ref, ms_ref, mp_ref, wall_ref):
    s = pl.program_id(0)
    w = wa_ref[...].astype(BF16)
    b = b_ref[...]
    ms_ref[...] = _dot(_silu(cs_ref[...]).astype(BF16), w) + b
    mp_ref[...] = _dot(_silu(cp_ref[...]).astype(BF16), w) + b
    n_valid = jnp.where(s < N_WS_SRC - 1, WS_BLK, jnp.where(s == N_WS_SRC - 1, SSM_H, 0))
    lane = lax.broadcasted_iota(jnp.int32, wall_ref.shape, 1)
    wall_ref[...] = jnp.where(lane < n_valid, wt_ref[...].T, 0.0).astype(BF16)


def _prep(c_s, c_p8, w_ada, b_ada, w_in_t):
    k_dim = w_in_t.shape[1]
    n = w_ada.shape[1]
    n_ada = n // ADA_BLK
    assert n % ADA_BLK == 0 and n_ada <= N_WS_SRC + 1
    ada_col = lambda s: (0, jnp.minimum(s, n_ada - 1))
    return pl.pallas_call(
        _prep_kernel,
        grid=(N_WS_SRC + 1,),
        in_specs=[
            pl.BlockSpec(c_s.shape, lambda s: (0, 0)),
            pl.BlockSpec(c_p8.shape, lambda s: (0, 0)),
            pl.BlockSpec((D_MODEL, ADA_BLK), ada_col),
            pl.BlockSpec((1, ADA_BLK), ada_col),
            pl.BlockSpec((WS_BLK, k_dim), lambda s: (jnp.minimum(s, N_WS_SRC - 1), 0)),
        ],
        out_specs=[
            pl.BlockSpec((c_s.shape[0], ADA_BLK), ada_col),
            pl.BlockSpec((c_p8.shape[0], ADA_BLK), ada_col),
            pl.BlockSpec((k_dim, WS_BLK), lambda s: (0, _ws_dst_block(s))),
        ],
        out_shape=[
            jax.ShapeDtypeStruct((c_s.shape[0], n), F32),
            jax.ShapeDtypeStruct((c_p8.shape[0], n), F32),
            jax.ShapeDtypeStruct((k_dim, W_ALL_COLS), BF16),
        ],
        compiler_params=_params(("arbitrary",)),
        name="prep",
    )(c_s, c_p8, w_ada, b_ada, w_in_t)


def _modulated_norm(x, g, shift, scale):
    return (_rms(x, g) * (1.0 + scale) + shift).astype(BF16)


def _inproj_kernel(x_ref, g_ref, sh_ref, sc_ref, wq_ref, wkv_ref, wz_ref, wx_ref, cw_ref, cb_ref,
                   xs_ref, shs_ref, scs_ref, perm_ref,
                   q_ref, kv_ref, z_ref, xb_ref, kvlast_ref, xtail_ref, qs_ref, kvs_ref, zs_ref, xbs_ref,
                   xp_scr, *, rows):
    i = pl.program_id(0)
    tile = 2 * LANES

    @pl.when(i == 0)
    def _():
        xp_scr[:, rows:rows + 8, :] = jnp.zeros((CONV_DIM // LANES, 8, LANES), F32)

    u = _modulated_norm(x_ref[...], g_ref[...], sh_ref[0:1, :], sc_ref[0:1, :])

    def conv_slab(s, xs):
        ls = slice(s * LANES, (s + 1) * LANES)
        xtail_ref[:, ls] = xs[rows - 8:, :]
        xp_scr[s, 0:8, :] = xp_scr[s, rows:rows + 8, :]
        xp_scr[s, 8:, :] = xs
        acc = cb_ref[:, ls] + xp_scr[s, pl.ds(8 - (CONV_W - 1), rows), :] * cw_ref[0:1, ls]
        for k in range(1, CONV_W):
            acc = acc + xp_scr[s, pl.ds(8 - (CONV_W - 1) + k, rows), :] * cw_ref[k:k + 1, ls]
        xb_ref[:, ls] = _silu(acc)

    def post_z(c0, c1):
        def post(r):
            z_ref[:, c0:c1] = _silu(r)
        return post

    def post_q(c0, c1):
        def post(r):
            q_ref[:, c0:c1] = (r * (HEAD_DIM ** -0.5)).astype(BF16)
        return post

    def post_kv(c0, c1):
        def post(r):
            kv_ref[:, c0:c1] = r.astype(BF16)
            kvlast_ref[:, c0:c1] = r[rows - WINDOW:, :]
        return post

    xb = _dot(u, wx_ref[...])
    xb_ref[:, CONV_DIM:] = xb[:, CONV_DIM:]
    slabs = [functools.partial(conv_slab, s, xb[:, s * LANES:(s + 1) * LANES]) for s in range(CONV_DIM // LANES)]
    jobs = []
    for w_ref, width, post_of in ((wz_ref, SSM_W, post_z), (wq_ref, ATT_W, post_q), (wkv_ref, 2 * KV_W, post_kv)):
        for c0 in range(0, width, tile):
            jobs.append((w_ref, c0, c0 + tile, post_of(c0, c0 + tile)))
    pending = None
    for n, (w_ref, c0, c1, post) in enumerate(jobs):
        r = _dot(u, w_ref[:, c0:c1])
        for slab in slabs[n * len(slabs) // len(jobs):(n + 1) * len(slabs) // len(jobs)]:
            slab()
        if pending is not None:
            pending[0](pending[1])
        pending = (post, r)
    pending[0](pending[1])

    @pl.when(i == pl.num_programs(0) - 1)
    def _():
        us = _modulated_norm(xs_ref[...], g_ref[...], shs_ref[...], scs_ref[...])
        qs = (_dot(us, wq_ref[...]) * (HEAD_DIM ** -0.5)).astype(BF16)
        qs_ref[...] = _dot(qs, perm_ref[...]).astype(BF16)
        kvs_ref[...] = _dot(us, wkv_ref[...])
        zs_ref[...] = _dot(us, wz_ref[...])
        xbs_ref[...] = _dot(us, wx_ref[...])


def _const_spec(shape):
    return pl.BlockSpec(shape, lambda i, *_: (0,) * len(shape), pipeline_mode=pl.Buffered(1))


def _inproj(x, x_s, g, mod_p, mod_s, w_all, cw8, cb, perm):
    n_rows = x.shape[0]
    n_s = x_s.shape[0]
    rows = ROWS_IN
    row_spec = lambda w: pl.BlockSpec((rows, w), lambda i: (i, 0))
    seg_spec = lambda width, col: pl.BlockSpec((D_MODEL, width), lambda i: (0, col // width),
                                               pipeline_mode=pl.Buffered(1))
    sample_in = lambda col: pl.BlockSpec((n_s, D_MODEL), lambda i: (0, col), pipeline_mode=pl.Buffered(1))
    sample_out = lambda w: pl.BlockSpec((n_s, w), lambda i: (0, 0))
    return pl.pallas_call(
        functools.partial(_inproj_kernel, rows=rows),
        grid=(n_rows // rows,),
        in_specs=[
            row_spec(D_MODEL),
            _const_spec((1, D_MODEL)),
            _mod_spec(0),
            _mod_spec(1),
            seg_spec(ATT_W, COL_Q), seg_spec(2 * KV_W, COL_KV), seg_spec(SSM_W, COL_Z), seg_spec(XBCDT_W, COL_X),
            _const_spec(cw8.shape), _const_spec(cb.shape),
            sample_in(0), sample_in(0), sample_in(1), _const_spec(perm.shape),
        ],
        out_specs=[
            row_spec(ATT_W), row_spec(2 * KV_W), row_spec(SSM_W), row_spec(XBCDT_W),
            pl.BlockSpec((WINDOW, 2 * KV_W), lambda i: (0, 0)),
            pl.BlockSpec((8, CONV_DIM), lambda i: (0, 0)),
            sample_out(ATT_W), sample_out(2 * KV_W), sample_out(SSM_W), sample_out(XBCDT_W),
        ],
        out_shape=[
            jax.ShapeDtypeStruct((n_rows, ATT_W), BF16),
            jax.ShapeDtypeStruct((n_rows, 2 * KV_W), BF16),
            jax.ShapeDtypeStruct((n_rows, SSM_W), F32),
            jax.ShapeDtypeStruct((n_rows, XBCDT_W), F32),
            jax.ShapeDtypeStruct((WINDOW, 2 * KV_W), F32),
            jax.ShapeDtypeStruct((8, CONV_DIM), F32),
            jax.ShapeDtypeStruct((n_s, ATT_W), BF16),
            jax.ShapeDtypeStruct((n_s, 2 * KV_W), F32),
            jax.ShapeDtypeStruct((n_s, SSM_W), F32),
            jax.ShapeDtypeStruct((n_s, XBCDT_W), F32),
        ],
        scratch_shapes=[pltpu.VMEM((CONV_DIM // LANES, rows + 8, LANES), F32)],
        compiler_params=_params(("arbitrary",), VMEM_LIMIT_FFN),
        name="inproj",
    )(x, g, mod_p, mod_p, w_all, w_all, w_all, w_all, cw8, cb, x_s, mod_s, mod_s, perm)


def _pair_blockdiag(lo_src, hi_src, keep_lo, keep_hi):
    zero = jnp.zeros_like(lo_src)
    return jnp.concatenate([jnp.where(keep_lo, lo_src, zero), jnp.where(keep_hi, hi_src, zero)], axis=0)


def _attn_prompt_kernel(slopes_ref, sinks_ref, q_ref, kvc_ref, kvp_ref, g_ref, wg_ref, wu_ref, wd_ref, wo_ref,
                        o_ref, wg_o, wu_o, wd_o, wo_o, bias_scr, kvdup_scr, *, n_sub):
    i = pl.program_id(0)
    blk = WINDOW

    wg_o[...] = wg_ref[...].astype(BF16)
    wu_o[...] = wu_ref[...].astype(BF16)
    wd_o[...] = wd_ref[...].astype(BF16)
    wo_o[...] = wo_ref[...].astype(BF16)

    @pl.when(i == 0)
    def _():
        a = lax.broadcasted_iota(jnp.int32, (blk, 2 * blk), 0)
        j = lax.broadcasted_iota(jnp.int32, (blk, 2 * blk), 1)
        dist = a + blk - j
        valid = (dist >= 0) & (dist < WINDOW)
        distf = dist.astype(F32)
        for h in range(N_HEADS):
            kvh, g = divmod(h, Q_PER_KV)
            pr, t = divmod(g, 2)
            b = jnp.where(valid, -(slopes_ref[h] * distf), NEG_INF)
            b = jnp.where(j == 0, sinks_ref[h], b)
            rs = slice(pr * blk, (pr + 1) * blk)
            cs = slice(t * 2 * blk, (t + 1) * 2 * blk)
            bias_scr[1, kvh, rs, cs] = b
            bias_scr[0, kvh, rs, cs] = jnp.where((j >= blk) | (j == 0), b, NEG_INF)

    for src_ref, r_lo, r_hi in ((kvp_ref, 0, blk), (kvc_ref, blk, kvdup_scr.shape[0])):
        lo_half = lax.broadcasted_iota(jnp.int32, (r_hi - r_lo, LANES), 1) < HEAD_DIM
        for part in range(2):
            for s in range(2):
                c_in = part * KV_W + s * LANES
                c_out = part * N_KV * LANES + 2 * s * LANES
                x = src_ref[:, c_in:c_in + LANES].astype(F32)
                xr = pltpu.roll(x, HEAD_DIM, 1)
                kvdup_scr[r_lo:r_hi, c_out:c_out + LANES] = jnp.where(lo_half, x, xr).astype(BF16)
                kvdup_scr[r_lo:r_hi, c_out + LANES:c_out + 2 * LANES] = jnp.where(lo_half, xr, x).astype(BF16)

    lane = lax.broadcasted_iota(jnp.int32, (2 * blk, LANES), 1)
    not_sink = lax.broadcasted_iota(jnp.int32, (2 * blk, LANES), 0) != 0
    keep_lo = (lane < HEAD_DIM) & not_sink
    keep_hi = (lane >= HEAD_DIM) & not_sink
    ones_cols = jnp.concatenate([jnp.where(lane < HEAD_DIM, 1.0, 0.0),
                                 jnp.where(lane >= HEAD_DIM, 1.0, 0.0)], axis=0).astype(BF16)
    n_chunk = 4
    rc = 2 * blk // n_chunk

    def body(b, carry):
        r0 = pl.multiple_of(b * blk, blk)
        variant = jnp.where(jnp.logical_and(i == 0, b == 0), 0, 1)
        vbds, scores = [], []
        for kvh in range(N_KV):
            c0 = kvh * Q_PER_KV * HEAD_DIM
            kd = kvdup_scr[pl.ds(r0, 2 * blk), kvh * LANES:(kvh + 1) * LANES]
            vd = kvdup_scr[pl.ds(r0, 2 * blk), (N_KV + kvh) * LANES:(N_KV + kvh + 1) * LANES]
            kbd = _pair_blockdiag(kd, kd, keep_lo, keep_hi)
            vbds.append(jnp.concatenate([_pair_blockdiag(vd, vd, keep_lo, keep_hi), ones_cols], axis=1))
            q2 = jnp.concatenate([q_ref[pl.ds(r0, blk), c0:c0 + LANES],
                                  q_ref[pl.ds(r0, blk), c0 + LANES:c0 + 2 * LANES]], axis=0)
            scores.append(_dot_nt(q2, kbd) + bias_scr[variant, kvh])
        probs = []
        for kvh in range(N_KV):
            chunks = []
            for c in range(n_chunk):
                st = scores[kvh][c * rc:(c + 1) * rc, :]
                sl = st[:, 0:2 * blk]
                sr = st[:, 2 * blk:]
                p_l = jnp.exp(sl - jnp.max(sl, axis=-1, keepdims=True))
                p_r = jnp.exp(sr - jnp.max(sr, axis=-1, keepdims=True))
                chunks.append(jnp.concatenate([p_l, p_r], axis=1).astype(BF16))
            probs.append(jnp.concatenate(chunks, axis=0))
        outs = []
        for kvh in range(N_KV):
            o2 = _dot(probs[kvh], vbds[kvh])
            o = o2[:, 0:LANES] * (1.0 / o2[:, LANES:])
            outs += [o[0:blk], o[blk:]]
        att = jnp.concatenate(outs, axis=1)
        o_ref[pl.ds(r0, blk), :] = _rms(att, g_ref[...]).astype(BF16)
        return carry

    lax.fori_loop(0, n_sub, body, 0, unroll=True)


def _attn_prompt(q, kv, slopes, sinks, g_att, w_gate, w_up, w_down, w_out):
    n_rows = q.shape[0]
    rows = ROWS_ATT
    n_sub = rows // WINDOW
    n_steps = n_rows // rows
    smem = pl.BlockSpec(memory_space=pltpu.SMEM)
    slab = lambda w: pl.BlockSpec((w.shape[0] // n_steps, w.shape[1]), lambda i: (i, 0))
    ffn_w = (w_gate, w_up, w_down, w_out)
    assert all(w.shape[0] % (16 * n_steps) == 0 for w in ffn_w)
    return pl.pallas_call(
        functools.partial(_attn_prompt_kernel, n_sub=n_sub),
        grid=(n_rows // rows,),
        in_specs=[
            smem, smem,
            pl.BlockSpec((rows, ATT_W), lambda i: (i, 0)),
            pl.BlockSpec((rows, 2 * KV_W), lambda i: (i, 0)),
            pl.BlockSpec((WINDOW, 2 * KV_W), lambda i: (jnp.maximum(i * n_sub - 1, 0), 0)),
            pl.BlockSpec((1, ATT_W), lambda i: (0, 0)),
            *[slab(w) for w in ffn_w],
        ],
        out_specs=[pl.BlockSpec((rows, ATT_W), lambda i: (i, 0)), *[slab(w) for w in ffn_w]],
        out_shape=[jax.ShapeDtypeStruct((n_rows, ATT_W), BF16),
                   *[jax.ShapeDtypeStruct(w.shape, BF16) for w in ffn_w]],
        scratch_shapes=[
            pltpu.VMEM((2, N_KV, 2 * WINDOW, 4 * WINDOW), F32),
            pltpu.VMEM((rows + WINDOW, 2 * N_KV * LANES), BF16),
        ],
        compiler_params=_params(("arbitrary",)),
        name="attn_prompt",
    )(slopes, sinks, q, kv, kv, g_att, *ffn_w)


def _attn_sample_kernel(q_ref, kn_ref, vn_ref, ck_ref, cv_ref, slope_ref, sink_ref, g_ref,
                        o_ref, ko_ref, vo_ref, att_scr, *, n_seq):
    r16 = lax.broadcasted_iota(jnp.int32, (N_HEADS, ATT_W), 0)
    c16 = lax.broadcasted_iota(jnp.int32, (N_HEADS, ATT_W), 1)
    own_head = (c16 // HEAD_DIM) == r16
    newest = lax.broadcasted_iota(jnp.int32, (KV_W, WINDOW), 1) == WINDOW - 1
    jj = lax.broadcasted_iota(jnp.int32, (N_HEADS, WINDOW), 1)
    bias = -(slope_ref[...] * (WINDOW - 1 - jj).astype(F32))
    pad = jnp.zeros((LANES - n_seq, KV_W), F32)
    kn_t = jnp.concatenate([kn_ref[...], pad], axis=0).T
    vn_t = jnp.concatenate([vn_ref[...], pad], axis=0).T

    def fold4(v):
        return (v[:, 0:KV_W] + v[:, KV_W:2 * KV_W]) + (v[:, 2 * KV_W:3 * KV_W] + v[:, 3 * KV_W:])

    scores, values = [], []
    for b in range(n_seq):
        kw = jnp.where(newest, jnp.broadcast_to(kn_t[:, b:b + 1], (KV_W, WINDOW)),
                       pltpu.roll(ck_ref[b], WINDOW - 1, 1))
        vw = jnp.where(newest, jnp.broadcast_to(vn_t[:, b:b + 1], (KV_W, WINDOW)),
                       pltpu.roll(cv_ref[b], WINDOW - 1, 1))
        ko_ref[b] = kw
        vo_ref[b] = vw
        qb = jnp.broadcast_to(q_ref[b:b + 1, :].astype(F32), (N_HEADS, ATT_W))
        qbd = fold4(jnp.where(own_head, qb, 0.0))
        scores.append(_dot(qbd.astype(BF16), kw.astype(BF16)))
        values.append(vw.astype(BF16))
    st = jnp.concatenate(scores, axis=0) + jnp.concatenate([bias] * n_seq, axis=0)
    sink = jnp.concatenate([sink_ref[...]] * n_seq, axis=0)
    m = jnp.maximum(jnp.max(st, axis=-1, keepdims=True), sink)
    p = jnp.exp(st - m)
    inv_l = 1.0 / (jnp.sum(p, axis=-1, keepdims=True) + jnp.exp(sink - m))
    pb = p.astype(BF16)
    for b in range(n_seq):
        hs = slice(b * N_HEADS, (b + 1) * N_HEADS)
        o = _dot_nt(pb[hs, :], values[b]) * inv_l[hs, :]
        o4 = jnp.concatenate([o, o, o, o], axis=1)
        att_scr[b:b + 1, :] = jnp.sum(jnp.where(own_head, o4, 0.0), axis=0, keepdims=True)
    o_ref[...] = _rms(att_scr[...], g_ref[...]).astype(BF16)


def _attn_sample(q, kn, vn, ck, cv, slope_col, sink_col, g_att):
    n = q.shape[0]
    nb = SEQ_ATT
    cache_spec = pl.BlockSpec((nb, KV_W, WINDOW), lambda i: (i, 0, 0))
    row_spec = lambda w: pl.BlockSpec((nb, w), lambda i: (i, 0))
    full = lambda a: pl.BlockSpec(a.shape, lambda i: (0,) * a.ndim)
    return pl.pallas_call(
        functools.partial(_attn_sample_kernel, n_seq=nb),
        grid=(n // nb,),
        in_specs=[row_spec(ATT_W), row_spec(KV_W), row_spec(KV_W), cache_spec, cache_spec,
                  full(slope_col), full(sink_col), full(g_att)],
        out_specs=[row_spec(ATT_W), cache_spec, cache_spec],
        out_shape=[
            jax.ShapeDtypeStruct((n, ATT_W), BF16),
            jax.ShapeDtypeStruct(ck.shape, F32),
            jax.ShapeDtypeStruct(cv.shape, F32),
        ],
        scratch_shapes=[pltpu.VMEM((nb, ATT_W), F32)],
        compiler_params=_params(("parallel",)),
        name="attn_sample",
    )(q, kn, vn, ck, cv, slope_col, sink_col, g_att)


def _ssd_prompt_kernel(xb_ref, z_ref, dtb_ref, alog_ref, dskip_ref, g_ref,
                       y_ref, hout_ref, ht_scr, e_scr, tri_scr, yd_scr, *, rows):
    i = pl.program_id(0)
    T = CHUNK

    @pl.when(i == 0)
    def _():
        ht_scr[...] = jnp.zeros_like(ht_scr)
        e_scr[...] = _head_expand_matrix()
        l = lax.broadcasted_iota(jnp.int32, (T, T), 0)
        s = lax.broadcasted_iota(jnp.int32, (T, T), 1)
        tri_scr[...] = jnp.where(s <= l, 1.0, 0.0).astype(BF16)

    a_log2 = -jnp.exp(alog_ref[...]) * LOG2E
    e_mat = e_scr[...]
    tri = tri_scr[...]
    causal = lax.broadcasted_iota(jnp.int32, (T, T), 0) >= lax.broadcasted_iota(jnp.int32, (T, T), 1)
    lo = lax.broadcasted_iota(jnp.int32, (T, LANES), 1) < SSM_P
    gw = SSM_W // SSM_G

    for c in range(rows // T):
        rs = slice(c * T, (c + 1) * T)
        dt = _softplus(xb_ref[rs, CONV_DIM:XBCDT_W] + dtb_ref[...])
        dta = dt * a_log2
        hi, mid, lw = _split3(dta)
        acum = _dot(tri, hi) + _dot(tri, mid) + _dot(tri, lw)
        acum_e = _expand_heads(acum, e_mat)
        dt_e = _expand_heads(dt, e_mat)
        acum_t = acum.T
        xs = xb_ref[rs, 0:SSM_W]
        bm = xb_ref[rs, SSM_W:SSM_W + SSM_G * D_STATE]
        cm = xb_ref[rs, SSM_W + SSM_G * D_STATE:CONV_DIM]
        xdt = xs * dt_e
        last = acum_e[T - 1:T, :]
        xdt_b = xdt.astype(BF16)
        xdec_b = (xdt * jnp.exp2(last - acum_e)).astype(BF16)
        bm_b = bm.astype(BF16)
        cm_b = cm.astype(BF16)
        cbs = [_dot_nt(cm_b[:, g * D_STATE:(g + 1) * D_STATE], bm_b[:, g * D_STATE:(g + 1) * D_STATE])
               for g in range(SSM_G)]
        for pr in range(SSM_H // 2):
            g = (2 * pr) // (SSM_H // SSM_G)
            ws = []
            for t in range(2):
                h = 2 * pr + t
                seg = jnp.broadcast_to(acum[:, h:h + 1], (T, T)) - acum_t[h:h + 1, :]
                ws.append((cbs[g] * jnp.exp2(jnp.where(causal, seg, NEG_INF))).astype(BF16))
            xsl = xdt_b[:, pr * LANES:(pr + 1) * LANES]
            yd_scr[:, pr * LANES:(pr + 1) * LANES] = _dot(jnp.concatenate(ws, axis=1),
                                                          _pair_blockdiag(xsl, xsl, lo, ~lo))
        yoff = []
        for g in range(SSM_G):
            gs = slice(g * gw, (g + 1) * gw)
            ht_g = ht_scr[:, gs]
            yoff.append(_dot(cm_b[:, g * D_STATE:(g + 1) * D_STATE], ht_g.astype(BF16)))
            cst = _dot_tn(bm_b[:, g * D_STATE:(g + 1) * D_STATE], xdec_b[:, gs])
            ht_scr[:, gs] = ht_g * jnp.exp2(last[:, gs]) + cst
        y = yd_scr[...] + jnp.concatenate(yoff, axis=1) * jnp.exp2(acum_e)
        y = y + dskip_ref[...] * xs
        y = y * z_ref[rs, :]
        y_ref[rs, :] = _rms(y, g_ref[...]).astype(BF16)

    @pl.when(i == pl.num_programs(0) - 1)
    def _():
        hout_ref[...] = ht_scr[...].T


def _ssd_prompt(xb, z, dtb, alog, dskip_e, g_ssm):
    n_rows = xb.shape[0]
    rows = ROWS_SSD
    full = lambda a: pl.BlockSpec(a.shape, lambda i: (0,) * a.ndim)
    return pl.pallas_call(
        functools.partial(_ssd_prompt_kernel, rows=rows),
        grid=(n_rows // rows,),
        in_specs=[
            pl.BlockSpec((rows, XBCDT_W), lambda i: (i, 0)),
            pl.BlockSpec((rows, SSM_W), lambda i: (i, 0)),
            full(dtb), full(alog), full(dskip_e), full(g_ssm),
        ],
        out_specs=[
            pl.BlockSpec((rows, SSM_W), lambda i: (i, 0)),
            pl.BlockSpec((SSM_W, D_STATE), lambda i: (0, 0)),
        ],
        out_shape=[
            jax.ShapeDtypeStruct((n_rows, SSM_W), BF16),
            jax.ShapeDtypeStruct((SSM_W, D_STATE), F32),
        ],
        scratch_shapes=[
            pltpu.VMEM((D_STATE, SSM_W), F32),
            pltpu.VMEM((LANES, SSM_W), BF16),
            pltpu.VMEM((CHUNK, CHUNK), BF16),
            pltpu.VMEM((CHUNK, SSM_W), F32),
        ],
        compiler_params=_params(("arbitrary",)),
        name="ssd_prompt",
    )(xb, z, dtb, alog, dskip_e, g_ssm)


def _ssd_sample_kernel(xb_ref, sconv_ref, z_ref, h0_ref, cw_ref, cb_ref, dtb_ref, alog_ref, dskip_ref, g_ref,
                       y_ref, conv_ref, hout_ref, yoff_scr, *, n_seq):
    gw = SSM_W // SSM_G
    gn = SSM_G * D_STATE
    x_new = xb_ref[:, 0:CONV_DIM]
    taps = [sconv_ref[:, k * CONV_DIM:(k + 1) * CONV_DIM] for k in range(CONV_W - 1)] + [x_new]
    acc = cb_ref[...]
    for k in range(CONV_W):
        acc = acc + taps[k] * cw_ref[k:k + 1, :]
    for k in range(1, CONV_W):
        conv_ref[:, (k - 1) * CONV_DIM:k * CONV_DIM] = taps[k]
    xc = _silu(acc)
    xs = xc[:, 0:SSM_W]
    bm = xc[:, SSM_W:SSM_W + gn]
    cm = xc[:, SSM_W + gn:CONV_DIM]

    e_mat = _head_expand_matrix()
    dt = _softplus(xb_ref[:, CONV_DIM:XBCDT_W] + dtb_ref[...])
    dt_e = _expand_heads(dt, e_mat)
    dec_e = jnp.exp(_expand_heads(dt * (-jnp.exp(alog_ref[...])), e_mat))
    xdt = xs * dt_e

    lane_w = lax.broadcasted_iota(jnp.int32, (n_seq, SSM_W), 1)
    first_grp = lane_w < gw
    cbv = [jnp.sum(cm[:, g * D_STATE:(g + 1) * D_STATE] * bm[:, g * D_STATE:(g + 1) * D_STATE],
                   axis=-1, keepdims=True) for g in range(SSM_G)]
    cb_e = jnp.where(first_grp, cbv[0], cbv[1])

    pad = jnp.zeros((LANES - n_seq, SSM_W), F32)
    xdt_t = jnp.concatenate([xdt, pad], axis=0).T
    dec_t = jnp.concatenate([dec_e, pad], axis=0).T
    sub8 = lax.broadcasted_iota(jnp.int32, (8, D_STATE), 0)
    lane_r = lax.broadcasted_iota(jnp.int32, (8, SSM_W), 1)

    for b in range(n_seq):
        h0 = h0_ref[b]
        c_row = jnp.broadcast_to(cm[b:b + 1, :], (8, gn))
        c8 = jnp.where(sub8 == 0, c_row[:, 0:D_STATE], jnp.where(sub8 == 1, c_row[:, D_STATE:], 0.0))
        r = _dot_nt(c8.astype(BF16), h0.astype(BF16))
        yoff_scr[b:b + 1, :] = jnp.where(lane_r[0:1] < gw, r[0:1, :], r[1:2, :])
        b_row = bm[b:b + 1, :]
        for g in range(SSM_G):
            rs = slice(g * gw, (g + 1) * gw)
            dcol = jnp.broadcast_to(dec_t[rs, b:b + 1], (gw, D_STATE))
            xcol = jnp.broadcast_to(xdt_t[rs, b:b + 1], (gw, D_STATE))
            hout_ref[b, rs, :] = h0[rs, :] * dcol + xcol * b_row[:, g * D_STATE:(g + 1) * D_STATE]

    y = cb_e * xdt + yoff_scr[...] * dec_e
    y = y + dskip_ref[...] * xs
    y = y * _silu(z_ref[...])
    y_ref[...] = _rms(y, g_ref[...]).astype(BF16)


def _ssd_sample(xb, sconv, z, h0, cw8, cb, dtb, alog, dskip_e, g_ssm):
    n = xb.shape[0]
    nb = SEQ_SSD
    row_spec = lambda w: pl.BlockSpec((nb, w), lambda i: (i, 0))
    st_spec = pl.BlockSpec((nb, SSM_W, D_STATE), lambda i: (i, 0, 0))
    full = lambda a: pl.BlockSpec(a.shape, lambda i: (0,) * a.ndim)
    return pl.pallas_call(
        functools.partial(_ssd_sample_kernel, n_seq=nb),
        grid=(n // nb,),
        in_specs=[row_spec(XBCDT_W), row_spec((CONV_W - 1) * CONV_DIM), row_spec(SSM_W), st_spec,
                  full(cw8), full(cb), full(dtb), full(alog), full(dskip_e), full(g_ssm)],
        out_specs=[row_spec(SSM_W), row_spec((CONV_W - 1) * CONV_DIM), st_spec],
        out_shape=[
            jax.ShapeDtypeStruct((n, SSM_W), BF16),
            jax.ShapeDtypeStruct((n, (CONV_W - 1) * CONV_DIM), F32),
            jax.ShapeDtypeStruct(h0.shape, F32),
        ],
        scratch_shapes=[pltpu.VMEM((nb, SSM_W), F32)],
        compiler_params=_params(("parallel",)),
        name="ssd_sample",
    )(xb, sconv, z, h0, cw8, cb, dtb, alog, dskip_e, g_ssm)


def _outproj_rows(att, ssm, x, woa_ref, wos_ref, gpost, gt1, gpre, sh2, sc2):
    mix = _dot(att, woa_ref[...]) + _dot(ssm, wos_ref[...])
    x1 = x + _rms(mix, gpost * gt1)
    u2 = _rms(x1, gpre * (1.0 + sc2)) + sh2
    return x1, u2.astype(BF16)


def _outproj_kernel(att_ref, ssm_ref, x_ref, woa_ref, wos_ref, gpost_ref, gt1_ref, gpre_ref, sh2_ref, sc2_ref,
                    atts_ref, ssms_ref, xs_ref, gt1s_ref, sh2s_ref, sc2s_ref, perm_ref,
                    x1_ref, u2_ref, x1s_ref, u2s_ref, *, rows):
    for r0 in range(0, rows, SUB_OUT):
        rs = slice(r0, r0 + SUB_OUT)
        x1_ref[rs, :], u2_ref[rs, :] = _outproj_rows(
            att_ref[rs, :], ssm_ref[rs, :], x_ref[rs, :], woa_ref, wos_ref, gpost_ref[...], gt1_ref[0:1, :],
            gpre_ref[...], sh2_ref[0:1, :], sc2_ref[0:1, :])

    @pl.when(pl.program_id(0) == pl.num_programs(0) - 1)
    def _():
        att_s = _dot_nt(atts_ref[...], perm_ref[...]).astype(BF16)
        x1s_ref[...], u2s_ref[...] = _outproj_rows(
            att_s, ssms_ref[...], xs_ref[...], woa_ref, wos_ref, gpost_ref[...], gt1s_ref[...],
            gpre_ref[...], sh2s_ref[...], sc2s_ref[...])


def _outproj(att, ssm, x, att_s, ssm_s, x_s, mod_p, mod_s, w_out, g_post, g_pre, perm):
    n_rows = x.shape[0]
    n_s = x_s.shape[0]
    rows = ROWS_OUT
    assert rows % SUB_OUT == 0 and n_rows % rows == 0
    half = lambda r: pl.BlockSpec((ATT_W, D_MODEL), lambda i: (r, 0), pipeline_mode=pl.Buffered(1))
    row_spec = lambda w: pl.BlockSpec((rows, w), lambda i: (i, 0))
    sample = lambda w, col=0: pl.BlockSpec((n_s, w), lambda i: (0, col))
    return pl.pallas_call(
        functools.partial(_outproj_kernel, rows=rows),
        grid=(n_rows // rows,),
        in_specs=[
            row_spec(ATT_W), row_spec(SSM_W), row_spec(D_MODEL),
            half(0), half(1),
            _const_spec((1, D_MODEL)),
            _mod_spec(2),
            _const_spec((1, D_MODEL)),
            _mod_spec(3),
            _mod_spec(4),
            sample(ATT_W), sample(SSM_W), sample(D_MODEL),
            sample(D_MODEL, 2), sample(D_MODEL, 3), sample(D_MODEL, 4), _const_spec(perm.shape),
        ],
        out_specs=[row_spec(D_MODEL), row_spec(D_MODEL), sample(D_MODEL), sample(D_MODEL)],
        out_shape=[
            jax.ShapeDtypeStruct((n_rows, D_MODEL), F32),
            jax.ShapeDtypeStruct((n_rows, D_MODEL), BF16),
            jax.ShapeDtypeStruct((n_s, D_MODEL), F32),
            jax.ShapeDtypeStruct((n_s, D_MODEL), BF16),
        ],
        compiler_params=_params(("arbitrary",)),
        name="outproj",
    )(att, ssm, x, w_out, w_out, g_post, mod_p, g_pre, mod_p, mod_p, att_s, ssm_s, x_s, mod_s, mod_s, mod_s, perm)


def _ffn_kernel(u_ref, x1_hbm, wg_ref, wu_ref, wd_ref, gpost_ref, gt2_ref, us_ref, x1s_ref, gt2s_ref,
                y_ref, ys_ref, x1_buf, x1_sem, *, rows):
    i = pl.program_id(0)
    j = pl.program_id(1)
    x1_copy = pltpu.make_async_copy(x1_hbm.at[pl.ds(i * rows, rows), :], x1_buf, x1_sem)
    last = pl.num_programs(1) - 1

    def ffn_rows(u, first, final, acc_prev, resid, gt2):
        hid = (_silu(_dot(u, wg_ref[...])) * _dot(u, wu_ref[...])).astype(BF16)
        acc = _dot(hid, wd_ref[...])
        if not first:
            acc = acc_prev() + acc
        if final:
            acc = resid() + gt2() * _rms(acc, gpost_ref[...])
        return acc

    def d_ff_slice(first, final):
        sub = min(SUB_FFN, rows)
        for r0 in range(0, rows, sub):
            rs = slice(r0, r0 + sub)
            y_ref[rs, :] = ffn_rows(u_ref[rs, :], first, final, lambda: y_ref[rs, :],
                                    lambda: x1_buf[rs, :], lambda: gt2_ref[0:1, :])

        @pl.when(i == pl.num_programs(0) - 1)
        def _():
            ys_ref[...] = ffn_rows(us_ref[...], first, final, lambda: ys_ref[...],
                                   lambda: x1s_ref[...], lambda: gt2s_ref[...])

    @pl.when(j == 0)
    def _():
        x1_copy.start()
        d_ff_slice(True, False)

    @pl.when(jnp.logical_and(j > 0, j < last))
    def _():
        d_ff_slice(False, False)

    @pl.when(j == last)
    def _():
        x1_copy.wait()
        d_ff_slice(False, True)


def _ffn(u2, x1, mod_p, u2_s, x1_s, mod_s, wg, wu, wd, g_post):
    n_rows = x1.shape[0]
    n_s = x1_s.shape[0]
    rows = ROWS_FFN
    fb = FF_BLK
    assert D_FF // fb >= 2 and n_rows % rows == 0
    whole = lambda r, c: pl.BlockSpec((r, D_MODEL), lambda i, j: (0, c))
    return pl.pallas_call(
        functools.partial(_ffn_kernel, rows=rows),
        grid=(n_rows // rows, D_FF // fb),
        in_specs=[
            pl.BlockSpec((rows, D_MODEL), lambda i, j: (i, 0)),
            pl.BlockSpec(memory_space=pl.ANY),
            pl.BlockSpec((D_MODEL, fb), lambda i, j: (0, j)),
            pl.BlockSpec((D_MODEL, fb), lambda i, j: (0, j)),
            pl.BlockSpec((fb, D_MODEL), lambda i, j: (j, 0)),
            whole(1, 0),
            _mod_spec(5),
            whole(n_s, 0), whole(n_s, 0), whole(n_s, 5),
        ],
        out_specs=[pl.BlockSpec((rows, D_MODEL), lambda i, j: (i, 0)), whole(n_s, 0)],
        out_shape=[jax.ShapeDtypeStruct((n_rows, D_MODEL), F32), jax.ShapeDtypeStruct((n_s, D_MODEL), F32)],
        scratch_shapes=[pltpu.VMEM((rows, D_MODEL), F32), pltpu.SemaphoreType.DMA],
        compiler_params=_params(("arbitrary", "arbitrary"), VMEM_LIMIT_FFN),
        name="ffn",
    )(u2, x1, wg, wu, wd, g_post, mod_p, u2_s, x1_s, mod_s)


def _alibi_slopes():
    return (2.0 ** (-8.0 * np.arange(1, N_HEADS + 1) / N_HEADS)).astype(np.float32)


def kernel(x_prompt, x_sample, cache_k, cache_v, state_conv, state_ssm, c_prompt, c_sample, w_ada, b_ada, g_pre_mix, g_post_mix, w_in, attn_sinks, g_attn_out, conv_w, conv_b, dt_bias, a_log, d_skip, g_ssm_out, w_out, g_pre_ffn, g_post_ffn, w_gate, w_up, w_down):
    assert w_ada.shape[0] == 1, "one layer"
    n_s = x_sample.shape[0]
    row = lambda v: v.reshape(1, -1)

    g_att = row(g_attn_out[0])
    swapped_cols = np.arange(ATT_W).reshape(N_KV, Q_PER_KV, HEAD_DIM).swapaxes(0, 1).reshape(-1)
    perm = np.zeros((ATT_W, ATT_W), np.float32)
    perm[swapped_cols, np.arange(ATT_W)] = 1.0
    perm = jnp.asarray(perm, BF16)
    g_att_s = row(g_attn_out[0].reshape(N_KV, Q_PER_KV, HEAD_DIM).swapaxes(0, 1))
    slopes_np = _alibi_slopes()
    slopes = jnp.asarray(slopes_np)
    slopes_s = jnp.asarray(slopes_np.reshape(N_KV, Q_PER_KV).T.reshape(N_HEADS, 1))
    sinks = attn_sinks[0]
    sinks_s = sinks.reshape(N_KV, Q_PER_KV).T.reshape(N_HEADS, 1)
    cw8 = jnp.pad(conv_w[0], ((0, 8 - CONV_W), (0, 0)))
    cb = row(conv_b[0])
    dtb = jnp.pad(row(dt_bias[0]), ((0, 0), (0, LANES - SSM_H)))
    alog = jnp.pad(row(a_log[0]), ((0, 0), (0, LANES - SSM_H)))
    dskip_e = row(jnp.repeat(d_skip[0], SSM_P))
    g_ssm = row(g_ssm_out[0])

    c_p8 = jnp.pad(c_prompt, ((0, 8 - c_prompt.shape[0]), (0, 0)))
    mod_s, mod_p, w_all = _prep(c_sample, c_p8, w_ada[0], row(b_ada[0]), jnp.transpose(w_in[0]))

    xp = x_prompt[0]
    xs_ = x_sample[:, 0, :]
    q, kv, z, xb, kv_last, x_tail, q_s, kv_new, z_s, xb_s = _inproj(
        xp, xs_, row(g_pre_mix[0]), mod_p, mod_s, w_all, cw8, cb, perm)

    att, wg, wu, wd, wo = _attn_prompt(q, kv, slopes, sinks, g_att, w_gate[0], w_up[0], w_down[0], w_out[0])
    ssm, h_p = _ssd_prompt(xb, z, dtb, alog, dskip_e, g_ssm)
    keys_minor = lambda c: jnp.transpose(c, (0, 2, 3, 1)).reshape(n_s, KV_W, WINDOW)
    keys_major = lambda c: jnp.transpose(c.reshape(n_s, N_KV, HEAD_DIM, WINDOW), (0, 3, 1, 2))[None]
    att_s, k_s, v_s = _attn_sample(
        q_s, kv_new[:, :KV_W], kv_new[:, KV_W:],
        keys_minor(cache_k[0]), keys_minor(cache_v[0]), slopes_s, sinks_s, g_att_s)
    ssm_s, conv_s, h_s = _ssd_sample(
        xb_s, state_conv[0].reshape(n_s, (CONV_W - 1) * CONV_DIM), z_s,
        state_ssm[0].reshape(n_s, SSM_W, D_STATE), cw8, cb, dtb, alog, dskip_e, g_ssm)

    x1, u2, x1_s, u2_s = _outproj(att, ssm, xp, att_s, ssm_s, xs_, mod_p, mod_s, wo,
                                  row(g_post_mix[0]), row(g_pre_ffn[0]), perm)
    y_p, y_s = _ffn(u2, x1, mod_p, u2_s, x1_s, mod_s, wg, wu, wd, row(g_post_ffn[0]))

    return (
        y_p[None],
        y_s[:, None, :],
        kv_last[:, :KV_W].reshape(1, 1, WINDOW, N_KV, HEAD_DIM),
        kv_last[:, KV_W:].reshape(1, 1, WINDOW, N_KV, HEAD_DIM),
        x_tail[8 - (CONV_W - 1):].reshape(1, 1, CONV_W - 1, CONV_DIM),
        h_p.reshape(1, 1, SSM_H, SSM_P, D_STATE),
        keys_major(k_s),
        keys_major(v_s),
        conv_s.reshape(1, n_s, CONV_W - 1, CONV_DIM),
        h_s.reshape(1, n_s, SSM_H, SSM_P, D_STATE),
    )
```
